```python
import math
import jax, jax.numpy as jnp
from jax import lax
import numpy as np

D_MODEL = 1024
BATCH = 4
SEQ = 4096
DEPTH = 2

N_A_LAYERS = DEPTH // 2
N_B_LAYERS = DEPTH - N_A_LAYERS
D_FF = 2816
MOBA_HEADS = 16
MOBA_HEAD_DIM = D_MODEL // MOBA_HEADS
MOBA_BLOCK = 256
MOBA_TOPK = 3
MOBA_Q_CHUNK = 64
REL_BUCKETS = 32
REL_MAX_DIST = 128
MLA_HEADS = 16
MLA_Q_LORA = 384
MLA_KV_LORA = 256
MLA_NOPE = 64
MLA_ROPE = 32
MLA_V = 64
MLA_QK = MLA_NOPE + MLA_ROPE
ATTN_Q_BLOCK = 128
ROPE_BASE = 10000.0
EPS = 1e-6
NEG = -1e30

kernel_name = "yoco_moba_mla_macaron_adaln"


def rms_norm(x, g):
    xf = x.astype(jnp.float32)
    y = xf * lax.rsqrt(jnp.mean(xf * xf, axis=-1, keepdims=True) + EPS)
    return (y * g.astype(jnp.float32)).astype(x.dtype)


def modulate(h, shift, scale):
    return h * (1 + scale[:, None, :]) + shift[:, None, :]


def swiglu(h, w_gate, w_up, w_down):
    return (jax.nn.silu(h @ w_gate) * (h @ w_up)) @ w_down


def t5_bucket(rel):
    n = jnp.maximum(rel, 0)
    max_exact = REL_BUCKETS // 2
    nf = jnp.maximum(n, 1).astype(jnp.float32)
    large = max_exact + (jnp.log(nf / max_exact) / math.log(REL_MAX_DIST / max_exact)
                         * (REL_BUCKETS - max_exact)).astype(jnp.int32)
    large = jnp.minimum(large, REL_BUCKETS - 1)
    return jnp.where(n < max_exact, n, large)


def rope(x, positions):
    half = x.shape[-1] // 2
    inv = ROPE_BASE ** (-jnp.arange(half, dtype=jnp.float32) / half)
    ang = positions.astype(jnp.float32)[..., None] * inv
    cos = jnp.cos(ang)[:, :, None, :]
    sin = jnp.sin(ang)[:, :, None, :]
    x1 = x[..., :half].astype(jnp.float32)
    x2 = x[..., half:].astype(jnp.float32)
    out = jnp.concatenate([x1 * cos - x2 * sin, x2 * cos + x1 * sin], axis=-1)
    return out.astype(x.dtype)


def moba_attention(h, w_qkv, q_g, k_g, w_o, rel_bias, positions):
    B, S, _ = h.shape
    H, Dh, L = MOBA_HEADS, MOBA_HEAD_DIM, MOBA_BLOCK
    qkv = (h @ w_qkv).reshape(B, S, 3, H, Dh)
    q = rms_norm(qkv[:, :, 0], q_g).transpose(0, 2, 1, 3)
    k = rms_norm(qkv[:, :, 1], k_g)
    v = qkv[:, :, 2]
    nb = -(-S // L)
    pad = nb * L - S
    kb = jnp.pad(k, ((0, 0), (0, pad), (0, 0), (0, 0))).reshape(B, nb, L, H, Dh).transpose(0, 3, 1, 2, 4)
    vb = jnp.pad(v, ((0, 0), (0, pad), (0, 0), (0, 0))).reshape(B, nb, L, H, Dh).transpose(0, 3, 1, 2, 4)
    pos_p = jnp.pad(positions, ((0, 0), (0, pad)), mode="edge")
    kmean = jnp.mean(kb.astype(jnp.float32), axis=3)
    gate = jnp.einsum("bhsd,bhnd->bhsn", q.astype(jnp.float32), kmean)
    q_index = jnp.arange(S)
    q_blk = q_index // L
    past = jnp.arange(nb)[None, :] < q_blk[:, None]
    gate = jnp.where(past, gate, -jnp.inf)
    k_sel = min(MOBA_TOPK, nb)
    _, sel = lax.top_k(gate, k_sel)
    own = jnp.broadcast_to(q_blk[None, None, :, None], (B, H, S, 1)).astype(sel.dtype)
    idx = jnp.concatenate([sel, own], axis=-1)
    nslot = k_sel + 1
    slot_valid = jnp.concatenate([jnp.arange(k_sel)[None, :] < q_blk[:, None],
                                  jnp.ones((S, 1), dtype=bool)], axis=-1)
    QC = MOBA_Q_CHUNK
    nc = S // QC
    qs = q.reshape(B, H, nc, QC, Dh).transpose(2, 0, 1, 3, 4)
    idxs = idx.reshape(B, H, nc, QC, nslot).transpose(2, 0, 1, 3, 4)
    qis = q_index.reshape(nc, QC)
    qps = positions.reshape(B, nc, QC).transpose(1, 0, 2)
    svs = slot_valid.reshape(nc, QC, nslot)
    b4 = jnp.arange(B)[:, None, None, None]
    h4 = jnp.arange(H)[None, :, None, None]
    b5 = b4[..., None]
    h5 = h4[..., None]
    bias_t = rel_bias.T
    scale = Dh ** -0.5

    def chunk(args):
        qc, ic, qi, qp, sv = args
        kg = kb[b4, h4, ic]
        vg = vb[b4, h4, ic]
        kidx = ic[..., None] * L + jnp.arange(L)
        kpos = pos_p[b5, kidx]
        bias = bias_t[h5, t5_bucket(qp[:, None, :, None, None] - kpos)]
        s = jnp.einsum("bhqd,bhqnkd->bhqnk", qc, kg).astype(jnp.float32) * scale + bias.astype(jnp.float32)
        mask = (kidx <= qi[None, None, :, None, None]) & sv[None, None, :, :, None]
        s = jnp.where(mask, s, NEG)
        p = jax.nn.softmax(s.reshape(B, H, QC, nslot * L), axis=-1).reshape(s.shape).astype(vg.dtype)
        return jnp.einsum("bhqnk,bhqnkd->bhqd", p, vg)

    o = lax.map(chunk, (qs, idxs, qis, qps, svs))
    o = o.transpose(1, 0, 3, 2, 4).reshape(B, S, H * Dh)
    return o @ w_o


def causal_block_attention(q, k, v, scale):
    B, S, H, D = q.shape
    nq = S // ATTN_Q_BLOCK
    qb = q.reshape(B, nq, ATTN_Q_BLOCK, H, D).transpose(1, 0, 2, 3, 4)
    kpos = jnp.arange(S)

    def one(args):
        qi, start = args
        s = jnp.einsum("bqhd,bkhd->bhqk", qi, k).astype(jnp.float32) * scale
        mask = kpos[None, :] <= (start + jnp.arange(ATTN_Q_BLOCK))[:, None]
        s = jnp.where(mask, s, NEG)
        p = jax.nn.softmax(s, axis=-1).astype(v.dtype)
        return jnp.einsum("bhqk,bkhd->bqhd", p, v)

    o = lax.map(one, (qb, jnp.arange(nq) * ATTN_Q_BLOCK))
    return o.transpose(1, 0, 2, 3, 4).reshape(B, S, H, v.shape[-1])


def mla_shared_kv(x, c_act, kv_ada_w, kv_ada_b, kv_norm_g, w_dkv, kv_a_norm_g, w_uk, w_uv, k_g, positions):
    B, S, _ = x.shape
    shift, scale = jnp.split(c_act @ kv_ada_w + kv_ada_b, 2, axis=-1)
    hs = modulate(rms_norm(x, kv_norm_g), shift, scale)
    ckv_kpe = hs @ w_dkv
    ckv = rms_norm(ckv_kpe[..., :MLA_KV_LORA], kv_a_norm_g)
    kpe = ckv_kpe[..., MLA_KV_LORA:]
    k_nope = (ckv @ w_uk).reshape(B, S, MLA_HEADS, MLA_NOPE)
    v = (ckv @ w_uv).reshape(B, S, MLA_HEADS, MLA_V)
    k = jnp.concatenate([k_nope, jnp.broadcast_to(kpe[:, :, None, :], (B, S, MLA_HEADS, MLA_ROPE))], axis=-1)
    k = rms_norm(k, k_g)
    k = jnp.concatenate([k[..., :MLA_NOPE], rope(k[..., MLA_NOPE:], positions)], axis=-1)
    return k, v


def mla_attention(h, w_dq, q_a_norm_g, w_uq, q_g, w_o, k, v, positions):
    B, S, _ = h.shape
    q = (rms_norm(h @ w_dq, q_a_norm_g) @ w_uq).reshape(B, S, MLA_HEADS, MLA_QK)
    q = rms_norm(q, q_g)
    q = jnp.concatenate([q[..., :MLA_NOPE], rope(q[..., MLA_NOPE:], positions)], axis=-1)
    o = causal_block_attention(q, k, v, MLA_QK ** -0.5)
    return o.reshape(B, S, MLA_HEADS * MLA_V) @ w_o


def setup_inputs(seed: int = 0) -> dict:
    key = jax.random.key(seed)
    ks = iter(jax.random.split(key, 32))
    D, F = D_MODEL, D_FF
    NA, NB = N_A_LAYERS, N_B_LAYERS

    def nrm(shape, fan_in, mult=1.0):
        return jax.random.normal(next(ks), shape, jnp.float32) * (mult * fan_in ** -0.5)

    def gain(shape):
        return 1.0 + 0.02 * jax.random.normal(next(ks), shape, jnp.float32)

    def small(shape, s=0.02):
        return s * jax.random.normal(next(ks), shape, jnp.float32)

    x = jax.random.normal(next(ks), (BATCH, SEQ, D), jnp.float32)
    c = jax.random.normal(next(ks), (BATCH, D), jnp.float32)
    offsets = jax.random.randint(next(ks), (BATCH,), 0, 1024, dtype=jnp.int32)
    positions = offsets[:, None] + jnp.arange(SEQ, dtype=jnp.int32)[None, :]
    return {
        "x": x,
        "c": c,
        "positions": positions,
        "ada_w": nrm((DEPTH, D, 9 * D), D, 0.5),
        "ada_b": small((DEPTH, 9 * D)),
        "norm_g": gain((DEPTH, 3, D)),
        "ffn_w_gate": nrm((DEPTH, 2, D, F), D),
        "ffn_w_up": nrm((DEPTH, 2, D, F), D),
        "ffn_w_down": nrm((DEPTH, 2, F, D), F),
        "rel_bias": small((REL_BUCKETS, MOBA_HEADS), 0.5),
        "moba_w_qkv": nrm((NA, D, 3 * MOBA_HEADS * MOBA_HEAD_DIM), D),
        "moba_q_g": gain((NA, MOBA_HEAD_DIM)),
        "moba_k_g": gain((NA, MOBA_HEAD_DIM)),
        "moba_w_o": nrm((NA, MOBA_HEADS * MOBA_HEAD_DIM, D), MOBA_HEADS * MOBA_HEAD_DIM),
        "kv_ada_w": nrm((D, 2 * D), D, 0.5),
        "kv_ada_b": small((2 * D,)),
        "kv_norm_g": gain((D,)),
        "w_dkv": nrm((D, MLA_KV_LORA + MLA_ROPE), D),
        "kv_a_norm_g": gain((MLA_KV_LORA,)),
        "w_uk": nrm((MLA_KV_LORA, MLA_HEADS * MLA_NOPE), MLA_KV_LORA),
        "w_uv": nrm((MLA_KV_LORA, MLA_HEADS * MLA_V), MLA_KV_LORA),
        "mla_k_g": gain((MLA_QK,)),
        "mla_w_dq": nrm((NB, D, MLA_Q_LORA), D),
        "mla_q_a_norm_g": gain((NB, MLA_Q_LORA)),
        "mla_w_uq": nrm((NB, MLA_Q_LORA, MLA_HEADS * MLA_QK), MLA_Q_LORA),
        "mla_q_g": gain((NB, MLA_QK)),
        "mla_w_o": nrm((NB, MLA_HEADS * MLA_V, D), MLA_HEADS * MLA_V),
    }


def reference(x, c, positions, ada_w, ada_b, norm_g, ffn_w_gate, ffn_w_up, ffn_w_down, rel_bias,
              moba_w_qkv, moba_q_g, moba_k_g, moba_w_o, kv_ada_w, kv_ada_b, kv_norm_g, w_dkv,
              kv_a_norm_g, w_uk, w_uv, mla_k_g, mla_w_dq, mla_q_a_norm_g, mla_w_uq, mla_q_g, mla_w_o):
    c_act = jax.nn.silu(c)
    shared_k = None
    shared_v = None
    for layer in range(DEPTH):
        mods = c_act @ ada_w[layer] + ada_b[layer]
        sh1, sc1, g1, sh2, sc2, g2, sh3, sc3, g3 = jnp.split(mods, 9, axis=-1)
        h = modulate(rms_norm(x, norm_g[layer, 0]), sh1, sc1)
        x = x + 0.5 * g1[:, None, :] * swiglu(h, ffn_w_gate[layer, 0], ffn_w_up[layer, 0], ffn_w_down[layer, 0])
        h = modulate(rms_norm(x, norm_g[layer, 1]), sh2, sc2)
        if layer < N_A_LAYERS:
            mix = moba_attention(h, moba_w_qkv[layer], moba_q_g[layer], moba_k_g[layer],
                                 moba_w_o[layer], rel_bias, positions)
        else:
            if layer == N_A_LAYERS:
                shared_k, shared_v = mla_shared_kv(x, c_act, kv_ada_w, kv_ada_b, kv_norm_g, w_dkv,
                                                   kv_a_norm_g, w_uk, w_uv, mla_k_g, positions)
            j = layer - N_A_LAYERS
            mix = mla_attention(h, mla_w_dq[j], mla_q_a_norm_g[j], mla_w_uq[j], mla_q_g[j],
                                mla_w_o[j], shared_k, shared_v, positions)
        x = x + g2[:, None, :] * mix
        h = modulate(rms_norm(x, norm_g[layer, 2]), sh3, sc3)
        x = x + 0.5 * g3[:, None, :] * swiglu(h, ffn_w_gate[layer, 1], ffn_w_up[layer, 1], ffn_w_down[layer, 1])
    return x
```

```python
import functools
import math

import jax
import jax.numpy as jnp
from jax import lax
from jax.experimental import pallas as pl
from jax.experimental.pallas import tpu as pltpu

F32 = jnp.float32
BF16 = jnp.bfloat16

LANES = 128
SUBLANES = 8
VMEM_LIMIT_BYTES = 56 * 1024 * 1024

MOBA_HEADS = 16
MOBA_HEAD_DIM = 64
MOBA_BLOCK = 256
MOBA_TOPK = 3
REL_BUCKETS = 32
REL_MAX_DIST = 128
MLA_HEADS = 16
MLA_KV_LORA = 256
MLA_NOPE = 64
MLA_ROPE = 32
MLA_V = 64
MLA_QK = MLA_NOPE + MLA_ROPE
ROPE_BASE = 10000.0
EPS = 1e-6
NEG = -1e30

ATTN_TILE = 256
BIAS_TABLE = LANES


def _params(*sem):
    return pltpu.CompilerParams(dimension_semantics=sem, vmem_limit_bytes=VMEM_LIMIT_BYTES)


def _dot(a, b):
    return jnp.dot(a, b, preferred_element_type=F32)


def _dot_nt(a, b):
    return lax.dot_general(a, b, (((1,), (1,)), ((), ())), preferred_element_type=F32)


def _rms_mod(x, g, shift, scale):
    ms = jnp.mean(x * x, axis=-1, keepdims=True)
    y = x * lax.rsqrt(ms + EPS) * g
    return y * (1.0 + scale) + shift


def _mods_kernel(c_ref, w_ref, b_ref, o_ref):
    c = c_ref[...]
    ca = c * jax.nn.sigmoid(c)
    o_ref[0] = jnp.dot(ca, w_ref[0], precision=lax.Precision.HIGHEST,
                       preferred_element_type=F32) + b_ref[0]


def _mods(c_pad, w, b, tn):
    nl, d, n = w.shape
    return pl.pallas_call(
        _mods_kernel,
        grid=(nl, n // tn),
        in_specs=[pl.BlockSpec((SUBLANES, d), lambda l, j: (0, 0)),
                  pl.BlockSpec((1, d, tn), lambda l, j: (l, 0, j)),
                  pl.BlockSpec((1, 1, tn), lambda l, j: (l, 0, j))],
        out_specs=pl.BlockSpec((1, SUBLANES, tn), lambda l, j: (l, 0, j)),
        out_shape=jax.ShapeDtypeStruct((nl, SUBLANES, n), F32),
        compiler_params=_params("arbitrary", "arbitrary"),
        name="adaln_mods",
    )(c_pad, w, b)


def _ffn_kernel(x_ref, sh_ref, sc_ref, gt_ref, g_ref, wg_ref, wu_ref, wd_ref, o_ref, h_scr, acc_scr):
    f = pl.program_id(2)

    @pl.when(f == 0)
    def _():
        h_scr[...] = _rms_mod(x_ref[0], g_ref[...], sh_ref[0], sc_ref[0]).astype(BF16)
        acc_scr[...] = jnp.zeros_like(acc_scr)

    h = h_scr[...]
    g = _dot(h, wg_ref[...])
    u = _dot(h, wu_ref[...])
    a = (g * jax.nn.sigmoid(g) * u).astype(BF16)
    acc_scr[...] += _dot(a, wd_ref[...])

    @pl.when(f == pl.num_programs(2) - 1)
    def _():
        o_ref[0] = x_ref[0] + (0.5 * gt_ref[0]) * acc_scr[...]


def _ffn(x, shift, scale, gate, g, wg, wu, wd, tm, tf):
    b, s, d = x.shape
    f = wg.shape[1]
    row = pl.BlockSpec((1, tm, d), lambda bi, i, j: (bi, i, 0))
    vec = pl.BlockSpec((1, 1, d), lambda bi, i, j: (bi, 0, 0))
    return pl.pallas_call(
        _ffn_kernel,
        grid=(b, s // tm, f // tf),
        in_specs=[row, vec, vec, vec,
                  pl.BlockSpec((1, d), lambda bi, i, j: (0, 0)),
                  pl.BlockSpec((d, tf), lambda bi, i, j: (0, j)),
                  pl.BlockSpec((d, tf), lambda bi, i, j: (0, j)),
                  pl.BlockSpec((tf, d), lambda bi, i, j: (j, 0))],
        out_specs=row,
        out_shape=jax.ShapeDtypeStruct((b, s, d), F32),
        scratch_shapes=[pltpu.VMEM((tm, d), BF16), pltpu.VMEM((tm, d), F32)],
        compiler_params=_params("arbitrary", "arbitrary", "arbitrary"),
        name="ffn_swiglu",
    )(x, shift, scale, gate, g, wg, wu, wd)


def _headnorm_pair(y, g, dim):
    lane = lax.broadcasted_iota(jnp.int32, (1, LANES), 1)
    left = lane < dim
    y2 = y * y
    ss_a = jnp.sum(jnp.where(left, y2, 0.0), axis=-1, keepdims=True)
    ss_b = jnp.sum(jnp.where(left, 0.0, y2), axis=-1, keepdims=True)
    inv = jnp.where(left, lax.rsqrt(ss_a * (1.0 / dim) + EPS), lax.rsqrt(ss_b * (1.0 / dim) + EPS))
    return y * inv * g


def _moba_proj_kernel(x_ref, sh_ref, sc_ref, g_ref, w_ref, hg_ref, o_ref, h_scr, *, n_norm_tiles):
    n = pl.program_id(2)

    @pl.when(n == 0)
    def _():
        h_scr[...] = _rms_mod(x_ref[0], g_ref[...], sh_ref[0], sc_ref[0]).astype(BF16)

    res = _dot(h_scr[...], w_ref[...])
    tn = res.shape[1]

    @pl.when(n < n_norm_tiles)
    def _():
        for s in range(tn // LANES):
            sl = slice(s * LANES, (s + 1) * LANES)
            o_ref[0, :, sl] = _headnorm_pair(res[:, sl], hg_ref[:, sl], MOBA_HEAD_DIM).astype(BF16)

    @pl.when(n >= n_norm_tiles)
    def _():
        o_ref[0] = res.astype(BF16)


def _moba_proj(x, shift, scale, g, w, head_gain, tm, tn):
    b, s, d = x.shape
    nout = w.shape[1]
    n_norm_tiles = (2 * MOBA_HEADS * MOBA_HEAD_DIM) // tn
    vec = pl.BlockSpec((1, 1, d), lambda bi, i, j: (bi, 0, 0))
    return pl.pallas_call(
        functools.partial(_moba_proj_kernel, n_norm_tiles=n_norm_tiles),
        grid=(b, s // tm, nout // tn),
        in_specs=[pl.BlockSpec((1, tm, d), lambda bi, i, j: (bi, i, 0)), vec, vec,
                  pl.BlockSpec((1, d), lambda bi, i, j: (0, 0)),
                  pl.BlockSpec((d, tn), lambda bi, i, j: (0, j)),
                  pl.BlockSpec((1, tn), lambda bi, i, j: (0, j))],
        out_specs=pl.BlockSpec((1, tm, tn), lambda bi, i, j: (bi, i, j)),
        out_shape=jax.ShapeDtypeStruct((b, s, nout), BF16),
        scratch_shapes=[pltpu.VMEM((tm, d), BF16)],
        compiler_params=_params("arbitrary", "arbitrary", "arbitrary"),
        name="moba_qkv_proj",
    )(x, shift, scale, g, w, head_gain)


def _softmax_step(s, m, l, v2):
    m_new = jnp.maximum(m, jnp.max(s, axis=-1, keepdims=True))
    alpha = jnp.exp(m - m_new)
    p = jnp.exp(s - m_new)
    l_new = alpha * l + jnp.sum(p, axis=-1, keepdims=True)
    pv = _dot(p.astype(BF16), v2)
    return m_new, l_new, alpha, pv


def _causal_mask(t):
    row = lax.broadcasted_iota(jnp.int32, (t, t), 0)
    col = lax.broadcasted_iota(jnp.int32, (t, t), 1)
    return col <= row


def _moba_attn_kernel(q_ref, k_ref, v_ref, bt_ref, pq_ref, pk_ref, o_ref, km_scr, *, nb):
    t = ATTN_TILE
    hd = MOBA_HEAD_DIM
    qb = pl.program_id(2)
    lane = lax.broadcasted_iota(jnp.int32, (1, LANES), 1)
    left = lane < hd

    @pl.when(qb == 0)
    def _():
        km_scr[...] = jnp.zeros_like(km_scr)
        for r in range(nb):
            blk = k_ref[0, r * t:(r + 1) * t, :].astype(F32)
            km_scr[r:r + 1, :] = jnp.sum(blk, axis=0, keepdims=True) * (1.0 / t)

    q2 = q_ref[0]
    zero = jnp.zeros_like(q2)
    q_heads = (jnp.where(left, q2, zero), jnp.where(left, zero, q2))
    scale = hd ** -0.5
    qs_heads = tuple((qh.astype(F32) * scale).astype(BF16) for qh in q_heads)

    km = km_scr[...]
    km_hi = km.astype(BF16)
    km_lo = (km - km_hi.astype(F32)).astype(BF16)
    past = lane < qb

    def select(qh):
        gate = _dot_nt(qh, km_hi) + _dot_nt(qh, km_lo)
        gate = jnp.where(past, gate, -jnp.inf)
        sel = jnp.zeros(gate.shape, F32)
        for r in range(MOBA_TOPK):
            mx = jnp.max(gate, axis=-1, keepdims=True)
            idx = jnp.min(jnp.where(gate == mx, lane, LANES), axis=-1, keepdims=True)
            pick = (lane == idx) & (mx > -jnp.inf)
            sel = jnp.where(pick, 1.0, sel)
            gate = jnp.where(pick, -jnp.inf, gate)
        return sel

    sel_heads = (select(q_heads[0]), select(q_heads[1]))

    pq = pq_ref[0]
    tables = (jnp.broadcast_to(bt_ref[0, 0:1, :], (t, LANES)),
              jnp.broadcast_to(bt_ref[0, 1:2, :], (t, LANES)))

    def tile(j, carry, diag):
        (m_a, l_a, m_b, l_b, acc) = carry
        start = pl.multiple_of(j * t, t)
        k2 = k_ref[0, pl.ds(start, t), :]
        v2 = v_ref[0, pl.ds(start, t), :]
        dist = jnp.clip(pq - pk_ref[0, j], 0, BIAS_TABLE - 1)
        d_lo, d_hi = dist[:, :LANES], dist[:, LANES:]
        outs = []
        for h, (m, l) in enumerate(((m_a, l_a), (m_b, l_b))):
            s = _dot_nt(qs_heads[h], k2)
            bias = jnp.concatenate(
                [jnp.take_along_axis(tables[h], d_lo, axis=1, mode="promise_in_bounds"),
                 jnp.take_along_axis(tables[h], d_hi, axis=1, mode="promise_in_bounds")], axis=1)
            s = s + bias
            if diag:
                s = jnp.where(_causal_mask(t), s, NEG)
            else:
                chosen = jnp.max(jnp.where(lane == j, sel_heads[h], 0.0), axis=-1, keepdims=True)
                s = jnp.where(chosen > 0.0, s, NEG)
            outs.append(_softmax_step(s, m, l, v2))
        (m_a, l_a, al_a, pv_a), (m_b, l_b, al_b, pv_b) = outs
        acc = acc * jnp.where(left, al_a, al_b) + jnp.where(left, pv_a, pv_b)
        return (m_a, l_a, m_b, l_b, acc)

    col0 = jnp.full((t, 1), NEG, F32)
    zero1 = jnp.zeros((t, 1), F32)
    carry = tile(qb, (col0, zero1, col0, zero1, jnp.zeros((t, LANES), F32)), True)
    carry = lax.fori_loop(0, qb, lambda j, c: tile(j, c, False), carry)
    (_, l_a, _, l_b, acc) = carry
    o_ref[0] = (acc / jnp.where(left, l_a, l_b)).astype(BF16)


def _moba_attn(qkv, bias_table, pos_col, pos_row):
    b, s, _ = qkv.shape
    t = ATTN_TILE
    nb = s // t
    npair = MOBA_HEADS * MOBA_HEAD_DIM // LANES
    return pl.pallas_call(
        functools.partial(_moba_attn_kernel, nb=nb),
        grid=(b, npair, nb),
        in_specs=[pl.BlockSpec((1, t, LANES), lambda bi, hp, qi: (bi, qi, hp)),
                  pl.BlockSpec((1, s, LANES), lambda bi, hp, qi: (bi, 0, npair + hp)),
                  pl.BlockSpec((1, s, LANES), lambda bi, hp, qi: (bi, 0, 2 * npair + hp)),
                  pl.BlockSpec((1, 2, LANES), lambda bi, hp, qi: (hp, 0, 0)),
                  pl.BlockSpec((1, t, 1), lambda bi, hp, qi: (bi, qi, 0)),
                  pl.BlockSpec((1, nb, 1, t), lambda bi, hp, qi: (bi, 0, 0, 0))],
        out_specs=pl.BlockSpec((1, t, LANES), lambda bi, hp, qi: (bi, qi, hp)),
        out_shape=jax.ShapeDtypeStruct((b, s, npair * LANES), BF16),
        scratch_shapes=[pltpu.VMEM((LANES, LANES), F32)],
        compiler_params=_params("arbitrary", "arbitrary", "arbitrary"),
        name="moba_attention",
    )(qkv, qkv, qkv, bias_table, pos_col, pos_row)


def _oproj_kernel(x_ref, a_ref, gt_ref, w_ref, o_ref):
    o_ref[0] = x_ref[0] + gt_ref[0] * _dot(a_ref[0], w_ref[...])


def _oproj(x, a, gate, w, tm):
    b, s, d = x.shape
    k = a.shape[2]
    return pl.pallas_call(
        _oproj_kernel,
        grid=(b, s // tm),
        in_specs=[pl.BlockSpec((1, tm, d), lambda bi, i: (bi, i, 0)),
                  pl.BlockSpec((1, tm, k), lambda bi, i: (bi, i, 0)),
                  pl.BlockSpec((1, 1, d), lambda bi, i: (bi, 0, 0)),
                  pl.BlockSpec((k, d), lambda bi, i: (0, 0))],
        out_specs=pl.BlockSpec((1, tm, d), lambda bi, i: (bi, i, 0)),
        out_shape=jax.ShapeDtypeStruct((b, s, d), F32),
        compiler_params=_params("arbitrary", "arbitrary"),
        name="attn_out_proj",
    )(x, a, gate, w)


def _rope_tab_kernel(pos_ref, inv_ref, c_ref, s1_ref, s2_ref):
    half = MLA_ROPE // 2
    ang = pos_ref[0].astype(F32) * inv_ref[...]
    lane = lax.broadcasted_iota(jnp.int32, (1, LANES), 1)
    sn = jnp.sin(ang)
    c_ref[0] = jnp.cos(ang)
    s1_ref[0] = jnp.where((lane >= MLA_NOPE) & (lane < MLA_NOPE + half), -sn, 0.0)
    s2_ref[0] = jnp.where((lane >= MLA_NOPE + half) & (lane < MLA_QK), sn, 0.0)


def _rope_tables(pos_col, inv_lane, tm):
    b, s, _ = pos_col.shape
    out = pl.BlockSpec((1, tm, LANES), lambda bi, i: (bi, i, 0))
    shape = jax.ShapeDtypeStruct((b, s, LANES), F32)
    return pl.pallas_call(
        _rope_tab_kernel,
        grid=(b, s // tm),
        in_specs=[pl.BlockSpec((1, tm, 1), lambda bi, i: (bi, i, 0)),
                  pl.BlockSpec((1, LANES), lambda bi, i: (0, 0))],
        out_specs=[out, out, out],
        out_shape=[shape, shape, shape],
        compiler_params=_params("arbitrary", "arbitrary"),
        name="rope_tables",
    )(pos_col, inv_lane)


def _slab_norm_rope(y, g, cos, s1, s2):
    ss = jnp.sum(y * y, axis=-1, keepdims=True)
    y = y * lax.rsqrt(ss * (1.0 / MLA_QK) + EPS) * g
    half = MLA_ROPE // 2
    return y * cos + pltpu.roll(y, LANES - half, axis=1) * s1 + pltpu.roll(y, half, axis=1) * s2


def _mla_q_kernel(x_ref, sh_ref, sc_ref, g_ref, wdq_ref, qag_ref, wuq_ref, qg_ref,
                  c_ref, s1_ref, s2_ref, o_ref):
    h = _rms_mod(x_ref[0], g_ref[...], sh_ref[0], sc_ref[0]).astype(BF16)
    dq = _dot(h, wdq_ref[...])
    dqn = dq * lax.rsqrt(jnp.mean(dq * dq, axis=-1, keepdims=True) + EPS) * qag_ref[...]
    uq = _dot(dqn.astype(BF16), wuq_ref[...])
    cos, s1, s2 = c_ref[0], s1_ref[0], s2_ref[0]
    scale = MLA_QK ** -0.5
    for hd in range(MLA_HEADS):
        sl = slice(hd * LANES, (hd + 1) * LANES)
        y = _slab_norm_rope(uq[:, sl], qg_ref[...], cos, s1, s2)
        o_ref[0, :, sl] = (y * scale).astype(BF16)


def _mla_q(x, shift, scale, g, wdq, qag, wuq_p, qg_p, cos, s1, s2, tm):
    b, s, d = x.shape
    r = wdq.shape[1]
    nq = wuq_p.shape[1]
    vec = pl.BlockSpec((1, 1, d), lambda bi, i: (bi, 0, 0))
    tab = pl.BlockSpec((1, tm, LANES), lambda bi, i: (bi, i, 0))
    full = lambda shp: pl.BlockSpec(shp, lambda bi, i: (0, 0))
    return pl.pallas_call(
        _mla_q_kernel,
        grid=(b, s // tm),
        in_specs=[pl.BlockSpec((1, tm, d), lambda bi, i: (bi, i, 0)), vec, vec, full((1, d)),
                  full((d, r)), full((1, r)), full((r, nq)), full((1, LANES)), tab, tab, tab],
        out_specs=pl.BlockSpec((1, tm, nq), lambda bi, i: (bi, i, 0)),
        out_shape=jax.ShapeDtypeStruct((b, s, nq), BF16),
        compiler_params=_params("arbitrary", "arbitrary"),
        name="mla_q_path",
    )(x, shift, scale, g, wdq, qag, wuq_p, qg_p, cos, s1, s2)


def _mla_kv_kernel(x_ref, sh_ref, sc_ref, g_ref, wdkv_ref, kvg_ref, wuk_ref, wuv_ref, kg_ref,
                   c_ref, s1_ref, s2_ref, k_ref, v_ref):
    hs = _rms_mod(x_ref[0], g_ref[...], sh_ref[0], sc_ref[0]).astype(BF16)
    ck = _dot(hs, wdkv_ref[...])
    ckv = ck[:, :MLA_KV_LORA]
    ckvn = (ckv * lax.rsqrt(jnp.mean(ckv * ckv, axis=-1, keepdims=True) + EPS) * kvg_ref[...]).astype(BF16)
    kn = _dot(ckvn, wuk_ref[...])
    v_ref[0] = _dot(ckvn, wuv_ref[...]).astype(BF16)
    kpe = pltpu.roll(ck[:, MLA_KV_LORA:MLA_KV_LORA + LANES], MLA_NOPE, axis=1)
    cos, s1, s2 = c_ref[0], s1_ref[0], s2_ref[0]
    for hd in range(MLA_HEADS):
        sl = slice(hd * LANES, (hd + 1) * LANES)
        k_ref[0, :, sl] = _slab_norm_rope(kn[:, sl] + kpe, kg_ref[...], cos, s1, s2).astype(BF16)


def _mla_kv(x, shift, scale, g, wdkv_p, kvg, wuk_p, wuv, kg_p, cos, s1, s2, tm):
    b, s, d = x.shape
    r = wdkv_p.shape[1]
    nk = wuk_p.shape[1]
    nv = wuv.shape[1]
    vec = pl.BlockSpec((1, 1, d), lambda bi, i: (bi, 0, 0))
    tab = pl.BlockSpec((1, tm, LANES), lambda bi, i: (bi, i, 0))
    full = lambda shp: pl.BlockSpec(shp, lambda bi, i: (0, 0))
    return pl.pallas_call(
        _mla_kv_kernel,
        grid=(b, s // tm),
        in_specs=[pl.BlockSpec((1, tm, d), lambda bi, i: (bi, i, 0)), vec, vec, full((1, d)),
                  full((d, r)), full((1, MLA_KV_LORA)), full((MLA_KV_LORA, nk)), full((MLA_KV_LORA, nv)),
                  full((1, LANES)), tab, tab, tab],
        out_specs=[pl.BlockSpec((1, tm, nk), lambda bi, i: (bi, i, 0)),
                   pl.BlockSpec((1, tm, nv), lambda bi, i: (bi, i, 0))],
        out_shape=[jax.ShapeDtypeStruct((b, s, nk), BF16), jax.ShapeDtypeStruct((b, s, nv), BF16)],
        compiler_params=_params("arbitrary", "arbitrary"),
        name="mla_shared_kv",
    )(x, shift, scale, g, wdkv_p, kvg, wuk_p, wuv, kg_p, cos, s1, s2)


def _mla_attn_kernel(q_ref, k_ref, v_ref, o_ref):
    t = ATTN_TILE
    qi = pl.program_id(2)
    lane = lax.broadcasted_iota(jnp.int32, (1, LANES), 1)
    left = lane < MLA_V
    q2 = q_ref[0]
    q_heads = (q2[:, :LANES], q2[:, LANES:])

    def tile(j, carry, diag):
        (m_a, l_a, m_b, l_b, acc) = carry
        start = pl.multiple_of(j * t, t)
        k2 = k_ref[0, pl.ds(start, t), :]
        v2 = v_ref[0, pl.ds(start, t), :]
        outs = []
        for h, (m, l) in enumerate(((m_a, l_a), (m_b, l_b))):
            s = _dot_nt(q_heads[h], k2[:, h * LANES:(h + 1) * LANES])
            if diag:
                s = jnp.where(_causal_mask(t), s, NEG)
            outs.append(_softmax_step(s, m, l, v2))
        (m_a, l_a, al_a, pv_a), (m_b, l_b, al_b, pv_b) = outs
        acc = acc * jnp.where(left, al_a, al_b) + jnp.where(left, pv_a, pv_b)
        return (m_a, l_a, m_b, l_b, acc)

    col0 = jnp.full((t, 1), NEG, F32)
    zero1 = jnp.zeros((t, 1), F32)
    carry = tile(qi, (col0, zero1, col0, zero1, jnp.zeros((t, LANES), F32)), True)
    carry = lax.fori_loop(0, qi, lambda j, c: tile(j, c, False), carry)
    (_, l_a, _, l_b, acc) = carry
    o_ref[0] = (acc / jnp.where(left, l_a, l_b)).astype(BF16)


def _mla_attn(q, k, v):
    b, s, _ = q.shape
    t = ATTN_TILE
    npair = MLA_HEADS // 2
    return pl.pallas_call(
        _mla_attn_kernel,
        grid=(b, npair, s // t),
        in_specs=[pl.BlockSpec((1, t, 2 * LANES), lambda bi, hp, qi: (bi, qi, hp)),
                  pl.BlockSpec((1, s, 2 * LANES), lambda bi, hp, qi: (bi, 0, hp)),
                  pl.BlockSpec((1, s, LANES), lambda bi, hp, qi: (bi, 0, hp))],
        out_specs=pl.BlockSpec((1, t, LANES), lambda bi, hp, qi: (bi, qi, hp)),
        out_shape=jax.ShapeDtypeStruct((b, s, npair * LANES), BF16),
        compiler_params=_params("arbitrary", "arbitrary", "arbitrary"),
        name="mla_attention",
    )(q, k, v)


def _t5_bias_table(rel_bias):
    n = jnp.arange(BIAS_TABLE)
    max_exact = REL_BUCKETS // 2
    nf = jnp.maximum(n, 1).astype(F32)
    large = max_exact + (jnp.log(nf / max_exact) / math.log(REL_MAX_DIST / max_exact)
                         * (REL_BUCKETS - max_exact)).astype(jnp.int32)
    large = jnp.minimum(large, REL_BUCKETS - 1)
    bucket = jnp.where(n < max_exact, n, large)
    return rel_bias.T[:, bucket].reshape(MOBA_HEADS // 2, 2, BIAS_TABLE)


def _pad_heads(w, width):
    k = w.shape[0]
    w3 = w.reshape(k, -1, width)
    return jnp.pad(w3, ((0, 0), (0, 0), (0, LANES - width))).reshape(k, -1)


def _pad_lanes(g):
    return jnp.pad(g, (0, LANES - g.shape[0])).reshape(1, LANES)


def kernel(x, c, positions, ada_w, ada_b, norm_g, ffn_w_gate, ffn_w_up, ffn_w_down, rel_bias,
           moba_w_qkv, moba_q_g, moba_k_g, moba_w_o, kv_ada_w, kv_ada_b, kv_norm_g, w_dkv,
           kv_a_norm_g, w_uk, w_uv, mla_k_g, mla_w_dq, mla_q_a_norm_g, mla_w_uq, mla_q_g, mla_w_o):
    b, s, d = x.shape
    depth = ada_w.shape[0]
    n_moba = moba_w_qkv.shape[0]
    tm_ffn = min(1024, s)
    tm_proj = min(512, s)

    c_pad = jnp.pad(c, ((0, SUBLANES - b), (0, 0)))
    mods = _mods(c_pad, ada_w, ada_b.reshape(depth, 1, -1), tn=1152)[:, :b]
    kv_mods = _mods(c_pad, kv_ada_w[None], kv_ada_b.reshape(1, 1, -1), tn=1024)[0, :b]

    def mod_vecs(m):
        return [v.reshape(b, 1, d) for v in jnp.split(m, m.shape[-1] // d, axis=-1)]

    pos_col = positions.reshape(b, s, 1)
    pos_row = positions.reshape(b, s // ATTN_TILE, 1, ATTN_TILE)

    for layer in range(depth):
        sh1, sc1, g1, sh2, sc2, g2, sh3, sc3, g3 = mod_vecs(mods[layer])
        ng = norm_g[layer]

        def ffn(xin, idx, sh, sc, gt):
            return _ffn(xin, sh, sc, gt, ng[2 * idx:2 * idx + 1],
                        ffn_w_gate[layer, idx].astype(BF16), ffn_w_up[layer, idx].astype(BF16),
                        ffn_w_down[layer, idx].astype(BF16), tm_ffn, 256)

        x = ffn(x, 0, sh1, sc1, g1)

        if layer < n_moba:
            ones = jnp.ones((MOBA_HEADS * MOBA_HEAD_DIM,), F32)
            head_gain = jnp.concatenate([jnp.tile(moba_q_g[layer], MOBA_HEADS),
                                         jnp.tile(moba_k_g[layer], MOBA_HEADS), ones]).reshape(1, -1)
            qkv = _moba_proj(x, sh2, sc2, ng[1:2], moba_w_qkv[layer].astype(BF16), head_gain, tm_proj, 512)
            o = _moba_attn(qkv, _t5_bias_table(rel_bias), pos_col, pos_row)
            x = _oproj(x, o, g2, moba_w_o[layer].astype(BF16), tm_proj)
        else:
            j = layer - n_moba
            if j == 0:
                half = MLA_ROPE // 2
                inv = ROPE_BASE ** (-jnp.arange(half, dtype=F32) / half)
                inv_lane = jnp.zeros((LANES,), F32).at[MLA_NOPE:MLA_QK].set(jnp.tile(inv, 2)).reshape(1, LANES)
                cos, s1, s2 = _rope_tables(pos_col, inv_lane, tm_proj)
                kv_sh, kv_sc = mod_vecs(kv_mods)
                wdkv_p = jnp.pad(w_dkv, ((0, 0), (0, MLA_KV_LORA + LANES - w_dkv.shape[1]))).astype(BF16)
                shared_k, shared_v = _mla_kv(
                    x, kv_sh, kv_sc, kv_norm_g.reshape(1, d), wdkv_p, kv_a_norm_g.reshape(1, -1),
                    _pad_heads(w_uk, MLA_NOPE).astype(BF16), w_uv.astype(BF16), _pad_lanes(mla_k_g),
                    cos, s1, s2, tm_proj)
            q = _mla_q(x, sh2, sc2, ng[1:2], mla_w_dq[j].astype(BF16), mla_q_a_norm_g[j].reshape(1, -1),
                       _pad_heads(mla_w_uq[j], MLA_QK).astype(BF16), _pad_lanes(mla_q_g[j]),
                       cos, s1, s2, tm_proj)
            o = _mla_attn(q, shared_k, shared_v)
            x = _oproj(x, o, g2, mla_w_o[j].astype(BF16), tm_proj)

        x = ffn(x, 1, sh3, sc3, g3)
    return x
```

```python
import functools
import math

import jax
import jax.numpy as jnp
from jax import lax
from jax.experimental import pallas as pl
from jax.experimental.pallas import tpu as pltpu

F32 = jnp.float32
BF16 = jnp.bfloat16

LANES = 128
SUBLANES = 8
VMEM_LIMIT_BYTES = 56 * 1024 * 1024

MOBA_HEADS = 16
MOBA_HEAD_DIM = 64
MOBA_BLOCK = 256
MOBA_TOPK = 3
REL_BUCKETS = 32
REL_MAX_DIST = 128
MLA_HEADS = 16
MLA_KV_LORA = 256
MLA_NOPE = 64
MLA_ROPE = 32
MLA_V = 64
MLA_QK = MLA_NOPE + MLA_ROPE
ROPE_BASE = 10000.0
EPS = 1e-6
NEG = -1e30

MOBA_TQ = 512
MOBA_TK = 512
MLA_TQ = 512
MLA_TK = 512
BIAS_TABLE = LANES


def _params(*sem):
    return pltpu.CompilerParams(dimension_semantics=sem, vmem_limit_bytes=VMEM_LIMIT_BYTES)


def _dot(a, b):
    return jnp.dot(a, b, preferred_element_type=F32)


def _dot_nt(a, b):
    return lax.dot_general(a, b, (((1,), (1,)), ((), ())), preferred_element_type=F32)


def _rms_mod(x, g, shift, scale):
    ms = jnp.mean(x * x, axis=-1, keepdims=True)
    y = x * lax.rsqrt(ms + EPS) * g
    return y * (1.0 + scale) + shift


def _mods_kernel(c_ref, w_ref, b_ref, o_ref):
    c = c_ref[...]
    ca = c * jax.nn.sigmoid(c)
    o_ref[0] = jnp.dot(ca, w_ref[0], precision=lax.Precision.HIGHEST,
                       preferred_element_type=F32) + b_ref[0]


def _mods(c_pad, w, b, tn):
    nl, d, n = w.shape
    return pl.pallas_call(
        _mods_kernel,
        grid=(nl, n // tn),
        in_specs=[pl.BlockSpec((SUBLANES, d), lambda l, j: (0, 0)),
                  pl.BlockSpec((1, d, tn), lambda l, j: (l, 0, j)),
                  pl.BlockSpec((1, 1, tn), lambda l, j: (l, 0, j))],
        out_specs=pl.BlockSpec((1, SUBLANES, tn), lambda l, j: (l, 0, j)),
        out_shape=jax.ShapeDtypeStruct((nl, SUBLANES, n), F32),
        compiler_params=_params("arbitrary", "arbitrary"),
        name="adaln_mods",
    )(c_pad, w, b)


def _ffn_kernel(x_ref, sh_ref, sc_ref, gt_ref, g_ref, wg_ref, wu_ref, wd_ref, o_ref, h_scr, acc_scr):
    f = pl.program_id(2)

    @pl.when(f == 0)
    def _():
        h_scr[...] = _rms_mod(x_ref[0], g_ref[...], sh_ref[0], sc_ref[0]).astype(BF16)
        acc_scr[...] = jnp.zeros_like(acc_scr)

    h = h_scr[...]
    g = _dot(h, wg_ref[...])
    u = _dot(h, wu_ref[...])
    a = (g * jax.nn.sigmoid(g) * u).astype(BF16)
    acc_scr[...] += _dot(a, wd_ref[...])

    @pl.when(f == pl.num_programs(2) - 1)
    def _():
        o_ref[0] = x_ref[0] + (0.5 * gt_ref[0]) * acc_scr[...]


def _ffn(x, shift, scale, gate, g, wg, wu, wd, tm, tf):
    b, s, d = x.shape
    f = wg.shape[1]
    row = pl.BlockSpec((1, tm, d), lambda bi, i, j: (bi, i, 0))
    vec = pl.BlockSpec((1, 1, d), lambda bi, i, j: (bi, 0, 0))
    return pl.pallas_call(
        _ffn_kernel,
        grid=(b, s // tm, f // tf),
        in_specs=[row, vec, vec, vec,
                  pl.BlockSpec((1, d), lambda bi, i, j: (0, 0)),
                  pl.BlockSpec((d, tf), lambda bi, i, j: (0, j)),
                  pl.BlockSpec((d, tf), lambda bi, i, j: (0, j)),
                  pl.BlockSpec((tf, d), lambda bi, i, j: (j, 0))],
        out_specs=row,
        out_shape=jax.ShapeDtypeStruct((b, s, d), F32),
        scratch_shapes=[pltpu.VMEM((tm, d), BF16), pltpu.VMEM((tm, d), F32)],
        compiler_params=_params("arbitrary", "arbitrary", "arbitrary"),
        name="ffn_swiglu",
    )(x, shift, scale, gate, g, wg, wu, wd)


def _headnorm_pair(y, g, dim):
    lane = lax.broadcasted_iota(jnp.int32, (1, LANES), 1)
    left = lane < dim
    y2 = y * y
    ss_a = jnp.sum(jnp.where(left, y2, 0.0), axis=-1, keepdims=True)
    ss_b = jnp.sum(jnp.where(left, 0.0, y2), axis=-1, keepdims=True)
    inv = jnp.where(left, lax.rsqrt(ss_a * (1.0 / dim) + EPS), lax.rsqrt(ss_b * (1.0 / dim) + EPS))
    return y * inv * g


def _moba_proj_kernel(x_ref, sh_ref, sc_ref, g_ref, w_ref, wvt_ref, hg_ref, qk_ref, vt_ref, h_scr, *, n_qk):
    n = pl.program_id(2)

    @pl.when(n == 0)
    def _():
        h_scr[...] = _rms_mod(x_ref[0], g_ref[...], sh_ref[0], sc_ref[0]).astype(BF16)

    @pl.when(n < n_qk)
    def _():
        res = _dot(h_scr[...], w_ref[...])
        for s in range(res.shape[1] // LANES):
            sl = slice(s * LANES, (s + 1) * LANES)
            qk_ref[0, :, sl] = _headnorm_pair(res[:, sl], hg_ref[:, sl], MOBA_HEAD_DIM).astype(BF16)

    @pl.when(n >= n_qk)
    def _():
        vt_ref[0] = _dot_nt(wvt_ref[...], h_scr[...]).astype(BF16)


def _moba_proj(x, shift, scale, g, w_qk, w_vt, head_gain, tm, tn):
    b, s, d = x.shape
    n_qk = w_qk.shape[1] // tn
    nv = w_vt.shape[0]
    vec = pl.BlockSpec((1, 1, d), lambda bi, i, j: (bi, 0, 0))
    qk_col = lambda j: jnp.minimum(j, n_qk - 1)
    vt_row = lambda j: jnp.maximum(j - n_qk, 0)
    return pl.pallas_call(
        functools.partial(_moba_proj_kernel, n_qk=n_qk),
        grid=(b, s // tm, n_qk + nv // tn),
        in_specs=[pl.BlockSpec((1, tm, d), lambda bi, i, j: (bi, i, 0)), vec, vec,
                  pl.BlockSpec((1, d), lambda bi, i, j: (0, 0)),
                  pl.BlockSpec((d, tn), lambda bi, i, j: (0, qk_col(j))),
                  pl.BlockSpec((tn, d), lambda bi, i, j: (vt_row(j), 0)),
                  pl.BlockSpec((1, tn), lambda bi, i, j: (0, qk_col(j)))],
        out_specs=[pl.BlockSpec((1, tm, tn), lambda bi, i, j: (bi, i, qk_col(j))),
                   pl.BlockSpec((1, tn, tm), lambda bi, i, j: (bi, vt_row(j), i))],
        out_shape=[jax.ShapeDtypeStruct((b, s, w_qk.shape[1]), BF16),
                   jax.ShapeDtypeStruct((b, nv, s), BF16)],
        scratch_shapes=[pltpu.VMEM((tm, d), BF16)],
        compiler_params=_params("arbitrary", "arbitrary", "arbitrary"),
        name="moba_qkv_proj",
    )(x, shift, scale, g, w_qk, w_vt, head_gain)


def _softmax_step_t(s, m, l, acc, vt):
    m_new = jnp.maximum(m, jnp.max(s, axis=0, keepdims=True))
    alpha = jnp.exp(m - m_new)
    p = jnp.exp(s - m_new)
    l_new = alpha * l + jnp.sum(p, axis=0, keepdims=True)
    acc_new = acc * alpha + _dot(vt, p.astype(BF16))
    return m_new, l_new, acc_new


def _moba_attn_kernel(qmin_ref, kmax_ref, q_ref, k_ref, vt_ref, bt_ref, pq_ref, pk_ref, o_ref,
                      km_scr, sel_scr, *, nb, tq, tk):
    blk = MOBA_BLOCK
    hd = MOBA_HEAD_DIM
    nsub = tk // blk
    bi = pl.program_id(0)
    qi = pl.program_id(2)
    lane = lax.broadcasted_iota(jnp.int32, (1, LANES), 1)
    left = lane < hd

    @pl.when(qi == 0)
    def _():
        km_scr[...] = jnp.zeros_like(km_scr)
        for r in range(nb):
            kblk = k_ref[0, r * blk:(r + 1) * blk, :].astype(F32)
            km_scr[r:r + 1, :] = jnp.sum(kblk, axis=0, keepdims=True) * (1.0 / blk)

    q2 = q_ref[0]
    zero = jnp.zeros_like(q2)
    q_heads = (jnp.where(left, q2, zero), jnp.where(left, zero, q2))
    scale = hd ** -0.5
    qs_heads = tuple((qh.astype(F32) * scale).astype(BF16) for qh in q_heads)

    q_lo = qi * tq
    col = lax.broadcasted_iota(jnp.int32, (1, tq), 1)
    q_idx = q_lo + col
    q_sub = sum((col >= r * blk).astype(jnp.int32) for r in range(1, tq // blk))
    q_blk = qi * (tq // blk) + q_sub
    q_blk_start = q_blk * blk

    km = km_scr[...]
    km_hi = km.astype(BF16)
    km_lo = (km - km_hi.astype(F32)).astype(BF16)
    blk_row = lax.broadcasted_iota(jnp.int32, (nb, tq), 0)
    for h in range(2):
        gate = (_dot_nt(km_hi, q_heads[h]) + _dot_nt(km_lo, q_heads[h]))[:nb]
        gate = jnp.where(blk_row < q_blk, gate, -jnp.inf)
        sel = jnp.zeros((nb, tq), F32)
        for _ in range(MOBA_TOPK):
            mx = jnp.max(gate, axis=0, keepdims=True)
            idx = jnp.min(jnp.where(gate == mx, blk_row, nb), axis=0, keepdims=True)
            pick = (blk_row == idx) & (mx > -jnp.inf)
            sel = jnp.where(pick, 1.0, sel)
            gate = jnp.where(pick, -jnp.inf, gate)
        sel_scr[h] = sel

    pq = pq_ref[0, 0]

    def chunk(j, carry, mode):
        start = pl.multiple_of(j * tk, tk)
        k2 = k_ref[0, pl.ds(start, tk), :]
        vt = vt_ref[0, :, pl.ds(start, tk)]
        if mode != "far":
            dist = jnp.clip(pq - pk_ref[0, pl.ds(start, tk), :], 0, BIAS_TABLE - 1)
        if mode == "diag":
            key_idx = start + lax.broadcasted_iota(jnp.int32, (blk, tq), 0)
        out = []
        for h in range(2):
            m, l, acc = carry[h]
            s = _dot_nt(k2, qs_heads[h])
            chosen = [sel_scr[h, pl.ds(j * nsub + r, 1), :] > 0.0 for r in range(nsub)]
            pieces = [s[r * blk:(r + 1) * blk] for r in range(nsub)]
            if mode == "far":
                c = bt_ref[0, h:h + 1, BIAS_TABLE - 1:BIAS_TABLE]
                tmax = [jnp.where(chosen[r], jnp.max(pieces[r], axis=0, keepdims=True) + c, NEG)
                        for r in range(nsub)]
                m_new = functools.reduce(jnp.maximum, tmax, m)
                alpha = jnp.exp(m - m_new)
                ps = [jnp.exp(pieces[r] - jnp.where(chosen[r], m_new - c, jnp.inf)) for r in range(nsub)]
                l_new = alpha * l + sum(jnp.sum(p, axis=0, keepdims=True) for p in ps)
                p = jnp.concatenate(ps, axis=0) if nsub > 1 else ps[0]
                acc_new = acc * alpha + _dot(vt[h * hd:(h + 1) * hd, :], p.astype(BF16))
                out.append((m_new, l_new, acc_new))
            else:
                table = jnp.broadcast_to(bt_ref[0, h:h + 1, :], (tk, LANES))
                bias = jnp.concatenate(
                    [jnp.take_along_axis(table, dist[:, g * LANES:(g + 1) * LANES], axis=1,
                                         mode="promise_in_bounds") for g in range(tq // LANES)], axis=1)
                s = s + bias
                masked = []
                for r in range(nsub):
                    piece = s[r * blk:(r + 1) * blk]
                    if mode == "diag":
                        kidx = key_idx + r * blk
                        keep_own = jnp.where(kidx <= q_idx, jnp.where(kidx >= q_blk_start, piece, NEG), NEG)
                        masked.append(jnp.where(chosen[r], piece, keep_own))
                    else:
                        masked.append(jnp.where(chosen[r], piece, NEG))
                s = jnp.concatenate(masked, axis=0) if nsub > 1 else masked[0]
                out.append(_softmax_step_t(s, m, l, acc, vt[h * hd:(h + 1) * hd, :]))
        return tuple(out)

    def past_chunk(j, carry):
        far = qmin_ref[bi, qi] - kmax_ref[bi, j] >= BIAS_TABLE - 1
        return lax.cond(far, lambda c: chunk(j, c, "far"), lambda c: chunk(j, c, "near"), carry)

    init = (jnp.full((1, tq), NEG, F32), jnp.zeros((1, tq), F32), jnp.zeros((hd, tq), F32))
    n_full = q_lo // tk
    n_end = (q_lo + tq + tk - 1) // tk
    carry = lax.fori_loop(0, n_full, past_chunk, (init, init))
    carry = lax.fori_loop(n_full, n_end, lambda j, c: chunk(j, c, "diag"), carry)
    out_t = jnp.concatenate([acc / l for (_, l, acc) in carry], axis=0)
    o_ref[0] = out_t.T.astype(BF16)


def _moba_attn(qk, vt, bias_table, positions, tq, tk):
    b, s, _ = qk.shape
    nb = s // MOBA_BLOCK
    npair = MOBA_HEADS * MOBA_HEAD_DIM // LANES
    q_min = jnp.min(positions.reshape(b, s // tq, tq), axis=-1)
    k_max = jnp.max(positions.reshape(b, s // tk, tk), axis=-1)
    pos_row = positions.reshape(b, s // tq, 1, tq)
    pos_col = positions.reshape(b, s, 1)
    grid_spec = pltpu.PrefetchScalarGridSpec(
        num_scalar_prefetch=2,
        grid=(b, npair, s // tq),
        in_specs=[pl.BlockSpec((1, tq, LANES), lambda bi, hp, qi, *_: (bi, qi, hp)),
                  pl.BlockSpec((1, s, LANES), lambda bi, hp, qi, *_: (bi, 0, npair + hp)),
                  pl.BlockSpec((1, LANES, s), lambda bi, hp, qi, *_: (bi, hp, 0)),
                  pl.BlockSpec((1, 2, LANES), lambda bi, hp, qi, *_: (hp, 0, 0)),
                  pl.BlockSpec((1, 1, 1, tq), lambda bi, hp, qi, *_: (bi, qi, 0, 0)),
                  pl.BlockSpec((1, s, 1), lambda bi, hp, qi, *_: (bi, 0, 0))],
        out_specs=pl.BlockSpec((1, tq, LANES), lambda bi, hp, qi, *_: (bi, qi, hp)),
        scratch_shapes=[pltpu.VMEM((LANES, LANES), F32), pltpu.VMEM((2, nb, tq), F32)],
    )
    return pl.pallas_call(
        functools.partial(_moba_attn_kernel, nb=nb, tq=tq, tk=tk),
        grid_spec=grid_spec,
        out_shape=jax.ShapeDtypeStruct((b, s, npair * LANES), BF16),
        compiler_params=_params("arbitrary", "arbitrary", "arbitrary"),
        name="moba_attention",
    )(q_min, k_max, qk, qk, vt, bias_table, pos_row, pos_col)


def _oproj_kernel(x_ref, a_ref, gt_ref, w_ref, o_ref):
    o_ref[0] = x_ref[0] + gt_ref[0] * _dot(a_ref[0], w_ref[...])


def _oproj(x, a, gate, w, tm):
    b, s, d = x.shape
    k = a.shape[2]
    return pl.pallas_call(
        _oproj_kernel,
        grid=(b, s // tm),
        in_specs=[pl.BlockSpec((1, tm, d), lambda bi, i: (bi, i, 0)),
                  pl.BlockSpec((1, tm, k), lambda bi, i: (bi, i, 0)),
                  pl.BlockSpec((1, 1, d), lambda bi, i: (bi, 0, 0)),
                  pl.BlockSpec((k, d), lambda bi, i: (0, 0))],
        out_specs=pl.BlockSpec((1, tm, d), lambda bi, i: (bi, i, 0)),
        out_shape=jax.ShapeDtypeStruct((b, s, d), F32),
        compiler_params=_params("arbitrary", "arbitrary"),
        name="attn_out_proj",
    )(x, a, gate, w)


def _rope_tab_kernel(pos_ref, inv_ref, c_ref, s1_ref, s2_ref):
    half = MLA_ROPE // 2
    ang = pos_ref[0].astype(F32) * inv_ref[...]
    lane = lax.broadcasted_iota(jnp.int32, (1, LANES), 1)
    sn = jnp.sin(ang)
    c_ref[0] = jnp.cos(ang)
    s1_ref[0] = jnp.where((lane >= MLA_NOPE) & (lane < MLA_NOPE + half), -sn, 0.0)
    s2_ref[0] = jnp.where((lane >= MLA_NOPE + half) & (lane < MLA_QK), sn, 0.0)


def _rope_tables(pos_col, inv_lane, tm):
    b, s, _ = pos_col.shape
    out = pl.BlockSpec((1, tm, LANES), lambda bi, i: (bi, i, 0))
    shape = jax.ShapeDtypeStruct((b, s, LANES), F32)
    return pl.pallas_call(
        _rope_tab_kernel,
        grid=(b, s // tm),
        in_specs=[pl.BlockSpec((1, tm, 1), lambda bi, i: (bi, i, 0)),
                  pl.BlockSpec((1, LANES), lambda bi, i: (0, 0))],
        out_specs=[out, out, out],
        out_shape=[shape, shape, shape],
        compiler_params=_params("arbitrary", "arbitrary"),
        name="rope_tables",
    )(pos_col, inv_lane)


def _slab_norm_rope(y, g, cos, s1, s2):
    ss = jnp.sum(y * y, axis=-1, keepdims=True)
    y = y * lax.rsqrt(ss * (1.0 / MLA_QK) + EPS) * g
    half = MLA_ROPE // 2
    return y * cos + pltpu.roll(y, LANES - half, axis=1) * s1 + pltpu.roll(y, half, axis=1) * s2


def _mla_q_kernel(x_ref, sh_ref, sc_ref, g_ref, wdq_ref, qag_ref, wuq_ref, qg_ref,
                  c_ref, s1_ref, s2_ref, o_ref):
    h = _rms_mod(x_ref[0], g_ref[...], sh_ref[0], sc_ref[0]).astype(BF16)
    dq = _dot(h, wdq_ref[...])
    dqn = dq * lax.rsqrt(jnp.mean(dq * dq, axis=-1, keepdims=True) + EPS) * qag_ref[...]
    uq = _dot(dqn.astype(BF16), wuq_ref[...])
    cos, s1, s2 = c_ref[0], s1_ref[0], s2_ref[0]
    scale = MLA_QK ** -0.5
    for hd in range(MLA_HEADS):
        sl = slice(hd * LANES, (hd + 1) * LANES)
        y = _slab_norm_rope(uq[:, sl], qg_ref[...], cos, s1, s2)
        o_ref[0, :, sl] = (y * scale).astype(BF16)


def _mla_q(x, shift, scale, g, wdq, qag, wuq_p, qg_p, cos, s1, s2, tm):
    b, s, d = x.shape
    r = wdq.shape[1]
    nq = wuq_p.shape[1]
    vec = pl.BlockSpec((1, 1, d), lambda bi, i: (bi, 0, 0))
    tab = pl.BlockSpec((1, tm, LANES), lambda bi, i: (bi, i, 0))
    full = lambda shp: pl.BlockSpec(shp, lambda bi, i: (0, 0))
    return pl.pallas_call(
        _mla_q_kernel,
        grid=(b, s // tm),
        in_specs=[pl.BlockSpec((1, tm, d), lambda bi, i: (bi, i, 0)), vec, vec, full((1, d)),
                  full((d, r)), full((1, r)), full((r, nq)), full((1, LANES)), tab, tab, tab],
        out_specs=pl.BlockSpec((1, tm, nq), lambda bi, i: (bi, i, 0)),
        out_shape=jax.ShapeDtypeStruct((b, s, nq), BF16),
        compiler_params=_params("arbitrary", "arbitrary"),
        name="mla_q_path",
    )(x, shift, scale, g, wdq, qag, wuq_p, qg_p, cos, s1, s2)


def _mla_kv_kernel(x_ref, sh_ref, sc_ref, g_ref, wdkv_ref, kvg_ref, wuk_ref, wuvt_ref, kg_ref,
                   c_ref, s1_ref, s2_ref, k_ref, vt_ref):
    hs = _rms_mod(x_ref[0], g_ref[...], sh_ref[0], sc_ref[0]).astype(BF16)
    ck = _dot(hs, wdkv_ref[...])
    ckv = ck[:, :MLA_KV_LORA]
    ckvn = (ckv * lax.rsqrt(jnp.mean(ckv * ckv, axis=-1, keepdims=True) + EPS) * kvg_ref[...]).astype(BF16)
    kn = _dot(ckvn, wuk_ref[...])
    vt_ref[0] = _dot_nt(wuvt_ref[...], ckvn).astype(BF16)
    kpe = pltpu.roll(ck[:, MLA_KV_LORA:MLA_KV_LORA + LANES], MLA_NOPE, axis=1)
    cos, s1, s2 = c_ref[0], s1_ref[0], s2_ref[0]
    for hd in range(MLA_HEADS):
        sl = slice(hd * LANES, (hd + 1) * LANES)
        k_ref[0, :, sl] = _slab_norm_rope(kn[:, sl] + kpe, kg_ref[...], cos, s1, s2).astype(BF16)


def _mla_kv(x, shift, scale, g, wdkv_p, kvg, wuk_p, wuv_t, kg_p, cos, s1, s2, tm):
    b, s, d = x.shape
    r = wdkv_p.shape[1]
    nk = wuk_p.shape[1]
    nv = wuv_t.shape[0]
    vec = pl.BlockSpec((1, 1, d), lambda bi, i: (bi, 0, 0))
    tab = pl.BlockSpec((1, tm, LANES), lambda bi, i: (bi, i, 0))
    full = lambda shp: pl.BlockSpec(shp, lambda bi, i: (0, 0))
    return pl.pallas_call(
        _mla_kv_kernel,
        grid=(b, s // tm),
        in_specs=[pl.BlockSpec((1, tm, d), lambda bi, i: (bi, i, 0)), vec, vec, full((1, d)),
                  full((d, r)), full((1, MLA_KV_LORA)), full((MLA_KV_LORA, nk)), full((nv, MLA_KV_LORA)),
                  full((1, LANES)), tab, tab, tab],
        out_specs=[pl.BlockSpec((1, tm, nk), lambda bi, i: (bi, i, 0)),
                   pl.BlockSpec((1, nv, tm), lambda bi, i: (bi, 0, i))],
        out_shape=[jax.ShapeDtypeStruct((b, s, nk), BF16), jax.ShapeDtypeStruct((b, nv, s), BF16)],
        compiler_params=_params("arbitrary", "arbitrary"),
        name="mla_shared_kv",
    )(x, shift, scale, g, wdkv_p, kvg, wuk_p, wuv_t, kg_p, cos, s1, s2)


def _mla_attn_kernel(q_ref, k_ref, vt_ref, o_ref, *, tq, tk):
    qi = pl.program_id(2)
    q2 = q_ref[0]
    q_heads = (q2[:, :LANES], q2[:, LANES:])
    q_lo = qi * tq

    def chunk(j, carry, masked):
        start = pl.multiple_of(j * tk, tk)
        k2 = k_ref[0, pl.ds(start, tk), :]
        vt = vt_ref[0, :, pl.ds(start, tk)]
        if masked:
            key_idx = start + lax.broadcasted_iota(jnp.int32, (tk, tq), 0)
            q_idx = q_lo + lax.broadcasted_iota(jnp.int32, (tk, tq), 1)
            allowed = key_idx <= q_idx
        out = []
        for h in range(2):
            m, l, acc = carry[h]
            s = _dot_nt(k2[:, h * LANES:(h + 1) * LANES], q_heads[h])
            if masked:
                s = jnp.where(allowed, s, NEG)
            out.append(_softmax_step_t(s, m, l, acc, vt[h * MLA_V:(h + 1) * MLA_V, :]))
        return tuple(out)

    init = (jnp.full((1, tq), NEG, F32), jnp.zeros((1, tq), F32), jnp.zeros((MLA_V, tq), F32))
    n_full = q_lo // tk
    n_end = (q_lo + tq + tk - 1) // tk
    carry = lax.fori_loop(0, n_full, lambda j, c: chunk(j, c, False), (init, init))
    carry = lax.fori_loop(n_full, n_end, lambda j, c: chunk(j, c, True), carry)
    out_t = jnp.concatenate([acc / l for (_, l, acc) in carry], axis=0)
    o_ref[0] = out_t.T.astype(BF16)


def _mla_attn(q, k, vt, tq, tk):
    b, s, _ = q.shape
    npair = MLA_HEADS // 2
    return pl.pallas_call(
        functools.partial(_mla_attn_kernel, tq=tq, tk=tk),
        grid=(b, npair, s // tq),
        in_specs=[pl.BlockSpec((1, tq, 2 * LANES), lambda bi, hp, qi: (bi, qi, hp)),
                  pl.BlockSpec((1, s, 2 * LANES), lambda bi, hp, qi: (bi, 0, hp)),
                  pl.BlockSpec((1, LANES, s), lambda bi, hp, qi: (bi, hp, 0))],
        out_specs=pl.BlockSpec((1, tq, LANES), lambda bi, hp, qi: (bi, qi, hp)),
        out_shape=jax.ShapeDtypeStruct((b, s, npair * LANES), BF16),
        compiler_params=_params("arbitrary", "arbitrary", "arbitrary"),
        name="mla_attention",
    )(q, k, vt)


def _t5_bias_table(rel_bias):
    n = jnp.arange(BIAS_TABLE)
    max_exact = REL_BUCKETS // 2
    nf = jnp.maximum(n, 1).astype(F32)
    large = max_exact + (jnp.log(nf / max_exact) / math.log(REL_MAX_DIST / max_exact)
                         * (REL_BUCKETS - max_exact)).astype(jnp.int32)
    large = jnp.minimum(large, REL_BUCKETS - 1)
    bucket = jnp.where(n < max_exact, n, large)
    return rel_bias.T[:, bucket].reshape(MOBA_HEADS // 2, 2, BIAS_TABLE)


def _pad_heads(w, width):
    k = w.shape[0]
    w3 = w.reshape(k, -1, width)
    return jnp.pad(w3, ((0, 0), (0, 0), (0, LANES - width))).reshape(k, -1)


def _pad_lanes(g):
    return jnp.pad(g, (0, LANES - g.shape[0])).reshape(1, LANES)


def kernel(x, c, positions, ada_w, ada_b, norm_g, ffn_w_gate, ffn_w_up, ffn_w_down, rel_bias,
           moba_w_qkv, moba_q_g, moba_k_g, moba_w_o, kv_ada_w, kv_ada_b, kv_norm_g, w_dkv,
           kv_a_norm_g, w_uk, w_uv, mla_k_g, mla_w_dq, mla_q_a_norm_g, mla_w_uq, mla_q_g, mla_w_o):
    b, s, d = x.shape
    depth = ada_w.shape[0]
    n_moba = moba_w_qkv.shape[0]
    tm_ffn = min(1024, s)
    tm_proj = min(512, s)

    c_pad = jnp.pad(c, ((0, SUBLANES - b), (0, 0)))
    mods = _mods(c_pad, ada_w, ada_b.reshape(depth, 1, -1), tn=1152)[:, :b]
    kv_mods = _mods(c_pad, kv_ada_w[None], kv_ada_b.reshape(1, 1, -1), tn=1024)[0, :b]

    def mod_vecs(m):
        return [v.reshape(b, 1, d) for v in jnp.split(m, m.shape[-1] // d, axis=-1)]

    pos_col = positions.reshape(b, s, 1)

    for layer in range(depth):
        sh1, sc1, g1, sh2, sc2, g2, sh3, sc3, g3 = mod_vecs(mods[layer])
        ng = norm_g[layer]

        def ffn(xin, idx, sh, sc, gt):
            return _ffn(xin, sh, sc, gt, ng[2 * idx:2 * idx + 1],
                        ffn_w_gate[layer, idx].astype(BF16), ffn_w_up[layer, idx].astype(BF16),
                        ffn_w_down[layer, idx].astype(BF16), tm_ffn, 256)

        x = ffn(x, 0, sh1, sc1, g1)

        if layer < n_moba:
            n_qk = 2 * MOBA_HEADS * MOBA_HEAD_DIM
            head_gain = jnp.concatenate([jnp.tile(moba_q_g[layer], MOBA_HEADS),
                                         jnp.tile(moba_k_g[layer], MOBA_HEADS)]).reshape(1, -1)
            w_qkv = moba_w_qkv[layer]
            qk, vt = _moba_proj(x, sh2, sc2, ng[1:2], w_qkv[:, :n_qk].astype(BF16),
                                w_qkv[:, n_qk:].T.astype(BF16), head_gain, tm_proj, 512)
            o = _moba_attn(qk, vt, _t5_bias_table(rel_bias), positions, MOBA_TQ, MOBA_TK)
            x = _oproj(x, o, g2, moba_w_o[layer].astype(BF16), tm_proj)
        else:
            j = layer - n_moba
            if j == 0:
                half = MLA_ROPE // 2
                inv = ROPE_BASE ** (-jnp.arange(half, dtype=F32) / half)
                inv_lane = jnp.zeros((LANES,), F32).at[MLA_NOPE:MLA_QK].set(jnp.tile(inv, 2)).reshape(1, LANES)
                cos, s1, s2 = _rope_tables(pos_col, inv_lane, tm_proj)
                kv_sh, kv_sc = mod_vecs(kv_mods)
                wdkv_p = jnp.pad(w_dkv, ((0, 0), (0, MLA_KV_LORA + LANES - w_dkv.shape[1]))).astype(BF16)
                shared_k, shared_v = _mla_kv(
                    x, kv_sh, kv_sc, kv_norm_g.reshape(1, d), wdkv_p, kv_a_norm_g.reshape(1, -1),
                    _pad_heads(w_uk, MLA_NOPE).astype(BF16), w_uv.T.astype(BF16), _pad_lanes(mla_k_g),
                    cos, s1, s2, tm_proj)
            q = _mla_q(x, sh2, sc2, ng[1:2], mla_w_dq[j].astype(BF16), mla_q_a_norm_g[j].reshape(1, -1),
                       _pad_heads(mla_w_uq[j], MLA_QK).astype(BF16), _pad_lanes(mla_q_g[j]),
                       cos, s1, s2, tm_proj)
            o = _mla_attn(q, shared_k, shared_v, MLA_TQ, MLA_TK)
            x = _oproj(x, o, g2, mla_w_o[j].astype(BF16), tm_proj)

        x = ffn(x, 1, sh3, sc3, g3)
    return x
```

```python
import functools
import math

import jax
import jax.numpy as jnp
from jax import lax
from jax.experimental import pallas as pl
from jax.experimental.pallas import tpu as pltpu

F32 = jnp.float32
BF16 = jnp.bfloat16

LANES = 128
SUBLANES = 8
VMEM_LIMIT_BYTES = 56 * 1024 * 1024

MOBA_HEADS = 16
MOBA_HEAD_DIM = 64
MOBA_BLOCK = 256
MOBA_TOPK = 3
REL_BUCKETS = 32
REL_MAX_DIST = 128
MLA_HEADS = 16
MLA_KV_LORA = 256
MLA_NOPE = 64
MLA_ROPE = 32
MLA_V = 64
MLA_QK = MLA_NOPE + MLA_ROPE
ROPE_BASE = 10000.0
EPS = 1e-6
NEG = -1e30

MOBA_TILE = 512
MLA_TILE = 512
ATTN_HEADS = 8
LOG2E = math.log2(math.e)
BIAS_TABLE = LANES


def _params(*sem):
    return pltpu.CompilerParams(dimension_semantics=sem, vmem_limit_bytes=VMEM_LIMIT_BYTES)


def _dot(a, b):
    return jnp.dot(a, b, preferred_element_type=F32)


def _dot_nt(a, b):
    return lax.dot_general(a, b, (((1,), (1,)), ((), ())), preferred_element_type=F32)


def _rms_mod(x, g, shift, scale):
    ms = jnp.mean(x * x, axis=-1, keepdims=True)
    y = x * lax.rsqrt(ms + EPS) * g
    return y * (1.0 + scale) + shift


def _mods_kernel(c_ref, w_ref, b_ref, o_ref):
    c = c_ref[...]
    ca = c * jax.nn.sigmoid(c)
    o_ref[0] = jnp.dot(ca, w_ref[0], precision=lax.Precision.HIGHEST,
                       preferred_element_type=F32) + b_ref[0]


def _mods(c_pad, w, b, tn):
    nl, d, n = w.shape
    return pl.pallas_call(
        _mods_kernel,
        grid=(nl, n // tn),
        in_specs=[pl.BlockSpec((SUBLANES, d), lambda l, j: (0, 0)),
                  pl.BlockSpec((1, d, tn), lambda l, j: (l, 0, j)),
                  pl.BlockSpec((1, 1, tn), lambda l, j: (l, 0, j))],
        out_specs=pl.BlockSpec((1, SUBLANES, tn), lambda l, j: (l, 0, j)),
        out_shape=jax.ShapeDtypeStruct((nl, SUBLANES, n), F32),
        compiler_params=_params("arbitrary", "arbitrary"),
        name="adaln_mods",
    )(c_pad, w, b)


def _ffn_kernel(x_ref, sh_ref, sc_ref, gt_ref, g_ref, wg_ref, wu_ref, wd_ref, o_ref, h_scr, acc_scr):
    f = pl.program_id(2)

    @pl.when(f == 0)
    def _():
        h_scr[...] = _rms_mod(x_ref[0], g_ref[...], sh_ref[0], sc_ref[0]).astype(BF16)
        acc_scr[...] = jnp.zeros_like(acc_scr)

    h = h_scr[...]
    g = _dot(h, wg_ref[...])
    u = _dot(h, wu_ref[...])
    a = (g * jax.nn.sigmoid(g) * u).astype(BF16)
    acc_scr[...] += _dot(a, wd_ref[...])

    @pl.when(f == pl.num_programs(2) - 1)
    def _():
        o_ref[0] = x_ref[0] + (0.5 * gt_ref[0]) * acc_scr[...]


def _ffn(x, shift, scale, gate, g, wg, wu, wd, tm, tf):
    b, s, d = x.shape
    f = wg.shape[1]
    row = pl.BlockSpec((1, tm, d), lambda bi, i, j: (bi, i, 0))
    vec = pl.BlockSpec((1, 1, d), lambda bi, i, j: (bi, 0, 0))
    return pl.pallas_call(
        _ffn_kernel,
        grid=(b, s // tm, f // tf),
        in_specs=[row, vec, vec, vec,
                  pl.BlockSpec((1, d), lambda bi, i, j: (0, 0)),
                  pl.BlockSpec((d, tf), lambda bi, i, j: (0, j)),
                  pl.BlockSpec((d, tf), lambda bi, i, j: (0, j)),
                  pl.BlockSpec((tf, d), lambda bi, i, j: (j, 0))],
        out_specs=row,
        out_shape=jax.ShapeDtypeStruct((b, s, d), F32),
        scratch_shapes=[pltpu.VMEM((tm, d), BF16), pltpu.VMEM((tm, d), F32)],
        compiler_params=_params("arbitrary", "arbitrary", "arbitrary"),
        name="ffn_swiglu",
    )(x, shift, scale, gate, g, wg, wu, wd)


def _headnorm_pair(y, g, dim):
    lane = lax.broadcasted_iota(jnp.int32, (1, LANES), 1)
    left = lane < dim
    y2 = y * y
    ss_a = jnp.sum(jnp.where(left, y2, 0.0), axis=-1, keepdims=True)
    ss_b = jnp.sum(jnp.where(left, 0.0, y2), axis=-1, keepdims=True)
    inv = jnp.where(left, lax.rsqrt(ss_a * (1.0 / dim) + EPS), lax.rsqrt(ss_b * (1.0 / dim) + EPS))
    return y * inv * g


def _moba_proj_kernel(x_ref, sh_ref, sc_ref, g_ref, w_ref, wvt_ref, hg_ref, qk_ref, vt_ref, h_scr, *, n_qk):
    n = pl.program_id(2)

    @pl.when(n == 0)
    def _():
        h_scr[...] = _rms_mod(x_ref[0], g_ref[...], sh_ref[0], sc_ref[0]).astype(BF16)

    @pl.when(n < n_qk)
    def _():
        res = _dot(h_scr[...], w_ref[...])
        for s in range(res.shape[1] // LANES):
            sl = slice(s * LANES, (s + 1) * LANES)
            qk_ref[0, :, sl] = _headnorm_pair(res[:, sl], hg_ref[:, sl], MOBA_HEAD_DIM).astype(BF16)

    @pl.when(n >= n_qk)
    def _():
        vt_ref[0] = _dot_nt(wvt_ref[...], h_scr[...]).astype(BF16)


def _moba_proj(x, shift, scale, g, w_qk, w_vt, head_gain, tm, tn):
    b, s, d = x.shape
    n_qk = w_qk.shape[1] // tn
    nv = w_vt.shape[0]
    vec = pl.BlockSpec((1, 1, d), lambda bi, i, j: (bi, 0, 0))
    qk_col = lambda j: jnp.minimum(j, n_qk - 1)
    vt_row = lambda j: jnp.maximum(j - n_qk, 0)
    return pl.pallas_call(
        functools.partial(_moba_proj_kernel, n_qk=n_qk),
        grid=(b, s // tm, n_qk + nv // tn),
        in_specs=[pl.BlockSpec((1, tm, d), lambda bi, i, j: (bi, i, 0)), vec, vec,
                  pl.BlockSpec((1, d), lambda bi, i, j: (0, 0)),
                  pl.BlockSpec((d, tn), lambda bi, i, j: (0, qk_col(j))),
                  pl.BlockSpec((tn, d), lambda bi, i, j: (vt_row(j), 0)),
                  pl.BlockSpec((1, tn), lambda bi, i, j: (0, qk_col(j)))],
        out_specs=[pl.BlockSpec((1, tm, tn), lambda bi, i, j: (bi, i, qk_col(j))),
                   pl.BlockSpec((1, tn, tm), lambda bi, i, j: (bi, vt_row(j), i))],
        out_shape=[jax.ShapeDtypeStruct((b, s, w_qk.shape[1]), BF16),
                   jax.ShapeDtypeStruct((b, nv, s), BF16)],
        scratch_shapes=[pltpu.VMEM((tm, d), BF16)],
        compiler_params=_params("arbitrary", "arbitrary", "arbitrary"),
        name="moba_qkv_proj",
    )(x, shift, scale, g, w_qk, w_vt, head_gain)


def _softmax_step_t(s, m, l, acc, vt):
    m_new = jnp.maximum(m, jnp.max(s, axis=0, keepdims=True))
    alpha = jnp.exp2(m - m_new)
    p = jnp.exp2(s - m_new)
    l_new = alpha * l + jnp.sum(p, axis=0, keepdims=True)
    acc_new = acc * alpha + _dot(vt, p.astype(BF16))
    return m_new, l_new, acc_new


def _moba_attn_kernel(qmin_ref, kmax_ref, q_ref, k_ref, vt_ref, bt_ref, pq_ref, pk_ref, o_ref,
                      km_scr, sel_scr, *, nb, t, npairs):
    blk = MOBA_BLOCK
    hd = MOBA_HEAD_DIM
    nsub = t // blk
    nh = 2 * npairs
    bi = pl.program_id(0)
    qi = pl.program_id(2)
    lane = lax.broadcasted_iota(jnp.int32, (1, LANES), 1)
    left = lane < hd

    @pl.when(qi == 0)
    def _():
        km_scr[...] = jnp.zeros_like(km_scr)
        for r in range(nb):
            kblk = k_ref[0, r * blk:(r + 1) * blk, :].astype(F32)
            km_scr[r:r + 1, :] = jnp.sum(kblk, axis=0, keepdims=True) * (1.0 / blk)

    q_heads = []
    for p in range(npairs):
        q2 = q_ref[0, :, p * LANES:(p + 1) * LANES]
        zero = jnp.zeros_like(q2)
        q_heads += [jnp.where(left, q2, zero), jnp.where(left, zero, q2)]

    col = lax.broadcasted_iota(jnp.int32, (1, t), 1)
    q_idx = qi * t + col
    q_sub = sum((col >= r * blk).astype(jnp.int32) for r in range(1, nsub))
    q_blk = qi * nsub + q_sub
    q_blk_start = q_blk * blk

    blk_row = lax.broadcasted_iota(jnp.int32, (nb, t), 0)
    for h in range(nh):
        km = km_scr[:, (h // 2) * LANES:(h // 2 + 1) * LANES]
        km_hi = km.astype(BF16)
        km_lo = (km - km_hi.astype(F32)).astype(BF16)
        gate = (_dot_nt(km_hi, q_heads[h]) + _dot_nt(km_lo, q_heads[h]))[:nb]
        gate = jnp.where(blk_row < q_blk, gate, -jnp.inf)
        sel = jnp.zeros((nb, t), F32)
        for _ in range(MOBA_TOPK):
            mx = jnp.max(gate, axis=0, keepdims=True)
            idx = jnp.min(jnp.where(gate == mx, blk_row, nb), axis=0, keepdims=True)
            pick = (blk_row == idx) & (mx > -jnp.inf)
            sel = jnp.where(pick, 1.0, sel)
            gate = jnp.where(pick, -jnp.inf, gate)
        sel_scr[h] = sel

    pq = pq_ref[0, 0]

    def chunk(j, carry, mode):
        carry = list(carry)
        start = pl.multiple_of(j * t, t)
        k2 = k_ref[0, pl.ds(start, t), :]
        vt = vt_ref[0, :, pl.ds(start, t)]
        s_heads = [_dot_nt(k2[:, (h // 2) * LANES:(h // 2 + 1) * LANES], q_heads[h]) for h in range(nh)]
        if mode != "far":
            dist = jnp.clip(pq - pk_ref[0, pl.ds(start, t), :], 0, BIAS_TABLE - 1)
        if mode == "diag":
            key_idx = start + lax.broadcasted_iota(jnp.int32, (blk, t), 0)
        for h in range(nh):
            m, l, acc = carry[h]
            s = s_heads[h]
            vt_h = vt[h * hd:(h + 1) * hd, :]
            chosen = [sel_scr[h, pl.ds(j * nsub + r, 1), :] > 0.0 for r in range(nsub)]
            if mode == "far":
                pieces = [s[r * blk:(r + 1) * blk] for r in range(nsub)]
                c = bt_ref[h // 2, h % 2:h % 2 + 1, BIAS_TABLE - 1:BIAS_TABLE]
                tmax = [jnp.where(chosen[r], jnp.max(pieces[r], axis=0, keepdims=True) + c, NEG)
                        for r in range(nsub)]
                m_new = functools.reduce(jnp.maximum, tmax, m)
                alpha = jnp.exp2(m - m_new)
                ps = [jnp.exp2(pieces[r] - jnp.where(chosen[r], m_new - c, jnp.inf)) for r in range(nsub)]
                l_new = alpha * l + sum(jnp.sum(p, axis=0, keepdims=True) for p in ps)
                p = jnp.concatenate(ps, axis=0) if nsub > 1 else ps[0]
                carry[h] = (m_new, l_new, acc * alpha + _dot(vt_h, p.astype(BF16)))
            else:
                table = jnp.broadcast_to(bt_ref[h // 2, h % 2:h % 2 + 1, :], (t, LANES))
                bias = jnp.concatenate(
                    [jnp.take_along_axis(table, dist[:, g * LANES:(g + 1) * LANES], axis=1,
                                         mode="promise_in_bounds") for g in range(t // LANES)], axis=1)
                s = s + bias
                masked = []
                for r in range(nsub):
                    piece = s[r * blk:(r + 1) * blk]
                    if mode == "diag":
                        kidx = key_idx + r * blk
                        keep_own = jnp.where(kidx <= q_idx, jnp.where(kidx >= q_blk_start, piece, NEG), NEG)
                        masked.append(jnp.where(chosen[r], piece, keep_own))
                    else:
                        masked.append(jnp.where(chosen[r], piece, NEG))
                s = jnp.concatenate(masked, axis=0) if nsub > 1 else masked[0]
                carry[h] = _softmax_step_t(s, m, l, acc, vt_h)
        return tuple(carry)

    def past_chunk(j, carry):
        far = qmin_ref[bi, qi] - kmax_ref[bi, j] >= BIAS_TABLE - 1
        return lax.cond(far, lambda c: chunk(j, c, "far"), lambda c: chunk(j, c, "near"), carry)

    init = (jnp.full((1, t), NEG, F32), jnp.zeros((1, t), F32), jnp.zeros((hd, t), F32))
    carry = lax.fori_loop(0, qi, past_chunk, (init,) * nh)
    carry = chunk(qi, carry, "diag")
    out_t = jnp.concatenate([acc / l for (_, l, acc) in carry], axis=0)
    o_ref[0] = out_t.T.astype(BF16)


def _moba_attn(qk, vt, bias_table, positions, t, npairs):
    b, s, _ = qk.shape
    nb = s // MOBA_BLOCK
    ngroups = MOBA_HEADS * MOBA_HEAD_DIM // (LANES * npairs)
    w = npairs * LANES
    pos_tiles = positions.reshape(b, s // t, t)
    q_min = jnp.min(pos_tiles, axis=-1)
    k_max = jnp.max(pos_tiles, axis=-1)
    pos_row = positions.reshape(b, s // t, 1, t)
    pos_col = positions.reshape(b, s, 1)
    grid_spec = pltpu.PrefetchScalarGridSpec(
        num_scalar_prefetch=2,
        grid=(b, ngroups, s // t),
        in_specs=[pl.BlockSpec((1, t, w), lambda bi, g, qi, *_: (bi, qi, g)),
                  pl.BlockSpec((1, s, w), lambda bi, g, qi, *_: (bi, 0, ngroups + g)),
                  pl.BlockSpec((1, w, s), lambda bi, g, qi, *_: (bi, g, 0)),
                  pl.BlockSpec((npairs, 2, LANES), lambda bi, g, qi, *_: (g, 0, 0)),
                  pl.BlockSpec((1, 1, 1, t), lambda bi, g, qi, *_: (bi, qi, 0, 0)),
                  pl.BlockSpec((1, s, 1), lambda bi, g, qi, *_: (bi, 0, 0))],
        out_specs=pl.BlockSpec((1, t, w), lambda bi, g, qi, *_: (bi, qi, g)),
        scratch_shapes=[pltpu.VMEM((LANES, w), F32), pltpu.VMEM((2 * npairs, nb, t), F32)],
    )
    return pl.pallas_call(
        functools.partial(_moba_attn_kernel, nb=nb, t=t, npairs=npairs),
        grid_spec=grid_spec,
        out_shape=jax.ShapeDtypeStruct((b, s, MOBA_HEADS * MOBA_HEAD_DIM), BF16),
        compiler_params=_params("arbitrary", "arbitrary", "arbitrary"),
        name="moba_attention",
    )(q_min, k_max, qk, qk, vt, bias_table, pos_row, pos_col)


def _oproj_kernel(x_ref, a_ref, gt_ref, w_ref, o_ref):
    o_ref[0] = x_ref[0] + gt_ref[0] * _dot(a_ref[0], w_ref[...])


def _oproj(x, a, gate, w, tm):
    b, s, d = x.shape
    k = a.shape[2]
    return pl.pallas_call(
        _oproj_kernel,
        grid=(b, s // tm),
        in_specs=[pl.BlockSpec((1, tm, d), lambda bi, i: (bi, i, 0)),
                  pl.BlockSpec((1, tm, k), lambda bi, i: (bi, i, 0)),
                  pl.BlockSpec((1, 1, d), lambda bi, i: (bi, 0, 0)),
                  pl.BlockSpec((k, d), lambda bi, i: (0, 0))],
        out_specs=pl.BlockSpec((1, tm, d), lambda bi, i: (bi, i, 0)),
        out_shape=jax.ShapeDtypeStruct((b, s, d), F32),
        compiler_params=_params("arbitrary", "arbitrary"),
        name="attn_out_proj",
    )(x, a, gate, w)


def _rope_tab_kernel(pos_ref, inv_ref, c_ref, s1_ref, s2_ref):
    half = MLA_ROPE // 2
    ang = pos_ref[0].astype(F32) * inv_ref[...]
    lane = lax.broadcasted_iota(jnp.int32, (1, LANES), 1)
    sn = jnp.sin(ang)
    c_ref[0] = jnp.cos(ang)
    s1_ref[0] = jnp.where((lane >= MLA_NOPE) & (lane < MLA_NOPE + half), -sn, 0.0)
    s2_ref[0] = jnp.where((lane >= MLA_NOPE + half) & (lane < MLA_QK), sn, 0.0)


def _rope_tables(pos_col, inv_lane, tm):
    b, s, _ = pos_col.shape
    out = pl.BlockSpec((1, tm, LANES), lambda bi, i: (bi, i, 0))
    shape = jax.ShapeDtypeStruct((b, s, LANES), F32)
    return pl.pallas_call(
        _rope_tab_kernel,
        grid=(b, s // tm),
        in_specs=[pl.BlockSpec((1, tm, 1), lambda bi, i: (bi, i, 0)),
                  pl.BlockSpec((1, LANES), lambda bi, i: (0, 0))],
        out_specs=[out, out, out],
        out_shape=[shape, shape, shape],
        compiler_params=_params("arbitrary", "arbitrary"),
        name="rope_tables",
    )(pos_col, inv_lane)


def _slab_norm_rope(y, g, cos, s1, s2):
    ss = jnp.sum(y * y, axis=-1, keepdims=True)
    y = y * lax.rsqrt(ss * (1.0 / MLA_QK) + EPS) * g
    half = MLA_ROPE // 2
    return y * cos + pltpu.roll(y, LANES - half, axis=1) * s1 + pltpu.roll(y, half, axis=1) * s2


def _mla_q_kernel(x_ref, sh_ref, sc_ref, g_ref, wdq_ref, qag_ref, wuq_ref, qg_ref,
                  c_ref, s1_ref, s2_ref, o_ref):
    h = _rms_mod(x_ref[0], g_ref[...], sh_ref[0], sc_ref[0]).astype(BF16)
    dq = _dot(h, wdq_ref[...])
    dqn = dq * lax.rsqrt(jnp.mean(dq * dq, axis=-1, keepdims=True) + EPS) * qag_ref[...]
    uq = _dot(dqn.astype(BF16), wuq_ref[...])
    cos, s1, s2 = c_ref[0], s1_ref[0], s2_ref[0]
    scale = MLA_QK ** -0.5 * LOG2E
    for hd in range(MLA_HEADS):
        sl = slice(hd * LANES, (hd + 1) * LANES)
        y = _slab_norm_rope(uq[:, sl], qg_ref[...], cos, s1, s2)
        o_ref[0, :, sl] = (y * scale).astype(BF16)


def _mla_q(x, shift, scale, g, wdq, qag, wuq_p, qg_p, cos, s1, s2, tm):
    b, s, d = x.shape
    r = wdq.shape[1]
    nq = wuq_p.shape[1]
    vec = pl.BlockSpec((1, 1, d), lambda bi, i: (bi, 0, 0))
    tab = pl.BlockSpec((1, tm, LANES), lambda bi, i: (bi, i, 0))
    full = lambda shp: pl.BlockSpec(shp, lambda bi, i: (0, 0))
    return pl.pallas_call(
        _mla_q_kernel,
        grid=(b, s // tm),
        in_specs=[pl.BlockSpec((1, tm, d), lambda bi, i: (bi, i, 0)), vec, vec, full((1, d)),
                  full((d, r)), full((1, r)), full((r, nq)), full((1, LANES)), tab, tab, tab],
        out_specs=pl.BlockSpec((1, tm, nq), lambda bi, i: (bi, i, 0)),
        out_shape=jax.ShapeDtypeStruct((b, s, nq), BF16),
        compiler_params=_params("arbitrary", "arbitrary"),
        name="mla_q_path",
    )(x, shift, scale, g, wdq, qag, wuq_p, qg_p, cos, s1, s2)


def _mla_kv_kernel(x_ref, sh_ref, sc_ref, g_ref, wdkv_ref, kvg_ref, wuk_ref, wuvt_ref, kg_ref,
                   c_ref, s1_ref, s2_ref, k_ref, vt_ref):
    hs = _rms_mod(x_ref[0], g_ref[...], sh_ref[0], sc_ref[0]).astype(BF16)
    ck = _dot(hs, wdkv_ref[...])
    ckv = ck[:, :MLA_KV_LORA]
    ckvn = (ckv * lax.rsqrt(jnp.mean(ckv * ckv, axis=-1, keepdims=True) + EPS) * kvg_ref[...]).astype(BF16)
    kn = _dot(ckvn, wuk_ref[...])
    vt_ref[0] = _dot_nt(wuvt_ref[...], ckvn).astype(BF16)
    kpe = pltpu.roll(ck[:, MLA_KV_LORA:MLA_KV_LORA + LANES], MLA_NOPE, axis=1)
    cos, s1, s2 = c_ref[0], s1_ref[0], s2_ref[0]
    for hd in range(MLA_HEADS):
        sl = slice(hd * LANES, (hd + 1) * LANES)
        k_ref[0, :, sl] = _slab_norm_rope(kn[:, sl] + kpe, kg_ref[...], cos, s1, s2).astype(BF16)


def _mla_kv(x, shift, scale, g, wdkv_p, kvg, wuk_p, wuv_t, kg_p, cos, s1, s2, tm):
    b, s, d = x.shape
    r = wdkv_p.shape[1]
    nk = wuk_p.shape[1]
    nv = wuv_t.shape[0]
    vec = pl.BlockSpec((1, 1, d), lambda bi, i: (bi, 0, 0))
    tab = pl.BlockSpec((1, tm, LANES), lambda bi, i: (bi, i, 0))
    full = lambda shp: pl.BlockSpec(shp, lambda bi, i: (0, 0))
    return pl.pallas_call(
        _mla_kv_kernel,
        grid=(b, s // tm),
        in_specs=[pl.BlockSpec((1, tm, d), lambda bi, i: (bi, i, 0)), vec, vec, full((1, d)),
                  full((d, r)), full((1, MLA_KV_LORA)), full((MLA_KV_LORA, nk)), full((nv, MLA_KV_LORA)),
                  full((1, LANES)), tab, tab, tab],
        out_specs=[pl.BlockSpec((1, tm, nk), lambda bi, i: (bi, i, 0)),
                   pl.BlockSpec((1, nv, tm), lambda bi, i: (bi, 0, i))],
        out_shape=[jax.ShapeDtypeStruct((b, s, nk), BF16), jax.ShapeDtypeStruct((b, nv, s), BF16)],
        compiler_params=_params("arbitrary", "arbitrary"),
        name="mla_shared_kv",
    )(x, shift, scale, g, wdkv_p, kvg, wuk_p, wuv_t, kg_p, cos, s1, s2)


def _mla_attn_kernel(q_ref, k_ref, vt_ref, o_ref, *, t, nh):
    qi = pl.program_id(2)
    q2 = q_ref[0]
    q_heads = [q2[:, h * LANES:(h + 1) * LANES] for h in range(nh)]

    def chunk(j, carry, masked):
        carry = list(carry)
        start = pl.multiple_of(j * t, t)
        k2 = k_ref[0, pl.ds(start, t), :]
        vt = vt_ref[0, :, pl.ds(start, t)]
        s_heads = [_dot_nt(k2[:, h * LANES:(h + 1) * LANES], q_heads[h]) for h in range(nh)]
        if masked:
            row = lax.broadcasted_iota(jnp.int32, (t, t), 0)
            col = lax.broadcasted_iota(jnp.int32, (t, t), 1)
            allowed = row <= col
        for h in range(nh):
            m, l, acc = carry[h]
            s = jnp.where(allowed, s_heads[h], NEG) if masked else s_heads[h]
            carry[h] = _softmax_step_t(s, m, l, acc, vt[h * MLA_V:(h + 1) * MLA_V, :])
        return tuple(carry)

    init = (jnp.full((1, t), NEG, F32), jnp.zeros((1, t), F32), jnp.zeros((MLA_V, t), F32))
    carry = lax.fori_loop(0, qi, lambda j, c: chunk(j, c, False), (init,) * nh)
    carry = chunk(qi, carry, True)
    out_t = jnp.concatenate([acc / l for (_, l, acc) in carry], axis=0)
    o_ref[0] = out_t.T.astype(BF16)


def _mla_attn(q, k, vt, t, nh):
    b, s, _ = q.shape
    return pl.pallas_call(
        functools.partial(_mla_attn_kernel, t=t, nh=nh),
        grid=(b, MLA_HEADS // nh, s // t),
        in_specs=[pl.BlockSpec((1, t, nh * LANES), lambda bi, g, qi: (bi, qi, g)),
                  pl.BlockSpec((1, s, nh * LANES), lambda bi, g, qi: (bi, 0, g)),
                  pl.BlockSpec((1, nh * MLA_V, s), lambda bi, g, qi: (bi, g, 0))],
        out_specs=pl.BlockSpec((1, t, nh * MLA_V), lambda bi, g, qi: (bi, qi, g)),
        out_shape=jax.ShapeDtypeStruct((b, s, MLA_HEADS * MLA_V), BF16),
        compiler_params=_params("arbitrary", "arbitrary", "arbitrary"),
        name="mla_attention",
    )(q, k, vt)


def _t5_bias_table(rel_bias):
    n = jnp.arange(BIAS_TABLE)
    max_exact = REL_BUCKETS // 2
    nf = jnp.maximum(n, 1).astype(F32)
    large = max_exact + (jnp.log(nf / max_exact) / math.log(REL_MAX_DIST / max_exact)
                         * (REL_BUCKETS - max_exact)).astype(jnp.int32)
    large = jnp.minimum(large, REL_BUCKETS - 1)
    bucket = jnp.where(n < max_exact, n, large)
    return rel_bias.T[:, bucket].reshape(MOBA_HEADS // 2, 2, BIAS_TABLE)


def _pad_heads(w, width):
    k = w.shape[0]
    w3 = w.reshape(k, -1, width)
    return jnp.pad(w3, ((0, 0), (0, 0), (0, LANES - width))).reshape(k, -1)


def _pad_lanes(g):
    return jnp.pad(g, (0, LANES - g.shape[0])).reshape(1, LANES)


def kernel(x, c, positions, ada_w, ada_b, norm_g, ffn_w_gate, ffn_w_up, ffn_w_down, rel_bias,
           moba_w_qkv, moba_q_g, moba_k_g, moba_w_o, kv_ada_w, kv_ada_b, kv_norm_g, w_dkv,
           kv_a_norm_g, w_uk, w_uv, mla_k_g, mla_w_dq, mla_q_a_norm_g, mla_w_uq, mla_q_g, mla_w_o):
    b, s, d = x.shape
    depth = ada_w.shape[0]
    n_moba = moba_w_qkv.shape[0]
    tm_ffn = min(1024, s)
    tm_proj = min(512, s)

    c_pad = jnp.pad(c, ((0, SUBLANES - b), (0, 0)))
    mods = _mods(c_pad, ada_w, ada_b.reshape(depth, 1, -1), tn=1152)[:, :b]
    kv_mods = _mods(c_pad, kv_ada_w[None], kv_ada_b.reshape(1, 1, -1), tn=1024)[0, :b]

    def mod_vecs(m):
        return [v.reshape(b, 1, d) for v in jnp.split(m, m.shape[-1] // d, axis=-1)]

    pos_col = positions.reshape(b, s, 1)

    for layer in range(depth):
        sh1, sc1, g1, sh2, sc2, g2, sh3, sc3, g3 = mod_vecs(mods[layer])
        ng = norm_g[layer]

        def ffn(xin, idx, sh, sc, gt):
            return _ffn(xin, sh, sc, gt, ng[2 * idx:2 * idx + 1],
                        ffn_w_gate[layer, idx].astype(BF16), ffn_w_up[layer, idx].astype(BF16),
                        ffn_w_down[layer, idx].astype(BF16), tm_ffn, 256)

        x = ffn(x, 0, sh1, sc1, g1)

        if layer < n_moba:
            n_qk = 2 * MOBA_HEADS * MOBA_HEAD_DIM
            q_scale = MOBA_HEAD_DIM ** -0.5 * LOG2E
            head_gain = jnp.concatenate([jnp.tile(moba_q_g[layer] * q_scale, MOBA_HEADS),
                                         jnp.tile(moba_k_g[layer], MOBA_HEADS)]).reshape(1, -1)
            w_qkv = moba_w_qkv[layer]
            qk, vt = _moba_proj(x, sh2, sc2, ng[1:2], w_qkv[:, :n_qk].astype(BF16),
                                w_qkv[:, n_qk:].T.astype(BF16), head_gain, tm_proj, 512)
            o = _moba_attn(qk, vt, _t5_bias_table(rel_bias) * LOG2E, positions, MOBA_TILE, ATTN_HEADS // 2)
            x = _oproj(x, o, g2, moba_w_o[layer].astype(BF16), tm_proj)
        else:
            j = layer - n_moba
            if j == 0:
                half = MLA_ROPE // 2
                inv = ROPE_BASE ** (-jnp.arange(half, dtype=F32) / half)
                inv_lane = jnp.zeros((LANES,), F32).at[MLA_NOPE:MLA_QK].set(jnp.tile(inv, 2)).reshape(1, LANES)
                cos, s1, s2 = _rope_tables(pos_col, inv_lane, tm_proj)
                kv_sh, kv_sc = mod_vecs(kv_mods)
                wdkv_p = jnp.pad(w_dkv, ((0, 0), (0, MLA_KV_LORA + LANES - w_dkv.shape[1]))).astype(BF16)
                shared_k, shared_v = _mla_kv(
                    x, kv_sh, kv_sc, kv_norm_g.reshape(1, d), wdkv_p, kv_a_norm_g.reshape(1, -1),
                    _pad_heads(w_uk, MLA_NOPE).astype(BF16), w_uv.T.astype(BF16), _pad_lanes(mla_k_g),
                    cos, s1, s2, tm_proj)
            q = _mla_q(x, sh2, sc2, ng[1:2], mla_w_dq[j].astype(BF16), mla_q_a_norm_g[j].reshape(1, -1),
                       _pad_heads(mla_w_uq[j], MLA_QK).astype(BF16), _pad_lanes(mla_q_g[j]),
                       cos, s1, s2, tm_proj)
            o = _mla_attn(q, shared_k, shared_v, MLA_TILE, ATTN_HEADS)
            x = _oproj(x, o, g2, mla_w_o[j].astype(BF16), tm_proj)

        x = ffn(x, 1, sh3, sc3, g3)
    return x
```

```python
import functools
import math

import jax
import jax.numpy as jnp
from jax import lax
from jax.experimental import pallas as pl
from jax.experimental.pallas import tpu as pltpu

F32 = jnp.float32
BF16 = jnp.bfloat16

LANES = 128
SUBLANES = 8
VMEM_LIMIT_BYTES = 56 * 1024 * 1024

MOBA_HEADS = 16
MOBA_HEAD_DIM = 64
MOBA_BLOCK = 256
MOBA_TOPK = 3
REL_BUCKETS = 32
REL_MAX_DIST = 128
MLA_HEADS = 16
MLA_KV_LORA = 256
MLA_NOPE = 64
MLA_ROPE = 32
MLA_V = 64
MLA_QK = MLA_NOPE + MLA_ROPE
ROPE_BASE = 10000.0
EPS = 1e-6
NEG = -1e30

MOBA_TILE = 512
MLA_TILE = 512
ATTN_HEADS = 8
LOG2E = math.log2(math.e)
BIAS_TABLE = LANES


def _params(*sem):
    return pltpu.CompilerParams(dimension_semantics=sem, vmem_limit_bytes=VMEM_LIMIT_BYTES)


def _dot(a, b):
    return jnp.dot(a, b, preferred_element_type=F32)


def _dot_nt(a, b):
    return lax.dot_general(a, b, (((1,), (1,)), ((), ())), preferred_element_type=F32)


def _rms_mod(x, g, shift, scale):
    ms = jnp.mean(x * x, axis=-1, keepdims=True)
    y = x * lax.rsqrt(ms + EPS) * g
    return y * (1.0 + scale) + shift


def _mods_kernel(c_ref, w_ref, b_ref, o_ref):
    c = c_ref[...]
    ca = c * jax.nn.sigmoid(c)
    o_ref[0] = jnp.dot(ca, w_ref[0], precision=lax.Precision.HIGHEST,
                       preferred_element_type=F32) + b_ref[0]


def _mods(c_pad, w, b, tn):
    nl, d, n = w.shape
    return pl.pallas_call(
        _mods_kernel,
        grid=(nl, n // tn),
        in_specs=[pl.BlockSpec((SUBLANES, d), lambda l, j: (0, 0)),
                  pl.BlockSpec((1, d, tn), lambda l, j: (l, 0, j)),
                  pl.BlockSpec((1, 1, tn), lambda l, j: (l, 0, j))],
        out_specs=pl.BlockSpec((1, SUBLANES, tn), lambda l, j: (l, 0, j)),
        out_shape=jax.ShapeDtypeStruct((nl, SUBLANES, n), F32),
        compiler_params=_params("arbitrary", "arbitrary"),
        name="adaln_mods",
    )(c_pad, w, b)


def _ffn_kernel(x_ref, sh_ref, sc_ref, gt_ref, g_ref, wg_ref, wu_ref, wd_ref, o_ref, *, tf):
    x = x_ref[0]
    h = _rms_mod(x, g_ref[...], sh_ref[0], sc_ref[0]).astype(BF16)
    acc = None
    for f in range(wg_ref.shape[1] // tf):
        cols = slice(f * tf, (f + 1) * tf)
        g = _dot(h, wg_ref[:, cols])
        u = _dot(h, wu_ref[:, cols])
        a = (g * jax.nn.sigmoid(g) * u).astype(BF16)
        d = _dot(a, wd_ref[cols, :])
        acc = d if acc is None else acc + d
    o_ref[0] = x + (0.5 * gt_ref[0]) * acc


def _ffn(x, shift, scale, gate, g, wg, wu, wd, tm, tf):
    b, s, d = x.shape
    f = wg.shape[1]
    row = pl.BlockSpec((1, tm, d), lambda bi, i: (bi, i, 0))
    vec = pl.BlockSpec((1, 1, d), lambda bi, i: (bi, 0, 0))
    full = lambda shp: pl.BlockSpec(shp, lambda bi, i: (0, 0))
    return pl.pallas_call(
        functools.partial(_ffn_kernel, tf=tf),
        grid=(b, s // tm),
        in_specs=[row, vec, vec, vec, full((1, d)), full((d, f)), full((d, f)), full((f, d))],
        out_specs=row,
        out_shape=jax.ShapeDtypeStruct((b, s, d), F32),
        compiler_params=_params("arbitrary", "arbitrary"),
        name="ffn_swiglu",
    )(x, shift, scale, gate, g, wg, wu, wd)


def _headnorm_pair(y, g, dim):
    lane = lax.broadcasted_iota(jnp.int32, (1, LANES), 1)
    left = lane < dim
    y2 = y * y
    ss_a = jnp.sum(jnp.where(left, y2, 0.0), axis=-1, keepdims=True)
    ss_b = jnp.sum(jnp.where(left, 0.0, y2), axis=-1, keepdims=True)
    inv = jnp.where(left, lax.rsqrt(ss_a * (1.0 / dim) + EPS), lax.rsqrt(ss_b * (1.0 / dim) + EPS))
    return y * inv * g


def _moba_proj_kernel(x_ref, sh_ref, sc_ref, g_ref, w_ref, wvt_ref, hg_ref, qk_ref, vt_ref, *, tn):
    h = _rms_mod(x_ref[0], g_ref[...], sh_ref[0], sc_ref[0]).astype(BF16)
    for n in range(w_ref.shape[1] // tn):
        res = _dot(h, w_ref[:, n * tn:(n + 1) * tn])
        for s in range(tn // LANES):
            sl = slice(n * tn + s * LANES, n * tn + (s + 1) * LANES)
            qk_ref[0, :, sl] = _headnorm_pair(res[:, s * LANES:(s + 1) * LANES], hg_ref[:, sl],
                                              MOBA_HEAD_DIM).astype(BF16)
    for n in range(wvt_ref.shape[0] // tn):
        rows = slice(n * tn, (n + 1) * tn)
        vt_ref[0, rows, :] = _dot_nt(wvt_ref[rows, :], h).astype(BF16)


def _moba_proj(x, shift, scale, g, w_qk, w_vt, head_gain, tm, tn):
    b, s, d = x.shape
    nqk = w_qk.shape[1]
    nv = w_vt.shape[0]
    vec = pl.BlockSpec((1, 1, d), lambda bi, i: (bi, 0, 0))
    full = lambda shp: pl.BlockSpec(shp, lambda bi, i: (0, 0))
    return pl.pallas_call(
        functools.partial(_moba_proj_kernel, tn=tn),
        grid=(b, s // tm),
        in_specs=[pl.BlockSpec((1, tm, d), lambda bi, i: (bi, i, 0)), vec, vec, full((1, d)),
                  full((d, nqk)), full((nv, d)), full((1, nqk))],
        out_specs=[pl.BlockSpec((1, tm, nqk), lambda bi, i: (bi, i, 0)),
                   pl.BlockSpec((1, nv, tm), lambda bi, i: (bi, 0, i))],
        out_shape=[jax.ShapeDtypeStruct((b, s, nqk), BF16), jax.ShapeDtypeStruct((b, nv, s), BF16)],
        compiler_params=_params("arbitrary", "arbitrary"),
        name="moba_qkv_proj",
    )(x, shift, scale, g, w_qk, w_vt, head_gain)


def _softmax_step_t(s, m, l, acc, vt):
    m_new = jnp.maximum(m, jnp.max(s, axis=0, keepdims=True))
    alpha = jnp.exp2(m - m_new)
    p = jnp.exp2(s - m_new)
    l_new = alpha * l + jnp.sum(p, axis=0, keepdims=True)
    acc_new = acc * alpha + _dot(vt, p.astype(BF16))
    return m_new, l_new, acc_new


def _moba_attn_kernel(qmin_ref, kmax_ref, q_ref, k_ref, vt_ref, bt_ref, pq_ref, pk_ref, o_ref,
                      km_scr, sel_scr, *, nb, t, npairs):
    blk = MOBA_BLOCK
    hd = MOBA_HEAD_DIM
    nsub = t // blk
    nh = 2 * npairs
    bi = pl.program_id(0)
    qi = pl.program_id(2)
    lane = lax.broadcasted_iota(jnp.int32, (1, LANES), 1)
    left = lane < hd

    @pl.when(qi == 0)
    def _():
        km_scr[...] = jnp.zeros_like(km_scr)
        for r in range(nb):
            kblk = k_ref[0, r * blk:(r + 1) * blk, :].astype(F32)
            km_scr[r:r + 1, :] = jnp.sum(kblk, axis=0, keepdims=True) * (1.0 / blk)

    q_heads = []
    for p in range(npairs):
        q2 = q_ref[0, :, p * LANES:(p + 1) * LANES]
        zero = jnp.zeros_like(q2)
        q_heads += [jnp.where(left, q2, zero), jnp.where(left, zero, q2)]

    col = lax.broadcasted_iota(jnp.int32, (1, t), 1)
    q_idx = qi * t + col
    q_sub = sum((col >= r * blk).astype(jnp.int32) for r in range(1, nsub))
    q_blk = qi * nsub + q_sub
    q_blk_start = q_blk * blk

    blk_row = lax.broadcasted_iota(jnp.int32, (nb, t), 0)
    for h in range(nh):
        km = km_scr[:, (h // 2) * LANES:(h // 2 + 1) * LANES]
        km_hi = km.astype(BF16)
        km_lo = (km - km_hi.astype(F32)).astype(BF16)
        gate = (_dot_nt(km_hi, q_heads[h]) + _dot_nt(km_lo, q_heads[h]))[:nb]
        gate = jnp.where(blk_row < q_blk, gate, -jnp.inf)
        sel = jnp.zeros((nb, t), F32)
        for _ in range(MOBA_TOPK):
            mx = jnp.max(gate, axis=0, keepdims=True)
            idx = jnp.min(jnp.where(gate == mx, blk_row, nb), axis=0, keepdims=True)
            pick = (blk_row == idx) & (mx > -jnp.inf)
            sel = jnp.where(pick, 1.0, sel)
            gate = jnp.where(pick, -jnp.inf, gate)
        sel_scr[h] = sel

    pq = pq_ref[0, 0]

    def chunk(j, carry, mode):
        carry = list(carry)
        start = pl.multiple_of(j * t, t)
        k2 = k_ref[0, pl.ds(start, t), :]
        vt = vt_ref[0, :, pl.ds(start, t)]
        s_heads = [_dot_nt(k2[:, (h // 2) * LANES:(h // 2 + 1) * LANES], q_heads[h]) for h in range(nh)]
        if mode != "far":
            dist = jnp.clip(pq - pk_ref[0, pl.ds(start, t), :], 0, BIAS_TABLE - 1)
        if mode == "diag":
            key_idx = start + lax.broadcasted_iota(jnp.int32, (blk, t), 0)
        for h in range(nh):
            m, l, acc = carry[h]
            s = s_heads[h]
            vt_h = vt[h * hd:(h + 1) * hd, :]
            chosen = [sel_scr[h, pl.ds(j * nsub + r, 1), :] > 0.0 for r in range(nsub)]
            if mode == "far":
                pieces = [s[r * blk:(r + 1) * blk] for r in range(nsub)]
                c = bt_ref[h // 2, h % 2:h % 2 + 1, BIAS_TABLE - 1:BIAS_TABLE]
                tmax = [jnp.where(chosen[r], jnp.max(pieces[r], axis=0, keepdims=True) + c, NEG)
                        for r in range(nsub)]
                m_new = functools.reduce(jnp.maximum, tmax, m)
                alpha = jnp.exp2(m - m_new)
                ps = [jnp.exp2(pieces[r] - jnp.where(chosen[r], m_new - c, jnp.inf)) for r in range(nsub)]
                l_new = alpha * l + sum(jnp.sum(p, axis=0, keepdims=True) for p in ps)
                p = jnp.concatenate(ps, axis=0) if nsub > 1 else ps[0]
                carry[h] = (m_new, l_new, acc * alpha + _dot(vt_h, p.astype(BF16)))
            else:
                table = jnp.broadcast_to(bt_ref[h // 2, h % 2:h % 2 + 1, :], (t, LANES))
                bias = jnp.concatenate(
                    [jnp.take_along_axis(table, dist[:, g * LANES:(g + 1) * LANES], axis=1,
                                         mode="promise_in_bounds") for g in range(t // LANES)], axis=1)
                s = s + bias
                masked = []
                for r in range(nsub):
                    piece = s[r * blk:(r + 1) * blk]
                    if mode == "diag":
                        kidx = key_idx + r * blk
                        keep_own = jnp.where(kidx <= q_idx, jnp.where(kidx >= q_blk_start, piece, NEG), NEG)
                        masked.append(jnp.where(chosen[r], piece, keep_own))
                    else:
                        masked.append(jnp.where(chosen[r], piece, NEG))
                s = jnp.concatenate(masked, axis=0) if nsub > 1 else masked[0]
                carry[h] = _softmax_step_t(s, m, l, acc, vt_h)
        return tuple(carry)

    def past_chunk(j, carry):
        far = qmin_ref[bi, qi] - kmax_ref[bi, j] >= BIAS_TABLE - 1
        return lax.cond(far, lambda c: chunk(j, c, "far"), lambda c: chunk(j, c, "near"), carry)

    init = (jnp.full((1, t), NEG, F32), jnp.zeros((1, t), F32), jnp.zeros((hd, t), F32))
    carry = lax.fori_loop(0, qi, past_chunk, (init,) * nh)
    carry = chunk(qi, carry, "diag")
    out_t = jnp.concatenate([acc / l for (_, l, acc) in carry], axis=0)
    o_ref[0] = out_t.T.astype(BF16)


def _moba_attn(qk, vt, bias_table, positions, t, npairs):
    b, s, _ = qk.shape
    nb = s // MOBA_BLOCK
    ngroups = MOBA_HEADS * MOBA_HEAD_DIM // (LANES * npairs)
    w = npairs * LANES
    pos_tiles = positions.reshape(b, s // t, t)
    q_min = jnp.min(pos_tiles, axis=-1)
    k_max = jnp.max(pos_tiles, axis=-1)
    pos_row = positions.reshape(b, s // t, 1, t)
    pos_col = positions.reshape(b, s, 1)
    grid_spec = pltpu.PrefetchScalarGridSpec(
        num_scalar_prefetch=2,
        grid=(b, ngroups, s // t),
        in_specs=[pl.BlockSpec((1, t, w), lambda bi, g, qi, *_: (bi, qi, g)),
                  pl.BlockSpec((1, s, w), lambda bi, g, qi, *_: (bi, 0, ngroups + g)),
                  pl.BlockSpec((1, w, s), lambda bi, g, qi, *_: (bi, g, 0)),
                  pl.BlockSpec((npairs, 2, LANES), lambda bi, g, qi, *_: (g, 0, 0)),
                  pl.BlockSpec((1, 1, 1, t), lambda bi, g, qi, *_: (bi, qi, 0, 0)),
                  pl.BlockSpec((1, s, 1), lambda bi, g, qi, *_: (bi, 0, 0))],
        out_specs=pl.BlockSpec((1, t, w), lambda bi, g, qi, *_: (bi, qi, g)),
        scratch_shapes=[pltpu.VMEM((LANES, w), F32), pltpu.VMEM((2 * npairs, nb, t), F32)],
    )
    return pl.pallas_call(
        functools.partial(_moba_attn_kernel, nb=nb, t=t, npairs=npairs),
        grid_spec=grid_spec,
        out_shape=jax.ShapeDtypeStruct((b, s, MOBA_HEADS * MOBA_HEAD_DIM), BF16),
        compiler_params=_params("arbitrary", "arbitrary", "arbitrary"),
        name="moba_attention",
    )(q_min, k_max, qk, qk, vt, bias_table, pos_row, pos_col)


def _oproj_kernel(x_ref, a_ref, gt_ref, w_ref, o_ref):
    o_ref[0] = x_ref[0] + gt_ref[0] * _dot(a_ref[0], w_ref[...])


def _oproj(x, a, gate, w, tm):
    b, s, d = x.shape
    k = a.shape[2]
    return pl.pallas_call(
        _oproj_kernel,
        grid=(b, s // tm),
        in_specs=[pl.BlockSpec((1, tm, d), lambda bi, i: (bi, i, 0)),
                  pl.BlockSpec((1, tm, k), lambda bi, i: (bi, i, 0)),
                  pl.BlockSpec((1, 1, d), lambda bi, i: (bi, 0, 0)),
                  pl.BlockSpec((k, d), lambda bi, i: (0, 0))],
        out_specs=pl.BlockSpec((1, tm, d), lambda bi, i: (bi, i, 0)),
        out_shape=jax.ShapeDtypeStruct((b, s, d), F32),
        compiler_params=_params("arbitrary", "arbitrary"),
        name="attn_out_proj",
    )(x, a, gate, w)


def _rope_tab_kernel(pos_ref, inv_ref, c_ref, s_ref):
    half = MLA_ROPE // 2
    ang = pos_ref[0].astype(F32) * inv_ref[...]
    lane = lax.broadcasted_iota(jnp.int32, (1, LANES), 1)
    sn = jnp.sin(ang)
    c_ref[0] = jnp.cos(ang)
    s_ref[0] = jnp.where(lane < MLA_NOPE + half, -sn, sn)


def _rope_tables(pos_col, inv_lane, tm):
    b, s, _ = pos_col.shape
    out = pl.BlockSpec((1, tm, LANES), lambda bi, i: (bi, i, 0))
    shape = jax.ShapeDtypeStruct((b, s, LANES), F32)
    return pl.pallas_call(
        _rope_tab_kernel,
        grid=(b, s // tm),
        in_specs=[pl.BlockSpec((1, tm, 1), lambda bi, i: (bi, i, 0)),
                  pl.BlockSpec((1, LANES), lambda bi, i: (0, 0))],
        out_specs=[out, out],
        out_shape=[shape, shape],
        compiler_params=_params("arbitrary", "arbitrary"),
        name="rope_tables",
    )(pos_col, inv_lane)


def _mla_q_kernel(x_ref, sh_ref, sc_ref, g_ref, wdq_ref, qag_ref, wuq_ref, wuqp_ref, qg_ref, qgp_ref,
                  c_ref, s_ref, o_ref):
    h = _rms_mod(x_ref[0], g_ref[...], sh_ref[0], sc_ref[0]).astype(BF16)
    dq = _dot(h, wdq_ref[...])
    dqn = (dq * lax.rsqrt(jnp.mean(dq * dq, axis=-1, keepdims=True) + EPS) * qag_ref[...]).astype(BF16)
    uq = _dot(dqn, wuq_ref[...])
    up = _dot(dqn, wuqp_ref[...])
    cg = c_ref[0] * qg_ref[...]
    sg = s_ref[0] * qgp_ref[...]
    scale = MLA_QK ** -0.5 * LOG2E
    for hd in range(MLA_HEADS):
        sl = slice(hd * LANES, (hd + 1) * LANES)
        u = uq[:, sl]
        inv = lax.rsqrt(jnp.sum(u * u, axis=-1, keepdims=True) * (1.0 / MLA_QK) + EPS) * scale
        o_ref[0, :, sl] = ((u * cg + up[:, sl] * sg) * inv).astype(BF16)


def _mla_q(x, shift, scale, g, wdq, qag, wuq_p, wuq_partner, qg_p, qg_partner, cos, sin, tm):
    b, s, d = x.shape
    r = wdq.shape[1]
    nq = wuq_p.shape[1]
    vec = pl.BlockSpec((1, 1, d), lambda bi, i: (bi, 0, 0))
    tab = pl.BlockSpec((1, tm, LANES), lambda bi, i: (bi, i, 0))
    full = lambda shp: pl.BlockSpec(shp, lambda bi, i: (0, 0))
    return pl.pallas_call(
        _mla_q_kernel,
        grid=(b, s // tm),
        in_specs=[pl.BlockSpec((1, tm, d), lambda bi, i: (bi, i, 0)), vec, vec, full((1, d)),
                  full((d, r)), full((1, r)), full((r, nq)), full((r, nq)), full((1, LANES)), full((1, LANES)),
                  tab, tab],
        out_specs=pl.BlockSpec((1, tm, nq), lambda bi, i: (bi, i, 0)),
        out_shape=jax.ShapeDtypeStruct((b, s, nq), BF16),
        compiler_params=_params("arbitrary", "arbitrary"),
        name="mla_q_path",
    )(x, shift, scale, g, wdq, qag, wuq_p, wuq_partner, qg_p, qg_partner, cos, sin)


def _mla_kv_kernel(x_ref, sh_ref, sc_ref, g_ref, wdkv_ref, kvg_ref, wuk_ref, wuvt_ref, kg_ref, kgp_ref,
                   c_ref, s_ref, k_ref, vt_ref):
    hs = _rms_mod(x_ref[0], g_ref[...], sh_ref[0], sc_ref[0]).astype(BF16)
    ck = _dot(hs, wdkv_ref[...])
    ckv = ck[:, :MLA_KV_LORA]
    ckvn = (ckv * lax.rsqrt(jnp.mean(ckv * ckv, axis=-1, keepdims=True) + EPS) * kvg_ref[...]).astype(BF16)
    kn = _dot(ckvn, wuk_ref[...])
    vt_ref[0] = _dot_nt(wuvt_ref[...], ckvn).astype(BF16)
    lane = lax.broadcasted_iota(jnp.int32, (1, LANES), 1)
    on_rope = (lane >= MLA_NOPE) & (lane < MLA_QK)
    tail = ck[:, MLA_KV_LORA:MLA_KV_LORA + LANES]
    kpe = jnp.where(on_rope, pltpu.roll(tail, MLA_NOPE, axis=1), 0.0)
    partner = pltpu.roll(tail, MLA_ROPE, axis=1)
    cg = c_ref[0] * kg_ref[...]
    sg = s_ref[0] * kgp_ref[...]
    rot = kpe * cg + partner * sg
    ss_pe = jnp.sum(kpe * kpe, axis=-1, keepdims=True)
    for hd in range(MLA_HEADS):
        sl = slice(hd * LANES, (hd + 1) * LANES)
        nope = kn[:, sl]
        inv = lax.rsqrt((jnp.sum(nope * nope, axis=-1, keepdims=True) + ss_pe) * (1.0 / MLA_QK) + EPS)
        k_ref[0, :, sl] = ((nope * cg + rot) * inv).astype(BF16)


def _mla_kv(x, shift, scale, g, wdkv_p, kvg, wuk_p, wuv_t, kg_p, kg_partner, cos, sin, tm):
    b, s, d = x.shape
    r = wdkv_p.shape[1]
    nk = wuk_p.shape[1]
    nv = wuv_t.shape[0]
    vec = pl.BlockSpec((1, 1, d), lambda bi, i: (bi, 0, 0))
    tab = pl.BlockSpec((1, tm, LANES), lambda bi, i: (bi, i, 0))
    full = lambda shp: pl.BlockSpec(shp, lambda bi, i: (0, 0))
    return pl.pallas_call(
        _mla_kv_kernel,
        grid=(b, s // tm),
        in_specs=[pl.BlockSpec((1, tm, d), lambda bi, i: (bi, i, 0)), vec, vec, full((1, d)),
                  full((d, r)), full((1, MLA_KV_LORA)), full((MLA_KV_LORA, nk)), full((nv, MLA_KV_LORA)),
                  full((1, LANES)), full((1, LANES)), tab, tab],
        out_specs=[pl.BlockSpec((1, tm, nk), lambda bi, i: (bi, i, 0)),
                   pl.BlockSpec((1, nv, tm), lambda bi, i: (bi, 0, i))],
        out_shape=[jax.ShapeDtypeStruct((b, s, nk), BF16), jax.ShapeDtypeStruct((b, nv, s), BF16)],
        compiler_params=_params("arbitrary", "arbitrary"),
        name="mla_shared_kv",
    )(x, shift, scale, g, wdkv_p, kvg, wuk_p, wuv_t, kg_p, kg_partner, cos, sin)


def _mla_attn_kernel(q_ref, k_ref, vt_ref, o_ref, *, t, nh):
    qi = pl.program_id(2)
    q2 = q_ref[0]
    q_heads = [q2[:, h * LANES:(h + 1) * LANES] for h in range(nh)]

    def chunk(j, carry, masked):
        carry = list(carry)
        start = pl.multiple_of(j * t, t)
        k2 = k_ref[0, pl.ds(start, t), :]
        vt = vt_ref[0, :, pl.ds(start, t)]
        s_heads = [_dot_nt(k2[:, h * LANES:(h + 1) * LANES], q_heads[h]) for h in range(nh)]
        if masked:
            row = lax.broadcasted_iota(jnp.int32, (t, t), 0)
            col = lax.broadcasted_iota(jnp.int32, (t, t), 1)
            allowed = row <= col
        for h in range(nh):
            m, l, acc = carry[h]
            s = jnp.where(allowed, s_heads[h], NEG) if masked else s_heads[h]
            carry[h] = _softmax_step_t(s, m, l, acc, vt[h * MLA_V:(h + 1) * MLA_V, :])
        return tuple(carry)

    init = (jnp.full((1, t), NEG, F32), jnp.zeros((1, t), F32), jnp.zeros((MLA_V, t), F32))
    carry = lax.fori_loop(0, qi, lambda j, c: chunk(j, c, False), (init,) * nh)
    carry = chunk(qi, carry, True)
    out_t = jnp.concatenate([acc / l for (_, l, acc) in carry], axis=0)
    o_ref[0] = out_t.T.astype(BF16)


def _mla_attn(q, k, vt, t, nh):
    b, s, _ = q.shape
    return pl.pallas_call(
        functools.partial(_mla_attn_kernel, t=t, nh=nh),
        grid=(b, MLA_HEADS // nh, s // t),
        in_specs=[pl.BlockSpec((1, t, nh * LANES), lambda bi, g, qi: (bi, qi, g)),
                  pl.BlockSpec((1, s, nh * LANES), lambda bi, g, qi: (bi, 0, g)),
                  pl.BlockSpec((1, nh * MLA_V, s), lambda bi, g, qi: (bi, g, 0))],
        out_specs=pl.BlockSpec((1, t, nh * MLA_V), lambda bi, g, qi: (bi, qi, g)),
        out_shape=jax.ShapeDtypeStruct((b, s, MLA_HEADS * MLA_V), BF16),
        compiler_params=_params("arbitrary", "arbitrary", "arbitrary"),
        name="mla_attention",
    )(q, k, vt)


def _t5_bias_table(rel_bias):
    n = jnp.arange(BIAS_TABLE)
    max_exact = REL_BUCKETS // 2
    nf = jnp.maximum(n, 1).astype(F32)
    large = max_exact + (jnp.log(nf / max_exact) / math.log(REL_MAX_DIST / max_exact)
                         * (REL_BUCKETS - max_exact)).astype(jnp.int32)
    large = jnp.minimum(large, REL_BUCKETS - 1)
    bucket = jnp.where(n < max_exact, n, large)
    return rel_bias.T[:, bucket].reshape(MOBA_HEADS // 2, 2, BIAS_TABLE)


def _pad_heads(w, width):
    k = w.shape[0]
    w3 = w.reshape(k, -1, width)
    return jnp.pad(w3, ((0, 0), (0, 0), (0, LANES - width))).reshape(k, -1)


def _pad_lanes(g):
    return jnp.pad(g, (0, LANES - g.shape[0])).reshape(1, LANES)


def _swap_rope_halves(a):
    half = MLA_ROPE // 2
    return jnp.concatenate([jnp.zeros_like(a[..., :MLA_NOPE]), a[..., MLA_NOPE + half:],
                            a[..., MLA_NOPE:MLA_NOPE + half]], axis=-1)


def kernel(x, c, positions, ada_w, ada_b, norm_g, ffn_w_gate, ffn_w_up, ffn_w_down, rel_bias,
           moba_w_qkv, moba_q_g, moba_k_g, moba_w_o, kv_ada_w, kv_ada_b, kv_norm_g, w_dkv,
           kv_a_norm_g, w_uk, w_uv, mla_k_g, mla_w_dq, mla_q_a_norm_g, mla_w_uq, mla_q_g, mla_w_o):
    b, s, d = x.shape
    depth = ada_w.shape[0]
    n_moba = moba_w_qkv.shape[0]
    tm_ffn = min(512, s)
    tm_proj = min(512, s)

    c_pad = jnp.pad(c, ((0, SUBLANES - b), (0, 0)))
    mods = _mods(c_pad, ada_w, ada_b.reshape(depth, 1, -1), tn=1152)[:, :b]
    kv_mods = _mods(c_pad, kv_ada_w[None], kv_ada_b.reshape(1, 1, -1), tn=1024)[0, :b]

    def mod_vecs(m):
        return [v.reshape(b, 1, d) for v in jnp.split(m, m.shape[-1] // d, axis=-1)]

    pos_col = positions.reshape(b, s, 1)

    for layer in range(depth):
        sh1, sc1, g1, sh2, sc2, g2, sh3, sc3, g3 = mod_vecs(mods[layer])
        ng = norm_g[layer]

        def ffn(xin, idx, sh, sc, gt):
            return _ffn(xin, sh, sc, gt, ng[2 * idx:2 * idx + 1],
                        ffn_w_gate[layer, idx].astype(BF16), ffn_w_up[layer, idx].astype(BF16),
                        ffn_w_down[layer, idx].astype(BF16), tm_ffn, 256)

        x = ffn(x, 0, sh1, sc1, g1)

        if layer < n_moba:
            n_qk = 2 * MOBA_HEADS * MOBA_HEAD_DIM
            q_scale = MOBA_HEAD_DIM ** -0.5 * LOG2E
            head_gain = jnp.concatenate([jnp.tile(moba_q_g[layer] * q_scale, MOBA_HEADS),
                                         jnp.tile(moba_k_g[layer], MOBA_HEADS)]).reshape(1, -1)
            w_qkv = moba_w_qkv[layer]
            qk, vt = _moba_proj(x, sh2, sc2, ng[1:2], w_qkv[:, :n_qk].astype(BF16),
                                w_qkv[:, n_qk:].T.astype(BF16), head_gain, tm_proj, 512)
            o = _moba_attn(qk, vt, _t5_bias_table(rel_bias) * LOG2E, positions, MOBA_TILE, ATTN_HEADS // 2)
            x = _oproj(x, o, g2, moba_w_o[layer].astype(BF16), tm_proj)
        else:
            j = layer - n_moba
            if j == 0:
                half = MLA_ROPE // 2
                inv = ROPE_BASE ** (-jnp.arange(half, dtype=F32) / half)
                inv_lane = jnp.zeros((LANES,), F32).at[MLA_NOPE:MLA_QK].set(jnp.tile(inv, 2)).reshape(1, LANES)
                cos, sin = _rope_tables(pos_col, inv_lane, tm_proj)
                kv_sh, kv_sc = mod_vecs(kv_mods)
                w_kpe = w_dkv[:, MLA_KV_LORA:]
                w_kpe_swapped = jnp.concatenate([w_kpe[:, half:], w_kpe[:, :half]], axis=1)
                wdkv_p = jnp.pad(jnp.concatenate([w_dkv, w_kpe_swapped], axis=1),
                                 ((0, 0), (0, LANES - 2 * MLA_ROPE))).astype(BF16)
                shared_k, shared_v = _mla_kv(
                    x, kv_sh, kv_sc, kv_norm_g.reshape(1, d), wdkv_p, kv_a_norm_g.reshape(1, -1),
                    _pad_heads(w_uk, MLA_NOPE).astype(BF16), w_uv.T.astype(BF16), _pad_lanes(mla_k_g),
                    _pad_lanes(_swap_rope_halves(mla_k_g)), cos, sin, tm_proj)
            w_uq = mla_w_uq[j]
            w_uq_partner = _swap_rope_halves(w_uq.reshape(w_uq.shape[0], MLA_HEADS, MLA_QK)).reshape(w_uq.shape)
            q = _mla_q(x, sh2, sc2, ng[1:2], mla_w_dq[j].astype(BF16), mla_q_a_norm_g[j].reshape(1, -1),
                       _pad_heads(w_uq, MLA_QK).astype(BF16), _pad_heads(w_uq_partner, MLA_QK).astype(BF16),
                       _pad_lanes(mla_q_g[j]), _pad_lanes(_swap_rope_halves(mla_q_g[j])), cos, sin, tm_proj)
            o = _mla_attn(q, shared_k, shared_v, MLA_TILE, ATTN_HEADS)
            x = _oproj(x, o, g2, mla_w_o[j].astype(BF16), tm_proj)

        x = ffn(x, 1, sh3, sc3, g3)
    return x
```

```python
import functools
import math

import jax
import jax.numpy as jnp
from jax import lax
from jax.experimental import pallas as pl
from jax.experimental.pallas import tpu as pltpu

F32 = jnp.float32
BF16 = jnp.bfloat16

LANES = 128
SUBLANES = 8
VMEM_LIMIT_BYTES = 56 * 1024 * 1024

MOBA_HEADS = 16
MOBA_HEAD_DIM = 64
MOBA_BLOCK = 256
MOBA_TOPK = 3
REL_BUCKETS = 32
REL_MAX_DIST = 128
MLA_HEADS = 16
MLA_KV_LORA = 256
MLA_NOPE = 64
MLA_ROPE = 32
MLA_V = 64
MLA_QK = MLA_NOPE + MLA_ROPE
ROPE_BASE = 10000.0
EPS = 1e-6
NEG = -1e30

MOBA_TILE = 512
MLA_TILE = 512
ATTN_HEADS = 8
LOG2E = math.log2(math.e)
BIAS_TABLE = LANES
SCORE_BOUND_MAX = 60.0


def _params(*sem):
    return pltpu.CompilerParams(dimension_semantics=sem, vmem_limit_bytes=VMEM_LIMIT_BYTES)


def _dot(a, b):
    return jnp.dot(a, b, preferred_element_type=F32)


def _dot_nt(a, b):
    return lax.dot_general(a, b, (((1,), (1,)), ((), ())), preferred_element_type=F32)


def _rms_mod(x, g, shift, scale):
    ms = jnp.mean(x * x, axis=-1, keepdims=True)
    y = x * lax.rsqrt(ms + EPS) * g
    return y * (1.0 + scale) + shift


def _mods_kernel(c_ref, w_ref, b_ref, o_ref):
    c = c_ref[...]
    ca = c * jax.nn.sigmoid(c)
    o_ref[0] = jnp.dot(ca, w_ref[0], precision=lax.Precision.HIGHEST,
                       preferred_element_type=F32) + b_ref[0]


def _mods(c_pad, w, b, tn):
    nl, d, n = w.shape
    return pl.pallas_call(
        _mods_kernel,
        grid=(nl, n // tn),
        in_specs=[pl.BlockSpec((SUBLANES, d), lambda l, j: (0, 0)),
                  pl.BlockSpec((1, d, tn), lambda l, j: (l, 0, j)),
                  pl.BlockSpec((1, 1, tn), lambda l, j: (l, 0, j))],
        out_specs=pl.BlockSpec((1, SUBLANES, tn), lambda l, j: (l, 0, j)),
        out_shape=jax.ShapeDtypeStruct((nl, SUBLANES, n), F32),
        compiler_params=_params("arbitrary", "arbitrary"),
        name="adaln_mods",
    )(c_pad, w, b)


def _ffn_kernel(x_ref, sh_ref, sc_ref, gt_ref, g_ref, wg_ref, wu_ref, wd_ref, o_ref, *, tf):
    x = x_ref[0]
    h = _rms_mod(x, g_ref[...], sh_ref[0], sc_ref[0]).astype(BF16)
    acc = None
    for f in range(wg_ref.shape[1] // tf):
        cols = slice(f * tf, (f + 1) * tf)
        g = _dot(h, wg_ref[:, cols])
        u = _dot(h, wu_ref[:, cols])
        a = (g * jax.nn.sigmoid(g) * u).astype(BF16)
        d = _dot(a, wd_ref[cols, :])
        acc = d if acc is None else acc + d
    o_ref[0] = x + (0.5 * gt_ref[0]) * acc


def _ffn(x, shift, scale, gate, g, wg, wu, wd, tm, tf):
    b, s, d = x.shape
    f = wg.shape[1]
    row = pl.BlockSpec((1, tm, d), lambda bi, i: (bi, i, 0))
    vec = pl.BlockSpec((1, 1, d), lambda bi, i: (bi, 0, 0))
    full = lambda shp: pl.BlockSpec(shp, lambda bi, i: (0, 0))
    return pl.pallas_call(
        functools.partial(_ffn_kernel, tf=tf),
        grid=(b, s // tm),
        in_specs=[row, vec, vec, vec, full((1, d)), full((d, f)), full((d, f)), full((f, d))],
        out_specs=row,
        out_shape=jax.ShapeDtypeStruct((b, s, d), F32),
        compiler_params=_params("arbitrary", "arbitrary"),
        name="ffn_swiglu",
    )(x, shift, scale, gate, g, wg, wu, wd)


def _headnorm_pair(y, g, dim):
    lane = lax.broadcasted_iota(jnp.int32, (1, LANES), 1)
    left = lane < dim
    y2 = y * y
    ss_a = jnp.sum(jnp.where(left, y2, 0.0), axis=-1, keepdims=True)
    ss_b = jnp.sum(jnp.where(left, 0.0, y2), axis=-1, keepdims=True)
    inv = jnp.where(left, lax.rsqrt(ss_a * (1.0 / dim) + EPS), lax.rsqrt(ss_b * (1.0 / dim) + EPS))
    return y * inv * g


def _moba_proj_kernel(x_ref, sh_ref, sc_ref, g_ref, w_ref, wvt_ref, hg_ref, qk_ref, vt_ref, *, tn):
    h = _rms_mod(x_ref[0], g_ref[...], sh_ref[0], sc_ref[0]).astype(BF16)
    for n in range(w_ref.shape[1] // tn):
        res = _dot(h, w_ref[:, n * tn:(n + 1) * tn])
        for s in range(tn // LANES):
            sl = slice(n * tn + s * LANES, n * tn + (s + 1) * LANES)
            qk_ref[0, :, sl] = _headnorm_pair(res[:, s * LANES:(s + 1) * LANES], hg_ref[:, sl],
                                              MOBA_HEAD_DIM).astype(BF16)
    for n in range(wvt_ref.shape[0] // tn):
        rows = slice(n * tn, (n + 1) * tn)
        vt_ref[0, rows, :] = _dot_nt(wvt_ref[rows, :], h).astype(BF16)


def _moba_proj(x, shift, scale, g, w_qk, w_vt, head_gain, tm, tn):
    b, s, d = x.shape
    nqk = w_qk.shape[1]
    nv = w_vt.shape[0]
    vec = pl.BlockSpec((1, 1, d), lambda bi, i: (bi, 0, 0))
    full = lambda shp: pl.BlockSpec(shp, lambda bi, i: (0, 0))
    return pl.pallas_call(
        functools.partial(_moba_proj_kernel, tn=tn),
        grid=(b, s // tm),
        in_specs=[pl.BlockSpec((1, tm, d), lambda bi, i: (bi, i, 0)), vec, vec, full((1, d)),
                  full((d, nqk)), full((nv, d)), full((1, nqk))],
        out_specs=[pl.BlockSpec((1, tm, nqk), lambda bi, i: (bi, i, 0)),
                   pl.BlockSpec((1, nv, tm), lambda bi, i: (bi, 0, i))],
        out_shape=[jax.ShapeDtypeStruct((b, s, nqk), BF16), jax.ShapeDtypeStruct((b, nv, s), BF16)],
        compiler_params=_params("arbitrary", "arbitrary"),
        name="moba_qkv_proj",
    )(x, shift, scale, g, w_qk, w_vt, head_gain)


def _softmax_step_t(s, m, l, acc, vt, bounded):
    if bounded:
        p = jnp.exp2(s)
        return m, l + jnp.sum(p, axis=0, keepdims=True), acc + _dot(vt, p.astype(BF16))
    m_new = jnp.maximum(m, jnp.max(s, axis=0, keepdims=True))
    alpha = jnp.exp2(m - m_new)
    p = jnp.exp2(s - m_new)
    l_new = alpha * l + jnp.sum(p, axis=0, keepdims=True)
    acc_new = acc * alpha + _dot(vt, p.astype(BF16))
    return m_new, l_new, acc_new


def _moba_attn_kernel(qmin_ref, kmax_ref, q_ref, k_ref, vt_ref, bt_ref, pq_ref, pk_ref, o_ref,
                      km_scr, sel_scr, *, nb, t, npairs, bounded):
    blk = MOBA_BLOCK
    hd = MOBA_HEAD_DIM
    nsub = t // blk
    nh = 2 * npairs
    bi = pl.program_id(0)
    qi = pl.program_id(2)
    lane = lax.broadcasted_iota(jnp.int32, (1, LANES), 1)
    left = lane < hd

    @pl.when(qi == 0)
    def _():
        km_scr[...] = jnp.zeros_like(km_scr)
        for r in range(nb):
            kblk = k_ref[0, r * blk:(r + 1) * blk, :].astype(F32)
            km_scr[r:r + 1, :] = jnp.sum(kblk, axis=0, keepdims=True) * (1.0 / blk)

    q_heads = []
    for p in range(npairs):
        q2 = q_ref[0, :, p * LANES:(p + 1) * LANES]
        zero = jnp.zeros_like(q2)
        q_heads += [jnp.where(left, q2, zero), jnp.where(left, zero, q2)]

    col = lax.broadcasted_iota(jnp.int32, (1, t), 1)
    q_idx = qi * t + col
    q_sub = sum((col >= r * blk).astype(jnp.int32) for r in range(1, nsub))
    q_blk = qi * nsub + q_sub
    q_blk_start = q_blk * blk

    blk_row = lax.broadcasted_iota(jnp.int32, (nb, t), 0)
    for h in range(nh):
        km = km_scr[:, (h // 2) * LANES:(h // 2 + 1) * LANES]
        km_hi = km.astype(BF16)
        km_lo = (km - km_hi.astype(F32)).astype(BF16)
        gate = (_dot_nt(km_hi, q_heads[h]) + _dot_nt(km_lo, q_heads[h]))[:nb]
        gate = jnp.where(blk_row < q_blk, gate, -jnp.inf)
        sel = jnp.zeros((nb, t), F32)
        for _ in range(MOBA_TOPK):
            mx = jnp.max(gate, axis=0, keepdims=True)
            idx = jnp.min(jnp.where(gate == mx, blk_row, nb), axis=0, keepdims=True)
            pick = (blk_row == idx) & (mx > -jnp.inf)
            sel = jnp.where(pick, 1.0, sel)
            gate = jnp.where(pick, -jnp.inf, gate)
        sel_scr[h] = sel

    pq = pq_ref[0, 0]

    def chunk(j, carry, mode):
        carry = list(carry)
        start = pl.multiple_of(j * t, t)
        k2 = k_ref[0, pl.ds(start, t), :]
        vt = vt_ref[0, :, pl.ds(start, t)]
        s_heads = [_dot_nt(k2[:, (h // 2) * LANES:(h // 2 + 1) * LANES], q_heads[h]) for h in range(nh)]
        if mode != "far":
            dist = jnp.clip(pq - pk_ref[0, pl.ds(start, t), :], 0, BIAS_TABLE - 1)
        if mode == "diag":
            key_idx = start + lax.broadcasted_iota(jnp.int32, (blk, t), 0)
        for h in range(nh):
            m, l, acc = carry[h]
            s = s_heads[h]
            vt_h = vt[h * hd:(h + 1) * hd, :]
            chosen = [sel_scr[h, pl.ds(j * nsub + r, 1), :] > 0.0 for r in range(nsub)]
            if mode == "far" and bounded:
                c = bt_ref[h // 2, h % 2:h % 2 + 1, BIAS_TABLE - 1:BIAS_TABLE]
                for r in range(nsub):
                    p = jnp.exp2(s[r * blk:(r + 1) * blk])
                    w = jnp.where(chosen[r], jnp.exp2(c), 0.0)
                    l = l + jnp.sum(p, axis=0, keepdims=True) * w
                    acc = acc + _dot(vt_h[:, r * blk:(r + 1) * blk], p.astype(BF16)) * w
                carry[h] = (m, l, acc)
            elif mode == "far":
                pieces = [s[r * blk:(r + 1) * blk] for r in range(nsub)]
                c = bt_ref[h // 2, h % 2:h % 2 + 1, BIAS_TABLE - 1:BIAS_TABLE]
                tmax = [jnp.where(chosen[r], jnp.max(pieces[r], axis=0, keepdims=True) + c, NEG)
                        for r in range(nsub)]
                m_new = functools.reduce(jnp.maximum, tmax, m)
                alpha = jnp.exp2(m - m_new)
                ps = [jnp.exp2(pieces[r] - jnp.where(chosen[r], m_new - c, jnp.inf)) for r in range(nsub)]
                l_new = alpha * l + sum(jnp.sum(p, axis=0, keepdims=True) for p in ps)
                p = jnp.concatenate(ps, axis=0) if nsub > 1 else ps[0]
                carry[h] = (m_new, l_new, acc * alpha + _dot(vt_h, p.astype(BF16)))
            else:
                table = jnp.broadcast_to(bt_ref[h // 2, h % 2:h % 2 + 1, :], (t, LANES))
                bias = jnp.concatenate(
                    [jnp.take_along_axis(table, dist[:, g * LANES:(g + 1) * LANES], axis=1,
                                         mode="promise_in_bounds") for g in range(t // LANES)], axis=1)
                s = s + bias
                masked = []
                for r in range(nsub):
                    piece = s[r * blk:(r + 1) * blk]
                    if mode == "diag":
                        kidx = key_idx + r * blk
                        keep_own = jnp.where(kidx <= q_idx, jnp.where(kidx >= q_blk_start, piece, NEG), NEG)
                        masked.append(jnp.where(chosen[r], piece, keep_own))
                    else:
                        masked.append(jnp.where(chosen[r], piece, NEG))
                s = jnp.concatenate(masked, axis=0) if nsub > 1 else masked[0]
                carry[h] = _softmax_step_t(s, m, l, acc, vt_h, bounded)
        return tuple(carry)

    def past_chunk(j, carry):
        far = qmin_ref[bi, qi] - kmax_ref[bi, j] >= BIAS_TABLE - 1
        return lax.cond(far, lambda c: chunk(j, c, "far"), lambda c: chunk(j, c, "near"), carry)

    init = (jnp.full((1, t), NEG, F32), jnp.zeros((1, t), F32), jnp.zeros((hd, t), F32))
    carry = lax.fori_loop(0, qi, past_chunk, (init,) * nh)
    carry = chunk(qi, carry, "diag")
    out_t = jnp.concatenate([acc / l for (_, l, acc) in carry], axis=0)
    o_ref[0] = out_t.T.astype(BF16)


def _moba_attn(qk, vt, bias_table, positions, t, npairs, bounded):
    b, s, _ = qk.shape
    nb = s // MOBA_BLOCK
    ngroups = MOBA_HEADS * MOBA_HEAD_DIM // (LANES * npairs)
    w = npairs * LANES
    pos_tiles = positions.reshape(b, s // t, t)
    q_min = jnp.min(pos_tiles, axis=-1)
    k_max = jnp.max(pos_tiles, axis=-1)
    pos_row = positions.reshape(b, s // t, 1, t)
    pos_col = positions.reshape(b, s, 1)
    grid_spec = pltpu.PrefetchScalarGridSpec(
        num_scalar_prefetch=2,
        grid=(b, ngroups, s // t),
        in_specs=[pl.BlockSpec((1, t, w), lambda bi, g, qi, *_: (bi, qi, g)),
                  pl.BlockSpec((1, s, w), lambda bi, g, qi, *_: (bi, 0, ngroups + g)),
                  pl.BlockSpec((1, w, s), lambda bi, g, qi, *_: (bi, g, 0)),
                  pl.BlockSpec((npairs, 2, LANES), lambda bi, g, qi, *_: (g, 0, 0)),
                  pl.BlockSpec((1, 1, 1, t), lambda bi, g, qi, *_: (bi, qi, 0, 0)),
                  pl.BlockSpec((1, s, 1), lambda bi, g, qi, *_: (bi, 0, 0))],
        out_specs=pl.BlockSpec((1, t, w), lambda bi, g, qi, *_: (bi, qi, g)),
        scratch_shapes=[pltpu.VMEM((LANES, w), F32), pltpu.VMEM((2 * npairs, nb, t), F32)],
    )
    return pl.pallas_call(
        functools.partial(_moba_attn_kernel, nb=nb, t=t, npairs=npairs, bounded=bounded),
        grid_spec=grid_spec,
        out_shape=jax.ShapeDtypeStruct((b, s, MOBA_HEADS * MOBA_HEAD_DIM), BF16),
        compiler_params=_params("arbitrary", "arbitrary", "arbitrary"),
        name="moba_attention",
    )(q_min, k_max, qk, qk, vt, bias_table, pos_row, pos_col)


def _oproj_kernel(x_ref, a_ref, gt_ref, w_ref, o_ref):
    o_ref[0] = x_ref[0] + gt_ref[0] * _dot(a_ref[0], w_ref[...])


def _oproj(x, a, gate, w, tm):
    b, s, d = x.shape
    k = a.shape[2]
    return pl.pallas_call(
        _oproj_kernel,
        grid=(b, s // tm),
        in_specs=[pl.BlockSpec((1, tm, d), lambda bi, i: (bi, i, 0)),
                  pl.BlockSpec((1, tm, k), lambda bi, i: (bi, i, 0)),
                  pl.BlockSpec((1, 1, d), lambda bi, i: (bi, 0, 0)),
                  pl.BlockSpec((k, d), lambda bi, i: (0, 0))],
        out_specs=pl.BlockSpec((1, tm, d), lambda bi, i: (bi, i, 0)),
        out_shape=jax.ShapeDtypeStruct((b, s, d), F32),
        compiler_params=_params("arbitrary", "arbitrary"),
        name="attn_out_proj",
    )(x, a, gate, w)


def _rope_tab_kernel(pos_ref, inv_ref, c_ref, s_ref):
    half = MLA_ROPE // 2
    ang = pos_ref[0].astype(F32) * inv_ref[...]
    lane = lax.broadcasted_iota(jnp.int32, (1, LANES), 1)
    sn = jnp.sin(ang)
    c_ref[0] = jnp.cos(ang)
    s_ref[0] = jnp.where(lane < MLA_NOPE + half, -sn, sn)


def _rope_tables(pos_col, inv_lane, tm):
    b, s, _ = pos_col.shape
    out = pl.BlockSpec((1, tm, LANES), lambda bi, i: (bi, i, 0))
    shape = jax.ShapeDtypeStruct((b, s, LANES), F32)
    return pl.pallas_call(
        _rope_tab_kernel,
        grid=(b, s // tm),
        in_specs=[pl.BlockSpec((1, tm, 1), lambda bi, i: (bi, i, 0)),
                  pl.BlockSpec((1, LANES), lambda bi, i: (0, 0))],
        out_specs=[out, out],
        out_shape=[shape, shape],
        compiler_params=_params("arbitrary", "arbitrary"),
        name="rope_tables",
    )(pos_col, inv_lane)


def _mla_q_kernel(x_ref, sh_ref, sc_ref, g_ref, wdq_ref, qag_ref, wuq_ref, wuqp_ref, qg_ref, qgp_ref,
                  c_ref, s_ref, o_ref):
    h = _rms_mod(x_ref[0], g_ref[...], sh_ref[0], sc_ref[0]).astype(BF16)
    dq = _dot(h, wdq_ref[...])
    dqn = (dq * lax.rsqrt(jnp.mean(dq * dq, axis=-1, keepdims=True) + EPS) * qag_ref[...]).astype(BF16)
    uq = _dot(dqn, wuq_ref[...])
    up = _dot(dqn, wuqp_ref[...])
    cg = c_ref[0] * qg_ref[...]
    sg = s_ref[0] * qgp_ref[...]
    scale = MLA_QK ** -0.5 * LOG2E
    for hd in range(MLA_HEADS):
        sl = slice(hd * LANES, (hd + 1) * LANES)
        u = uq[:, sl]
        inv = lax.rsqrt(jnp.sum(u * u, axis=-1, keepdims=True) * (1.0 / MLA_QK) + EPS) * scale
        o_ref[0, :, sl] = ((u * cg + up[:, sl] * sg) * inv).astype(BF16)


def _mla_q(x, shift, scale, g, wdq, qag, wuq_p, wuq_partner, qg_p, qg_partner, cos, sin, tm):
    b, s, d = x.shape
    r = wdq.shape[1]
    nq = wuq_p.shape[1]
    vec = pl.BlockSpec((1, 1, d), lambda bi, i: (bi, 0, 0))
    tab = pl.BlockSpec((1, tm, LANES), lambda bi, i: (bi, i, 0))
    full = lambda shp: pl.BlockSpec(shp, lambda bi, i: (0, 0))
    return pl.pallas_call(
        _mla_q_kernel,
        grid=(b, s // tm),
        in_specs=[pl.BlockSpec((1, tm, d), lambda bi, i: (bi, i, 0)), vec, vec, full((1, d)),
                  full((d, r)), full((1, r)), full((r, nq)), full((r, nq)), full((1, LANES)), full((1, LANES)),
                  tab, tab],
        out_specs=pl.BlockSpec((1, tm, nq), lambda bi, i: (bi, i, 0)),
        out_shape=jax.ShapeDtypeStruct((b, s, nq), BF16),
        compiler_params=_params("arbitrary", "arbitrary"),
        name="mla_q_path",
    )(x, shift, scale, g, wdq, qag, wuq_p, wuq_partner, qg_p, qg_partner, cos, sin)


def _mla_kv_kernel(x_ref, sh_ref, sc_ref, g_ref, wdkv_ref, kvg_ref, wuk_ref, wuvt_ref, kg_ref, kgp_ref,
                   c_ref, s_ref, k_ref, vt_ref):
    hs = _rms_mod(x_ref[0], g_ref[...], sh_ref[0], sc_ref[0]).astype(BF16)
    ck = _dot(hs, wdkv_ref[...])
    ckv = ck[:, :MLA_KV_LORA]
    ckvn = (ckv * lax.rsqrt(jnp.mean(ckv * ckv, axis=-1, keepdims=True) + EPS) * kvg_ref[...]).astype(BF16)
    kn = _dot(ckvn, wuk_ref[...])
    vt_ref[0] = _dot_nt(wuvt_ref[...], ckvn).astype(BF16)
    lane = lax.broadcasted_iota(jnp.int32, (1, LANES), 1)
    on_rope = (lane >= MLA_NOPE) & (lane < MLA_QK)
    tail = ck[:, MLA_KV_LORA:MLA_KV_LORA + LANES]
    kpe = jnp.where(on_rope, pltpu.roll(tail, MLA_NOPE, axis=1), 0.0)
    partner = pltpu.roll(tail, MLA_ROPE, axis=1)
    cg = c_ref[0] * kg_ref[...]
    sg = s_ref[0] * kgp_ref[...]
    rot = kpe * cg + partner * sg
    ss_pe = jnp.sum(kpe * kpe, axis=-1, keepdims=True)
    for hd in range(MLA_HEADS):
        sl = slice(hd * LANES, (hd + 1) * LANES)
        nope = kn[:, sl]
        inv = lax.rsqrt((jnp.sum(nope * nope, axis=-1, keepdims=True) + ss_pe) * (1.0 / MLA_QK) + EPS)
        k_ref[0, :, sl] = ((nope * cg + rot) * inv).astype(BF16)


def _mla_kv(x, shift, scale, g, wdkv_p, kvg, wuk_p, wuv_t, kg_p, kg_partner, cos, sin, tm):
    b, s, d = x.shape
    r = wdkv_p.shape[1]
    nk = wuk_p.shape[1]
    nv = wuv_t.shape[0]
    vec = pl.BlockSpec((1, 1, d), lambda bi, i: (bi, 0, 0))
    tab = pl.BlockSpec((1, tm, LANES), lambda bi, i: (bi, i, 0))
    full = lambda shp: pl.BlockSpec(shp, lambda bi, i: (0, 0))
    return pl.pallas_call(
        _mla_kv_kernel,
        grid=(b, s // tm),
        in_specs=[pl.BlockSpec((1, tm, d), lambda bi, i: (bi, i, 0)), vec, vec, full((1, d)),
                  full((d, r)), full((1, MLA_KV_LORA)), full((MLA_KV_LORA, nk)), full((nv, MLA_KV_LORA)),
                  full((1, LANES)), full((1, LANES)), tab, tab],
        out_specs=[pl.BlockSpec((1, tm, nk), lambda bi, i: (bi, i, 0)),
                   pl.BlockSpec((1, nv, tm), lambda bi, i: (bi, 0, i))],
        out_shape=[jax.ShapeDtypeStruct((b, s, nk), BF16), jax.ShapeDtypeStruct((b, nv, s), BF16)],
        compiler_params=_params("arbitrary", "arbitrary"),
        name="mla_shared_kv",
    )(x, shift, scale, g, wdkv_p, kvg, wuk_p, wuv_t, kg_p, kg_partner, cos, sin)


def _mla_attn_kernel(q_ref, k_ref, vt_ref, o_ref, *, t, nh, bounded):
    qi = pl.program_id(2)
    q2 = q_ref[0]
    q_heads = [q2[:, h * LANES:(h + 1) * LANES] for h in range(nh)]

    def chunk(j, carry, masked):
        carry = list(carry)
        start = pl.multiple_of(j * t, t)
        k2 = k_ref[0, pl.ds(start, t), :]
        vt = vt_ref[0, :, pl.ds(start, t)]
        s_heads = [_dot_nt(k2[:, h * LANES:(h + 1) * LANES], q_heads[h]) for h in range(nh)]
        if masked:
            row = lax.broadcasted_iota(jnp.int32, (t, t), 0)
            col = lax.broadcasted_iota(jnp.int32, (t, t), 1)
            allowed = row <= col
        for h in range(nh):
            m, l, acc = carry[h]
            s = jnp.where(allowed, s_heads[h], NEG) if masked else s_heads[h]
            carry[h] = _softmax_step_t(s, m, l, acc, vt[h * MLA_V:(h + 1) * MLA_V, :], bounded)
        return tuple(carry)

    init = (jnp.full((1, t), NEG, F32), jnp.zeros((1, t), F32), jnp.zeros((MLA_V, t), F32))
    carry = lax.fori_loop(0, qi, lambda j, c: chunk(j, c, False), (init,) * nh)
    carry = chunk(qi, carry, True)
    out_t = jnp.concatenate([acc / l for (_, l, acc) in carry], axis=0)
    o_ref[0] = out_t.T.astype(BF16)


def _mla_attn(q, k, vt, t, nh, bounded):
    b, s, _ = q.shape
    return pl.pallas_call(
        functools.partial(_mla_attn_kernel, t=t, nh=nh, bounded=bounded),
        grid=(b, MLA_HEADS // nh, s // t),
        in_specs=[pl.BlockSpec((1, t, nh * LANES), lambda bi, g, qi: (bi, qi, g)),
                  pl.BlockSpec((1, s, nh * LANES), lambda bi, g, qi: (bi, 0, g)),
                  pl.BlockSpec((1, nh * MLA_V, s), lambda bi, g, qi: (bi, g, 0))],
        out_specs=pl.BlockSpec((1, t, nh * MLA_V), lambda bi, g, qi: (bi, qi, g)),
        out_shape=jax.ShapeDtypeStruct((b, s, MLA_HEADS * MLA_V), BF16),
        compiler_params=_params("arbitrary", "arbitrary", "arbitrary"),
        name="mla_attention",
    )(q, k, vt)


def _t5_bias_table(rel_bias):
    n = jnp.arange(BIAS_TABLE)
    max_exact = REL_BUCKETS // 2
    nf = jnp.maximum(n, 1).astype(F32)
    large = max_exact + (jnp.log(nf / max_exact) / math.log(REL_MAX_DIST / max_exact)
                         * (REL_BUCKETS - max_exact)).astype(jnp.int32)
    large = jnp.minimum(large, REL_BUCKETS - 1)
    bucket = jnp.where(n < max_exact, n, large)
    return rel_bias.T[:, bucket].reshape(MOBA_HEADS // 2, 2, BIAS_TABLE)


def _pad_heads(w, width):
    k = w.shape[0]
    w3 = w.reshape(k, -1, width)
    return jnp.pad(w3, ((0, 0), (0, 0), (0, LANES - width))).reshape(k, -1)


def _pad_lanes(g):
    return jnp.pad(g, (0, LANES - g.shape[0])).reshape(1, LANES)


def _swap_rope_halves(a):
    half = MLA_ROPE // 2
    return jnp.concatenate([jnp.zeros_like(a[..., :MLA_NOPE]), a[..., MLA_NOPE + half:],
                            a[..., MLA_NOPE:MLA_NOPE + half]], axis=-1)


def _scores_bounded(q_gain, k_gain, dim, scale, extra=0.0):
    bound = dim * jnp.max(jnp.abs(q_gain)) * jnp.max(jnp.abs(k_gain)) * scale * 1.01 + extra
    return bound <= SCORE_BOUND_MAX


def kernel(x, c, positions, ada_w, ada_b, norm_g, ffn_w_gate, ffn_w_up, ffn_w_down, rel_bias,
           moba_w_qkv, moba_q_g, moba_k_g, moba_w_o, kv_ada_w, kv_ada_b, kv_norm_g, w_dkv,
           kv_a_norm_g, w_uk, w_uv, mla_k_g, mla_w_dq, mla_q_a_norm_g, mla_w_uq, mla_q_g, mla_w_o):
    b, s, d = x.shape
    depth = ada_w.shape[0]
    n_moba = moba_w_qkv.shape[0]
    tm_ffn = min(512, s)
    tm_proj = min(512, s)

    c_pad = jnp.pad(c, ((0, SUBLANES - b), (0, 0)))
    mods = _mods(c_pad, ada_w, ada_b.reshape(depth, 1, -1), tn=1152)[:, :b]
    kv_mods = _mods(c_pad, kv_ada_w[None], kv_ada_b.reshape(1, 1, -1), tn=1024)[0, :b]

    def mod_vecs(m):
        return [v.reshape(b, 1, d) for v in jnp.split(m, m.shape[-1] // d, axis=-1)]

    pos_col = positions.reshape(b, s, 1)

    for layer in range(depth):
        sh1, sc1, g1, sh2, sc2, g2, sh3, sc3, g3 = mod_vecs(mods[layer])
        ng = norm_g[layer]

        def ffn(xin, idx, sh, sc, gt):
            return _ffn(xin, sh, sc, gt, ng[2 * idx:2 * idx + 1],
                        ffn_w_gate[layer, idx].astype(BF16), ffn_w_up[layer, idx].astype(BF16),
                        ffn_w_down[layer, idx].astype(BF16), tm_ffn, 256)

        x = ffn(x, 0, sh1, sc1, g1)

        if layer < n_moba:
            n_qk = 2 * MOBA_HEADS * MOBA_HEAD_DIM
            q_scale = MOBA_HEAD_DIM ** -0.5 * LOG2E
            head_gain = jnp.concatenate([jnp.tile(moba_q_g[layer] * q_scale, MOBA_HEADS),
                                         jnp.tile(moba_k_g[layer], MOBA_HEADS)]).reshape(1, -1)
            w_qkv = moba_w_qkv[layer]
            qk, vt = _moba_proj(x, sh2, sc2, ng[1:2], w_qkv[:, :n_qk].astype(BF16),
                                w_qkv[:, n_qk:].T.astype(BF16), head_gain, tm_proj, 512)
            bias_table = _t5_bias_table(rel_bias) * LOG2E
            moba = functools.partial(_moba_attn, qk, vt, bias_table, positions, MOBA_TILE, ATTN_HEADS // 2)
            ok = _scores_bounded(moba_q_g[layer], moba_k_g[layer], MOBA_HEAD_DIM, q_scale,
                                 jnp.max(jnp.abs(bias_table)))
            o = lax.cond(ok, lambda: moba(True), lambda: moba(False))
            x = _oproj(x, o, g2, moba_w_o[layer].astype(BF16), tm_proj)
        else:
            j = layer - n_moba
            if j == 0:
                half = MLA_ROPE // 2
                inv = ROPE_BASE ** (-jnp.arange(half, dtype=F32) / half)
                inv_lane = jnp.zeros((LANES,), F32).at[MLA_NOPE:MLA_QK].set(jnp.tile(inv, 2)).reshape(1, LANES)
                cos, sin = _rope_tables(pos_col, inv_lane, tm_proj)
                kv_sh, kv_sc = mod_vecs(kv_mods)
                w_kpe = w_dkv[:, MLA_KV_LORA:]
                w_kpe_swapped = jnp.concatenate([w_kpe[:, half:], w_kpe[:, :half]], axis=1)
                wdkv_p = jnp.pad(jnp.concatenate([w_dkv, w_kpe_swapped], axis=1),
                                 ((0, 0), (0, LANES - 2 * MLA_ROPE))).astype(BF16)
                shared_k, shared_v = _mla_kv(
                    x, kv_sh, kv_sc, kv_norm_g.reshape(1, d), wdkv_p, kv_a_norm_g.reshape(1, -1),
                    _pad_heads(w_uk, MLA_NOPE).astype(BF16), w_uv.T.astype(BF16), _pad_lanes(mla_k_g),
                    _pad_lanes(_swap_rope_halves(mla_k_g)), cos, sin, tm_proj)
            w_uq = mla_w_uq[j]
            w_uq_partner = _swap_rope_halves(w_uq.reshape(w_uq.shape[0], MLA_HEADS, MLA_QK)).reshape(w_uq.shape)
            q = _mla_q(x, sh2, sc2, ng[1:2], mla_w_dq[j].astype(BF16), mla_q_a_norm_g[j].reshape(1, -1),
                       _pad_heads(w_uq, MLA_QK).astype(BF16), _pad_heads(w_uq_partner, MLA_QK).astype(BF16),
                       _pad_lanes(mla_q_g[j]), _pad_lanes(_swap_rope_halves(mla_q_g[j])), cos, sin, tm_proj)
            mla = functools.partial(_mla_attn, q, shared_k, shared_v, MLA_TILE, ATTN_HEADS)
            ok = _scores_bounded(mla_q_g[j], mla_k_g, MLA_QK, MLA_QK ** -0.5 * LOG2E)
            o = lax.cond(ok, lambda: mla(True), lambda: mla(False))
            x = _oproj(x, o, g2, mla_w_o[j].astype(BF16), tm_proj)

        x = ffn(x, 1, sh3, sc3, g3)
    return x
```

```python
import functools
import math

import jax
import jax.numpy as jnp
from jax import lax
from jax.experimental import pallas as pl
from jax.experimental.pallas import tpu as pltpu

F32 = jnp.float32
BF16 = jnp.bfloat16

LANES = 128
SUBLANES = 8
VMEM_LIMIT_BYTES = 56 * 1024 * 1024

MOBA_HEADS = 16
MOBA_HEAD_DIM = 64
MOBA_BLOCK = 256
MOBA_TOPK = 3
REL_BUCKETS = 32
REL_MAX_DIST = 128
MLA_HEADS = 16
MLA_KV_LORA = 256
MLA_NOPE = 64
MLA_ROPE = 32
MLA_V = 64
MLA_QK = MLA_NOPE + MLA_ROPE
ROPE_BASE = 10000.0
EPS = 1e-6
NEG = -1e30

MOBA_TILE = 512
MLA_TILE = 512
ATTN_HEADS = 8
LOG2E = math.log2(math.e)
BIAS_TABLE = LANES
MAX_DIST = BIAS_TABLE - 2
MASKED_DIST = BIAS_TABLE - 1
SCORE_BOUND_MAX = 60.0


def _params(*sem):
    return pltpu.CompilerParams(dimension_semantics=sem, vmem_limit_bytes=VMEM_LIMIT_BYTES)


def _dot(a, b):
    return jnp.dot(a, b, preferred_element_type=F32)


def _dot_nt(a, b):
    return lax.dot_general(a, b, (((1,), (1,)), ((), ())), preferred_element_type=F32)


def _rms_mod(x, g, shift, scale):
    ms = jnp.mean(x * x, axis=-1, keepdims=True)
    y = x * lax.rsqrt(ms + EPS) * g
    return y * (1.0 + scale) + shift


def _mods_kernel(c_ref, w_ref, b_ref, o_ref):
    c = c_ref[...]
    ca = c * jax.nn.sigmoid(c)
    o_ref[0] = jnp.dot(ca, w_ref[0], precision=lax.Precision.HIGHEST,
                       preferred_element_type=F32) + b_ref[0]


def _mods(c_pad, w, b, tn):
    nl, d, n = w.shape
    return pl.pallas_call(
        _mods_kernel,
        grid=(nl, n // tn),
        in_specs=[pl.BlockSpec((SUBLANES, d), lambda l, j: (0, 0)),
                  pl.BlockSpec((1, d, tn), lambda l, j: (l, 0, j)),
                  pl.BlockSpec((1, 1, tn), lambda l, j: (l, 0, j))],
        out_specs=pl.BlockSpec((1, SUBLANES, tn), lambda l, j: (l, 0, j)),
        out_shape=jax.ShapeDtypeStruct((nl, SUBLANES, n), F32),
        compiler_params=_params("arbitrary", "arbitrary"),
        name="adaln_mods",
    )(c_pad, w, b)


def _ffn_kernel(x_ref, sh_ref, sc_ref, gt_ref, g_ref, wg_ref, wu_ref, wd_ref, o_ref, *, tf):
    x = x_ref[0]
    h = _rms_mod(x, g_ref[...], sh_ref[0], sc_ref[0]).astype(BF16)
    acc = None
    for f in range(wg_ref.shape[1] // tf):
        cols = slice(f * tf, (f + 1) * tf)
        g = _dot(h, wg_ref[:, cols])
        u = _dot(h, wu_ref[:, cols])
        a = (g * jax.nn.sigmoid(g) * u).astype(BF16)
        d = _dot(a, wd_ref[cols, :])
        acc = d if acc is None else acc + d
    o_ref[0] = x + (0.5 * gt_ref[0]) * acc


def _ffn(x, shift, scale, gate, g, wg, wu, wd, tm, tf):
    b, s, d = x.shape
    f = wg.shape[1]
    row = pl.BlockSpec((1, tm, d), lambda bi, i: (bi, i, 0))
    vec = pl.BlockSpec((1, 1, d), lambda bi, i: (bi, 0, 0))
    full = lambda shp: pl.BlockSpec(shp, lambda bi, i: (0, 0))
    return pl.pallas_call(
        functools.partial(_ffn_kernel, tf=tf),
        grid=(b, s // tm),
        in_specs=[row, vec, vec, vec, full((1, d)), full((d, f)), full((d, f)), full((f, d))],
        out_specs=row,
        out_shape=jax.ShapeDtypeStruct((b, s, d), F32),
        compiler_params=_params("arbitrary", "arbitrary"),
        name="ffn_swiglu",
    )(x, shift, scale, gate, g, wg, wu, wd)


def _headnorm_pair(y, g, dim):
    lane = lax.broadcasted_iota(jnp.int32, (1, LANES), 1)
    left = lane < dim
    y2 = y * y
    ss_a = jnp.sum(jnp.where(left, y2, 0.0), axis=-1, keepdims=True)
    ss_b = jnp.sum(jnp.where(left, 0.0, y2), axis=-1, keepdims=True)
    inv = jnp.where(left, lax.rsqrt(ss_a * (1.0 / dim) + EPS), lax.rsqrt(ss_b * (1.0 / dim) + EPS))
    return y * inv * g


def _moba_proj_kernel(x_ref, sh_ref, sc_ref, g_ref, w_ref, wvt_ref, hg_ref, qk_ref, vt_ref, *, tn):
    h = _rms_mod(x_ref[0], g_ref[...], sh_ref[0], sc_ref[0]).astype(BF16)
    for n in range(w_ref.shape[1] // tn):
        res = _dot(h, w_ref[:, n * tn:(n + 1) * tn])
        for s in range(tn // LANES):
            sl = slice(n * tn + s * LANES, n * tn + (s + 1) * LANES)
            qk_ref[0, :, sl] = _headnorm_pair(res[:, s * LANES:(s + 1) * LANES], hg_ref[:, sl],
                                              MOBA_HEAD_DIM).astype(BF16)
    for n in range(wvt_ref.shape[0] // tn):
        rows = slice(n * tn, (n + 1) * tn)
        vt_ref[0, rows, :] = _dot_nt(wvt_ref[rows, :], h).astype(BF16)


def _moba_proj(x, shift, scale, g, w_qk, w_vt, head_gain, tm, tn):
    b, s, d = x.shape
    nqk = w_qk.shape[1]
    nv = w_vt.shape[0]
    vec = pl.BlockSpec((1, 1, d), lambda bi, i: (bi, 0, 0))
    full = lambda shp: pl.BlockSpec(shp, lambda bi, i: (0, 0))
    return pl.pallas_call(
        functools.partial(_moba_proj_kernel, tn=tn),
        grid=(b, s // tm),
        in_specs=[pl.BlockSpec((1, tm, d), lambda bi, i: (bi, i, 0)), vec, vec, full((1, d)),
                  full((d, nqk)), full((nv, d)), full((1, nqk))],
        out_specs=[pl.BlockSpec((1, tm, nqk), lambda bi, i: (bi, i, 0)),
                   pl.BlockSpec((1, nv, tm), lambda bi, i: (bi, 0, i))],
        out_shape=[jax.ShapeDtypeStruct((b, s, nqk), BF16), jax.ShapeDtypeStruct((b, nv, s), BF16)],
        compiler_params=_params("arbitrary", "arbitrary"),
        name="moba_qkv_proj",
    )(x, shift, scale, g, w_qk, w_vt, head_gain)


def _softmax_step_t(s, m, l, acc, vt, bounded):
    if bounded:
        p = jnp.exp2(s)
        return m, l + jnp.sum(p, axis=0, keepdims=True), acc + _dot(vt, p.astype(BF16))
    m_new = jnp.maximum(m, jnp.max(s, axis=0, keepdims=True))
    alpha = jnp.exp2(m - m_new)
    p = jnp.exp2(s - m_new)
    l_new = alpha * l + jnp.sum(p, axis=0, keepdims=True)
    acc_new = acc * alpha + _dot(vt, p.astype(BF16))
    return m_new, l_new, acc_new


def _moba_attn_kernel(qmin_ref, kmax_ref, qgmin_ref, kbmax_ref, q_ref, k_ref, vt_ref, bt_ref, pq_ref, pk_ref, o_ref,
                      km_scr, sel_scr, *, nb, t, npairs, bounded):
    blk = MOBA_BLOCK
    hd = MOBA_HEAD_DIM
    nsub = t // blk
    nh = 2 * npairs
    bi = pl.program_id(0)
    qi = pl.program_id(2)
    lane = lax.broadcasted_iota(jnp.int32, (1, LANES), 1)
    left = lane < hd

    @pl.when(qi == 0)
    def _():
        km_scr[...] = jnp.zeros_like(km_scr)
        for r in range(nb):
            kblk = k_ref[0, r * blk:(r + 1) * blk, :].astype(F32)
            km_scr[r:r + 1, :] = jnp.sum(kblk, axis=0, keepdims=True) * (1.0 / blk)

    q_heads = []
    for p in range(npairs):
        q2 = q_ref[0, :, p * LANES:(p + 1) * LANES]
        zero = jnp.zeros_like(q2)
        q_heads += [jnp.where(left, q2, zero), jnp.where(left, zero, q2)]

    col = lax.broadcasted_iota(jnp.int32, (1, t), 1)
    q_idx = qi * t + col
    q_sub = sum((col >= r * blk).astype(jnp.int32) for r in range(1, nsub))
    q_blk = qi * nsub + q_sub
    q_blk_start = q_blk * blk

    blk_row = lax.broadcasted_iota(jnp.int32, (nb, t), 0)
    for h in range(nh):
        km = km_scr[:, (h // 2) * LANES:(h // 2 + 1) * LANES]
        km_hi = km.astype(BF16)
        km_lo = (km - km_hi.astype(F32)).astype(BF16)
        gate = (_dot_nt(km_hi, q_heads[h]) + _dot_nt(km_lo, q_heads[h]))[:nb]
        gate = jnp.where(blk_row < q_blk, gate, -jnp.inf)
        sel = jnp.zeros((nb, t), F32)
        for _ in range(MOBA_TOPK):
            mx = jnp.max(gate, axis=0, keepdims=True)
            idx = jnp.min(jnp.where(gate == mx, blk_row, nb), axis=0, keepdims=True)
            pick = (blk_row == idx) & (mx > -jnp.inf)
            sel = jnp.where(pick, 1.0, sel)
            gate = jnp.where(pick, -jnp.inf, gate)
        sel_scr[h] = sel

    pq = pq_ref[0, 0]

    def gather_bias(h, dist):
        table = jnp.broadcast_to(bt_ref[h // 2, h % 2:h % 2 + 1, :], (dist.shape[0], LANES))
        return jnp.concatenate(
            [jnp.take_along_axis(table, dist[:, g * LANES:(g + 1) * LANES], axis=1, mode="promise_in_bounds")
             for g in range(dist.shape[1] // LANES)], axis=1)

    def far_bias(h):
        return bt_ref[h // 2, h % 2:h % 2 + 1, MAX_DIST:MAX_DIST + 1]

    def chosen_rows(h, j, lanes=slice(None)):
        return [sel_scr[h, pl.ds(j * nsub + r, 1), lanes] > 0.0 for r in range(nsub)]

    def own_chunk_bounded(carry):
        assert nsub == 2
        carry = list(carry)
        start = pl.multiple_of(qi * t, t)
        k2 = k_ref[0, pl.ds(start, t), :]
        vt = vt_ref[0, :, pl.ds(start, t)]
        pk0 = pk_ref[0, pl.ds(start, blk), :]
        pk1 = pk_ref[0, pl.ds(start + blk, blk), :]
        causal = (lax.broadcasted_iota(jnp.int32, (blk, blk), 0) <= lax.broadcasted_iota(jnp.int32, (blk, blk), 1))
        d00 = jnp.where(causal, jnp.clip(pq[:, :blk] - pk0, 0, MAX_DIST), MASKED_DIST)
        d01 = jnp.clip(pq[:, blk:] - pk0, 0, MAX_DIST)
        d11 = jnp.where(causal, jnp.clip(pq[:, blk:] - pk1, 0, MAX_DIST), MASKED_DIST)
        quads = []
        for h in range(nh):
            kh = k2[:, (h // 2) * LANES:(h // 2 + 1) * LANES]
            qh = q_heads[h]
            quads.append((_dot_nt(kh[:blk], qh[:blk]), _dot_nt(kh[:blk], qh[blk:]), _dot_nt(kh[blk:], qh[blk:])))
        for h in range(nh):
            m, l, acc = carry[h]
            s00, s01, s11 = quads[h]
            e00 = jnp.exp2(s00 + gather_bias(h, d00))
            e01 = jnp.exp2(s01 + gather_bias(h, d01))
            e11 = jnp.exp2(s11 + gather_bias(h, d11))
            w01 = jnp.where(chosen_rows(h, qi, slice(blk, t))[0], 1.0, 0.0)
            vt0 = vt[h * hd:(h + 1) * hd, :blk]
            vt1 = vt[h * hd:(h + 1) * hd, blk:]
            l_add = jnp.concatenate(
                [jnp.sum(e00, axis=0, keepdims=True),
                 jnp.sum(e01, axis=0, keepdims=True) * w01 + jnp.sum(e11, axis=0, keepdims=True)], axis=1)
            acc_add = jnp.concatenate(
                [_dot(vt0, e00.astype(BF16)),
                 _dot(vt0, e01.astype(BF16)) * w01 + _dot(vt1, e11.astype(BF16))], axis=1)
            carry[h] = (m, l + l_add, acc + acc_add)
        return tuple(carry)

    def chunk(j, carry, mode):
        carry = list(carry)
        start = pl.multiple_of(j * t, t)
        k2 = k_ref[0, pl.ds(start, t), :]
        vt = vt_ref[0, :, pl.ds(start, t)]
        s_heads = [_dot_nt(k2[:, (h // 2) * LANES:(h // 2 + 1) * LANES], q_heads[h]) for h in range(nh)]
        if mode in ("near", "diag"):
            dist = jnp.clip(pq - pk_ref[0, pl.ds(start, t), :], 0, MAX_DIST)
        if mode == "corner":
            dist_c = jnp.clip(pq[:, :LANES] - pk_ref[0, pl.ds(start + (nsub - 1) * blk, blk), :], 0, MAX_DIST)
        if mode == "diag":
            key_idx = start + lax.broadcasted_iota(jnp.int32, (blk, t), 0)
        for h in range(nh):
            m, l, acc = carry[h]
            s = s_heads[h]
            vt_h = vt[h * hd:(h + 1) * hd, :]
            chosen = chosen_rows(h, j)
            if mode in ("far", "corner") and bounded:
                two_c = jnp.exp2(far_bias(h))
                for r in range(nsub):
                    piece = s[r * blk:(r + 1) * blk]
                    w = jnp.where(chosen[r], two_c, 0.0)
                    if mode == "corner" and r == nsub - 1:
                        piece = jnp.concatenate([piece[:, :LANES] + gather_bias(h, dist_c), piece[:, LANES:]], axis=1)
                        w = jnp.where(col < LANES, jnp.where(chosen[r], 1.0, 0.0), w)
                    p = jnp.exp2(piece)
                    l = l + jnp.sum(p, axis=0, keepdims=True) * w
                    acc = acc + _dot(vt_h[:, r * blk:(r + 1) * blk], p.astype(BF16)) * w
                carry[h] = (m, l, acc)
            elif mode == "far":
                pieces = [s[r * blk:(r + 1) * blk] for r in range(nsub)]
                c = far_bias(h)
                tmax = [jnp.where(chosen[r], jnp.max(pieces[r], axis=0, keepdims=True) + c, NEG)
                        for r in range(nsub)]
                m_new = functools.reduce(jnp.maximum, tmax, m)
                alpha = jnp.exp2(m - m_new)
                ps = [jnp.exp2(pieces[r] - jnp.where(chosen[r], m_new - c, jnp.inf)) for r in range(nsub)]
                l_new = alpha * l + sum(jnp.sum(p, axis=0, keepdims=True) for p in ps)
                p = jnp.concatenate(ps, axis=0) if nsub > 1 else ps[0]
                carry[h] = (m_new, l_new, acc * alpha + _dot(vt_h, p.astype(BF16)))
            else:
                s = s + gather_bias(h, dist)
                masked = []
                for r in range(nsub):
                    piece = s[r * blk:(r + 1) * blk]
                    if mode == "diag":
                        kidx = key_idx + r * blk
                        keep_own = jnp.where(kidx <= q_idx, jnp.where(kidx >= q_blk_start, piece, NEG), NEG)
                        masked.append(jnp.where(chosen[r], piece, keep_own))
                    else:
                        masked.append(jnp.where(chosen[r], piece, NEG))
                s = jnp.concatenate(masked, axis=0) if nsub > 1 else masked[0]
                carry[h] = _softmax_step_t(s, m, l, acc, vt_h, bounded)
        return tuple(carry)

    def past_chunk(j, carry):
        q_lo = qmin_ref[bi, qi]
        far = q_lo - kmax_ref[bi, j] >= MAX_DIST
        if not bounded:
            return lax.cond(far, lambda c: chunk(j, c, "far"), lambda c: chunk(j, c, "near"), carry)
        groups = t // LANES
        q_rest = functools.reduce(jnp.minimum, [qgmin_ref[bi, qi * groups + g] for g in range(1, groups)])
        k_first = functools.reduce(jnp.maximum, [kbmax_ref[bi, j * nsub + r] for r in range(nsub - 1)])
        corner = (q_lo - k_first >= MAX_DIST) & (q_rest - kbmax_ref[bi, j * nsub + nsub - 1] >= MAX_DIST)
        return lax.cond(far, lambda c: chunk(j, c, "far"),
                        lambda c: lax.cond(corner, lambda c2: chunk(j, c2, "corner"),
                                           lambda c2: chunk(j, c2, "near"), c), carry)

    init = (jnp.full((1, t), NEG, F32), jnp.zeros((1, t), F32), jnp.zeros((hd, t), F32))
    carry = lax.fori_loop(0, qi, past_chunk, (init,) * nh)
    carry = own_chunk_bounded(carry) if bounded else chunk(qi, carry, "diag")
    out_t = jnp.concatenate([acc / l for (_, l, acc) in carry], axis=0)
    o_ref[0] = out_t.T.astype(BF16)


def _moba_attn(qk, vt, bias_table, positions, t, npairs, bounded):
    b, s, _ = qk.shape
    nb = s // MOBA_BLOCK
    ngroups = MOBA_HEADS * MOBA_HEAD_DIM // (LANES * npairs)
    w = npairs * LANES
    pos_tiles = positions.reshape(b, s // t, t)
    q_min = jnp.min(pos_tiles, axis=-1)
    k_max = jnp.max(pos_tiles, axis=-1)
    qg_min = jnp.min(positions.reshape(b, s // LANES, LANES), axis=-1)
    kb_max = jnp.max(positions.reshape(b, nb, MOBA_BLOCK), axis=-1)
    pos_row = positions.reshape(b, s // t, 1, t)
    pos_col = positions.reshape(b, s, 1)
    grid_spec = pltpu.PrefetchScalarGridSpec(
        num_scalar_prefetch=4,
        grid=(b, ngroups, s // t),
        in_specs=[pl.BlockSpec((1, t, w), lambda bi, g, qi, *_: (bi, qi, g)),
                  pl.BlockSpec((1, s, w), lambda bi, g, qi, *_: (bi, 0, ngroups + g)),
                  pl.BlockSpec((1, w, s), lambda bi, g, qi, *_: (bi, g, 0)),
                  pl.BlockSpec((npairs, 2, LANES), lambda bi, g, qi, *_: (g, 0, 0)),
                  pl.BlockSpec((1, 1, 1, t), lambda bi, g, qi, *_: (bi, qi, 0, 0)),
                  pl.BlockSpec((1, s, 1), lambda bi, g, qi, *_: (bi, 0, 0))],
        out_specs=pl.BlockSpec((1, t, w), lambda bi, g, qi, *_: (bi, qi, g)),
        scratch_shapes=[pltpu.VMEM((LANES, w), F32), pltpu.VMEM((2 * npairs, nb, t), F32)],
    )
    return pl.pallas_call(
        functools.partial(_moba_attn_kernel, nb=nb, t=t, npairs=npairs, bounded=bounded),
        grid_spec=grid_spec,
        out_shape=jax.ShapeDtypeStruct((b, s, MOBA_HEADS * MOBA_HEAD_DIM), BF16),
        compiler_params=_params("arbitrary", "arbitrary", "arbitrary"),
        name="moba_attention",
    )(q_min, k_max, qg_min, kb_max, qk, qk, vt, bias_table, pos_row, pos_col)


def _oproj_kernel(x_ref, a_ref, gt_ref, w_ref, o_ref):
    o_ref[0] = x_ref[0] + gt_ref[0] * _dot(a_ref[0], w_ref[...])


def _oproj(x, a, gate, w, tm):
    b, s, d = x.shape
    k = a.shape[2]
    return pl.pallas_call(
        _oproj_kernel,
        grid=(b, s // tm),
        in_specs=[pl.BlockSpec((1, tm, d), lambda bi, i: (bi, i, 0)),
                  pl.BlockSpec((1, tm, k), lambda bi, i: (bi, i, 0)),
                  pl.BlockSpec((1, 1, d), lambda bi, i: (bi, 0, 0)),
                  pl.BlockSpec((k, d), lambda bi, i: (0, 0))],
        out_specs=pl.BlockSpec((1, tm, d), lambda bi, i: (bi, i, 0)),
        out_shape=jax.ShapeDtypeStruct((b, s, d), F32),
        compiler_params=_params("arbitrary", "arbitrary"),
        name="attn_out_proj",
    )(x, a, gate, w)


def _rope_tab_kernel(pos_ref, inv_ref, c_ref, s_ref):
    half = MLA_ROPE // 2
    ang = pos_ref[0].astype(F32) * inv_ref[...]
    lane = lax.broadcasted_iota(jnp.int32, (1, LANES), 1)
    sn = jnp.sin(ang)
    c_ref[0] = jnp.cos(ang)
    s_ref[0] = jnp.where(lane < MLA_NOPE + half, -sn, sn)


def _rope_tables(pos_col, inv_lane, tm):
    b, s, _ = pos_col.shape
    out = pl.BlockSpec((1, tm, LANES), lambda bi, i: (bi, i, 0))
    shape = jax.ShapeDtypeStruct((b, s, LANES), F32)
    return pl.pallas_call(
        _rope_tab_kernel,
        grid=(b, s // tm),
        in_specs=[pl.BlockSpec((1, tm, 1), lambda bi, i: (bi, i, 0)),
                  pl.BlockSpec((1, LANES), lambda bi, i: (0, 0))],
        out_specs=[out, out],
        out_shape=[shape, shape],
        compiler_params=_params("arbitrary", "arbitrary"),
        name="rope_tables",
    )(pos_col, inv_lane)


def _mla_q_kernel(x_ref, sh_ref, sc_ref, g_ref, wdq_ref, qag_ref, wuq_ref, wuqp_ref, qg_ref, qgp_ref,
                  c_ref, s_ref, o_ref):
    h = _rms_mod(x_ref[0], g_ref[...], sh_ref[0], sc_ref[0]).astype(BF16)
    dq = _dot(h, wdq_ref[...])
    dqn = (dq * lax.rsqrt(jnp.mean(dq * dq, axis=-1, keepdims=True) + EPS) * qag_ref[...]).astype(BF16)
    uq = _dot(dqn, wuq_ref[...])
    up = _dot(dqn, wuqp_ref[...])
    cg = c_ref[0] * qg_ref[...]
    sg = s_ref[0] * qgp_ref[...]
    scale = MLA_QK ** -0.5 * LOG2E
    for hd in range(MLA_HEADS):
        sl = slice(hd * LANES, (hd + 1) * LANES)
        u = uq[:, sl]
        inv = lax.rsqrt(jnp.sum(u * u, axis=-1, keepdims=True) * (1.0 / MLA_QK) + EPS) * scale
        o_ref[0, :, sl] = ((u * cg + up[:, sl] * sg) * inv).astype(BF16)


def _mla_q(x, shift, scale, g, wdq, qag, wuq_p, wuq_partner, qg_p, qg_partner, cos, sin, tm):
    b, s, d = x.shape
    r = wdq.shape[1]
    nq = wuq_p.shape[1]
    vec = pl.BlockSpec((1, 1, d), lambda bi, i: (bi, 0, 0))
    tab = pl.BlockSpec((1, tm, LANES), lambda bi, i: (bi, i, 0))
    full = lambda shp: pl.BlockSpec(shp, lambda bi, i: (0, 0))
    return pl.pallas_call(
        _mla_q_kernel,
        grid=(b, s // tm),
        in_specs=[pl.BlockSpec((1, tm, d), lambda bi, i: (bi, i, 0)), vec, vec, full((1, d)),
                  full((d, r)), full((1, r)), full((r, nq)), full((r, nq)), full((1, LANES)), full((1, LANES)),
                  tab, tab],
        out_specs=pl.BlockSpec((1, tm, nq), lambda bi, i: (bi, i, 0)),
        out_shape=jax.ShapeDtypeStruct((b, s, nq), BF16),
        compiler_params=_params("arbitrary", "arbitrary"),
        name="mla_q_path",
    )(x, shift, scale, g, wdq, qag, wuq_p, wuq_partner, qg_p, qg_partner, cos, sin)


def _mla_kv_kernel(x_ref, sh_ref, sc_ref, g_ref, wdkv_ref, kvg_ref, wuk_ref, wuvt_ref, kg_ref, kgp_ref,
                   c_ref, s_ref, k_ref, vt_ref):
    hs = _rms_mod(x_ref[0], g_ref[...], sh_ref[0], sc_ref[0]).astype(BF16)
    ck = _dot(hs, wdkv_ref[...])
    ckv = ck[:, :MLA_KV_LORA]
    ckvn = (ckv * lax.rsqrt(jnp.mean(ckv * ckv, axis=-1, keepdims=True) + EPS) * kvg_ref[...]).astype(BF16)
    kn = _dot(ckvn, wuk_ref[...])
    vt_ref[0] = _dot_nt(wuvt_ref[...], ckvn).astype(BF16)
    lane = lax.broadcasted_iota(jnp.int32, (1, LANES), 1)
    on_rope = (lane >= MLA_NOPE) & (lane < MLA_QK)
    tail = ck[:, MLA_KV_LORA:MLA_KV_LORA + LANES]
    kpe = jnp.where(on_rope, pltpu.roll(tail, MLA_NOPE, axis=1), 0.0)
    partner = pltpu.roll(tail, MLA_ROPE, axis=1)
    cg = c_ref[0] * kg_ref[...]
    sg = s_ref[0] * kgp_ref[...]
    rot = kpe * cg + partner * sg
    ss_pe = jnp.sum(kpe * kpe, axis=-1, keepdims=True)
    for hd in range(MLA_HEADS):
        sl = slice(hd * LANES, (hd + 1) * LANES)
        nope = kn[:, sl]
        inv = lax.rsqrt((jnp.sum(nope * nope, axis=-1, keepdims=True) + ss_pe) * (1.0 / MLA_QK) + EPS)
        k_ref[0, :, sl] = ((nope * cg + rot) * inv).astype(BF16)


def _mla_kv(x, shift, scale, g, wdkv_p, kvg, wuk_p, wuv_t, kg_p, kg_partner, cos, sin, tm):
    b, s, d = x.shape
    r = wdkv_p.shape[1]
    nk = wuk_p.shape[1]
    nv = wuv_t.shape[0]
    vec = pl.BlockSpec((1, 1, d), lambda bi, i: (bi, 0, 0))
    tab = pl.BlockSpec((1, tm, LANES), lambda bi, i: (bi, i, 0))
    full = lambda shp: pl.BlockSpec(shp, lambda bi, i: (0, 0))
    return pl.pallas_call(
        _mla_kv_kernel,
        grid=(b, s // tm),
        in_specs=[pl.BlockSpec((1, tm, d), lambda bi, i: (bi, i, 0)), vec, vec, full((1, d)),
                  full((d, r)), full((1, MLA_KV_LORA)), full((MLA_KV_LORA, nk)), full((nv, MLA_KV_LORA)),
                  full((1, LANES)), full((1, LANES)), tab, tab],
        out_specs=[pl.BlockSpec((1, tm, nk), lambda bi, i: (bi, i, 0)),
                   pl.BlockSpec((1, nv, tm), lambda bi, i: (bi, 0, i))],
        out_shape=[jax.ShapeDtypeStruct((b, s, nk), BF16), jax.ShapeDtypeStruct((b, nv, s), BF16)],
        compiler_params=_params("arbitrary", "arbitrary"),
        name="mla_shared_kv",
    )(x, shift, scale, g, wdkv_p, kvg, wuk_p, wuv_t, kg_p, kg_partner, cos, sin)


def _mla_attn_kernel(q_ref, k_ref, vt_ref, o_ref, *, t, nh, bounded):
    qi = pl.program_id(2)
    q2 = q_ref[0]
    q_heads = [q2[:, h * LANES:(h + 1) * LANES] for h in range(nh)]

    def chunk(j, carry, masked):
        carry = list(carry)
        start = pl.multiple_of(j * t, t)
        k2 = k_ref[0, pl.ds(start, t), :]
        vt = vt_ref[0, :, pl.ds(start, t)]
        s_heads = [_dot_nt(k2[:, h * LANES:(h + 1) * LANES], q_heads[h]) for h in range(nh)]
        if masked:
            row = lax.broadcasted_iota(jnp.int32, (t, t), 0)
            col = lax.broadcasted_iota(jnp.int32, (t, t), 1)
            allowed = row <= col
        for h in range(nh):
            m, l, acc = carry[h]
            s = jnp.where(allowed, s_heads[h], NEG) if masked else s_heads[h]
            carry[h] = _softmax_step_t(s, m, l, acc, vt[h * MLA_V:(h + 1) * MLA_V, :], bounded)
        return tuple(carry)

    init = (jnp.full((1, t), NEG, F32), jnp.zeros((1, t), F32), jnp.zeros((MLA_V, t), F32))
    carry = lax.fori_loop(0, qi, lambda j, c: chunk(j, c, False), (init,) * nh)
    carry = chunk(qi, carry, True)
    out_t = jnp.concatenate([acc / l for (_, l, acc) in carry], axis=0)
    o_ref[0] = out_t.T.astype(BF16)


def _mla_attn(q, k, vt, t, nh, bounded):
    b, s, _ = q.shape
    return pl.pallas_call(
        functools.partial(_mla_attn_kernel, t=t, nh=nh, bounded=bounded),
        grid=(b, MLA_HEADS // nh, s // t),
        in_specs=[pl.BlockSpec((1, t, nh * LANES), lambda bi, g, qi: (bi, qi, g)),
                  pl.BlockSpec((1, s, nh * LANES), lambda bi, g, qi: (bi, 0, g)),
                  pl.BlockSpec((1, nh * MLA_V, s), lambda bi, g, qi: (bi, g, 0))],
        out_specs=pl.BlockSpec((1, t, nh * MLA_V), lambda bi, g, qi: (bi, qi, g)),
        out_shape=jax.ShapeDtypeStruct((b, s, MLA_HEADS * MLA_V), BF16),
        compiler_params=_params("arbitrary", "arbitrary", "arbitrary"),
        name="mla_attention",
    )(q, k, vt)


def _t5_bias_table(rel_bias):
    n = jnp.arange(MASKED_DIST)
    max_exact = REL_BUCKETS // 2
    nf = jnp.maximum(n, 1).astype(F32)
    large = max_exact + (jnp.log(nf / max_exact) / math.log(REL_MAX_DIST / max_exact)
                         * (REL_BUCKETS - max_exact)).astype(jnp.int32)
    large = jnp.minimum(large, REL_BUCKETS - 1)
    bucket = jnp.where(n < max_exact, n, large)
    table = jnp.concatenate([rel_bias.T[:, bucket] * LOG2E, jnp.full((MOBA_HEADS, 1), NEG, F32)], axis=1)
    return table.reshape(MOBA_HEADS // 2, 2, BIAS_TABLE)


def _pad_heads(w, width):
    k = w.shape[0]
    w3 = w.reshape(k, -1, width)
    return jnp.pad(w3, ((0, 0), (0, 0), (0, LANES - width))).reshape(k, -1)


def _pad_lanes(g):
    return jnp.pad(g, (0, LANES - g.shape[0])).reshape(1, LANES)


def _swap_rope_halves(a):
    half = MLA_ROPE // 2
    return jnp.concatenate([jnp.zeros_like(a[..., :MLA_NOPE]), a[..., MLA_NOPE + half:],
                            a[..., MLA_NOPE:MLA_NOPE + half]], axis=-1)


def _scores_bounded(q_gain, k_gain, dim, scale, extra=0.0):
    bound = dim * jnp.max(jnp.abs(q_gain)) * jnp.max(jnp.abs(k_gain)) * scale * 1.01 + extra
    return bound <= SCORE_BOUND_MAX


def kernel(x, c, positions, ada_w, ada_b, norm_g, ffn_w_gate, ffn_w_up, ffn_w_down, rel_bias,
           moba_w_qkv, moba_q_g, moba_k_g, moba_w_o, kv_ada_w, kv_ada_b, kv_norm_g, w_dkv,
           kv_a_norm_g, w_uk, w_uv, mla_k_g, mla_w_dq, mla_q_a_norm_g, mla_w_uq, mla_q_g, mla_w_o):
    b, s, d = x.shape
    depth = ada_w.shape[0]
    n_moba = moba_w_qkv.shape[0]
    tm_ffn = min(512, s)
    tm_proj = min(512, s)

    c_pad = jnp.pad(c, ((0, SUBLANES - b), (0, 0)))
    mods = _mods(c_pad, ada_w, ada_b.reshape(depth, 1, -1), tn=1152)[:, :b]
    kv_mods = _mods(c_pad, kv_ada_w[None], kv_ada_b.reshape(1, 1, -1), tn=1024)[0, :b]

    def mod_vecs(m):
        return [v.reshape(b, 1, d) for v in jnp.split(m, m.shape[-1] // d, axis=-1)]

    pos_col = positions.reshape(b, s, 1)

    for layer in range(depth):
        sh1, sc1, g1, sh2, sc2, g2, sh3, sc3, g3 = mod_vecs(mods[layer])
        ng = norm_g[layer]

        def ffn(xin, idx, sh, sc, gt):
            return _ffn(xin, sh, sc, gt, ng[2 * idx:2 * idx + 1],
                        ffn_w_gate[layer, idx].astype(BF16), ffn_w_up[layer, idx].astype(BF16),
                        ffn_w_down[layer, idx].astype(BF16), tm_ffn, 256)

        x = ffn(x, 0, sh1, sc1, g1)

        if layer < n_moba:
            n_qk = 2 * MOBA_HEADS * MOBA_HEAD_DIM
            q_scale = MOBA_HEAD_DIM ** -0.5 * LOG2E
            head_gain = jnp.concatenate([jnp.tile(moba_q_g[layer] * q_scale, MOBA_HEADS),
                                         jnp.tile(moba_k_g[layer], MOBA_HEADS)]).reshape(1, -1)
            w_qkv = moba_w_qkv[layer]
            qk, vt = _moba_proj(x, sh2, sc2, ng[1:2], w_qkv[:, :n_qk].astype(BF16),
                                w_qkv[:, n_qk:].T.astype(BF16), head_gain, tm_proj, 512)
            bias_table = _t5_bias_table(rel_bias)
            moba = functools.partial(_moba_attn, qk, vt, bias_table, positions, MOBA_TILE, ATTN_HEADS // 2)
            ok = _scores_bounded(moba_q_g[layer], moba_k_g[layer], MOBA_HEAD_DIM, q_scale,
                                 jnp.max(jnp.abs(rel_bias)) * LOG2E)
            o = lax.cond(ok, lambda: moba(True), lambda: moba(False))
            x = _oproj(x, o, g2, moba_w_o[layer].astype(BF16), tm_proj)
        else:
            j = layer - n_moba
            if j == 0:
                half = MLA_ROPE // 2
                inv = ROPE_BASE ** (-jnp.arange(half, dtype=F32) / half)
                inv_lane = jnp.zeros((LANES,), F32).at[MLA_NOPE:MLA_QK].set(jnp.tile(inv, 2)).reshape(1, LANES)
                cos, sin = _rope_tables(pos_col, inv_lane, tm_proj)
                kv_sh, kv_sc = mod_vecs(kv_mods)
                w_kpe = w_dkv[:, MLA_KV_LORA:]
                w_kpe_swapped = jnp.concatenate([w_kpe[:, half:], w_kpe[:, :half]], axis=1)
                wdkv_p = jnp.pad(jnp.concatenate([w_dkv, w_kpe_swapped], axis=1),
                                 ((0, 0), (0, LANES - 2 * MLA_ROPE))).astype(BF16)
                shared_k, shared_v = _mla_kv(
                    x, kv_sh, kv_sc, kv_norm_g.reshape(1, d), wdkv_p, kv_a_norm_g.reshape(1, -1),
                    _pad_heads(w_uk, MLA_NOPE).astype(BF16), w_uv.T.astype(BF16), _pad_lanes(mla_k_g),
                    _pad_lanes(_swap_rope_halves(mla_k_g)), cos, sin, tm_proj)
            w_uq = mla_w_uq[j]
            w_uq_partner = _swap_rope_halves(w_uq.reshape(w_uq.shape[0], MLA_HEADS, MLA_QK)).reshape(w_uq.shape)
            q = _mla_q(x, sh2, sc2, ng[1:2], mla_w_dq[j].astype(BF16), mla_q_a_norm_g[j].reshape(1, -1),
                       _pad_heads(w_uq, MLA_QK).astype(BF16), _pad_heads(w_uq_partner, MLA_QK).astype(BF16),
                       _pad_lanes(mla_q_g[j]), _pad_lanes(_swap_rope_halves(mla_q_g[j])), cos, sin, tm_proj)
            mla = functools.partial(_mla_attn, q, shared_k, shared_v, MLA_TILE, ATTN_HEADS)
            ok = _scores_bounded(mla_q_g[j], mla_k_g, MLA_QK, MLA_QK ** -0.5 * LOG2E)
            o = lax.cond(ok, lambda: mla(True), lambda: mla(False))
            x = _oproj(x, o, g2, mla_w_o[j].astype(BF16), tm_proj)

        x = ffn(x, 1, sh3, sc3, g3)
    return x
```

```python
import functools
import math

import jax
import jax.numpy as jnp
from jax import lax
from jax.experimental import pallas as pl
from jax.experimental.pallas import tpu as pltpu

F32 = jnp.float32
BF16 = jnp.bfloat16

LANES = 128
SUBLANES = 8
VMEM_LIMIT_BYTES = 56 * 1024 * 1024

MOBA_HEADS = 16
MOBA_HEAD_DIM = 64
MOBA_BLOCK = 256
MOBA_TOPK = 3
REL_BUCKETS = 32
REL_MAX_DIST = 128
MLA_HEADS = 16
MLA_KV_LORA = 256
MLA_NOPE = 64
MLA_ROPE = 32
MLA_V = 64
MLA_QK = MLA_NOPE + MLA_ROPE
ROPE_BASE = 10000.0
EPS = 1e-6
NEG = -1e30

MOBA_TILE = 512
MLA_TILE = 512
ATTN_HEADS = 8
LOG2E = math.log2(math.e)
BIAS_TABLE = LANES
MAX_DIST = BIAS_TABLE - 2
MASKED_DIST = BIAS_TABLE - 1
SCORE_BOUND_MAX = 60.0


def _params(*sem):
    return pltpu.CompilerParams(dimension_semantics=sem, vmem_limit_bytes=VMEM_LIMIT_BYTES)


def _dot(a, b):
    return jnp.dot(a, b, preferred_element_type=F32)


def _dot_nt(a, b):
    return lax.dot_general(a, b, (((1,), (1,)), ((), ())), preferred_element_type=F32)


def _rms_mod(x, g, shift, scale):
    ms = jnp.mean(x * x, axis=-1, keepdims=True)
    y = x * lax.rsqrt(ms + EPS) * g
    return y * (1.0 + scale) + shift


def _mods_kernel(c_ref, w_ref, b_ref, o_ref):
    c = c_ref[...]
    ca = c * jax.nn.sigmoid(c)
    o_ref[0] = jnp.dot(ca, w_ref[0], precision=lax.Precision.HIGHEST,
                       preferred_element_type=F32) + b_ref[0]


def _mods(c_pad, w, b, tn):
    nl, d, n = w.shape
    return pl.pallas_call(
        _mods_kernel,
        grid=(nl, n // tn),
        in_specs=[pl.BlockSpec((SUBLANES, d), lambda l, j: (0, 0)),
                  pl.BlockSpec((1, d, tn), lambda l, j: (l, 0, j)),
                  pl.BlockSpec((1, 1, tn), lambda l, j: (l, 0, j))],
        out_specs=pl.BlockSpec((1, SUBLANES, tn), lambda l, j: (l, 0, j)),
        out_shape=jax.ShapeDtypeStruct((nl, SUBLANES, n), F32),
        compiler_params=_params("arbitrary", "arbitrary"),
        name="adaln_mods",
    )(c_pad, w, b)


def _ffn_kernel(*refs, tf, with_mixer):
    if with_mixer:
        x_ref, a_ref, gm_ref, wo_ref = refs[:4]
        refs = refs[4:]
        x = x_ref[0] + gm_ref[0] * _dot(a_ref[0], wo_ref[...])
    else:
        x_ref = refs[0]
        refs = refs[1:]
        x = x_ref[0]
    sh_ref, sc_ref, gt_ref, g_ref, wg_ref, wu_ref, wd_ref, o_ref = refs
    h = _rms_mod(x, g_ref[...], sh_ref[0], sc_ref[0]).astype(BF16)
    acc = None
    for f in range(wg_ref.shape[1] // tf):
        cols = slice(f * tf, (f + 1) * tf)
        g = _dot(h, wg_ref[:, cols])
        u = _dot(h, wu_ref[:, cols])
        a = (g * jax.nn.sigmoid(g) * u).astype(BF16)
        d = _dot(a, wd_ref[cols, :])
        acc = d if acc is None else acc + d
    o_ref[0] = x + (0.5 * gt_ref[0]) * acc


def _ffn(x, shift, scale, gate, g, wg, wu, wd, tm, tf, mixer=None):
    b, s, d = x.shape
    f = wg.shape[1]
    row = pl.BlockSpec((1, tm, d), lambda bi, i: (bi, i, 0))
    vec = pl.BlockSpec((1, 1, d), lambda bi, i: (bi, 0, 0))
    full = lambda shp: pl.BlockSpec(shp, lambda bi, i: (0, 0), pipeline_mode=pl.Buffered(1))
    ins, specs = [x], [row]
    if mixer is not None:
        a, gate_mix, w_o = mixer
        k = a.shape[2]
        ins += [a, gate_mix, w_o]
        specs += [pl.BlockSpec((1, tm, k), lambda bi, i: (bi, i, 0)), vec, full((k, d))]
    ins += [shift, scale, gate, g, wg, wu, wd]
    specs += [vec, vec, vec, full((1, d)), full((d, f)), full((d, f)), full((f, d))]
    return pl.pallas_call(
        functools.partial(_ffn_kernel, tf=tf, with_mixer=mixer is not None),
        grid=(b, s // tm),
        in_specs=specs,
        out_specs=row,
        out_shape=jax.ShapeDtypeStruct((b, s, d), F32),
        compiler_params=_params("arbitrary", "arbitrary"),
        name="ffn_swiglu",
    )(*ins)


def _headnorm_pair(y, g, dim):
    lane = lax.broadcasted_iota(jnp.int32, (1, LANES), 1)
    left = lane < dim
    y2 = y * y
    ss_a = jnp.sum(jnp.where(left, y2, 0.0), axis=-1, keepdims=True)
    ss_b = jnp.sum(jnp.where(left, 0.0, y2), axis=-1, keepdims=True)
    inv = jnp.where(left, lax.rsqrt(ss_a * (1.0 / dim) + EPS), lax.rsqrt(ss_b * (1.0 / dim) + EPS))
    return y * inv * g


def _moba_proj_kernel(x_ref, sh_ref, sc_ref, g_ref, w_ref, wvt_ref, hg_ref, qk_ref, vt_ref, *, tn):
    h = _rms_mod(x_ref[0], g_ref[...], sh_ref[0], sc_ref[0]).astype(BF16)
    for n in range(w_ref.shape[1] // tn):
        res = _dot(h, w_ref[:, n * tn:(n + 1) * tn])
        for s in range(tn // LANES):
            sl = slice(n * tn + s * LANES, n * tn + (s + 1) * LANES)
            qk_ref[0, :, sl] = _headnorm_pair(res[:, s * LANES:(s + 1) * LANES], hg_ref[:, sl],
                                              MOBA_HEAD_DIM).astype(BF16)
    for n in range(wvt_ref.shape[0] // tn):
        rows = slice(n * tn, (n + 1) * tn)
        vt_ref[0, rows, :] = _dot_nt(wvt_ref[rows, :], h).astype(BF16)


def _moba_proj(x, shift, scale, g, w_qk, w_vt, head_gain, tm, tn):
    b, s, d = x.shape
    nqk = w_qk.shape[1]
    nv = w_vt.shape[0]
    vec = pl.BlockSpec((1, 1, d), lambda bi, i: (bi, 0, 0))
    full = lambda shp: pl.BlockSpec(shp, lambda bi, i: (0, 0))
    return pl.pallas_call(
        functools.partial(_moba_proj_kernel, tn=tn),
        grid=(b, s // tm),
        in_specs=[pl.BlockSpec((1, tm, d), lambda bi, i: (bi, i, 0)), vec, vec, full((1, d)),
                  full((d, nqk)), full((nv, d)), full((1, nqk))],
        out_specs=[pl.BlockSpec((1, tm, nqk), lambda bi, i: (bi, i, 0)),
                   pl.BlockSpec((1, nv, tm), lambda bi, i: (bi, 0, i))],
        out_shape=[jax.ShapeDtypeStruct((b, s, nqk), BF16), jax.ShapeDtypeStruct((b, nv, s), BF16)],
        compiler_params=_params("arbitrary", "arbitrary"),
        name="moba_qkv_proj",
    )(x, shift, scale, g, w_qk, w_vt, head_gain)


def _softmax_step_t(s, m, l, acc, vt, bounded):
    if bounded:
        p = jnp.exp2(s)
        return m, l + jnp.sum(p, axis=0, keepdims=True), acc + _dot(vt, p.astype(BF16))
    m_new = jnp.maximum(m, jnp.max(s, axis=0, keepdims=True))
    alpha = jnp.exp2(m - m_new)
    p = jnp.exp2(s - m_new)
    l_new = alpha * l + jnp.sum(p, axis=0, keepdims=True)
    acc_new = acc * alpha + _dot(vt, p.astype(BF16))
    return m_new, l_new, acc_new


def _moba_attn_kernel(qmin_ref, kmax_ref, qgmin_ref, kbmax_ref, q_ref, k_ref, vt_ref, bt_ref, pq_ref, pk_ref, o_ref,
                      km_scr, sel_scr, *, nb, t, npairs, bounded):
    blk = MOBA_BLOCK
    hd = MOBA_HEAD_DIM
    nsub = t // blk
    nh = 2 * npairs
    bi = pl.program_id(0)
    qi = pl.program_id(2)
    lane = lax.broadcasted_iota(jnp.int32, (1, LANES), 1)
    left = lane < hd

    @pl.when(qi == 0)
    def _():
        km_scr[...] = jnp.zeros_like(km_scr)
        for r in range(nb):
            kblk = k_ref[0, r * blk:(r + 1) * blk, :].astype(F32)
            km_scr[r:r + 1, :] = jnp.sum(kblk, axis=0, keepdims=True) * (1.0 / blk)

    q_heads = []
    for p in range(npairs):
        q2 = q_ref[0, :, p * LANES:(p + 1) * LANES]
        zero = jnp.zeros_like(q2)
        q_heads += [jnp.where(left, q2, zero), jnp.where(left, zero, q2)]

    col = lax.broadcasted_iota(jnp.int32, (1, t), 1)
    q_idx = qi * t + col
    q_sub = sum((col >= r * blk).astype(jnp.int32) for r in range(1, nsub))
    q_blk = qi * nsub + q_sub
    q_blk_start = q_blk * blk

    blk_row = lax.broadcasted_iota(jnp.int32, (nb, t), 0)
    for h in range(nh):
        km = km_scr[:, (h // 2) * LANES:(h // 2 + 1) * LANES]
        km_hi = km.astype(BF16)
        km_lo = (km - km_hi.astype(F32)).astype(BF16)
        gate = (_dot_nt(km_hi, q_heads[h]) + _dot_nt(km_lo, q_heads[h]))[:nb]
        gate = jnp.where(blk_row < q_blk, gate, -jnp.inf)
        sel = jnp.zeros((nb, t), F32)
        for _ in range(MOBA_TOPK):
            mx = jnp.max(gate, axis=0, keepdims=True)
            idx = jnp.min(jnp.where(gate == mx, blk_row, nb), axis=0, keepdims=True)
            pick = (blk_row == idx) & (mx > -jnp.inf)
            sel = jnp.where(pick, 1.0, sel)
            gate = jnp.where(pick, -jnp.inf, gate)
        sel_scr[h] = sel

    pq = pq_ref[0, 0]

    def gather_bias(h, dist):
        table = jnp.broadcast_to(bt_ref[h // 2, h % 2:h % 2 + 1, :], (dist.shape[0], LANES))
        return jnp.concatenate(
            [jnp.take_along_axis(table, dist[:, g * LANES:(g + 1) * LANES], axis=1, mode="promise_in_bounds")
             for g in range(dist.shape[1] // LANES)], axis=1)

    def far_bias(h):
        return bt_ref[h // 2, h % 2:h % 2 + 1, MAX_DIST:MAX_DIST + 1]

    def chosen_rows(h, j, lanes=slice(None)):
        return [sel_scr[h, pl.ds(j * nsub + r, 1), lanes] > 0.0 for r in range(nsub)]

    def own_chunk_bounded(carry):
        assert nsub == 2
        carry = list(carry)
        start = pl.multiple_of(qi * t, t)
        k2 = k_ref[0, pl.ds(start, t), :]
        vt = vt_ref[0, :, pl.ds(start, t)]
        pk0 = pk_ref[0, pl.ds(start, blk), :]
        pk1 = pk_ref[0, pl.ds(start + blk, blk), :]
        causal = (lax.broadcasted_iota(jnp.int32, (blk, blk), 0) <= lax.broadcasted_iota(jnp.int32, (blk, blk), 1))
        d00 = jnp.where(causal, jnp.clip(pq[:, :blk] - pk0, 0, MAX_DIST), MASKED_DIST)
        d01 = jnp.clip(pq[:, blk:] - pk0, 0, MAX_DIST)
        d11 = jnp.where(causal, jnp.clip(pq[:, blk:] - pk1, 0, MAX_DIST), MASKED_DIST)
        quads = []
        for h in range(nh):
            kh = k2[:, (h // 2) * LANES:(h // 2 + 1) * LANES]
            qh = q_heads[h]
            quads.append((_dot_nt(kh[:blk], qh[:blk]), _dot_nt(kh[:blk], qh[blk:]), _dot_nt(kh[blk:], qh[blk:])))
        for h in range(nh):
            m, l, acc = carry[h]
            s00, s01, s11 = quads[h]
            e00 = jnp.exp2(s00 + gather_bias(h, d00))
            e01 = jnp.exp2(s01 + gather_bias(h, d01))
            e11 = jnp.exp2(s11 + gather_bias(h, d11))
            w01 = jnp.where(chosen_rows(h, qi, slice(blk, t))[0], 1.0, 0.0)
            vt0 = vt[h * hd:(h + 1) * hd, :blk]
            vt1 = vt[h * hd:(h + 1) * hd, blk:]
            l_add = jnp.concatenate(
                [jnp.sum(e00, axis=0, keepdims=True),
                 jnp.sum(e01, axis=0, keepdims=True) * w01 + jnp.sum(e11, axis=0, keepdims=True)], axis=1)
            acc_add = jnp.concatenate(
                [_dot(vt0, e00.astype(BF16)),
                 _dot(vt0, e01.astype(BF16)) * w01 + _dot(vt1, e11.astype(BF16))], axis=1)
            carry[h] = (m, l + l_add, acc + acc_add)
        return tuple(carry)

    def chunk(j, carry, mode):
        carry = list(carry)
        start = pl.multiple_of(j * t, t)
        k2 = k_ref[0, pl.ds(start, t), :]
        vt = vt_ref[0, :, pl.ds(start, t)]
        s_heads = [_dot_nt(k2[:, (h // 2) * LANES:(h // 2 + 1) * LANES], q_heads[h]) for h in range(nh)]
        if mode in ("near", "diag"):
            dist = jnp.clip(pq - pk_ref[0, pl.ds(start, t), :], 0, MAX_DIST)
        if mode == "corner":
            dist_c = jnp.clip(pq[:, :LANES] - pk_ref[0, pl.ds(start + (nsub - 1) * blk, blk), :], 0, MAX_DIST)
        if mode == "diag":
            key_idx = start + lax.broadcasted_iota(jnp.int32, (blk, t), 0)
        for h in range(nh):
            m, l, acc = carry[h]
            s = s_heads[h]
            vt_h = vt[h * hd:(h + 1) * hd, :]
            chosen = chosen_rows(h, j)
            if mode in ("far", "corner") and bounded:
                two_c = jnp.exp2(far_bias(h))
                for r in range(nsub):
                    piece = s[r * blk:(r + 1) * blk]
                    w = jnp.where(chosen[r], two_c, 0.0)
                    if mode == "corner" and r == nsub - 1:
                        piece = jnp.concatenate([piece[:, :LANES] + gather_bias(h, dist_c), piece[:, LANES:]], axis=1)
                        w = jnp.where(col < LANES, jnp.where(chosen[r], 1.0, 0.0), w)
                    p = jnp.exp2(piece)
                    l = l + jnp.sum(p, axis=0, keepdims=True) * w
                    acc = acc + _dot(vt_h[:, r * blk:(r + 1) * blk], p.astype(BF16)) * w
                carry[h] = (m, l, acc)
            elif mode == "far":
                pieces = [s[r * blk:(r + 1) * blk] for r in range(nsub)]
                c = far_bias(h)
                tmax = [jnp.where(chosen[r], jnp.max(pieces[r], axis=0, keepdims=True) + c, NEG)
                        for r in range(nsub)]
                m_new = functools.reduce(jnp.maximum, tmax, m)
                alpha = jnp.exp2(m - m_new)
                ps = [jnp.exp2(pieces[r] - jnp.where(chosen[r], m_new - c, jnp.inf)) for r in range(nsub)]
                l_new = alpha * l + sum(jnp.sum(p, axis=0, keepdims=True) for p in ps)
                p = jnp.concatenate(ps, axis=0) if nsub > 1 else ps[0]
                carry[h] = (m_new, l_new, acc * alpha + _dot(vt_h, p.astype(BF16)))
            else:
                s = s + gather_bias(h, dist)
                masked = []
                for r in range(nsub):
                    piece = s[r * blk:(r + 1) * blk]
                    if mode == "diag":
                        kidx = key_idx + r * blk
                        keep_own = jnp.where(kidx <= q_idx, jnp.where(kidx >= q_blk_start, piece, NEG), NEG)
                        masked.append(jnp.where(chosen[r], piece, keep_own))
                    else:
                        masked.append(jnp.where(chosen[r], piece, NEG))
                s = jnp.concatenate(masked, axis=0) if nsub > 1 else masked[0]
                carry[h] = _softmax_step_t(s, m, l, acc, vt_h, bounded)
        return tuple(carry)

    def past_chunk(j, carry):
        q_lo = qmin_ref[bi, qi]
        far = q_lo - kmax_ref[bi, j] >= MAX_DIST
        if not bounded:
            return lax.cond(far, lambda c: chunk(j, c, "far"), lambda c: chunk(j, c, "near"), carry)
        groups = t // LANES
        q_rest = functools.reduce(jnp.minimum, [qgmin_ref[bi, qi * groups + g] for g in range(1, groups)])
        k_first = functools.reduce(jnp.maximum, [kbmax_ref[bi, j * nsub + r] for r in range(nsub - 1)])
        corner = (q_lo - k_first >= MAX_DIST) & (q_rest - kbmax_ref[bi, j * nsub + nsub - 1] >= MAX_DIST)
        return lax.cond(far, lambda c: chunk(j, c, "far"),
                        lambda c: lax.cond(corner, lambda c2: chunk(j, c2, "corner"),
                                           lambda c2: chunk(j, c2, "near"), c), carry)

    init = (jnp.full((1, t), NEG, F32), jnp.zeros((1, t), F32), jnp.zeros((hd, t), F32))
    carry = lax.fori_loop(0, qi, past_chunk, (init,) * nh)
    carry = own_chunk_bounded(carry) if bounded else chunk(qi, carry, "diag")
    out_t = jnp.concatenate([acc / l for (_, l, acc) in carry], axis=0)
    o_ref[0] = out_t.T.astype(BF16)


def _moba_attn(qk, vt, bias_table, positions, t, npairs, bounded):
    b, s, _ = qk.shape
    nb = s // MOBA_BLOCK
    ngroups = MOBA_HEADS * MOBA_HEAD_DIM // (LANES * npairs)
    w = npairs * LANES
    pos_tiles = positions.reshape(b, s // t, t)
    q_min = jnp.min(pos_tiles, axis=-1)
    k_max = jnp.max(pos_tiles, axis=-1)
    qg_min = jnp.min(positions.reshape(b, s // LANES, LANES), axis=-1)
    kb_max = jnp.max(positions.reshape(b, nb, MOBA_BLOCK), axis=-1)
    pos_row = positions.reshape(b, s // t, 1, t)
    pos_col = positions.reshape(b, s, 1)
    grid_spec = pltpu.PrefetchScalarGridSpec(
        num_scalar_prefetch=4,
        grid=(b, ngroups, s // t),
        in_specs=[pl.BlockSpec((1, t, w), lambda bi, g, qi, *_: (bi, qi, g)),
                  pl.BlockSpec((1, s, w), lambda bi, g, qi, *_: (bi, 0, ngroups + g)),
                  pl.BlockSpec((1, w, s), lambda bi, g, qi, *_: (bi, g, 0)),
                  pl.BlockSpec((npairs, 2, LANES), lambda bi, g, qi, *_: (g, 0, 0)),
                  pl.BlockSpec((1, 1, 1, t), lambda bi, g, qi, *_: (bi, qi, 0, 0)),
                  pl.BlockSpec((1, s, 1), lambda bi, g, qi, *_: (bi, 0, 0))],
        out_specs=pl.BlockSpec((1, t, w), lambda bi, g, qi, *_: (bi, qi, g)),
        scratch_shapes=[pltpu.VMEM((LANES, w), F32), pltpu.VMEM((2 * npairs, nb, t), F32)],
    )
    return pl.pallas_call(
        functools.partial(_moba_attn_kernel, nb=nb, t=t, npairs=npairs, bounded=bounded),
        grid_spec=grid_spec,
        out_shape=jax.ShapeDtypeStruct((b, s, MOBA_HEADS * MOBA_HEAD_DIM), BF16),
        compiler_params=_params("arbitrary", "arbitrary", "arbitrary"),
        name="moba_attention",
    )(q_min, k_max, qg_min, kb_max, qk, qk, vt, bias_table, pos_row, pos_col)


def _rope_tab_kernel(pos_ref, inv_ref, c_ref, s_ref):
    half = MLA_ROPE // 2
    ang = pos_ref[0].astype(F32) * inv_ref[...]
    lane = lax.broadcasted_iota(jnp.int32, (1, LANES), 1)
    sn = jnp.sin(ang)
    c_ref[0] = jnp.cos(ang)
    s_ref[0] = jnp.where(lane < MLA_NOPE + half, -sn, sn)


def _rope_tables(pos_col, inv_lane, tm):
    b, s, _ = pos_col.shape
    out = pl.BlockSpec((1, tm, LANES), lambda bi, i: (bi, i, 0))
    shape = jax.ShapeDtypeStruct((b, s, LANES), F32)
    return pl.pallas_call(
        _rope_tab_kernel,
        grid=(b, s // tm),
        in_specs=[pl.BlockSpec((1, tm, 1), lambda bi, i: (bi, i, 0)),
                  pl.BlockSpec((1, LANES), lambda bi, i: (0, 0))],
        out_specs=[out, out],
        out_shape=[shape, shape],
        compiler_params=_params("arbitrary", "arbitrary"),
        name="rope_tables",
    )(pos_col, inv_lane)


def _mla_q_kernel(x_ref, sh_ref, sc_ref, g_ref, wdq_ref, qag_ref, wuq_ref, wuqp_ref, qg_ref, qgp_ref,
                  c_ref, s_ref, o_ref):
    h = _rms_mod(x_ref[0], g_ref[...], sh_ref[0], sc_ref[0]).astype(BF16)
    dq = _dot(h, wdq_ref[...])
    dqn = (dq * lax.rsqrt(jnp.mean(dq * dq, axis=-1, keepdims=True) + EPS) * qag_ref[...]).astype(BF16)
    uq = _dot(dqn, wuq_ref[...])
    up = _dot(dqn, wuqp_ref[...])
    cg = c_ref[0] * qg_ref[...]
    sg = s_ref[0] * qgp_ref[...]
    scale = MLA_QK ** -0.5 * LOG2E
    for hd in range(MLA_HEADS):
        sl = slice(hd * LANES, (hd + 1) * LANES)
        u = uq[:, sl]
        inv = lax.rsqrt(jnp.sum(u * u, axis=-1, keepdims=True) * (1.0 / MLA_QK) + EPS) * scale
        o_ref[0, :, sl] = ((u * cg + up[:, sl] * sg) * inv).astype(BF16)


def _mla_q(x, shift, scale, g, wdq, qag, wuq_p, wuq_partner, qg_p, qg_partner, cos, sin, tm):
    b, s, d = x.shape
    r = wdq.shape[1]
    nq = wuq_p.shape[1]
    vec = pl.BlockSpec((1, 1, d), lambda bi, i: (bi, 0, 0))
    tab = pl.BlockSpec((1, tm, LANES), lambda bi, i: (bi, i, 0))
    full = lambda shp: pl.BlockSpec(shp, lambda bi, i: (0, 0))
    return pl.pallas_call(
        _mla_q_kernel,
        grid=(b, s // tm),
        in_specs=[pl.BlockSpec((1, tm, d), lambda bi, i: (bi, i, 0)), vec, vec, full((1, d)),
                  full((d, r)), full((1, r)), full((r, nq)), full((r, nq)), full((1, LANES)), full((1, LANES)),
                  tab, tab],
        out_specs=pl.BlockSpec((1, tm, nq), lambda bi, i: (bi, i, 0)),
        out_shape=jax.ShapeDtypeStruct((b, s, nq), BF16),
        compiler_params=_params("arbitrary", "arbitrary"),
        name="mla_q_path",
    )(x, shift, scale, g, wdq, qag, wuq_p, wuq_partner, qg_p, qg_partner, cos, sin)


def _mla_kv_kernel(x_ref, sh_ref, sc_ref, g_ref, wdkv_ref, kvg_ref, wuk_ref, wuvt_ref, kg_ref, kgp_ref,
                   c_ref, s_ref, k_ref, vt_ref):
    hs = _rms_mod(x_ref[0], g_ref[...], sh_ref[0], sc_ref[0]).astype(BF16)
    ck = _dot(hs, wdkv_ref[...])
    ckv = ck[:, :MLA_KV_LORA]
    ckvn = (ckv * lax.rsqrt(jnp.mean(ckv * ckv, axis=-1, keepdims=True) + EPS) * kvg_ref[...]).astype(BF16)
    kn = _dot(ckvn, wuk_ref[...])
    vt_ref[0] = _dot_nt(wuvt_ref[...], ckvn).astype(BF16)
    lane = lax.broadcasted_iota(jnp.int32, (1, LANES), 1)
    on_rope = (lane >= MLA_NOPE) & (lane < MLA_QK)
    tail = ck[:, MLA_KV_LORA:MLA_KV_LORA + LANES]
    kpe = jnp.where(on_rope, pltpu.roll(tail, MLA_NOPE, axis=1), 0.0)
    partner = pltpu.roll(tail, MLA_ROPE, axis=1)
    cg = c_ref[0] * kg_ref[...]
    sg = s_ref[0] * kgp_ref[...]
    rot = kpe * cg + partner * sg
    ss_pe = jnp.sum(kpe * kpe, axis=-1, keepdims=True)
    for hd in range(MLA_HEADS):
        sl = slice(hd * LANES, (hd + 1) * LANES)
        nope = kn[:, sl]
        inv = lax.rsqrt((jnp.sum(nope * nope, axis=-1, keepdims=True) + ss_pe) * (1.0 / MLA_QK) + EPS)
        k_ref[0, :, sl] = ((nope * cg + rot) * inv).astype(BF16)


def _mla_kv(x, shift, scale, g, wdkv_p, kvg, wuk_p, wuv_t, kg_p, kg_partner, cos, sin, tm):
    b, s, d = x.shape
    r = wdkv_p.shape[1]
    nk = wuk_p.shape[1]
    nv = wuv_t.shape[0]
    vec = pl.BlockSpec((1, 1, d), lambda bi, i: (bi, 0, 0))
    tab = pl.BlockSpec((1, tm, LANES), lambda bi, i: (bi, i, 0))
    full = lambda shp: pl.BlockSpec(shp, lambda bi, i: (0, 0))
    return pl.pallas_call(
        _mla_kv_kernel,
        grid=(b, s // tm),
        in_specs=[pl.BlockSpec((1, tm, d), lambda bi, i: (bi, i, 0)), vec, vec, full((1, d)),
                  full((d, r)), full((1, MLA_KV_LORA)), full((MLA_KV_LORA, nk)), full((nv, MLA_KV_LORA)),
                  full((1, LANES)), full((1, LANES)), tab, tab],
        out_specs=[pl.BlockSpec((1, tm, nk), lambda bi, i: (bi, i, 0)),
                   pl.BlockSpec((1, nv, tm), lambda bi, i: (bi, 0, i))],
        out_shape=[jax.ShapeDtypeStruct((b, s, nk), BF16), jax.ShapeDtypeStruct((b, nv, s), BF16)],
        compiler_params=_params("arbitrary", "arbitrary"),
        name="mla_shared_kv",
    )(x, shift, scale, g, wdkv_p, kvg, wuk_p, wuv_t, kg_p, kg_partner, cos, sin)


def _mla_attn_kernel(q_ref, k_ref, vt_ref, o_ref, *, t, nh, bounded):
    qi = pl.program_id(2)
    q2 = q_ref[0]
    q_heads = [q2[:, h * LANES:(h + 1) * LANES] for h in range(nh)]

    def chunk(j, carry, masked):
        carry = list(carry)
        start = pl.multiple_of(j * t, t)
        k2 = k_ref[0, pl.ds(start, t), :]
        vt = vt_ref[0, :, pl.ds(start, t)]
        s_heads = [_dot_nt(k2[:, h * LANES:(h + 1) * LANES], q_heads[h]) for h in range(nh)]
        if masked:
            row = lax.broadcasted_iota(jnp.int32, (t, t), 0)
            col = lax.broadcasted_iota(jnp.int32, (t, t), 1)
            allowed = row <= col
        for h in range(nh):
            m, l, acc = carry[h]
            s = jnp.where(allowed, s_heads[h], NEG) if masked else s_heads[h]
            carry[h] = _softmax_step_t(s, m, l, acc, vt[h * MLA_V:(h + 1) * MLA_V, :], bounded)
        return tuple(carry)

    def own_chunk_bounded(carry):
        carry = list(carry)
        half = t // 2
        start = pl.multiple_of(qi * t, t)
        k2 = k_ref[0, pl.ds(start, t), :]
        vt = vt_ref[0, :, pl.ds(start, t)]
        causal = (lax.broadcasted_iota(jnp.int32, (half, half), 0) <= lax.broadcasted_iota(jnp.int32, (half, half), 1))
        quads = []
        for h in range(nh):
            kh = k2[:, h * LANES:(h + 1) * LANES]
            qh = q_heads[h]
            quads.append((_dot_nt(kh[:half], qh[:half]), _dot_nt(kh[:half], qh[half:]), _dot_nt(kh[half:], qh[half:])))
        for h in range(nh):
            m, l, acc = carry[h]
            s00, s01, s11 = quads[h]
            e00 = jnp.exp2(jnp.where(causal, s00, NEG))
            e01 = jnp.exp2(s01)
            e11 = jnp.exp2(jnp.where(causal, s11, NEG))
            vt0 = vt[h * MLA_V:(h + 1) * MLA_V, :half]
            vt1 = vt[h * MLA_V:(h + 1) * MLA_V, half:]
            l_add = jnp.concatenate(
                [jnp.sum(e00, axis=0, keepdims=True),
                 jnp.sum(e01, axis=0, keepdims=True) + jnp.sum(e11, axis=0, keepdims=True)], axis=1)
            acc_add = jnp.concatenate(
                [_dot(vt0, e00.astype(BF16)), _dot(vt0, e01.astype(BF16)) + _dot(vt1, e11.astype(BF16))], axis=1)
            carry[h] = (m, l + l_add, acc + acc_add)
        return tuple(carry)

    init = (jnp.full((1, t), NEG, F32), jnp.zeros((1, t), F32), jnp.zeros((MLA_V, t), F32))
    carry = lax.fori_loop(0, qi, lambda j, c: chunk(j, c, False), (init,) * nh)
    carry = own_chunk_bounded(carry) if bounded else chunk(qi, carry, True)
    out_t = jnp.concatenate([acc / l for (_, l, acc) in carry], axis=0)
    o_ref[0] = out_t.T.astype(BF16)


def _mla_attn(q, k, vt, t, nh, bounded):
    b, s, _ = q.shape
    return pl.pallas_call(
        functools.partial(_mla_attn_kernel, t=t, nh=nh, bounded=bounded),
        grid=(b, MLA_HEADS // nh, s // t),
        in_specs=[pl.BlockSpec((1, t, nh * LANES), lambda bi, g, qi: (bi, qi, g)),
                  pl.BlockSpec((1, s, nh * LANES), lambda bi, g, qi: (bi, 0, g)),
                  pl.BlockSpec((1, nh * MLA_V, s), lambda bi, g, qi: (bi, g, 0))],
        out_specs=pl.BlockSpec((1, t, nh * MLA_V), lambda bi, g, qi: (bi, qi, g)),
        out_shape=jax.ShapeDtypeStruct((b, s, MLA_HEADS * MLA_V), BF16),
        compiler_params=_params("arbitrary", "arbitrary", "arbitrary"),
        name="mla_attention",
    )(q, k, vt)


def _t5_bias_table(rel_bias):
    n = jnp.arange(MASKED_DIST)
    max_exact = REL_BUCKETS // 2
    nf = jnp.maximum(n, 1).astype(F32)
    large = max_exact + (jnp.log(nf / max_exact) / math.log(REL_MAX_DIST / max_exact)
                         * (REL_BUCKETS - max_exact)).astype(jnp.int32)
    large = jnp.minimum(large, REL_BUCKETS - 1)
    bucket = jnp.where(n < max_exact, n, large)
    table = jnp.concatenate([rel_bias.T[:, bucket] * LOG2E, jnp.full((MOBA_HEADS, 1), NEG, F32)], axis=1)
    return table.reshape(MOBA_HEADS // 2, 2, BIAS_TABLE)


def _pad_heads(w, width):
    k = w.shape[0]
    w3 = w.reshape(k, -1, width)
    return jnp.pad(w3, ((0, 0), (0, 0), (0, LANES - width))).reshape(k, -1)


def _pad_lanes(g):
    return jnp.pad(g, (0, LANES - g.shape[0])).reshape(1, LANES)


def _swap_rope_halves(a):
    half = MLA_ROPE // 2
    return jnp.concatenate([jnp.zeros_like(a[..., :MLA_NOPE]), a[..., MLA_NOPE + half:],
                            a[..., MLA_NOPE:MLA_NOPE + half]], axis=-1)


def _scores_bounded(q_gain, k_gain, dim, scale, extra=0.0):
    bound = dim * jnp.max(jnp.abs(q_gain)) * jnp.max(jnp.abs(k_gain)) * scale * 1.01 + extra
    return bound <= SCORE_BOUND_MAX


def kernel(x, c, positions, ada_w, ada_b, norm_g, ffn_w_gate, ffn_w_up, ffn_w_down, rel_bias,
           moba_w_qkv, moba_q_g, moba_k_g, moba_w_o, kv_ada_w, kv_ada_b, kv_norm_g, w_dkv,
           kv_a_norm_g, w_uk, w_uv, mla_k_g, mla_w_dq, mla_q_a_norm_g, mla_w_uq, mla_q_g, mla_w_o):
    b, s, d = x.shape
    depth = ada_w.shape[0]
    n_moba = moba_w_qkv.shape[0]
    tm_ffn = min(512, s)
    tm_proj = min(512, s)

    c_pad = jnp.pad(c, ((0, SUBLANES - b), (0, 0)))
    mods = _mods(c_pad, ada_w, ada_b.reshape(depth, 1, -1), tn=1152)[:, :b]
    kv_mods = _mods(c_pad, kv_ada_w[None], kv_ada_b.reshape(1, 1, -1), tn=1024)[0, :b]

    def mod_vecs(m):
        return [v.reshape(b, 1, d) for v in jnp.split(m, m.shape[-1] // d, axis=-1)]

    pos_col = positions.reshape(b, s, 1)

    for layer in range(depth):
        sh1, sc1, g1, sh2, sc2, g2, sh3, sc3, g3 = mod_vecs(mods[layer])
        ng = norm_g[layer]

        def ffn(xin, idx, sh, sc, gt, mixer=None):
            return _ffn(xin, sh, sc, gt, ng[2 * idx:2 * idx + 1],
                        ffn_w_gate[layer, idx].astype(BF16), ffn_w_up[layer, idx].astype(BF16),
                        ffn_w_down[layer, idx].astype(BF16), tm_ffn, 256, mixer)

        x = ffn(x, 0, sh1, sc1, g1)

        if layer < n_moba:
            n_qk = 2 * MOBA_HEADS * MOBA_HEAD_DIM
            q_scale = MOBA_HEAD_DIM ** -0.5 * LOG2E
            head_gain = jnp.concatenate([jnp.tile(moba_q_g[layer] * q_scale, MOBA_HEADS),
                                         jnp.tile(moba_k_g[layer], MOBA_HEADS)]).reshape(1, -1)
            w_qkv = moba_w_qkv[layer]
            qk, vt = _moba_proj(x, sh2, sc2, ng[1:2], w_qkv[:, :n_qk].astype(BF16),
                                w_qkv[:, n_qk:].T.astype(BF16), head_gain, tm_proj, 512)
            bias_table = _t5_bias_table(rel_bias)
            moba = functools.partial(_moba_attn, qk, vt, bias_table, positions, MOBA_TILE, ATTN_HEADS // 2)
            ok = _scores_bounded(moba_q_g[layer], moba_k_g[layer], MOBA_HEAD_DIM, q_scale,
                                 jnp.max(jnp.abs(rel_bias)) * LOG2E)
            o = lax.cond(ok, lambda: moba(True), lambda: moba(False))
            w_o = moba_w_o[layer]
        else:
            j = layer - n_moba
            if j == 0:
                half = MLA_ROPE // 2
                inv = ROPE_BASE ** (-jnp.arange(half, dtype=F32) / half)
                inv_lane = jnp.zeros((LANES,), F32).at[MLA_NOPE:MLA_QK].set(jnp.tile(inv, 2)).reshape(1, LANES)
                cos, sin = _rope_tables(pos_col, inv_lane, tm_proj)
                kv_sh, kv_sc = mod_vecs(kv_mods)
                w_kpe = w_dkv[:, MLA_KV_LORA:]
                w_kpe_swapped = jnp.concatenate([w_kpe[:, half:], w_kpe[:, :half]], axis=1)
                wdkv_p = jnp.pad(jnp.concatenate([w_dkv, w_kpe_swapped], axis=1),
                                 ((0, 0), (0, LANES - 2 * MLA_ROPE))).astype(BF16)
                shared_k, shared_v = _mla_kv(
                    x, kv_sh, kv_sc, kv_norm_g.reshape(1, d), wdkv_p, kv_a_norm_g.reshape(1, -1),
                    _pad_heads(w_uk, MLA_NOPE).astype(BF16), w_uv.T.astype(BF16), _pad_lanes(mla_k_g),
                    _pad_lanes(_swap_rope_halves(mla_k_g)), cos, sin, tm_proj)
            w_uq = mla_w_uq[j]
            w_uq_partner = _swap_rope_halves(w_uq.reshape(w_uq.shape[0], MLA_HEADS, MLA_QK)).reshape(w_uq.shape)
            q = _mla_q(x, sh2, sc2, ng[1:2], mla_w_dq[j].astype(BF16), mla_q_a_norm_g[j].reshape(1, -1),
                       _pad_heads(w_uq, MLA_QK).astype(BF16), _pad_heads(w_uq_partner, MLA_QK).astype(BF16),
                       _pad_lanes(mla_q_g[j]), _pad_lanes(_swap_rope_halves(mla_q_g[j])), cos, sin, tm_proj)
            mla = functools.partial(_mla_attn, q, shared_k, shared_v, MLA_TILE, ATTN_HEADS)
            ok = _scores_bounded(mla_q_g[j], mla_k_g, MLA_QK, MLA_QK ** -0.5 * LOG2E)
            o = lax.cond(ok, lambda: mla(True), lambda: mla(False))
            w_o = mla_w_o[j]

        x = ffn(x, 1, sh3, sc3, g3, mixer=(o, g2, w_o.astype(BF16)))
    return x
```

```python
import functools
import math

import jax
import jax.numpy as jnp
from jax import lax
from jax.experimental import pallas as pl
from jax.experimental.pallas import tpu as pltpu

F32 = jnp.float32
BF16 = jnp.bfloat16

LANES = 128
SUBLANES = 8
VMEM_LIMIT_BYTES = 56 * 1024 * 1024

MOBA_HEADS = 16
MOBA_HEAD_DIM = 64
MOBA_BLOCK = 256
MOBA_TOPK = 3
REL_BUCKETS = 32
REL_MAX_DIST = 128
MLA_HEADS = 16
MLA_KV_LORA = 256
MLA_NOPE = 64
MLA_ROPE = 32
MLA_V = 64
MLA_QK = MLA_NOPE + MLA_ROPE
ROPE_BASE = 10000.0
EPS = 1e-6
NEG = -1e30

MOBA_TILE = 512
MLA_TILE = 512
ATTN_HEADS = 8
LOG2E = math.log2(math.e)
BIAS_TABLE = LANES
MAX_DIST = BIAS_TABLE - 2
MASKED_DIST = BIAS_TABLE - 1
SCORE_BOUND_MAX = 60.0


def _params(*sem):
    return pltpu.CompilerParams(dimension_semantics=sem, vmem_limit_bytes=VMEM_LIMIT_BYTES)


def _dot(a, b):
    return jnp.dot(a, b, preferred_element_type=F32)


def _dot_nt(a, b):
    return lax.dot_general(a, b, (((1,), (1,)), ((), ())), preferred_element_type=F32)


def _rms_mod(x, g, shift, scale):
    ms = jnp.mean(x * x, axis=-1, keepdims=True)
    y = x * lax.rsqrt(ms + EPS) * g
    return y * (1.0 + scale) + shift


def _mods_kernel(c_ref, w_ref, b_ref, o_ref):
    c = c_ref[...]
    ca = c * jax.nn.sigmoid(c)
    o_ref[0] = jnp.dot(ca, w_ref[0], precision=lax.Precision.HIGHEST,
                       preferred_element_type=F32) + b_ref[0]


def _mods(c_pad, w, b, tn):
    nl, d, n = w.shape
    return pl.pallas_call(
        _mods_kernel,
        grid=(nl, n // tn),
        in_specs=[pl.BlockSpec((SUBLANES, d), lambda l, j: (0, 0)),
                  pl.BlockSpec((1, d, tn), lambda l, j: (l, 0, j)),
                  pl.BlockSpec((1, 1, tn), lambda l, j: (l, 0, j))],
        out_specs=pl.BlockSpec((1, SUBLANES, tn), lambda l, j: (l, 0, j)),
        out_shape=jax.ShapeDtypeStruct((nl, SUBLANES, n), F32),
        compiler_params=_params("arbitrary", "arbitrary"),
        name="adaln_mods",
    )(c_pad, w, b)


def _ffn_kernel(*refs, tf, with_mixer):
    if with_mixer:
        x_ref, a_ref, gm_ref, wo_ref = refs[:4]
        refs = refs[4:]
        x = x_ref[0] + gm_ref[0] * _dot(a_ref[0], wo_ref[...])
    else:
        x_ref = refs[0]
        refs = refs[1:]
        x = x_ref[0]
    sh_ref, sc_ref, gt_ref, g_ref, wg_ref, wu_ref, wd_ref, o_ref = refs
    h = _rms_mod(x, g_ref[...], sh_ref[0], sc_ref[0]).astype(BF16)
    acc = None
    for f in range(wg_ref.shape[1] // tf):
        cols = slice(f * tf, (f + 1) * tf)
        g = _dot(h, wg_ref[:, cols])
        u = _dot(h, wu_ref[:, cols])
        a = (g * jax.nn.sigmoid(g) * u).astype(BF16)
        d = _dot(a, wd_ref[cols, :])
        acc = d if acc is None else acc + d
    o_ref[0] = x + (0.5 * gt_ref[0]) * acc


def _ffn(x, shift, scale, gate, g, wg, wu, wd, which, tm, tf, mixer=None):
    b, s, d = x.shape
    f = wg.shape[-1]
    row = pl.BlockSpec((1, tm, d), lambda bi, i: (bi, i, 0))
    vec = pl.BlockSpec((1, 1, d), lambda bi, i: (bi, 0, 0))
    full = lambda shp: pl.BlockSpec(shp, lambda bi, i: (0, 0), pipeline_mode=pl.Buffered(1))
    picked = lambda r, c: pl.BlockSpec((None, None, r, c), lambda bi, i: (*which, 0, 0),
                                       pipeline_mode=pl.Buffered(1))
    ins, specs = [x], [row]
    if mixer is not None:
        a, gate_mix, w_o = mixer
        k = a.shape[2]
        ins += [a, gate_mix, w_o]
        specs += [pl.BlockSpec((1, tm, k), lambda bi, i: (bi, i, 0)), vec, full((k, d))]
    ins += [shift, scale, gate, g, wg, wu, wd]
    specs += [vec, vec, vec, full((1, d)), picked(d, f), picked(d, f), picked(f, d)]
    return pl.pallas_call(
        functools.partial(_ffn_kernel, tf=tf, with_mixer=mixer is not None),
        grid=(b, s // tm),
        in_specs=specs,
        out_specs=row,
        out_shape=jax.ShapeDtypeStruct((b, s, d), F32),
        compiler_params=_params("arbitrary", "arbitrary"),
        name="ffn_swiglu",
    )(*ins)


def _headnorm_pair(y, g, dim):
    lane = lax.broadcasted_iota(jnp.int32, (1, LANES), 1)
    left = lane < dim
    y2 = y * y
    ss_a = jnp.sum(jnp.where(left, y2, 0.0), axis=-1, keepdims=True)
    ss_b = jnp.sum(jnp.where(left, 0.0, y2), axis=-1, keepdims=True)
    inv = jnp.where(left, lax.rsqrt(ss_a * (1.0 / dim) + EPS), lax.rsqrt(ss_b * (1.0 / dim) + EPS))
    return y * inv * g


def _moba_proj_kernel(x_ref, sh_ref, sc_ref, g_ref, w_ref, wvt_ref, hg_ref, qk_ref, vt_ref, *, tn):
    h = _rms_mod(x_ref[0], g_ref[...], sh_ref[0], sc_ref[0]).astype(BF16)
    for n in range(w_ref.shape[1] // tn):
        res = _dot(h, w_ref[:, n * tn:(n + 1) * tn])
        for s in range(tn // LANES):
            sl = slice(n * tn + s * LANES, n * tn + (s + 1) * LANES)
            qk_ref[0, :, sl] = _headnorm_pair(res[:, s * LANES:(s + 1) * LANES], hg_ref[:, sl],
                                              MOBA_HEAD_DIM).astype(BF16)
    for n in range(wvt_ref.shape[0] // tn):
        rows = slice(n * tn, (n + 1) * tn)
        vt_ref[0, rows, :] = _dot_nt(wvt_ref[rows, :], h).astype(BF16)


def _moba_proj(x, shift, scale, g, w_qk, w_vt, head_gain, tm, tn):
    b, s, d = x.shape
    nqk = w_qk.shape[1]
    nv = w_vt.shape[0]
    vec = pl.BlockSpec((1, 1, d), lambda bi, i: (bi, 0, 0))
    full = lambda shp: pl.BlockSpec(shp, lambda bi, i: (0, 0))
    return pl.pallas_call(
        functools.partial(_moba_proj_kernel, tn=tn),
        grid=(b, s // tm),
        in_specs=[pl.BlockSpec((1, tm, d), lambda bi, i: (bi, i, 0)), vec, vec, full((1, d)),
                  full((d, nqk)), full((nv, d)), full((1, nqk))],
        out_specs=[pl.BlockSpec((1, tm, nqk), lambda bi, i: (bi, i, 0)),
                   pl.BlockSpec((1, nv, tm), lambda bi, i: (bi, 0, i))],
        out_shape=[jax.ShapeDtypeStruct((b, s, nqk), BF16), jax.ShapeDtypeStruct((b, nv, s), BF16)],
        compiler_params=_params("arbitrary", "arbitrary"),
        name="moba_qkv_proj",
    )(x, shift, scale, g, w_qk, w_vt, head_gain)


def _softmax_step_t(s, m, l, acc, vt, bounded):
    if bounded:
        p = jnp.exp2(s)
        return m, l + jnp.sum(p, axis=0, keepdims=True), acc + _dot(vt, p.astype(BF16))
    m_new = jnp.maximum(m, jnp.max(s, axis=0, keepdims=True))
    alpha = jnp.exp2(m - m_new)
    p = jnp.exp2(s - m_new)
    l_new = alpha * l + jnp.sum(p, axis=0, keepdims=True)
    acc_new = acc * alpha + _dot(vt, p.astype(BF16))
    return m_new, l_new, acc_new


def _moba_attn_kernel(qmin_ref, kmax_ref, qgmin_ref, kbmax_ref, q_ref, k_ref, vt_ref, bt_ref, pq_ref, pk_ref, o_ref,
                      km_scr, sel_scr, *, nb, t, npairs, bounded):
    blk = MOBA_BLOCK
    hd = MOBA_HEAD_DIM
    nsub = t // blk
    nh = 2 * npairs
    bi = pl.program_id(0)
    qi = pl.program_id(2)
    lane = lax.broadcasted_iota(jnp.int32, (1, LANES), 1)
    left = lane < hd

    @pl.when(qi == 0)
    def _():
        km_scr[...] = jnp.zeros_like(km_scr)
        for r in range(nb):
            kblk = k_ref[0, r * blk:(r + 1) * blk, :].astype(F32)
            km_scr[r:r + 1, :] = jnp.sum(kblk, axis=0, keepdims=True) * (1.0 / blk)

    q_heads = []
    for p in range(npairs):
        q2 = q_ref[0, :, p * LANES:(p + 1) * LANES]
        zero = jnp.zeros_like(q2)
        q_heads += [jnp.where(left, q2, zero), jnp.where(left, zero, q2)]

    col = lax.broadcasted_iota(jnp.int32, (1, t), 1)
    q_idx = qi * t + col
    q_sub = sum((col >= r * blk).astype(jnp.int32) for r in range(1, nsub))
    q_blk = qi * nsub + q_sub
    q_blk_start = q_blk * blk

    blk_row = lax.broadcasted_iota(jnp.int32, (nb, t), 0)
    for h in range(nh):
        km = km_scr[:, (h // 2) * LANES:(h // 2 + 1) * LANES]
        km_hi = km.astype(BF16)
        km_lo = (km - km_hi.astype(F32)).astype(BF16)
        gate = (_dot_nt(km_hi, q_heads[h]) + _dot_nt(km_lo, q_heads[h]))[:nb]
        gate = jnp.where(blk_row < q_blk, gate, -jnp.inf)
        sel = jnp.zeros((nb, t), F32)
        for _ in range(MOBA_TOPK):
            mx = jnp.max(gate, axis=0, keepdims=True)
            idx = jnp.min(jnp.where(gate == mx, blk_row, nb), axis=0, keepdims=True)
            pick = (blk_row == idx) & (mx > -jnp.inf)
            sel = jnp.where(pick, 1.0, sel)
            gate = jnp.where(pick, -jnp.inf, gate)
        sel_scr[h] = sel

    pq = pq_ref[0, 0]

    def gather_bias(h, dist):
        table = jnp.broadcast_to(bt_ref[h // 2, h % 2:h % 2 + 1, :], (dist.shape[0], LANES))
        return jnp.concatenate(
            [jnp.take_along_axis(table, dist[:, g * LANES:(g + 1) * LANES], axis=1, mode="promise_in_bounds")
             for g in range(dist.shape[1] // LANES)], axis=1)

    def far_bias(h):
        return bt_ref[h // 2, h % 2:h % 2 + 1, MAX_DIST:MAX_DIST + 1]

    def chosen_rows(h, j, lanes=slice(None)):
        return [sel_scr[h, pl.ds(j * nsub + r, 1), lanes] > 0.0 for r in range(nsub)]

    def own_chunk_bounded(carry):
        assert nsub == 2
        carry = list(carry)
        start = pl.multiple_of(qi * t, t)
        k2 = k_ref[0, pl.ds(start, t), :]
        vt = vt_ref[0, :, pl.ds(start, t)]
        pk0 = pk_ref[0, pl.ds(start, blk), :]
        pk1 = pk_ref[0, pl.ds(start + blk, blk), :]
        causal = (lax.broadcasted_iota(jnp.int32, (blk, blk), 0) <= lax.broadcasted_iota(jnp.int32, (blk, blk), 1))
        d00 = jnp.where(causal, jnp.clip(pq[:, :blk] - pk0, 0, MAX_DIST), MASKED_DIST)
        d01 = jnp.clip(pq[:, blk:] - pk0, 0, MAX_DIST)
        d11 = jnp.where(causal, jnp.clip(pq[:, blk:] - pk1, 0, MAX_DIST), MASKED_DIST)
        quads = []
        for h in range(nh):
            kh = k2[:, (h // 2) * LANES:(h // 2 + 1) * LANES]
            qh = q_heads[h]
            quads.append((_dot_nt(kh[:blk], qh[:blk]), _dot_nt(kh[:blk], qh[blk:]), _dot_nt(kh[blk:], qh[blk:])))
        for h in range(nh):
            m, l, acc = carry[h]
            s00, s01, s11 = quads[h]
            e00 = jnp.exp2(s00 + gather_bias(h, d00))
            e01 = jnp.exp2(s01 + gather_bias(h, d01))
            e11 = jnp.exp2(s11 + gather_bias(h, d11))
            w01 = jnp.where(chosen_rows(h, qi, slice(blk, t))[0], 1.0, 0.0)
            vt0 = vt[h * hd:(h + 1) * hd, :blk]
            vt1 = vt[h * hd:(h + 1) * hd, blk:]
            l_add = jnp.concatenate(
                [jnp.sum(e00, axis=0, keepdims=True),
                 jnp.sum(e01, axis=0, keepdims=True) * w01 + jnp.sum(e11, axis=0, keepdims=True)], axis=1)
            acc_add = jnp.concatenate(
                [_dot(vt0, e00.astype(BF16)),
                 _dot(vt0, e01.astype(BF16)) * w01 + _dot(vt1, e11.astype(BF16))], axis=1)
            carry[h] = (m, l + l_add, acc + acc_add)
        return tuple(carry)

    def chunk(j, carry, mode):
        carry = list(carry)
        start = pl.multiple_of(j * t, t)
        k2 = k_ref[0, pl.ds(start, t), :]
        vt = vt_ref[0, :, pl.ds(start, t)]
        s_heads = [_dot_nt(k2[:, (h // 2) * LANES:(h // 2 + 1) * LANES], q_heads[h]) for h in range(nh)]
        if mode in ("near", "diag"):
            dist = jnp.clip(pq - pk_ref[0, pl.ds(start, t), :], 0, MAX_DIST)
        if mode == "corner":
            dist_c = jnp.clip(pq[:, :LANES] - pk_ref[0, pl.ds(start + (nsub - 1) * blk, blk), :], 0, MAX_DIST)
        if mode == "diag":
            key_idx = start + lax.broadcasted_iota(jnp.int32, (blk, t), 0)
        for h in range(nh):
            m, l, acc = carry[h]
            s = s_heads[h]
            vt_h = vt[h * hd:(h + 1) * hd, :]
            chosen = chosen_rows(h, j)
            if mode in ("far", "corner") and bounded:
                two_c = jnp.exp2(far_bias(h))
                for r in range(nsub):
                    piece = s[r * blk:(r + 1) * blk]
                    w = jnp.where(chosen[r], two_c, 0.0)
                    if mode == "corner" and r == nsub - 1:
                        piece = jnp.concatenate([piece[:, :LANES] + gather_bias(h, dist_c), piece[:, LANES:]], axis=1)
                        w = jnp.where(col < LANES, jnp.where(chosen[r], 1.0, 0.0), w)
                    p = jnp.exp2(piece)
                    l = l + jnp.sum(p, axis=0, keepdims=True) * w
                    acc = acc + _dot(vt_h[:, r * blk:(r + 1) * blk], p.astype(BF16)) * w
                carry[h] = (m, l, acc)
            elif mode == "far":
                pieces = [s[r * blk:(r + 1) * blk] for r in range(nsub)]
                c = far_bias(h)
                tmax = [jnp.where(chosen[r], jnp.max(pieces[r], axis=0, keepdims=True) + c, NEG)
                        for r in range(nsub)]
                m_new = functools.reduce(jnp.maximum, tmax, m)
                alpha = jnp.exp2(m - m_new)
                ps = [jnp.exp2(pieces[r] - jnp.where(chosen[r], m_new - c, jnp.inf)) for r in range(nsub)]
                l_new = alpha * l + sum(jnp.sum(p, axis=0, keepdims=True) for p in ps)
                p = jnp.concatenate(ps, axis=0) if nsub > 1 else ps[0]
                carry[h] = (m_new, l_new, acc * alpha + _dot(vt_h, p.astype(BF16)))
            else:
                s = s + gather_bias(h, dist)
                masked = []
                for r in range(nsub):
                    piece = s[r * blk:(r + 1) * blk]
                    if mode == "diag":
                        kidx = key_idx + r * blk
                        keep_own = jnp.where(kidx <= q_idx, jnp.where(kidx >= q_blk_start, piece, NEG), NEG)
                        masked.append(jnp.where(chosen[r], piece, keep_own))
                    else:
                        masked.append(jnp.where(chosen[r], piece, NEG))
                s = jnp.concatenate(masked, axis=0) if nsub > 1 else masked[0]
                carry[h] = _softmax_step_t(s, m, l, acc, vt_h, bounded)
        return tuple(carry)

    def past_chunk(j, carry):
        q_lo = qmin_ref[bi, qi]
        far = q_lo - kmax_ref[bi, j] >= MAX_DIST
        if not bounded:
            return lax.cond(far, lambda c: chunk(j, c, "far"), lambda c: chunk(j, c, "near"), carry)
        groups = t // LANES
        q_rest = functools.reduce(jnp.minimum, [qgmin_ref[bi, qi * groups + g] for g in range(1, groups)])
        k_first = functools.reduce(jnp.maximum, [kbmax_ref[bi, j * nsub + r] for r in range(nsub - 1)])
        corner = (q_lo - k_first >= MAX_DIST) & (q_rest - kbmax_ref[bi, j * nsub + nsub - 1] >= MAX_DIST)
        return lax.cond(far, lambda c: chunk(j, c, "far"),
                        lambda c: lax.cond(corner, lambda c2: chunk(j, c2, "corner"),
                                           lambda c2: chunk(j, c2, "near"), c), carry)

    init = (jnp.full((1, t), NEG, F32), jnp.zeros((1, t), F32), jnp.zeros((hd, t), F32))
    carry = lax.fori_loop(0, qi, past_chunk, (init,) * nh)
    carry = own_chunk_bounded(carry) if bounded else chunk(qi, carry, "diag")
    out_t = jnp.concatenate([acc / l for (_, l, acc) in carry], axis=0)
    o_ref[0] = out_t.T.astype(BF16)


def _moba_attn(qk, vt, bias_table, positions, t, npairs, bounded):
    b, s, _ = qk.shape
    nb = s // MOBA_BLOCK
    ngroups = MOBA_HEADS * MOBA_HEAD_DIM // (LANES * npairs)
    w = npairs * LANES
    pos_tiles = positions.reshape(b, s // t, t)
    q_min = jnp.min(pos_tiles, axis=-1)
    k_max = jnp.max(pos_tiles, axis=-1)
    qg_min = jnp.min(positions.reshape(b, s // LANES, LANES), axis=-1)
    kb_max = jnp.max(positions.reshape(b, nb, MOBA_BLOCK), axis=-1)
    pos_row = positions.reshape(b, s // t, 1, t)
    pos_col = positions.reshape(b, s, 1)
    grid_spec = pltpu.PrefetchScalarGridSpec(
        num_scalar_prefetch=4,
        grid=(b, ngroups, s // t),
        in_specs=[pl.BlockSpec((1, t, w), lambda bi, g, qi, *_: (bi, qi, g)),
                  pl.BlockSpec((1, s, w), lambda bi, g, qi, *_: (bi, 0, ngroups + g)),
                  pl.BlockSpec((1, w, s), lambda bi, g, qi, *_: (bi, g, 0)),
                  pl.BlockSpec((npairs, 2, LANES), lambda bi, g, qi, *_: (g, 0, 0)),
                  pl.BlockSpec((1, 1, 1, t), lambda bi, g, qi, *_: (bi, qi, 0, 0)),
                  pl.BlockSpec((1, s, 1), lambda bi, g, qi, *_: (bi, 0, 0))],
        out_specs=pl.BlockSpec((1, t, w), lambda bi, g, qi, *_: (bi, qi, g)),
        scratch_shapes=[pltpu.VMEM((LANES, w), F32), pltpu.VMEM((2 * npairs, nb, t), F32)],
    )
    return pl.pallas_call(
        functools.partial(_moba_attn_kernel, nb=nb, t=t, npairs=npairs, bounded=bounded),
        grid_spec=grid_spec,
        out_shape=jax.ShapeDtypeStruct((b, s, MOBA_HEADS * MOBA_HEAD_DIM), BF16),
        compiler_params=_params("arbitrary", "arbitrary", "arbitrary"),
        name="moba_attention",
    )(q_min, k_max, qg_min, kb_max, qk, qk, vt, bias_table, pos_row, pos_col)


def _rope_tab_kernel(pos_ref, inv_ref, c_ref, s_ref):
    half = MLA_ROPE // 2
    ang = pos_ref[0].astype(F32) * inv_ref[...]
    lane = lax.broadcasted_iota(jnp.int32, (1, LANES), 1)
    sn = jnp.sin(ang)
    c_ref[0] = jnp.cos(ang)
    s_ref[0] = jnp.where(lane < MLA_NOPE + half, -sn, sn)


def _rope_tables(pos_col, inv_lane, tm):
    b, s, _ = pos_col.shape
    out = pl.BlockSpec((1, tm, LANES), lambda bi, i: (bi, i, 0))
    shape = jax.ShapeDtypeStruct((b, s, LANES), F32)
    return pl.pallas_call(
        _rope_tab_kernel,
        grid=(b, s // tm),
        in_specs=[pl.BlockSpec((1, tm, 1), lambda bi, i: (bi, i, 0)),
                  pl.BlockSpec((1, LANES), lambda bi, i: (0, 0))],
        out_specs=[out, out],
        out_shape=[shape, shape],
        compiler_params=_params("arbitrary", "arbitrary"),
        name="rope_tables",
    )(pos_col, inv_lane)


def _mla_q_kernel(x_ref, sh_ref, sc_ref, g_ref, wdq_ref, qag_ref, wuq_ref, wuqp_ref, qg_ref, qgp_ref,
                  c_ref, s_ref, o_ref):
    h = _rms_mod(x_ref[0], g_ref[...], sh_ref[0], sc_ref[0]).astype(BF16)
    dq = _dot(h, wdq_ref[...])
    dqn = (dq * lax.rsqrt(jnp.mean(dq * dq, axis=-1, keepdims=True) + EPS) * qag_ref[...]).astype(BF16)
    uq = _dot(dqn, wuq_ref[...])
    up = _dot(dqn, wuqp_ref[...])
    cg = c_ref[0] * qg_ref[...]
    sg = s_ref[0] * qgp_ref[...]
    scale = MLA_QK ** -0.5 * LOG2E
    for hd in range(MLA_HEADS):
        sl = slice(hd * LANES, (hd + 1) * LANES)
        u = uq[:, sl]
        inv = lax.rsqrt(jnp.sum(u * u, axis=-1, keepdims=True) * (1.0 / MLA_QK) + EPS) * scale
        o_ref[0, :, sl] = ((u * cg + up[:, sl] * sg) * inv).astype(BF16)


def _mla_q(x, shift, scale, g, wdq, qag, wuq_p, wuq_partner, qg_p, qg_partner, cos, sin, tm):
    b, s, d = x.shape
    r = wdq.shape[1]
    nq = wuq_p.shape[1]
    vec = pl.BlockSpec((1, 1, d), lambda bi, i: (bi, 0, 0))
    tab = pl.BlockSpec((1, tm, LANES), lambda bi, i: (bi, i, 0))
    full = lambda shp: pl.BlockSpec(shp, lambda bi, i: (0, 0))
    return pl.pallas_call(
        _mla_q_kernel,
        grid=(b, s // tm),
        in_specs=[pl.BlockSpec((1, tm, d), lambda bi, i: (bi, i, 0)), vec, vec, full((1, d)),
                  full((d, r)), full((1, r)), full((r, nq)), full((r, nq)), full((1, LANES)), full((1, LANES)),
                  tab, tab],
        out_specs=pl.BlockSpec((1, tm, nq), lambda bi, i: (bi, i, 0)),
        out_shape=jax.ShapeDtypeStruct((b, s, nq), BF16),
        compiler_params=_params("arbitrary", "arbitrary"),
        name="mla_q_path",
    )(x, shift, scale, g, wdq, qag, wuq_p, wuq_partner, qg_p, qg_partner, cos, sin)


def _mla_kv_kernel(x_ref, sh_ref, sc_ref, g_ref, wdkv_ref, kvg_ref, wuk_ref, wuvt_ref, kg_ref, kgp_ref,
                   c_ref, s_ref, k_ref, vt_ref):
    hs = _rms_mod(x_ref[0], g_ref[...], sh_ref[0], sc_ref[0]).astype(BF16)
    ck = _dot(hs, wdkv_ref[...])
    ckv = ck[:, :MLA_KV_LORA]
    ckvn = (ckv * lax.rsqrt(jnp.mean(ckv * ckv, axis=-1, keepdims=True) + EPS) * kvg_ref[...]).astype(BF16)
    kn = _dot(ckvn, wuk_ref[...])
    vt_ref[0] = _dot_nt(wuvt_ref[...], ckvn).astype(BF16)
    lane = lax.broadcasted_iota(jnp.int32, (1, LANES), 1)
    on_rope = (lane >= MLA_NOPE) & (lane < MLA_QK)
    tail = ck[:, MLA_KV_LORA:MLA_KV_LORA + LANES]
    kpe = jnp.where(on_rope, pltpu.roll(tail, MLA_NOPE, axis=1), 0.0)
    partner = pltpu.roll(tail, MLA_ROPE, axis=1)
    cg = c_ref[0] * kg_ref[...]
    sg = s_ref[0] * kgp_ref[...]
    rot = kpe * cg + partner * sg
    ss_pe = jnp.sum(kpe * kpe, axis=-1, keepdims=True)
    for hd in range(MLA_HEADS):
        sl = slice(hd * LANES, (hd + 1) * LANES)
        nope = kn[:, sl]
        inv = lax.rsqrt((jnp.sum(nope * nope, axis=-1, keepdims=True) + ss_pe) * (1.0 / MLA_QK) + EPS)
        k_ref[0, :, sl] = ((nope * cg + rot) * inv).astype(BF16)


def _mla_kv(x, shift, scale, g, wdkv_p, kvg, wuk_p, wuv_t, kg_p, kg_partner, cos, sin, tm):
    b, s, d = x.shape
    r = wdkv_p.shape[1]
    nk = wuk_p.shape[1]
    nv = wuv_t.shape[0]
    vec = pl.BlockSpec((1, 1, d), lambda bi, i: (bi, 0, 0))
    tab = pl.BlockSpec((1, tm, LANES), lambda bi, i: (bi, i, 0))
    full = lambda shp: pl.BlockSpec(shp, lambda bi, i: (0, 0))
    return pl.pallas_call(
        _mla_kv_kernel,
        grid=(b, s // tm),
        in_specs=[pl.BlockSpec((1, tm, d), lambda bi, i: (bi, i, 0)), vec, vec, full((1, d)),
                  full((d, r)), full((1, MLA_KV_LORA)), full((MLA_KV_LORA, nk)), full((nv, MLA_KV_LORA)),
                  full((1, LANES)), full((1, LANES)), tab, tab],
        out_specs=[pl.BlockSpec((1, tm, nk), lambda bi, i: (bi, i, 0)),
                   pl.BlockSpec((1, nv, tm), lambda bi, i: (bi, 0, i))],
        out_shape=[jax.ShapeDtypeStruct((b, s, nk), BF16), jax.ShapeDtypeStruct((b, nv, s), BF16)],
        compiler_params=_params("arbitrary", "arbitrary"),
        name="mla_shared_kv",
    )(x, shift, scale, g, wdkv_p, kvg, wuk_p, wuv_t, kg_p, kg_partner, cos, sin)


def _mla_attn_kernel(q_ref, k_ref, vt_ref, o_ref, *, t, nh, bounded):
    qi = pl.program_id(2)
    q2 = q_ref[0]
    q_heads = [q2[:, h * LANES:(h + 1) * LANES] for h in range(nh)]

    def chunk(j, carry, masked):
        carry = list(carry)
        start = pl.multiple_of(j * t, t)
        k2 = k_ref[0, pl.ds(start, t), :]
        vt = vt_ref[0, :, pl.ds(start, t)]
        s_heads = [_dot_nt(k2[:, h * LANES:(h + 1) * LANES], q_heads[h]) for h in range(nh)]
        if masked:
            row = lax.broadcasted_iota(jnp.int32, (t, t), 0)
            col = lax.broadcasted_iota(jnp.int32, (t, t), 1)
            allowed = row <= col
        for h in range(nh):
            m, l, acc = carry[h]
            s = jnp.where(allowed, s_heads[h], NEG) if masked else s_heads[h]
            carry[h] = _softmax_step_t(s, m, l, acc, vt[h * MLA_V:(h + 1) * MLA_V, :], bounded)
        return tuple(carry)

    def own_chunk_bounded(carry):
        carry = list(carry)
        half = t // 2
        start = pl.multiple_of(qi * t, t)
        k2 = k_ref[0, pl.ds(start, t), :]
        vt = vt_ref[0, :, pl.ds(start, t)]
        causal = (lax.broadcasted_iota(jnp.int32, (half, half), 0) <= lax.broadcasted_iota(jnp.int32, (half, half), 1))
        quads = []
        for h in range(nh):
            kh = k2[:, h * LANES:(h + 1) * LANES]
            qh = q_heads[h]
            quads.append((_dot_nt(kh[:half], qh[:half]), _dot_nt(kh[:half], qh[half:]), _dot_nt(kh[half:], qh[half:])))
        for h in range(nh):
            m, l, acc = carry[h]
            s00, s01, s11 = quads[h]
            e00 = jnp.exp2(jnp.where(causal, s00, NEG))
            e01 = jnp.exp2(s01)
            e11 = jnp.exp2(jnp.where(causal, s11, NEG))
            vt0 = vt[h * MLA_V:(h + 1) * MLA_V, :half]
            vt1 = vt[h * MLA_V:(h + 1) * MLA_V, half:]
            l_add = jnp.concatenate(
                [jnp.sum(e00, axis=0, keepdims=True),
                 jnp.sum(e01, axis=0, keepdims=True) + jnp.sum(e11, axis=0, keepdims=True)], axis=1)
            acc_add = jnp.concatenate(
                [_dot(vt0, e00.astype(BF16)), _dot(vt0, e01.astype(BF16)) + _dot(vt1, e11.astype(BF16))], axis=1)
            carry[h] = (m, l + l_add, acc + acc_add)
        return tuple(carry)

    init = (jnp.full((1, t), NEG, F32), jnp.zeros((1, t), F32), jnp.zeros((MLA_V, t), F32))
    carry = lax.fori_loop(0, qi, lambda j, c: chunk(j, c, False), (init,) * nh)
    carry = own_chunk_bounded(carry) if bounded else chunk(qi, carry, True)
    out_t = jnp.concatenate([acc / l for (_, l, acc) in carry], axis=0)
    o_ref[0] = out_t.T.astype(BF16)


def _mla_attn(q, k, vt, t, nh, bounded):
    b, s, _ = q.shape
    return pl.pallas_call(
        functools.partial(_mla_attn_kernel, t=t, nh=nh, bounded=bounded),
        grid=(b, MLA_HEADS // nh, s // t),
        in_specs=[pl.BlockSpec((1, t, nh * LANES), lambda bi, g, qi: (bi, qi, g)),
                  pl.BlockSpec((1, s, nh * LANES), lambda bi, g, qi: (bi, 0, g)),
                  pl.BlockSpec((1, nh * MLA_V, s), lambda bi, g, qi: (bi, g, 0))],
        out_specs=pl.BlockSpec((1, t, nh * MLA_V), lambda bi, g, qi: (bi, qi, g)),
        out_shape=jax.ShapeDtypeStruct((b, s, MLA_HEADS * MLA_V), BF16),
        compiler_params=_params("arbitrary", "arbitrary", "arbitrary"),
        name="mla_attention",
    )(q, k, vt)


def _t5_bias_table(rel_bias):
    n = jnp.arange(MASKED_DIST)
    max_exact = REL_BUCKETS // 2
    nf = jnp.maximum(n, 1).astype(F32)
    large = max_exact + (jnp.log(nf / max_exact) / math.log(REL_MAX_DIST / max_exact)
                         * (REL_BUCKETS - max_exact)).astype(jnp.int32)
    large = jnp.minimum(large, REL_BUCKETS - 1)
    bucket = jnp.where(n < max_exact, n, large)
    table = jnp.concatenate([rel_bias.T[:, bucket] * LOG2E, jnp.full((MOBA_HEADS, 1), NEG, F32)], axis=1)
    return table.reshape(MOBA_HEADS // 2, 2, BIAS_TABLE)


def _pad_heads(w, width):
    k = w.shape[0]
    w3 = w.reshape(k, -1, width)
    return jnp.pad(w3, ((0, 0), (0, 0), (0, LANES - width))).reshape(k, -1)


def _pad_lanes(g):
    return jnp.pad(g, (0, LANES - g.shape[0])).reshape(1, LANES)


def _swap_rope_halves(a):
    half = MLA_ROPE // 2
    return jnp.concatenate([jnp.zeros_like(a[..., :MLA_NOPE]), a[..., MLA_NOPE + half:],
                            a[..., MLA_NOPE:MLA_NOPE + half]], axis=-1)


def _scores_bounded(q_gain, k_gain, dim, scale, extra=0.0):
    bound = dim * jnp.max(jnp.abs(q_gain)) * jnp.max(jnp.abs(k_gain)) * scale * 1.01 + extra
    return bound <= SCORE_BOUND_MAX


def kernel(x, c, positions, ada_w, ada_b, norm_g, ffn_w_gate, ffn_w_up, ffn_w_down, rel_bias,
           moba_w_qkv, moba_q_g, moba_k_g, moba_w_o, kv_ada_w, kv_ada_b, kv_norm_g, w_dkv,
           kv_a_norm_g, w_uk, w_uv, mla_k_g, mla_w_dq, mla_q_a_norm_g, mla_w_uq, mla_q_g, mla_w_o):
    b, s, d = x.shape
    depth = ada_w.shape[0]
    n_moba = moba_w_qkv.shape[0]
    tm_ffn = min(1024, s)
    tm_proj = min(512, s)

    c_pad = jnp.pad(c, ((0, SUBLANES - b), (0, 0)))
    mods = _mods(c_pad, ada_w, ada_b.reshape(depth, 1, -1), tn=1152)[:, :b]
    kv_mods = _mods(c_pad, kv_ada_w[None], kv_ada_b.reshape(1, 1, -1), tn=1024)[0, :b]

    def mod_vecs(m):
        return [v.reshape(b, 1, d) for v in jnp.split(m, m.shape[-1] // d, axis=-1)]

    pos_col = positions.reshape(b, s, 1)
    w_gate, w_up, w_down = (w.astype(BF16) for w in (ffn_w_gate, ffn_w_up, ffn_w_down))

    for layer in range(depth):
        sh1, sc1, g1, sh2, sc2, g2, sh3, sc3, g3 = mod_vecs(mods[layer])
        ng = norm_g[layer]

        def ffn(xin, idx, sh, sc, gt, mixer=None):
            return _ffn(xin, sh, sc, gt, ng[2 * idx:2 * idx + 1], w_gate, w_up, w_down, (layer, idx),
                        tm_ffn, 256, mixer)

        x = ffn(x, 0, sh1, sc1, g1)

        if layer < n_moba:
            n_qk = 2 * MOBA_HEADS * MOBA_HEAD_DIM
            q_scale = MOBA_HEAD_DIM ** -0.5 * LOG2E
            head_gain = jnp.concatenate([jnp.tile(moba_q_g[layer] * q_scale, MOBA_HEADS),
                                         jnp.tile(moba_k_g[layer], MOBA_HEADS)]).reshape(1, -1)
            w_qkv = moba_w_qkv[layer]
            qk, vt = _moba_proj(x, sh2, sc2, ng[1:2], w_qkv[:, :n_qk].astype(BF16),
                                w_qkv[:, n_qk:].T.astype(BF16), head_gain, tm_proj, 512)
            bias_table = _t5_bias_table(rel_bias)
            moba = functools.partial(_moba_attn, qk, vt, bias_table, positions, MOBA_TILE, ATTN_HEADS // 2)
            ok = _scores_bounded(moba_q_g[layer], moba_k_g[layer], MOBA_HEAD_DIM, q_scale,
                                 jnp.max(jnp.abs(rel_bias)) * LOG2E)
            o = lax.cond(ok, lambda: moba(True), lambda: moba(False))
            w_o = moba_w_o[layer]
        else:
            j = layer - n_moba
            if j == 0:
                half = MLA_ROPE // 2
                inv = ROPE_BASE ** (-jnp.arange(half, dtype=F32) / half)
                inv_lane = jnp.zeros((LANES,), F32).at[MLA_NOPE:MLA_QK].set(jnp.tile(inv, 2)).reshape(1, LANES)
                cos, sin = _rope_tables(pos_col, inv_lane, tm_proj)
                kv_sh, kv_sc = mod_vecs(kv_mods)
                w_kpe = w_dkv[:, MLA_KV_LORA:]
                w_kpe_swapped = jnp.concatenate([w_kpe[:, half:], w_kpe[:, :half]], axis=1)
                wdkv_p = jnp.pad(jnp.concatenate([w_dkv, w_kpe_swapped], axis=1),
                                 ((0, 0), (0, LANES - 2 * MLA_ROPE))).astype(BF16)
                shared_k, shared_v = _mla_kv(
                    x, kv_sh, kv_sc, kv_norm_g.reshape(1, d), wdkv_p, kv_a_norm_g.reshape(1, -1),
                    _pad_heads(w_uk, MLA_NOPE).astype(BF16), w_uv.T.astype(BF16), _pad_lanes(mla_k_g),
                    _pad_lanes(_swap_rope_halves(mla_k_g)), cos, sin, tm_proj)
            w_uq = mla_w_uq[j]
            w_uq_partner = _swap_rope_halves(w_uq.reshape(w_uq.shape[0], MLA_HEADS, MLA_QK)).reshape(w_uq.shape)
            q = _mla_q(x, sh2, sc2, ng[1:2], mla_w_dq[j].astype(BF16), mla_q_a_norm_g[j].reshape(1, -1),
                       _pad_heads(w_uq, MLA_QK).astype(BF16), _pad_heads(w_uq_partner, MLA_QK).astype(BF16),
                       _pad_lanes(mla_q_g[j]), _pad_lanes(_swap_rope_halves(mla_q_g[j])), cos, sin, tm_proj)
            mla = functools.partial(_mla_attn, q, shared_k, shared_v, MLA_TILE, ATTN_HEADS)
            ok = _scores_bounded(mla_q_g[j], mla_k_g, MLA_QK, MLA_QK ** -0.5 * LOG2E)
            o = lax.cond(ok, lambda: mla(True), lambda: mla(False))
            w_o = mla_w_o[j]

        x = ffn(x, 1, sh3, sc3, g3, mixer=(o, g2, w_o.astype(BF16)))
    return x
```

```python
import functools
import math

import jax
import jax.numpy as jnp
from jax import lax
from jax.experimental import pallas as pl
from jax.experimental.pallas import tpu as pltpu

F32 = jnp.float32
BF16 = jnp.bfloat16

LANES = 128
SUBLANES = 8
VMEM_LIMIT_BYTES = 56 * 1024 * 1024

MOBA_HEADS = 16
MOBA_HEAD_DIM = 64
MOBA_BLOCK = 256
MOBA_TOPK = 3
REL_BUCKETS = 32
REL_MAX_DIST = 128
MLA_HEADS = 16
MLA_KV_LORA = 256
MLA_NOPE = 64
MLA_ROPE = 32
MLA_V = 64
MLA_QK = MLA_NOPE + MLA_ROPE
ROPE_BASE = 10000.0
EPS = 1e-6
NEG = -1e30

MOBA_TILE = 512
MLA_TILE = 512
ATTN_HEADS = 8
LOG2E = math.log2(math.e)
BIAS_TABLE = LANES
MAX_DIST = BIAS_TABLE - 2
MASKED_DIST = BIAS_TABLE - 1
SCORE_BOUND_MAX = 60.0


def _params(*sem):
    return pltpu.CompilerParams(dimension_semantics=sem, vmem_limit_bytes=VMEM_LIMIT_BYTES)


def _dot(a, b):
    return jnp.dot(a, b, preferred_element_type=F32)


def _dot_nt(a, b):
    return lax.dot_general(a, b, (((1,), (1,)), ((), ())), preferred_element_type=F32)


def _rms_mod(x, g, shift, scale):
    ms = jnp.mean(x * x, axis=-1, keepdims=True)
    y = x * lax.rsqrt(ms + EPS) * g
    return y * (1.0 + scale) + shift


def _mods_kernel(c_ref, w_ref, b_ref, o_ref):
    c = c_ref[...]
    ca = c * jax.nn.sigmoid(c)
    w = w_ref[0]
    w_hi = w.astype(BF16)
    w_lo = (w - w_hi.astype(F32)).astype(BF16)
    ca_hi = ca.astype(BF16)
    ca_lo = (ca - ca_hi.astype(F32)).astype(BF16)
    o_ref[0] = _dot(ca_hi, w_hi) + _dot(ca_lo, w_hi) + _dot(ca_hi, w_lo) + b_ref[0]


def _mods(c_pad, w, b, tn):
    nl, d, n = w.shape
    return pl.pallas_call(
        _mods_kernel,
        grid=(nl, n // tn),
        in_specs=[pl.BlockSpec((SUBLANES, d), lambda l, j: (0, 0)),
                  pl.BlockSpec((1, d, tn), lambda l, j: (l, 0, j)),
                  pl.BlockSpec((1, 1, tn), lambda l, j: (l, 0, j))],
        out_specs=pl.BlockSpec((1, SUBLANES, tn), lambda l, j: (l, 0, j)),
        out_shape=jax.ShapeDtypeStruct((nl, SUBLANES, n), F32),
        compiler_params=_params("arbitrary", "arbitrary"),
        name="adaln_mods",
    )(c_pad, w, b)


def _ffn_kernel(*refs, tf, with_mixer):
    if with_mixer:
        x_ref, a_ref, gm_ref, wo_ref = refs[:4]
        refs = refs[4:]
        x = x_ref[0] + gm_ref[0] * _dot(a_ref[0], wo_ref[...])
    else:
        x_ref = refs[0]
        refs = refs[1:]
        x = x_ref[0]
    sh_ref, sc_ref, gt_ref, g_ref, wg_ref, wu_ref, wd_ref, o_ref = refs
    h = _rms_mod(x, g_ref[...], sh_ref[0], sc_ref[0]).astype(BF16)
    acc = None
    for f in range(wg_ref.shape[1] // tf):
        cols = slice(f * tf, (f + 1) * tf)
        g = _dot(h, wg_ref[:, cols])
        u = _dot(h, wu_ref[:, cols])
        a = (g * jax.nn.sigmoid(g) * u).astype(BF16)
        d = _dot(a, wd_ref[cols, :])
        acc = d if acc is None else acc + d
    o_ref[0] = x + (0.5 * gt_ref[0]) * acc


def _ffn(x, shift, scale, gate, g, wg, wu, wd, which, tm, tf, mixer=None):
    b, s, d = x.shape
    f = wg.shape[-1]
    row = pl.BlockSpec((1, tm, d), lambda bi, i: (bi, i, 0))
    vec = pl.BlockSpec((1, 1, d), lambda bi, i: (bi, 0, 0))
    full = lambda shp: pl.BlockSpec(shp, lambda bi, i: (0, 0), pipeline_mode=pl.Buffered(1))
    picked = lambda r, c: pl.BlockSpec((None, None, r, c), lambda bi, i: (*which, 0, 0),
                                       pipeline_mode=pl.Buffered(1))
    ins, specs = [x], [row]
    if mixer is not None:
        a, gate_mix, w_o = mixer
        k = a.shape[2]
        ins += [a, gate_mix, w_o]
        specs += [pl.BlockSpec((1, tm, k), lambda bi, i: (bi, i, 0)), vec, full((k, d))]
    ins += [shift, scale, gate, g, wg, wu, wd]
    specs += [vec, vec, vec, full((1, d)), picked(d, f), picked(d, f), picked(f, d)]
    return pl.pallas_call(
        functools.partial(_ffn_kernel, tf=tf, with_mixer=mixer is not None),
        grid=(b, s // tm),
        in_specs=specs,
        out_specs=row,
        out_shape=jax.ShapeDtypeStruct((b, s, d), F32),
        compiler_params=_params("arbitrary", "arbitrary"),
        name="ffn_swiglu",
    )(*ins)


def _headnorm_pair(y, g, dim):
    lane = lax.broadcasted_iota(jnp.int32, (1, LANES), 1)
    left = lane < dim
    y2 = y * y
    ss_a = jnp.sum(jnp.where(left, y2, 0.0), axis=-1, keepdims=True)
    ss_b = jnp.sum(jnp.where(left, 0.0, y2), axis=-1, keepdims=True)
    inv = jnp.where(left, lax.rsqrt(ss_a * (1.0 / dim) + EPS), lax.rsqrt(ss_b * (1.0 / dim) + EPS))
    return y * inv * g


def _moba_proj_kernel(x_ref, sh_ref, sc_ref, g_ref, w_ref, wvt_ref, hg_ref, qk_ref, vt_ref, *, tn):
    h = _rms_mod(x_ref[0], g_ref[...], sh_ref[0], sc_ref[0]).astype(BF16)
    for n in range(w_ref.shape[1] // tn):
        res = _dot(h, w_ref[:, n * tn:(n + 1) * tn])
        for s in range(tn // LANES):
            sl = slice(n * tn + s * LANES, n * tn + (s + 1) * LANES)
            qk_ref[0, :, sl] = _headnorm_pair(res[:, s * LANES:(s + 1) * LANES], hg_ref[:, sl],
                                              MOBA_HEAD_DIM).astype(BF16)
    for n in range(wvt_ref.shape[0] // tn):
        rows = slice(n * tn, (n + 1) * tn)
        vt_ref[0, rows, :] = _dot_nt(wvt_ref[rows, :], h).astype(BF16)


def _moba_proj(x, shift, scale, g, w_qk, w_vt, head_gain, tm, tn):
    b, s, d = x.shape
    nqk = w_qk.shape[1]
    nv = w_vt.shape[0]
    vec = pl.BlockSpec((1, 1, d), lambda bi, i: (bi, 0, 0))
    full = lambda shp: pl.BlockSpec(shp, lambda bi, i: (0, 0))
    return pl.pallas_call(
        functools.partial(_moba_proj_kernel, tn=tn),
        grid=(b, s // tm),
        in_specs=[pl.BlockSpec((1, tm, d), lambda bi, i: (bi, i, 0)), vec, vec, full((1, d)),
                  full((d, nqk)), full((nv, d)), full((1, nqk))],
        out_specs=[pl.BlockSpec((1, tm, nqk), lambda bi, i: (bi, i, 0)),
                   pl.BlockSpec((1, nv, tm), lambda bi, i: (bi, 0, i))],
        out_shape=[jax.ShapeDtypeStruct((b, s, nqk), BF16), jax.ShapeDtypeStruct((b, nv, s), BF16)],
        compiler_params=_params("arbitrary", "arbitrary"),
        name="moba_qkv_proj",
    )(x, shift, scale, g, w_qk, w_vt, head_gain)


def _softmax_step_t(s, m, l, acc, vt, bounded):
    if bounded:
        p = jnp.exp2(s)
        return m, l + jnp.sum(p, axis=0, keepdims=True), acc + _dot(vt, p.astype(BF16))
    m_new = jnp.maximum(m, jnp.max(s, axis=0, keepdims=True))
    alpha = jnp.exp2(m - m_new)
    p = jnp.exp2(s - m_new)
    l_new = alpha * l + jnp.sum(p, axis=0, keepdims=True)
    acc_new = acc * alpha + _dot(vt, p.astype(BF16))
    return m_new, l_new, acc_new


def _moba_attn_kernel(qmin_ref, kmax_ref, qgmin_ref, kbmax_ref, q_ref, k_ref, vt_ref, bt_ref, pq_ref, pk_ref, o_ref,
                      km_scr, sel_scr, *, nb, t, npairs, bounded):
    blk = MOBA_BLOCK
    hd = MOBA_HEAD_DIM
    nsub = t // blk
    nh = 2 * npairs
    bi = pl.program_id(0)
    qi = pl.program_id(2)
    lane = lax.broadcasted_iota(jnp.int32, (1, LANES), 1)
    left = lane < hd

    @pl.when(qi == 0)
    def _():
        km_scr[...] = jnp.zeros_like(km_scr)
        for r in range(nb):
            kblk = k_ref[0, r * blk:(r + 1) * blk, :].astype(F32)
            km_scr[r:r + 1, :] = jnp.sum(kblk, axis=0, keepdims=True) * (1.0 / blk)

    q_heads = []
    for p in range(npairs):
        q2 = q_ref[0, :, p * LANES:(p + 1) * LANES]
        zero = jnp.zeros_like(q2)
        q_heads += [jnp.where(left, q2, zero), jnp.where(left, zero, q2)]

    col = lax.broadcasted_iota(jnp.int32, (1, t), 1)
    q_idx = qi * t + col
    q_sub = sum((col >= r * blk).astype(jnp.int32) for r in range(1, nsub))
    q_blk = qi * nsub + q_sub
    q_blk_start = q_blk * blk

    blk_row = lax.broadcasted_iota(jnp.int32, (nb, t), 0)
    for h in range(nh):
        km = km_scr[:, (h // 2) * LANES:(h // 2 + 1) * LANES]
        km_hi = km.astype(BF16)
        km_lo = (km - km_hi.astype(F32)).astype(BF16)
        gate = (_dot_nt(km_hi, q_heads[h]) + _dot_nt(km_lo, q_heads[h]))[:nb]
        gate = jnp.where(blk_row < q_blk, gate, -jnp.inf)
        sel = jnp.zeros((nb, t), F32)
        for _ in range(MOBA_TOPK):
            mx = jnp.max(gate, axis=0, keepdims=True)
            idx = jnp.min(jnp.where(gate == mx, blk_row, nb), axis=0, keepdims=True)
            pick = (blk_row == idx) & (mx > -jnp.inf)
            sel = jnp.where(pick, 1.0, sel)
            gate = jnp.where(pick, -jnp.inf, gate)
        sel_scr[h] = sel

    pq = pq_ref[0, 0]

    def gather_bias(h, dist):
        table = jnp.broadcast_to(bt_ref[h // 2, h % 2:h % 2 + 1, :], (dist.shape[0], LANES))
        return jnp.concatenate(
            [jnp.take_along_axis(table, dist[:, g * LANES:(g + 1) * LANES], axis=1, mode="promise_in_bounds")
             for g in range(dist.shape[1] // LANES)], axis=1)

    def far_bias(h):
        return bt_ref[h // 2, h % 2:h % 2 + 1, MAX_DIST:MAX_DIST + 1]

    def chosen_rows(h, j, lanes=slice(None)):
        return [sel_scr[h, pl.ds(j * nsub + r, 1), lanes] > 0.0 for r in range(nsub)]

    def own_chunk_bounded(carry):
        assert nsub == 2
        carry = list(carry)
        start = pl.multiple_of(qi * t, t)
        k2 = k_ref[0, pl.ds(start, t), :]
        vt = vt_ref[0, :, pl.ds(start, t)]
        pk0 = pk_ref[0, pl.ds(start, blk), :]
        pk1 = pk_ref[0, pl.ds(start + blk, blk), :]
        causal = (lax.broadcasted_iota(jnp.int32, (blk, blk), 0) <= lax.broadcasted_iota(jnp.int32, (blk, blk), 1))
        d00 = jnp.where(causal, jnp.clip(pq[:, :blk] - pk0, 0, MAX_DIST), MASKED_DIST)
        d01 = jnp.clip(pq[:, blk:] - pk0, 0, MAX_DIST)
        d11 = jnp.where(causal, jnp.clip(pq[:, blk:] - pk1, 0, MAX_DIST), MASKED_DIST)
        quads = []
        for h in range(nh):
            kh = k2[:, (h // 2) * LANES:(h // 2 + 1) * LANES]
            qh = q_heads[h]
            quads.append((_dot_nt(kh[:blk], qh[:blk]), _dot_nt(kh[:blk], qh[blk:]), _dot_nt(kh[blk:], qh[blk:])))
        for h in range(nh):
            m, l, acc = carry[h]
            s00, s01, s11 = quads[h]
            e00 = jnp.exp2(s00 + gather_bias(h, d00))
            e01 = jnp.exp2(s01 + gather_bias(h, d01))
            e11 = jnp.exp2(s11 + gather_bias(h, d11))
            w01 = jnp.where(chosen_rows(h, qi, slice(blk, t))[0], 1.0, 0.0)
            vt0 = vt[h * hd:(h + 1) * hd, :blk]
            vt1 = vt[h * hd:(h + 1) * hd, blk:]
            l_add = jnp.concatenate(
                [jnp.sum(e00, axis=0, keepdims=True),
                 jnp.sum(e01, axis=0, keepdims=True) * w01 + jnp.sum(e11, axis=0, keepdims=True)], axis=1)
            acc_add = jnp.concatenate(
                [_dot(vt0, e00.astype(BF16)),
                 _dot(vt0, e01.astype(BF16)) * w01 + _dot(vt1, e11.astype(BF16))], axis=1)
            carry[h] = (m, l + l_add, acc + acc_add)
        return tuple(carry)

    def chunk(j, carry, mode):
        carry = list(carry)
        start = pl.multiple_of(j * t, t)
        k2 = k_ref[0, pl.ds(start, t), :]
        vt = vt_ref[0, :, pl.ds(start, t)]
        s_heads = [_dot_nt(k2[:, (h // 2) * LANES:(h // 2 + 1) * LANES], q_heads[h]) for h in range(nh)]
        if mode in ("near", "diag"):
            dist = jnp.clip(pq - pk_ref[0, pl.ds(start, t), :], 0, MAX_DIST)
        if mode == "corner":
            dist_c = jnp.clip(pq[:, :LANES] - pk_ref[0, pl.ds(start + (nsub - 1) * blk, blk), :], 0, MAX_DIST)
        if mode == "diag":
            key_idx = start + lax.broadcasted_iota(jnp.int32, (blk, t), 0)
        for h in range(nh):
            m, l, acc = carry[h]
            s = s_heads[h]
            vt_h = vt[h * hd:(h + 1) * hd, :]
            chosen = chosen_rows(h, j)
            if mode in ("far", "corner") and bounded:
                two_c = jnp.exp2(far_bias(h))
                for r in range(nsub):
                    piece = s[r * blk:(r + 1) * blk]
                    w = jnp.where(chosen[r], two_c, 0.0)
                    if mode == "corner" and r == nsub - 1:
                        piece = jnp.concatenate([piece[:, :LANES] + gather_bias(h, dist_c), piece[:, LANES:]], axis=1)
                        w = jnp.where(col < LANES, jnp.where(chosen[r], 1.0, 0.0), w)
                    p = jnp.exp2(piece)
                    l = l + jnp.sum(p, axis=0, keepdims=True) * w
                    acc = acc + _dot(vt_h[:, r * blk:(r + 1) * blk], p.astype(BF16)) * w
                carry[h] = (m, l, acc)
            elif mode == "far":
                pieces = [s[r * blk:(r + 1) * blk] for r in range(nsub)]
                c = far_bias(h)
                tmax = [jnp.where(chosen[r], jnp.max(pieces[r], axis=0, keepdims=True) + c, NEG)
                        for r in range(nsub)]
                m_new = functools.reduce(jnp.maximum, tmax, m)
                alpha = jnp.exp2(m - m_new)
                ps = [jnp.exp2(pieces[r] - jnp.where(chosen[r], m_new - c, jnp.inf)) for r in range(nsub)]
                l_new = alpha * l + sum(jnp.sum(p, axis=0, keepdims=True) for p in ps)
                p = jnp.concatenate(ps, axis=0) if nsub > 1 else ps[0]
                carry[h] = (m_new, l_new, acc * alpha + _dot(vt_h, p.astype(BF16)))
            else:
                s = s + gather_bias(h, dist)
                masked = []
                for r in range(nsub):
                    piece = s[r * blk:(r + 1) * blk]
                    if mode == "diag":
                        kidx = key_idx + r * blk
                        keep_own = jnp.where(kidx <= q_idx, jnp.where(kidx >= q_blk_start, piece, NEG), NEG)
                        masked.append(jnp.where(chosen[r], piece, keep_own))
                    else:
                        masked.append(jnp.where(chosen[r], piece, NEG))
                s = jnp.concatenate(masked, axis=0) if nsub > 1 else masked[0]
                carry[h] = _softmax_step_t(s, m, l, acc, vt_h, bounded)
        return tuple(carry)

    def past_chunk(j, carry):
        q_lo = qmin_ref[bi, qi]
        far = q_lo - kmax_ref[bi, j] >= MAX_DIST
        if not bounded:
            return lax.cond(far, lambda c: chunk(j, c, "far"), lambda c: chunk(j, c, "near"), carry)
        groups = t // LANES
        q_rest = functools.reduce(jnp.minimum, [qgmin_ref[bi, qi * groups + g] for g in range(1, groups)])
        k_first = functools.reduce(jnp.maximum, [kbmax_ref[bi, j * nsub + r] for r in range(nsub - 1)])
        corner = (q_lo - k_first >= MAX_DIST) & (q_rest - kbmax_ref[bi, j * nsub + nsub - 1] >= MAX_DIST)
        return lax.cond(far, lambda c: chunk(j, c, "far"),
                        lambda c: lax.cond(corner, lambda c2: chunk(j, c2, "corner"),
                                           lambda c2: chunk(j, c2, "near"), c), carry)

    init = (jnp.full((1, t), NEG, F32), jnp.zeros((1, t), F32), jnp.zeros((hd, t), F32))
    if bounded:
        carry = lax.fori_loop(0, qi, past_chunk, own_chunk_bounded((init,) * nh))
    else:
        carry = lax.fori_loop(0, qi, past_chunk, (init,) * nh)
        carry = chunk(qi, carry, "diag")
    out_t = jnp.concatenate([acc / l for (_, l, acc) in carry], axis=0)
    o_ref[0] = out_t.T.astype(BF16)


def _moba_position_tables(positions, t):
    b, s = positions.shape
    pos_tiles = positions.reshape(b, s // t, t)
    return (jnp.min(pos_tiles, axis=-1), jnp.max(pos_tiles, axis=-1),
            jnp.min(positions.reshape(b, s // LANES, LANES), axis=-1),
            jnp.max(positions.reshape(b, s // MOBA_BLOCK, MOBA_BLOCK), axis=-1),
            positions.reshape(b, s // t, 1, t), positions.reshape(b, s, 1))


def _moba_attn(qk, vt, bias_table, pos_tables, t, npairs, bounded):
    b, s, _ = qk.shape
    nb = s // MOBA_BLOCK
    ngroups = MOBA_HEADS * MOBA_HEAD_DIM // (LANES * npairs)
    w = npairs * LANES
    q_min, k_max, qg_min, kb_max, pos_row, pos_col = pos_tables
    grid_spec = pltpu.PrefetchScalarGridSpec(
        num_scalar_prefetch=4,
        grid=(b, ngroups, s // t),
        in_specs=[pl.BlockSpec((1, t, w), lambda bi, g, qi, *_: (bi, qi, g)),
                  pl.BlockSpec((1, s, w), lambda bi, g, qi, *_: (bi, 0, ngroups + g)),
                  pl.BlockSpec((1, w, s), lambda bi, g, qi, *_: (bi, g, 0)),
                  pl.BlockSpec((npairs, 2, LANES), lambda bi, g, qi, *_: (g, 0, 0)),
                  pl.BlockSpec((1, 1, 1, t), lambda bi, g, qi, *_: (bi, qi, 0, 0)),
                  pl.BlockSpec((1, s, 1), lambda bi, g, qi, *_: (bi, 0, 0))],
        out_specs=pl.BlockSpec((1, t, w), lambda bi, g, qi, *_: (bi, qi, g)),
        scratch_shapes=[pltpu.VMEM((LANES, w), F32), pltpu.VMEM((2 * npairs, nb, t), F32)],
    )
    return pl.pallas_call(
        functools.partial(_moba_attn_kernel, nb=nb, t=t, npairs=npairs, bounded=bounded),
        grid_spec=grid_spec,
        out_shape=jax.ShapeDtypeStruct((b, s, MOBA_HEADS * MOBA_HEAD_DIM), BF16),
        compiler_params=_params("arbitrary", "arbitrary", "arbitrary"),
        name="moba_attention",
    )(q_min, k_max, qg_min, kb_max, qk, qk, vt, bias_table, pos_row, pos_col)


def _rope_tab_kernel(pos_ref, inv_ref, c_ref, s_ref):
    half = MLA_ROPE // 2
    ang = pos_ref[0].astype(F32) * inv_ref[...]
    lane = lax.broadcasted_iota(jnp.int32, (1, LANES), 1)
    sn = jnp.sin(ang)
    c_ref[0] = jnp.cos(ang)
    s_ref[0] = jnp.where(lane < MLA_NOPE + half, -sn, sn)


def _rope_tables(pos_col, inv_lane, tm):
    b, s, _ = pos_col.shape
    out = pl.BlockSpec((1, tm, LANES), lambda bi, i: (bi, i, 0))
    shape = jax.ShapeDtypeStruct((b, s, LANES), F32)
    return pl.pallas_call(
        _rope_tab_kernel,
        grid=(b, s // tm),
        in_specs=[pl.BlockSpec((1, tm, 1), lambda bi, i: (bi, i, 0)),
                  pl.BlockSpec((1, LANES), lambda bi, i: (0, 0))],
        out_specs=[out, out],
        out_shape=[shape, shape],
        compiler_params=_params("arbitrary", "arbitrary"),
        name="rope_tables",
    )(pos_col, inv_lane)


def _mla_q_kernel(x_ref, sh_ref, sc_ref, g_ref, wdq_ref, qag_ref, wuq_ref, wuqp_ref, qg_ref, qgp_ref,
                  c_ref, s_ref, o_ref):
    h = _rms_mod(x_ref[0], g_ref[...], sh_ref[0], sc_ref[0]).astype(BF16)
    dq = _dot(h, wdq_ref[...])
    dqn = (dq * lax.rsqrt(jnp.mean(dq * dq, axis=-1, keepdims=True) + EPS) * qag_ref[...]).astype(BF16)
    uq = _dot(dqn, wuq_ref[...])
    up = _dot(dqn, wuqp_ref[...])
    cg = c_ref[0] * qg_ref[...]
    sg = s_ref[0] * qgp_ref[...]
    scale = MLA_QK ** -0.5 * LOG2E
    for hd in range(MLA_HEADS):
        sl = slice(hd * LANES, (hd + 1) * LANES)
        u = uq[:, sl]
        inv = lax.rsqrt(jnp.sum(u * u, axis=-1, keepdims=True) * (1.0 / MLA_QK) + EPS) * scale
        o_ref[0, :, sl] = ((u * cg + up[:, sl] * sg) * inv).astype(BF16)


def _mla_q(x, shift, scale, g, wdq, qag, wuq_p, wuq_partner, qg_p, qg_partner, cos, sin, tm):
    b, s, d = x.shape
    r = wdq.shape[1]
    nq = wuq_p.shape[1]
    vec = pl.BlockSpec((1, 1, d), lambda bi, i: (bi, 0, 0))
    tab = pl.BlockSpec((1, tm, LANES), lambda bi, i: (bi, i, 0))
    full = lambda shp: pl.BlockSpec(shp, lambda bi, i: (0, 0))
    return pl.pallas_call(
        _mla_q_kernel,
        grid=(b, s // tm),
        in_specs=[pl.BlockSpec((1, tm, d), lambda bi, i: (bi, i, 0)), vec, vec, full((1, d)),
                  full((d, r)), full((1, r)), full((r, nq)), full((r, nq)), full((1, LANES)), full((1, LANES)),
                  tab, tab],
        out_specs=pl.BlockSpec((1, tm, nq), lambda bi, i: (bi, i, 0)),
        out_shape=jax.ShapeDtypeStruct((b, s, nq), BF16),
        compiler_params=_params("arbitrary", "arbitrary"),
        name="mla_q_path",
    )(x, shift, scale, g, wdq, qag, wuq_p, wuq_partner, qg_p, qg_partner, cos, sin)


def _mla_kv_kernel(x_ref, sh_ref, sc_ref, g_ref, wdkv_ref, kvg_ref, wuk_ref, wuvt_ref, kg_ref, kgp_ref,
                   c_ref, s_ref, k_ref, vt_ref):
    hs = _rms_mod(x_ref[0], g_ref[...], sh_ref[0], sc_ref[0]).astype(BF16)
    ck = _dot(hs, wdkv_ref[...])
    ckv = ck[:, :MLA_KV_LORA]
    ckvn = (ckv * lax.rsqrt(jnp.mean(ckv * ckv, axis=-1, keepdims=True) + EPS) * kvg_ref[...]).astype(BF16)
    kn = _dot(ckvn, wuk_ref[...])
    vt_ref[0] = _dot_nt(wuvt_ref[...], ckvn).astype(BF16)
    lane = lax.broadcasted_iota(jnp.int32, (1, LANES), 1)
    on_rope = (lane >= MLA_NOPE) & (lane < MLA_QK)
    tail = ck[:, MLA_KV_LORA:MLA_KV_LORA + LANES]
    kpe = jnp.where(on_rope, pltpu.roll(tail, MLA_NOPE, axis=1), 0.0)
    partner = pltpu.roll(tail, MLA_ROPE, axis=1)
    cg = c_ref[0] * kg_ref[...]
    sg = s_ref[0] * kgp_ref[...]
    rot = kpe * cg + partner * sg
    ss_pe = jnp.sum(kpe * kpe, axis=-1, keepdims=True)
    for hd in range(MLA_HEADS):
        sl = slice(hd * LANES, (hd + 1) * LANES)
        nope = kn[:, sl]
        inv = lax.rsqrt((jnp.sum(nope * nope, axis=-1, keepdims=True) + ss_pe) * (1.0 / MLA_QK) + EPS)
        k_ref[0, :, sl] = ((nope * cg + rot) * inv).astype(BF16)


def _mla_kv(x, shift, scale, g, wdkv_p, kvg, wuk_p, wuv_t, kg_p, kg_partner, cos, sin, tm):
    b, s, d = x.shape
    r = wdkv_p.shape[1]
    nk = wuk_p.shape[1]
    nv = wuv_t.shape[0]
    vec = pl.BlockSpec((1, 1, d), lambda bi, i: (bi, 0, 0))
    tab = pl.BlockSpec((1, tm, LANES), lambda bi, i: (bi, i, 0))
    full = lambda shp: pl.BlockSpec(shp, lambda bi, i: (0, 0))
    return pl.pallas_call(
        _mla_kv_kernel,
        grid=(b, s // tm),
        in_specs=[pl.BlockSpec((1, tm, d), lambda bi, i: (bi, i, 0)), vec, vec, full((1, d)),
                  full((d, r)), full((1, MLA_KV_LORA)), full((MLA_KV_LORA, nk)), full((nv, MLA_KV_LORA)),
                  full((1, LANES)), full((1, LANES)), tab, tab],
        out_specs=[pl.BlockSpec((1, tm, nk), lambda bi, i: (bi, i, 0)),
                   pl.BlockSpec((1, nv, tm), lambda bi, i: (bi, 0, i))],
        out_shape=[jax.ShapeDtypeStruct((b, s, nk), BF16), jax.ShapeDtypeStruct((b, nv, s), BF16)],
        compiler_params=_params("arbitrary", "arbitrary"),
        name="mla_shared_kv",
    )(x, shift, scale, g, wdkv_p, kvg, wuk_p, wuv_t, kg_p, kg_partner, cos, sin)


def _mla_attn_kernel(q_ref, k_ref, vt_ref, o_ref, *, t, nh, bounded):
    qi = pl.program_id(2)
    q2 = q_ref[0]
    q_heads = [q2[:, h * LANES:(h + 1) * LANES] for h in range(nh)]

    def chunk(j, carry, masked):
        carry = list(carry)
        start = pl.multiple_of(j * t, t)
        k2 = k_ref[0, pl.ds(start, t), :]
        vt = vt_ref[0, :, pl.ds(start, t)]
        s_heads = [_dot_nt(k2[:, h * LANES:(h + 1) * LANES], q_heads[h]) for h in range(nh)]
        if masked:
            row = lax.broadcasted_iota(jnp.int32, (t, t), 0)
            col = lax.broadcasted_iota(jnp.int32, (t, t), 1)
            allowed = row <= col
        for h in range(nh):
            m, l, acc = carry[h]
            s = jnp.where(allowed, s_heads[h], NEG) if masked else s_heads[h]
            carry[h] = _softmax_step_t(s, m, l, acc, vt[h * MLA_V:(h + 1) * MLA_V, :], bounded)
        return tuple(carry)

    def own_chunk_bounded(carry):
        carry = list(carry)
        half = t // 2
        start = pl.multiple_of(qi * t, t)
        k2 = k_ref[0, pl.ds(start, t), :]
        vt = vt_ref[0, :, pl.ds(start, t)]
        causal = (lax.broadcasted_iota(jnp.int32, (half, half), 0) <= lax.broadcasted_iota(jnp.int32, (half, half), 1))
        quads = []
        for h in range(nh):
            kh = k2[:, h * LANES:(h + 1) * LANES]
            qh = q_heads[h]
            quads.append((_dot_nt(kh[:half], qh[:half]), _dot_nt(kh[:half], qh[half:]), _dot_nt(kh[half:], qh[half:])))
        for h in range(nh):
            m, l, acc = carry[h]
            s00, s01, s11 = quads[h]
            e00 = jnp.exp2(jnp.where(causal, s00, NEG))
            e01 = jnp.exp2(s01)
            e11 = jnp.exp2(jnp.where(causal, s11, NEG))
            vt0 = vt[h * MLA_V:(h + 1) * MLA_V, :half]
            vt1 = vt[h * MLA_V:(h + 1) * MLA_V, half:]
            l_add = jnp.concatenate(
                [jnp.sum(e00, axis=0, keepdims=True),
                 jnp.sum(e01, axis=0, keepdims=True) + jnp.sum(e11, axis=0, keepdims=True)], axis=1)
            acc_add = jnp.concatenate(
                [_dot(vt0, e00.astype(BF16)), _dot(vt0, e01.astype(BF16)) + _dot(vt1, e11.astype(BF16))], axis=1)
            carry[h] = (m, l + l_add, acc + acc_add)
        return tuple(carry)

    init = (jnp.full((1, t), NEG, F32), jnp.zeros((1, t), F32), jnp.zeros((MLA_V, t), F32))
    carry = lax.fori_loop(0, qi, lambda j, c: chunk(j, c, False), (init,) * nh)
    carry = own_chunk_bounded(carry) if bounded else chunk(qi, carry, True)
    out_t = jnp.concatenate([acc / l for (_, l, acc) in carry], axis=0)
    o_ref[0] = out_t.T.astype(BF16)


def _mla_attn(q, k, vt, t, nh, bounded):
    b, s, _ = q.shape
    return pl.pallas_call(
        functools.partial(_mla_attn_kernel, t=t, nh=nh, bounded=bounded),
        grid=(b, MLA_HEADS // nh, s // t),
        in_specs=[pl.BlockSpec((1, t, nh * LANES), lambda bi, g, qi: (bi, qi, g)),
                  pl.BlockSpec((1, s, nh * LANES), lambda bi, g, qi: (bi, 0, g)),
                  pl.BlockSpec((1, nh * MLA_V, s), lambda bi, g, qi: (bi, g, 0))],
        out_specs=pl.BlockSpec((1, t, nh * MLA_V), lambda bi, g, qi: (bi, qi, g)),
        out_shape=jax.ShapeDtypeStruct((b, s, MLA_HEADS * MLA_V), BF16),
        compiler_params=_params("arbitrary", "arbitrary", "arbitrary"),
        name="mla_attention",
    )(q, k, vt)


def _t5_bias_table(rel_bias):
    n = jnp.arange(MASKED_DIST)
    max_exact = REL_BUCKETS // 2
    nf = jnp.maximum(n, 1).astype(F32)
    large = max_exact + (jnp.log(nf / max_exact) / math.log(REL_MAX_DIST / max_exact)
                         * (REL_BUCKETS - max_exact)).astype(jnp.int32)
    large = jnp.minimum(large, REL_BUCKETS - 1)
    bucket = jnp.where(n < max_exact, n, large)
    table = jnp.concatenate([rel_bias.T[:, bucket] * LOG2E, jnp.full((MOBA_HEADS, 1), NEG, F32)], axis=1)
    return table.reshape(MOBA_HEADS // 2, 2, BIAS_TABLE)


def _pad_heads(w, width):
    k = w.shape[0]
    w3 = w.reshape(k, -1, width)
    return jnp.pad(w3, ((0, 0), (0, 0), (0, LANES - width))).reshape(k, -1)


def _pad_lanes(g):
    return jnp.pad(g, (0, LANES - g.shape[0])).reshape(1, LANES)


def _swap_rope_halves(a):
    half = MLA_ROPE // 2
    return jnp.concatenate([jnp.zeros_like(a[..., :MLA_NOPE]), a[..., MLA_NOPE + half:],
                            a[..., MLA_NOPE:MLA_NOPE + half]], axis=-1)


def _scores_bounded(q_gain, k_gain, dim, scale, extra=0.0):
    bound = dim * jnp.max(jnp.abs(q_gain)) * jnp.max(jnp.abs(k_gain)) * scale * 1.01 + extra
    return bound <= SCORE_BOUND_MAX


def kernel(x, c, positions, ada_w, ada_b, norm_g, ffn_w_gate, ffn_w_up, ffn_w_down, rel_bias,
           moba_w_qkv, moba_q_g, moba_k_g, moba_w_o, kv_ada_w, kv_ada_b, kv_norm_g, w_dkv,
           kv_a_norm_g, w_uk, w_uv, mla_k_g, mla_w_dq, mla_q_a_norm_g, mla_w_uq, mla_q_g, mla_w_o):
    b, s, d = x.shape
    depth = ada_w.shape[0]
    n_moba = moba_w_qkv.shape[0]
    tm_ffn = min(1024, s)
    tm_proj = min(512, s)

    c_pad = jnp.pad(c, ((0, SUBLANES - b), (0, 0)))
    mods = _mods(c_pad, ada_w, ada_b.reshape(depth, 1, -1), tn=1152)[:, :b]
    kv_mods = _mods(c_pad, kv_ada_w[None], kv_ada_b.reshape(1, 1, -1), tn=1024)[0, :b]

    def mod_vecs(m):
        return [v.reshape(b, 1, d) for v in jnp.split(m, m.shape[-1] // d, axis=-1)]

    pos_col = positions.reshape(b, s, 1)
    w_gate, w_up, w_down = (w.astype(BF16) for w in (ffn_w_gate, ffn_w_up, ffn_w_down))

    for layer in range(depth):
        sh1, sc1, g1, sh2, sc2, g2, sh3, sc3, g3 = mod_vecs(mods[layer])
        ng = norm_g[layer]

        def ffn(xin, idx, sh, sc, gt, mixer=None):
            return _ffn(xin, sh, sc, gt, ng[2 * idx:2 * idx + 1], w_gate, w_up, w_down, (layer, idx),
                        tm_ffn, 256, mixer)

        x = ffn(x, 0, sh1, sc1, g1)

        if layer < n_moba:
            n_qk = 2 * MOBA_HEADS * MOBA_HEAD_DIM
            q_scale = MOBA_HEAD_DIM ** -0.5 * LOG2E
            head_gain = jnp.concatenate([jnp.tile(moba_q_g[layer] * q_scale, MOBA_HEADS),
                                         jnp.tile(moba_k_g[layer], MOBA_HEADS)]).reshape(1, -1)
            w_qkv = moba_w_qkv[layer]
            qk, vt = _moba_proj(x, sh2, sc2, ng[1:2], w_qkv[:, :n_qk].astype(BF16),
                                w_qkv[:, n_qk:].T.astype(BF16), head_gain, tm_proj, 512)
            bias_table = _t5_bias_table(rel_bias)
            moba = functools.partial(_moba_attn, qk, vt, bias_table, _moba_position_tables(positions, MOBA_TILE),
                                     MOBA_TILE, ATTN_HEADS // 2)
            ok = _scores_bounded(moba_q_g[layer], moba_k_g[layer], MOBA_HEAD_DIM, q_scale,
                                 jnp.max(jnp.abs(rel_bias)) * LOG2E)
            o = lax.cond(ok, lambda: moba(True), lambda: moba(False))
            w_o = moba_w_o[layer]
        else:
            j = layer - n_moba
            if j == 0:
                half = MLA_ROPE // 2
                inv = ROPE_BASE ** (-jnp.arange(half, dtype=F32) / half)
                inv_lane = jnp.zeros((LANES,), F32).at[MLA_NOPE:MLA_QK].set(jnp.tile(inv, 2)).reshape(1, LANES)
                cos, sin = _rope_tables(pos_col, inv_lane, tm_proj)
                kv_sh, kv_sc = mod_vecs(kv_mods)
                w_kpe = w_dkv[:, MLA_KV_LORA:]
                w_kpe_swapped = jnp.concatenate([w_kpe[:, half:], w_kpe[:, :half]], axis=1)
                wdkv_p = jnp.pad(jnp.concatenate([w_dkv, w_kpe_swapped], axis=1),
                                 ((0, 0), (0, LANES - 2 * MLA_ROPE))).astype(BF16)
                shared_k, shared_v = _mla_kv(
                    x, kv_sh, kv_sc, kv_norm_g.reshape(1, d), wdkv_p, kv_a_norm_g.reshape(1, -1),
                    _pad_heads(w_uk, MLA_NOPE).astype(BF16), w_uv.T.astype(BF16), _pad_lanes(mla_k_g),
                    _pad_lanes(_swap_rope_halves(mla_k_g)), cos, sin, tm_proj)
            w_uq = mla_w_uq[j]
            w_uq_partner = _swap_rope_halves(w_uq.reshape(w_uq.shape[0], MLA_HEADS, MLA_QK)).reshape(w_uq.shape)
            q = _mla_q(x, sh2, sc2, ng[1:2], mla_w_dq[j].astype(BF16), mla_q_a_norm_g[j].reshape(1, -1),
                       _pad_heads(w_uq, MLA_QK).astype(BF16), _pad_heads(w_uq_partner, MLA_QK).astype(BF16),
                       _pad_lanes(mla_q_g[j]), _pad_lanes(_swap_rope_halves(mla_q_g[j])), cos, sin, tm_proj)
            mla = functools.partial(_mla_attn, q, shared_k, shared_v, MLA_TILE, ATTN_HEADS)
            ok = _scores_bounded(mla_q_g[j], mla_k_g, MLA_QK, MLA_QK ** -0.5 * LOG2E)
            o = lax.cond(ok, lambda: mla(True), lambda: mla(False))
            w_o = mla_w_o[j]

        x = ffn(x, 1, sh3, sc3, g3, mixer=(o, g2, w_o.astype(BF16)))
    return x
```

```python
import functools
import math

import jax
import jax.numpy as jnp
from jax import lax
from jax.experimental import pallas as pl
from jax.experimental.pallas import tpu as pltpu

F32 = jnp.float32
BF16 = jnp.bfloat16

LANES = 128
SUBLANES = 8
VMEM_LIMIT_BYTES = 56 * 1024 * 1024

MOBA_HEADS = 16
MOBA_HEAD_DIM = 64
MOBA_BLOCK = 256
MOBA_TOPK = 3
REL_BUCKETS = 32
REL_MAX_DIST = 128
MLA_HEADS = 16
MLA_KV_LORA = 256
MLA_NOPE = 64
MLA_ROPE = 32
MLA_V = 64
MLA_QK = MLA_NOPE + MLA_ROPE
ROPE_BASE = 10000.0
EPS = 1e-6
NEG = -1e30

FFN_ROWS = 1024
FFN_CHUNK = 256
PROJ_ROWS = 512
PROJ_COLS = 512
ADALN_COLS = 1152
MOBA_TILE = 512
MLA_TILE = 512
ATTN_HEADS = 8
BOUND_SLACK = 1.01
LOG2E = math.log2(math.e)
BIAS_TABLE = LANES
MAX_DIST = BIAS_TABLE - 2
MASKED_DIST = BIAS_TABLE - 1
SCORE_BOUND_MAX = 60.0


def _params(*sem):
    return pltpu.CompilerParams(dimension_semantics=sem, vmem_limit_bytes=VMEM_LIMIT_BYTES)


def _dot(a, b):
    return jnp.dot(a, b, preferred_element_type=F32)


def _dot_nt(a, b):
    return lax.dot_general(a, b, (((1,), (1,)), ((), ())), preferred_element_type=F32)


def _rms_mod(x, g, shift, scale):
    ms = jnp.mean(x * x, axis=-1, keepdims=True)
    y = x * lax.rsqrt(ms + EPS) * g
    return y * (1.0 + scale) + shift


def _mods_kernel(c_ref, w_ref, b_ref, o_ref):
    c = c_ref[...]
    ca = c * jax.nn.sigmoid(c)
    w = w_ref[0]
    w_hi = w.astype(BF16)
    w_lo = (w - w_hi.astype(F32)).astype(BF16)
    ca_hi = ca.astype(BF16)
    ca_lo = (ca - ca_hi.astype(F32)).astype(BF16)
    o_ref[0] = _dot(ca_hi, w_hi) + _dot(ca_lo, w_hi) + _dot(ca_hi, w_lo) + b_ref[0]


def _mods(c_pad, w, b, tn):
    nl, d, n = w.shape
    return pl.pallas_call(
        _mods_kernel,
        grid=(nl, n // tn),
        in_specs=[pl.BlockSpec((SUBLANES, d), lambda l, j: (0, 0)),
                  pl.BlockSpec((1, d, tn), lambda l, j: (l, 0, j)),
                  pl.BlockSpec((1, 1, tn), lambda l, j: (l, 0, j))],
        out_specs=pl.BlockSpec((1, SUBLANES, tn), lambda l, j: (l, 0, j)),
        out_shape=jax.ShapeDtypeStruct((nl, SUBLANES, n), F32),
        compiler_params=_params("arbitrary", "arbitrary"),
        name="adaln_mods",
    )(c_pad, w, b)


def _ffn_kernel(*refs, tf, with_mixer):
    if with_mixer:
        x_ref, a_ref, gm_ref, wo_ref = refs[:4]
        refs = refs[4:]
        x = x_ref[0] + gm_ref[0] * _dot(a_ref[0], wo_ref[...])
    else:
        x_ref = refs[0]
        refs = refs[1:]
        x = x_ref[0]
    sh_ref, sc_ref, gt_ref, g_ref, wg_ref, wu_ref, wd_ref, o_ref = refs
    h = _rms_mod(x, g_ref[...], sh_ref[0], sc_ref[0]).astype(BF16)
    acc = None
    for f in range(wg_ref.shape[1] // tf):
        cols = slice(f * tf, (f + 1) * tf)
        g = _dot(h, wg_ref[:, cols])
        u = _dot(h, wu_ref[:, cols])
        a = (g * jax.nn.sigmoid(g) * u).astype(BF16)
        d = _dot(a, wd_ref[cols, :])
        acc = d if acc is None else acc + d
    o_ref[0] = x + (0.5 * gt_ref[0]) * acc


def _ffn(x, shift, scale, gate, g, wg, wu, wd, which, tm, tf, mixer=None):
    b, s, d = x.shape
    f = wg.shape[-1]
    row = pl.BlockSpec((1, tm, d), lambda bi, i: (bi, i, 0))
    vec = pl.BlockSpec((1, 1, d), lambda bi, i: (bi, 0, 0))
    full = lambda shp: pl.BlockSpec(shp, lambda bi, i: (0, 0), pipeline_mode=pl.Buffered(1))
    picked = lambda r, c: pl.BlockSpec((None, None, r, c), lambda bi, i: (*which, 0, 0),
                                       pipeline_mode=pl.Buffered(1))
    ins, specs = [x], [row]
    if mixer is not None:
        a, gate_mix, w_o = mixer
        k = a.shape[2]
        ins += [a, gate_mix, w_o]
        specs += [pl.BlockSpec((1, tm, k), lambda bi, i: (bi, i, 0)), vec, full((k, d))]
    ins += [shift, scale, gate, g, wg, wu, wd]
    specs += [vec, vec, vec, full((1, d)), picked(d, f), picked(d, f), picked(f, d)]
    return pl.pallas_call(
        functools.partial(_ffn_kernel, tf=tf, with_mixer=mixer is not None),
        grid=(b, s // tm),
        in_specs=specs,
        out_specs=row,
        out_shape=jax.ShapeDtypeStruct((b, s, d), F32),
        compiler_params=_params("arbitrary", "arbitrary"),
        name="ffn_swiglu",
    )(*ins)


def _headnorm_pair(y, g, dim):
    lane = lax.broadcasted_iota(jnp.int32, (1, LANES), 1)
    left = lane < dim
    y2 = y * y
    ss_a = jnp.sum(jnp.where(left, y2, 0.0), axis=-1, keepdims=True)
    ss_b = jnp.sum(jnp.where(left, 0.0, y2), axis=-1, keepdims=True)
    inv = jnp.where(left, lax.rsqrt(ss_a * (1.0 / dim) + EPS), lax.rsqrt(ss_b * (1.0 / dim) + EPS))
    return y * inv * g


def _moba_proj_kernel(x_ref, sh_ref, sc_ref, g_ref, w_ref, wvt_ref, hg_ref, qk_ref, vt_ref, *, tn):
    h = _rms_mod(x_ref[0], g_ref[...], sh_ref[0], sc_ref[0]).astype(BF16)
    for n in range(w_ref.shape[1] // tn):
        res = _dot(h, w_ref[:, n * tn:(n + 1) * tn])
        for s in range(tn // LANES):
            sl = slice(n * tn + s * LANES, n * tn + (s + 1) * LANES)
            qk_ref[0, :, sl] = _headnorm_pair(res[:, s * LANES:(s + 1) * LANES], hg_ref[:, sl],
                                              MOBA_HEAD_DIM).astype(BF16)
    for n in range(wvt_ref.shape[0] // tn):
        rows = slice(n * tn, (n + 1) * tn)
        vt_ref[0, rows, :] = _dot_nt(wvt_ref[rows, :], h).astype(BF16)


def _moba_proj(x, shift, scale, g, w_qk, w_vt, head_gain, tm, tn):
    b, s, d = x.shape
    nqk = w_qk.shape[1]
    nv = w_vt.shape[0]
    vec = pl.BlockSpec((1, 1, d), lambda bi, i: (bi, 0, 0))
    full = lambda shp: pl.BlockSpec(shp, lambda bi, i: (0, 0))
    return pl.pallas_call(
        functools.partial(_moba_proj_kernel, tn=tn),
        grid=(b, s // tm),
        in_specs=[pl.BlockSpec((1, tm, d), lambda bi, i: (bi, i, 0)), vec, vec, full((1, d)),
                  full((d, nqk)), full((nv, d)), full((1, nqk))],
        out_specs=[pl.BlockSpec((1, tm, nqk), lambda bi, i: (bi, i, 0)),
                   pl.BlockSpec((1, nv, tm), lambda bi, i: (bi, 0, i))],
        out_shape=[jax.ShapeDtypeStruct((b, s, nqk), BF16), jax.ShapeDtypeStruct((b, nv, s), BF16)],
        compiler_params=_params("arbitrary", "arbitrary"),
        name="moba_qkv_proj",
    )(x, shift, scale, g, w_qk, w_vt, head_gain)


def _softmax_step_t(s, m, l, acc, vt, bounded):
    if bounded:
        p = jnp.exp2(s)
        return m, l + jnp.sum(p, axis=0, keepdims=True), acc + _dot(vt, p.astype(BF16))
    m_new = jnp.maximum(m, jnp.max(s, axis=0, keepdims=True))
    alpha = jnp.exp2(m - m_new)
    p = jnp.exp2(s - m_new)
    l_new = alpha * l + jnp.sum(p, axis=0, keepdims=True)
    acc_new = acc * alpha + _dot(vt, p.astype(BF16))
    return m_new, l_new, acc_new


def _moba_attn_kernel(qmin_ref, kmax_ref, qgmin_ref, kbmax_ref, q_ref, k_ref, vt_ref, bt_ref, pq_ref, pk_ref, o_ref,
                      km_scr, sel_scr, *, nb, t, npairs, bounded):
    blk = MOBA_BLOCK
    hd = MOBA_HEAD_DIM
    nsub = t // blk
    nh = 2 * npairs
    bi = pl.program_id(0)
    qi = pl.program_id(2)
    lane = lax.broadcasted_iota(jnp.int32, (1, LANES), 1)
    left = lane < hd

    @pl.when(qi == 0)
    def _():
        km_scr[...] = jnp.zeros_like(km_scr)
        for r in range(nb):
            kblk = k_ref[0, r * blk:(r + 1) * blk, :].astype(F32)
            km_scr[r:r + 1, :] = jnp.sum(kblk, axis=0, keepdims=True) * (1.0 / blk)

    q_heads = []
    for p in range(npairs):
        q2 = q_ref[0, :, p * LANES:(p + 1) * LANES]
        zero = jnp.zeros_like(q2)
        q_heads += [jnp.where(left, q2, zero), jnp.where(left, zero, q2)]

    col = lax.broadcasted_iota(jnp.int32, (1, t), 1)
    q_idx = qi * t + col
    q_sub = sum((col >= r * blk).astype(jnp.int32) for r in range(1, nsub))
    q_blk = qi * nsub + q_sub
    q_blk_start = q_blk * blk

    blk_row = lax.broadcasted_iota(jnp.int32, (nb, t), 0)
    for h in range(nh):
        km = km_scr[:, (h // 2) * LANES:(h // 2 + 1) * LANES]
        km_hi = km.astype(BF16)
        km_lo = (km - km_hi.astype(F32)).astype(BF16)
        gate = (_dot_nt(km_hi, q_heads[h]) + _dot_nt(km_lo, q_heads[h]))[:nb]
        gate = jnp.where(blk_row < q_blk, gate, -jnp.inf)
        sel = jnp.zeros((nb, t), F32)
        for _ in range(MOBA_TOPK):
            mx = jnp.max(gate, axis=0, keepdims=True)
            idx = jnp.min(jnp.where(gate == mx, blk_row, nb), axis=0, keepdims=True)
            pick = (blk_row == idx) & (mx > -jnp.inf)
            sel = jnp.where(pick, 1.0, sel)
            gate = jnp.where(pick, -jnp.inf, gate)
        sel_scr[h] = sel

    pq = pq_ref[0, 0]

    def gather_bias(h, dist):
        table = jnp.broadcast_to(bt_ref[h // 2, h % 2:h % 2 + 1, :], (dist.shape[0], LANES))
        return jnp.concatenate(
            [jnp.take_along_axis(table, dist[:, g * LANES:(g + 1) * LANES], axis=1, mode="promise_in_bounds")
             for g in range(dist.shape[1] // LANES)], axis=1)

    def far_bias(h):
        return bt_ref[h // 2, h % 2:h % 2 + 1, MAX_DIST:MAX_DIST + 1]

    def chosen_rows(h, j, lanes=slice(None)):
        return [sel_scr[h, pl.ds(j * nsub + r, 1), lanes] > 0.0 for r in range(nsub)]

    def own_chunk_bounded(carry):
        assert nsub == 2
        carry = list(carry)
        start = pl.multiple_of(qi * t, t)
        k2 = k_ref[0, pl.ds(start, t), :]
        vt = vt_ref[0, :, pl.ds(start, t)]
        pk0 = pk_ref[0, pl.ds(start, blk), :]
        pk1 = pk_ref[0, pl.ds(start + blk, blk), :]
        causal = (lax.broadcasted_iota(jnp.int32, (blk, blk), 0) <= lax.broadcasted_iota(jnp.int32, (blk, blk), 1))
        d00 = jnp.where(causal, jnp.clip(pq[:, :blk] - pk0, 0, MAX_DIST), MASKED_DIST)
        d01 = jnp.clip(pq[:, blk:] - pk0, 0, MAX_DIST)
        d11 = jnp.where(causal, jnp.clip(pq[:, blk:] - pk1, 0, MAX_DIST), MASKED_DIST)
        quads = []
        for h in range(nh):
            kh = k2[:, (h // 2) * LANES:(h // 2 + 1) * LANES]
            qh = q_heads[h]
            quads.append((_dot_nt(kh[:blk], qh[:blk]), _dot_nt(kh[:blk], qh[blk:]), _dot_nt(kh[blk:], qh[blk:])))
        for h in range(nh):
            m, l, acc = carry[h]
            s00, s01, s11 = quads[h]
            e00 = jnp.exp2(s00 + gather_bias(h, d00))
            e01 = jnp.exp2(s01 + gather_bias(h, d01))
            e11 = jnp.exp2(s11 + gather_bias(h, d11))
            w01 = jnp.where(chosen_rows(h, qi, slice(blk, t))[0], 1.0, 0.0)
            vt0 = vt[h * hd:(h + 1) * hd, :blk]
            vt1 = vt[h * hd:(h + 1) * hd, blk:]
            l_add = jnp.concatenate(
                [jnp.sum(e00, axis=0, keepdims=True),
                 jnp.sum(e01, axis=0, keepdims=True) * w01 + jnp.sum(e11, axis=0, keepdims=True)], axis=1)
            acc_add = jnp.concatenate(
                [_dot(vt0, e00.astype(BF16)),
                 _dot(vt0, e01.astype(BF16)) * w01 + _dot(vt1, e11.astype(BF16))], axis=1)
            carry[h] = (m, l + l_add, acc + acc_add)
        return tuple(carry)

    def chunk(j, carry, mode):
        carry = list(carry)
        start = pl.multiple_of(j * t, t)
        k2 = k_ref[0, pl.ds(start, t), :]
        vt = vt_ref[0, :, pl.ds(start, t)]
        s_heads = [_dot_nt(k2[:, (h // 2) * LANES:(h // 2 + 1) * LANES], q_heads[h]) for h in range(nh)]
        if mode in ("near", "diag"):
            dist = jnp.clip(pq - pk_ref[0, pl.ds(start, t), :], 0, MAX_DIST)
        if mode == "corner":
            dist_c = jnp.clip(pq[:, :LANES] - pk_ref[0, pl.ds(start + (nsub - 1) * blk, blk), :], 0, MAX_DIST)
        if mode == "diag":
            key_idx = start + lax.broadcasted_iota(jnp.int32, (blk, t), 0)
        for h in range(nh):
            m, l, acc = carry[h]
            s = s_heads[h]
            vt_h = vt[h * hd:(h + 1) * hd, :]
            chosen = chosen_rows(h, j)
            if mode in ("far", "corner") and bounded:
                two_c = jnp.exp2(far_bias(h))
                for r in range(nsub):
                    piece = s[r * blk:(r + 1) * blk]
                    w = jnp.where(chosen[r], two_c, 0.0)
                    if mode == "corner" and r == nsub - 1:
                        piece = jnp.concatenate([piece[:, :LANES] + gather_bias(h, dist_c), piece[:, LANES:]], axis=1)
                        w = jnp.where(col < LANES, jnp.where(chosen[r], 1.0, 0.0), w)
                    p = jnp.exp2(piece)
                    l = l + jnp.sum(p, axis=0, keepdims=True) * w
                    acc = acc + _dot(vt_h[:, r * blk:(r + 1) * blk], p.astype(BF16)) * w
                carry[h] = (m, l, acc)
            elif mode == "far":
                pieces = [s[r * blk:(r + 1) * blk] for r in range(nsub)]
                c = far_bias(h)
                tmax = [jnp.where(chosen[r], jnp.max(pieces[r], axis=0, keepdims=True) + c, NEG)
                        for r in range(nsub)]
                m_new = functools.reduce(jnp.maximum, tmax, m)
                alpha = jnp.exp2(m - m_new)
                ps = [jnp.exp2(pieces[r] - jnp.where(chosen[r], m_new - c, jnp.inf)) for r in range(nsub)]
                l_new = alpha * l + sum(jnp.sum(p, axis=0, keepdims=True) for p in ps)
                p = jnp.concatenate(ps, axis=0) if nsub > 1 else ps[0]
                carry[h] = (m_new, l_new, acc * alpha + _dot(vt_h, p.astype(BF16)))
            else:
                s = s + gather_bias(h, dist)
                masked = []
                for r in range(nsub):
                    piece = s[r * blk:(r + 1) * blk]
                    if mode == "diag":
                        kidx = key_idx + r * blk
                        keep_own = jnp.where(kidx <= q_idx, jnp.where(kidx >= q_blk_start, piece, NEG), NEG)
                        masked.append(jnp.where(chosen[r], piece, keep_own))
                    else:
                        masked.append(jnp.where(chosen[r], piece, NEG))
                s = jnp.concatenate(masked, axis=0) if nsub > 1 else masked[0]
                carry[h] = _softmax_step_t(s, m, l, acc, vt_h, bounded)
        return tuple(carry)

    def past_chunk(j, carry):
        q_lo = qmin_ref[bi, qi]
        far = q_lo - kmax_ref[bi, j] >= MAX_DIST
        if not bounded:
            return lax.cond(far, lambda c: chunk(j, c, "far"), lambda c: chunk(j, c, "near"), carry)
        groups = t // LANES
        q_rest = functools.reduce(jnp.minimum, [qgmin_ref[bi, qi * groups + g] for g in range(1, groups)])
        k_first = functools.reduce(jnp.maximum, [kbmax_ref[bi, j * nsub + r] for r in range(nsub - 1)])
        corner = (q_lo - k_first >= MAX_DIST) & (q_rest - kbmax_ref[bi, j * nsub + nsub - 1] >= MAX_DIST)
        return lax.cond(far, lambda c: chunk(j, c, "far"),
                        lambda c: lax.cond(corner, lambda c2: chunk(j, c2, "corner"),
                                           lambda c2: chunk(j, c2, "near"), c), carry)

    init = (jnp.full((1, t), NEG, F32), jnp.zeros((1, t), F32), jnp.zeros((hd, t), F32))
    if bounded:
        carry = lax.fori_loop(0, qi, past_chunk, own_chunk_bounded((init,) * nh))
    else:
        carry = lax.fori_loop(0, qi, past_chunk, (init,) * nh)
        carry = chunk(qi, carry, "diag")
    out_t = jnp.concatenate([acc / l for (_, l, acc) in carry], axis=0)
    o_ref[0] = out_t.T.astype(BF16)


def _moba_position_tables(positions, t):
    b, s = positions.shape
    pos_tiles = positions.reshape(b, s // t, t)
    return (jnp.min(pos_tiles, axis=-1), jnp.max(pos_tiles, axis=-1),
            jnp.min(positions.reshape(b, s // LANES, LANES), axis=-1),
            jnp.max(positions.reshape(b, s // MOBA_BLOCK, MOBA_BLOCK), axis=-1),
            positions.reshape(b, s // t, 1, t), positions.reshape(b, s, 1))


def _moba_attn(qk, vt, bias_table, pos_tables, t, npairs, bounded):
    b, s, _ = qk.shape
    nb = s // MOBA_BLOCK
    ngroups = MOBA_HEADS * MOBA_HEAD_DIM // (LANES * npairs)
    w = npairs * LANES
    q_min, k_max, qg_min, kb_max, pos_row, pos_col = pos_tables
    grid_spec = pltpu.PrefetchScalarGridSpec(
        num_scalar_prefetch=4,
        grid=(b, ngroups, s // t),
        in_specs=[pl.BlockSpec((1, t, w), lambda bi, g, qi, *_: (bi, qi, g)),
                  pl.BlockSpec((1, s, w), lambda bi, g, qi, *_: (bi, 0, ngroups + g)),
                  pl.BlockSpec((1, w, s), lambda bi, g, qi, *_: (bi, g, 0)),
                  pl.BlockSpec((npairs, 2, LANES), lambda bi, g, qi, *_: (g, 0, 0)),
                  pl.BlockSpec((1, 1, 1, t), lambda bi, g, qi, *_: (bi, qi, 0, 0)),
                  pl.BlockSpec((1, s, 1), lambda bi, g, qi, *_: (bi, 0, 0))],
        out_specs=pl.BlockSpec((1, t, w), lambda bi, g, qi, *_: (bi, qi, g)),
        scratch_shapes=[pltpu.VMEM((LANES, w), F32), pltpu.VMEM((2 * npairs, nb, t), F32)],
    )
    return pl.pallas_call(
        functools.partial(_moba_attn_kernel, nb=nb, t=t, npairs=npairs, bounded=bounded),
        grid_spec=grid_spec,
        out_shape=jax.ShapeDtypeStruct((b, s, MOBA_HEADS * MOBA_HEAD_DIM), BF16),
        compiler_params=_params("arbitrary", "arbitrary", "arbitrary"),
        name="moba_attention",
    )(q_min, k_max, qg_min, kb_max, qk, qk, vt, bias_table, pos_row, pos_col)


def _rope_tab_kernel(pos_ref, inv_ref, c_ref, s_ref):
    ang = pos_ref[0].astype(F32) * inv_ref[...]
    c_ref[0] = jnp.cos(ang)
    s_ref[0] = jnp.sin(ang)


def _rope_tables(positions, inv):
    b, s = positions.shape
    half = inv.shape[0]
    per_row = LANES // half
    rows = s // per_row
    pos_rep = jnp.repeat(positions.reshape(b, rows, per_row), half, axis=-1)
    blk = pl.BlockSpec((1, rows, LANES), lambda bi: (bi, 0, 0))
    shape = jax.ShapeDtypeStruct((b, rows, LANES), F32)
    cos_d, sin_d = pl.pallas_call(
        _rope_tab_kernel,
        grid=(b,),
        in_specs=[blk, pl.BlockSpec((1, LANES), lambda bi: (0, 0))],
        out_specs=[blk, blk],
        out_shape=[shape, shape],
        compiler_params=_params("arbitrary"),
        name="rope_tables",
    )(pos_rep, jnp.tile(inv, per_row).reshape(1, LANES))
    cos_h = cos_d.reshape(b, s, half)
    sin_h = sin_d.reshape(b, s, half)
    ones = lambda n: jnp.ones((b, s, n), F32)
    zeros = lambda n: jnp.zeros((b, s, n), F32)
    cos = jnp.concatenate([ones(MLA_NOPE), cos_h, cos_h, ones(LANES - MLA_QK)], axis=-1)
    sin = jnp.concatenate([zeros(MLA_NOPE), -sin_h, sin_h, zeros(LANES - MLA_QK)], axis=-1)
    return cos, sin


def _mla_q_kernel(x_ref, sh_ref, sc_ref, g_ref, wdq_ref, qag_ref, wuq_ref, wuqp_ref, qg_ref, qgp_ref,
                  c_ref, s_ref, o_ref):
    h = _rms_mod(x_ref[0], g_ref[...], sh_ref[0], sc_ref[0]).astype(BF16)
    dq = _dot(h, wdq_ref[...])
    dqn = (dq * lax.rsqrt(jnp.mean(dq * dq, axis=-1, keepdims=True) + EPS) * qag_ref[...]).astype(BF16)
    uq = _dot(dqn, wuq_ref[...])
    up = _dot(dqn, wuqp_ref[...])
    cg = c_ref[0] * qg_ref[...]
    sg = s_ref[0] * qgp_ref[...]
    scale = MLA_QK ** -0.5 * LOG2E
    for hd in range(MLA_HEADS):
        sl = slice(hd * LANES, (hd + 1) * LANES)
        u = uq[:, sl]
        inv = lax.rsqrt(jnp.sum(u * u, axis=-1, keepdims=True) * (1.0 / MLA_QK) + EPS) * scale
        o_ref[0, :, sl] = ((u * cg + up[:, sl] * sg) * inv).astype(BF16)


def _mla_q(x, shift, scale, g, wdq, qag, wuq_p, wuq_partner, qg_p, qg_partner, cos, sin, tm):
    b, s, d = x.shape
    r = wdq.shape[1]
    nq = wuq_p.shape[1]
    vec = pl.BlockSpec((1, 1, d), lambda bi, i: (bi, 0, 0))
    tab = pl.BlockSpec((1, tm, LANES), lambda bi, i: (bi, i, 0))
    full = lambda shp: pl.BlockSpec(shp, lambda bi, i: (0, 0))
    return pl.pallas_call(
        _mla_q_kernel,
        grid=(b, s // tm),
        in_specs=[pl.BlockSpec((1, tm, d), lambda bi, i: (bi, i, 0)), vec, vec, full((1, d)),
                  full((d, r)), full((1, r)), full((r, nq)), full((r, nq)), full((1, LANES)), full((1, LANES)),
                  tab, tab],
        out_specs=pl.BlockSpec((1, tm, nq), lambda bi, i: (bi, i, 0)),
        out_shape=jax.ShapeDtypeStruct((b, s, nq), BF16),
        compiler_params=_params("arbitrary", "arbitrary"),
        name="mla_q_path",
    )(x, shift, scale, g, wdq, qag, wuq_p, wuq_partner, qg_p, qg_partner, cos, sin)


def _mla_kv_kernel(x_ref, sh_ref, sc_ref, g_ref, wdkv_ref, kvg_ref, wuk_ref, wuvt_ref, kg_ref, kgp_ref,
                   c_ref, s_ref, k_ref, vt_ref):
    hs = _rms_mod(x_ref[0], g_ref[...], sh_ref[0], sc_ref[0]).astype(BF16)
    ck = _dot(hs, wdkv_ref[...])
    ckv = ck[:, :MLA_KV_LORA]
    ckvn = (ckv * lax.rsqrt(jnp.mean(ckv * ckv, axis=-1, keepdims=True) + EPS) * kvg_ref[...]).astype(BF16)
    kn = _dot(ckvn, wuk_ref[...])
    vt_ref[0] = _dot_nt(wuvt_ref[...], ckvn).astype(BF16)
    lane = lax.broadcasted_iota(jnp.int32, (1, LANES), 1)
    on_rope = (lane >= MLA_NOPE) & (lane < MLA_QK)
    tail = ck[:, MLA_KV_LORA:MLA_KV_LORA + LANES]
    kpe = jnp.where(on_rope, pltpu.roll(tail, MLA_NOPE, axis=1), 0.0)
    partner = pltpu.roll(tail, MLA_ROPE, axis=1)
    cg = c_ref[0] * kg_ref[...]
    sg = s_ref[0] * kgp_ref[...]
    rot = kpe * cg + partner * sg
    ss_pe = jnp.sum(kpe * kpe, axis=-1, keepdims=True)
    for hd in range(MLA_HEADS):
        sl = slice(hd * LANES, (hd + 1) * LANES)
        nope = kn[:, sl]
        inv = lax.rsqrt((jnp.sum(nope * nope, axis=-1, keepdims=True) + ss_pe) * (1.0 / MLA_QK) + EPS)
        k_ref[0, :, sl] = ((nope * cg + rot) * inv).astype(BF16)


def _mla_kv(x, shift, scale, g, wdkv_p, kvg, wuk_p, wuv_t, kg_p, kg_partner, cos, sin, tm):
    b, s, d = x.shape
    r = wdkv_p.shape[1]
    nk = wuk_p.shape[1]
    nv = wuv_t.shape[0]
    vec = pl.BlockSpec((1, 1, d), lambda bi, i: (bi, 0, 0))
    tab = pl.BlockSpec((1, tm, LANES), lambda bi, i: (bi, i, 0))
    full = lambda shp: pl.BlockSpec(shp, lambda bi, i: (0, 0))
    return pl.pallas_call(
        _mla_kv_kernel,
        grid=(b, s // tm),
        in_specs=[pl.BlockSpec((1, tm, d), lambda bi, i: (bi, i, 0)), vec, vec, full((1, d)),
                  full((d, r)), full((1, MLA_KV_LORA)), full((MLA_KV_LORA, nk)), full((nv, MLA_KV_LORA)),
                  full((1, LANES)), full((1, LANES)), tab, tab],
        out_specs=[pl.BlockSpec((1, tm, nk), lambda bi, i: (bi, i, 0)),
                   pl.BlockSpec((1, nv, tm), lambda bi, i: (bi, 0, i))],
        out_shape=[jax.ShapeDtypeStruct((b, s, nk), BF16), jax.ShapeDtypeStruct((b, nv, s), BF16)],
        compiler_params=_params("arbitrary", "arbitrary"),
        name="mla_shared_kv",
    )(x, shift, scale, g, wdkv_p, kvg, wuk_p, wuv_t, kg_p, kg_partner, cos, sin)


def _mla_attn_kernel(q_ref, k_ref, vt_ref, o_ref, *, t, nh, bounded):
    qi = pl.program_id(2)
    q2 = q_ref[0]
    q_heads = [q2[:, h * LANES:(h + 1) * LANES] for h in range(nh)]

    def chunk(j, carry, masked):
        carry = list(carry)
        start = pl.multiple_of(j * t, t)
        k2 = k_ref[0, pl.ds(start, t), :]
        vt = vt_ref[0, :, pl.ds(start, t)]
        s_heads = [_dot_nt(k2[:, h * LANES:(h + 1) * LANES], q_heads[h]) for h in range(nh)]
        if masked:
            row = lax.broadcasted_iota(jnp.int32, (t, t), 0)
            col = lax.broadcasted_iota(jnp.int32, (t, t), 1)
            allowed = row <= col
        for h in range(nh):
            m, l, acc = carry[h]
            s = jnp.where(allowed, s_heads[h], NEG) if masked else s_heads[h]
            carry[h] = _softmax_step_t(s, m, l, acc, vt[h * MLA_V:(h + 1) * MLA_V, :], bounded)
        return tuple(carry)

    def own_chunk_bounded(carry):
        carry = list(carry)
        half = t // 2
        start = pl.multiple_of(qi * t, t)
        k2 = k_ref[0, pl.ds(start, t), :]
        vt = vt_ref[0, :, pl.ds(start, t)]
        causal = (lax.broadcasted_iota(jnp.int32, (half, half), 0) <= lax.broadcasted_iota(jnp.int32, (half, half), 1))
        quads = []
        for h in range(nh):
            kh = k2[:, h * LANES:(h + 1) * LANES]
            qh = q_heads[h]
            quads.append((_dot_nt(kh[:half], qh[:half]), _dot_nt(kh[:half], qh[half:]), _dot_nt(kh[half:], qh[half:])))
        for h in range(nh):
            m, l, acc = carry[h]
            s00, s01, s11 = quads[h]
            e00 = jnp.exp2(jnp.where(causal, s00, NEG))
            e01 = jnp.exp2(s01)
            e11 = jnp.exp2(jnp.where(causal, s11, NEG))
            vt0 = vt[h * MLA_V:(h + 1) * MLA_V, :half]
            vt1 = vt[h * MLA_V:(h + 1) * MLA_V, half:]
            l_add = jnp.concatenate(
                [jnp.sum(e00, axis=0, keepdims=True),
                 jnp.sum(e01, axis=0, keepdims=True) + jnp.sum(e11, axis=0, keepdims=True)], axis=1)
            acc_add = jnp.concatenate(
                [_dot(vt0, e00.astype(BF16)), _dot(vt0, e01.astype(BF16)) + _dot(vt1, e11.astype(BF16))], axis=1)
            carry[h] = (m, l + l_add, acc + acc_add)
        return tuple(carry)

    init = (jnp.full((1, t), NEG, F32), jnp.zeros((1, t), F32), jnp.zeros((MLA_V, t), F32))
    carry = lax.fori_loop(0, qi, lambda j, c: chunk(j, c, False), (init,) * nh)
    carry = own_chunk_bounded(carry) if bounded else chunk(qi, carry, True)
    out_t = jnp.concatenate([acc / l for (_, l, acc) in carry], axis=0)
    o_ref[0] = out_t.T.astype(BF16)


def _mla_attn(q, k, vt, t, nh, bounded):
    b, s, _ = q.shape
    return pl.pallas_call(
        functools.partial(_mla_attn_kernel, t=t, nh=nh, bounded=bounded),
        grid=(b, MLA_HEADS // nh, s // t),
        in_specs=[pl.BlockSpec((1, t, nh * LANES), lambda bi, g, qi: (bi, qi, g)),
                  pl.BlockSpec((1, s, nh * LANES), lambda bi, g, qi: (bi, 0, g)),
                  pl.BlockSpec((1, nh * MLA_V, s), lambda bi, g, qi: (bi, g, 0))],
        out_specs=pl.BlockSpec((1, t, nh * MLA_V), lambda bi, g, qi: (bi, qi, g)),
        out_shape=jax.ShapeDtypeStruct((b, s, MLA_HEADS * MLA_V), BF16),
        compiler_params=_params("arbitrary", "arbitrary", "arbitrary"),
        name="mla_attention",
    )(q, k, vt)


def _t5_bias_table(rel_bias):
    n = jnp.arange(MASKED_DIST)
    max_exact = REL_BUCKETS // 2
    nf = jnp.maximum(n, 1).astype(F32)
    large = max_exact + (jnp.log(nf / max_exact) / math.log(REL_MAX_DIST / max_exact)
                         * (REL_BUCKETS - max_exact)).astype(jnp.int32)
    large = jnp.minimum(large, REL_BUCKETS - 1)
    bucket = jnp.where(n < max_exact, n, large)
    table = jnp.concatenate([rel_bias.T[:, bucket] * LOG2E, jnp.full((MOBA_HEADS, 1), NEG, F32)], axis=1)
    return table.reshape(MOBA_HEADS // 2, 2, BIAS_TABLE)


def _pad_heads(w, width):
    k = w.shape[0]
    w3 = w.reshape(k, -1, width)
    return jnp.pad(w3, ((0, 0), (0, 0), (0, LANES - width))).reshape(k, -1)


def _pad_lanes(g):
    return jnp.pad(g, (0, LANES - g.shape[0])).reshape(1, LANES)


def _swap_rope_halves(a):
    half = MLA_ROPE // 2
    return jnp.concatenate([jnp.zeros_like(a[..., :MLA_NOPE]), a[..., MLA_NOPE + half:],
                            a[..., MLA_NOPE:MLA_NOPE + half]], axis=-1)


def _scores_bounded(q_gain, k_gain, dim, scale, extra=0.0):
    bound = dim * jnp.max(jnp.abs(q_gain)) * jnp.max(jnp.abs(k_gain)) * scale * BOUND_SLACK + extra
    return bound <= SCORE_BOUND_MAX


def kernel(x, c, positions, ada_w, ada_b, norm_g, ffn_w_gate, ffn_w_up, ffn_w_down, rel_bias,
           moba_w_qkv, moba_q_g, moba_k_g, moba_w_o, kv_ada_w, kv_ada_b, kv_norm_g, w_dkv,
           kv_a_norm_g, w_uk, w_uv, mla_k_g, mla_w_dq, mla_q_a_norm_g, mla_w_uq, mla_q_g, mla_w_o):
    b, s, d = x.shape
    depth = ada_w.shape[0]
    n_moba = moba_w_qkv.shape[0]
    assert b <= SUBLANES and s % max(FFN_ROWS, MOBA_TILE, MLA_TILE) == 0 and MOBA_TILE == 2 * MOBA_BLOCK
    tm_ffn = FFN_ROWS
    tm_proj = PROJ_ROWS

    c_pad = jnp.pad(c, ((0, SUBLANES - b), (0, 0)))
    mods = _mods(c_pad, ada_w, ada_b.reshape(depth, 1, -1), tn=ADALN_COLS)[:, :b]
    kv_mods = _mods(c_pad, kv_ada_w[None], kv_ada_b.reshape(1, 1, -1), tn=d)[0, :b]

    def mod_vecs(m):
        return [v.reshape(b, 1, d) for v in jnp.split(m, m.shape[-1] // d, axis=-1)]

    w_gate, w_up, w_down = (w.astype(BF16) for w in (ffn_w_gate, ffn_w_up, ffn_w_down))

    for layer in range(depth):
        sh1, sc1, g1, sh2, sc2, g2, sh3, sc3, g3 = mod_vecs(mods[layer])
        ng = norm_g[layer]

        def ffn(xin, idx, sh, sc, gt, mixer=None):
            return _ffn(xin, sh, sc, gt, ng[2 * idx:2 * idx + 1], w_gate, w_up, w_down, (layer, idx),
                        tm_ffn, FFN_CHUNK, mixer)

        x = ffn(x, 0, sh1, sc1, g1)

        if layer < n_moba:
            n_qk = 2 * MOBA_HEADS * MOBA_HEAD_DIM
            q_scale = MOBA_HEAD_DIM ** -0.5 * LOG2E
            head_gain = jnp.concatenate([jnp.tile(moba_q_g[layer] * q_scale, MOBA_HEADS),
                                         jnp.tile(moba_k_g[layer], MOBA_HEADS)]).reshape(1, -1)
            w_qkv = moba_w_qkv[layer]
            qk, vt = _moba_proj(x, sh2, sc2, ng[1:2], w_qkv[:, :n_qk].astype(BF16),
                                w_qkv[:, n_qk:].T.astype(BF16), head_gain, tm_proj, PROJ_COLS)
            bias_table = _t5_bias_table(rel_bias)
            moba = functools.partial(_moba_attn, qk, vt, bias_table, _moba_position_tables(positions, MOBA_TILE),
                                     MOBA_TILE, ATTN_HEADS // 2)
            ok = _scores_bounded(moba_q_g[layer], moba_k_g[layer], MOBA_HEAD_DIM, q_scale,
                                 jnp.max(jnp.abs(rel_bias)) * LOG2E)
            o = lax.cond(ok, lambda: moba(True), lambda: moba(False))
            w_o = moba_w_o[layer]
        else:
            j = layer - n_moba
            if j == 0:
                half = MLA_ROPE // 2
                inv = ROPE_BASE ** (-jnp.arange(half, dtype=F32) / half)
                cos, sin = _rope_tables(positions, inv)
                kv_sh, kv_sc = mod_vecs(kv_mods)
                w_kpe = w_dkv[:, MLA_KV_LORA:]
                w_kpe_swapped = jnp.concatenate([w_kpe[:, half:], w_kpe[:, :half]], axis=1)
                wdkv_p = jnp.pad(jnp.concatenate([w_dkv, w_kpe_swapped], axis=1),
                                 ((0, 0), (0, LANES - 2 * MLA_ROPE))).astype(BF16)
                shared_k, shared_v = _mla_kv(
                    x, kv_sh, kv_sc, kv_norm_g.reshape(1, d), wdkv_p, kv_a_norm_g.reshape(1, -1),
                    _pad_heads(w_uk, MLA_NOPE).astype(BF16), w_uv.T.astype(BF16), _pad_lanes(mla_k_g),
                    _pad_lanes(_swap_rope_halves(mla_k_g)), cos, sin, tm_proj)
            w_uq = mla_w_uq[j]
            w_uq_partner = _swap_rope_halves(w_uq.reshape(w_uq.shape[0], MLA_HEADS, MLA_QK)).reshape(w_uq.shape)
            q = _mla_q(x, sh2, sc2, ng[1:2], mla_w_dq[j].astype(BF16), mla_q_a_norm_g[j].reshape(1, -1),
                       _pad_heads(w_uq, MLA_QK).astype(BF16), _pad_heads(w_uq_partner, MLA_QK).astype(BF16),
                       _pad_lanes(mla_q_g[j]), _pad_lanes(_swap_rope_halves(mla_q_g[j])), cos, sin, tm_proj)
            mla = functools.partial(_mla_attn, q, shared_k, shared_v, MLA_TILE, ATTN_HEADS)
            ok = _scores_bounded(mla_q_g[j], mla_k_g, MLA_QK, MLA_QK ** -0.5 * LOG2E)
            o = lax.cond(ok, lambda: mla(True), lambda: mla(False))
            w_o = mla_w_o[j]

        x = ffn(x, 1, sh3, sc3, g3, mixer=(o, g2, w_o.astype(BF16)))
    return x
```

```python
import functools
import math

import jax
import jax.numpy as jnp
from jax import lax
from jax.experimental import pallas as pl
from jax.experimental.pallas import tpu as pltpu

F32 = jnp.float32
BF16 = jnp.bfloat16

LANES = 128
SUBLANES = 8
VMEM_LIMIT_BYTES = 56 * 1024 * 1024

MOBA_HEADS = 16
MOBA_HEAD_DIM = 64
MOBA_BLOCK = 256
MOBA_TOPK = 3
REL_BUCKETS = 32
REL_MAX_DIST = 128
MLA_HEADS = 16
MLA_KV_LORA = 256
MLA_NOPE = 64
MLA_ROPE = 32
MLA_V = 64
MLA_QK = MLA_NOPE + MLA_ROPE
ROPE_BASE = 10000.0
EPS = 1e-6
NEG = -1e30

FFN_ROWS = 1024
FFN_CHUNK = 256
PROJ_ROWS = 512
PROJ_COLS = 512
ADALN_COLS = 1152
MOBA_TILE = 512
MLA_TILE = 512
ATTN_HEADS = 8
BOUND_SLACK = 1.01
LOG2E = math.log2(math.e)
BIAS_TABLE = LANES
MAX_DIST = BIAS_TABLE - 2
MASKED_DIST = BIAS_TABLE - 1
SCORE_BOUND_MAX = 60.0


def _params(*sem):
    return pltpu.CompilerParams(dimension_semantics=sem, vmem_limit_bytes=VMEM_LIMIT_BYTES)


def _dot(a, b):
    return jnp.dot(a, b, preferred_element_type=F32)


def _dot_nt(a, b):
    return lax.dot_general(a, b, (((1,), (1,)), ((), ())), preferred_element_type=F32)


def _rms_mod(x, g, shift, scale):
    ms = jnp.mean(x * x, axis=-1, keepdims=True)
    y = x * lax.rsqrt(ms + EPS) * g
    return y * (1.0 + scale) + shift


def _mods_kernel(c_ref, w_ref, b_ref, o_ref):
    c = c_ref[...]
    ca = c * jax.nn.sigmoid(c)
    w = w_ref[0]
    w_hi = w.astype(BF16)
    w_lo = (w - w_hi.astype(F32)).astype(BF16)
    ca_hi = ca.astype(BF16)
    ca_lo = (ca - ca_hi.astype(F32)).astype(BF16)
    o_ref[0] = _dot(ca_hi, w_hi) + _dot(ca_lo, w_hi) + _dot(ca_hi, w_lo) + b_ref[0]


def _mods(c_pad, w, b, tn):
    nl, d, n = w.shape
    return pl.pallas_call(
        _mods_kernel,
        grid=(nl, n // tn),
        in_specs=[pl.BlockSpec((SUBLANES, d), lambda l, j: (0, 0)),
                  pl.BlockSpec((1, d, tn), lambda l, j: (l, 0, j)),
                  pl.BlockSpec((1, 1, tn), lambda l, j: (l, 0, j))],
        out_specs=pl.BlockSpec((1, SUBLANES, tn), lambda l, j: (l, 0, j)),
        out_shape=jax.ShapeDtypeStruct((nl, SUBLANES, n), F32),
        compiler_params=_params("arbitrary", "arbitrary"),
        name="adaln_mods",
    )(c_pad, w, b)


def _ffn_kernel(*refs, tf, with_mixer):
    if with_mixer:
        x_ref, a_ref, gm_ref, wo_ref = refs[:4]
        refs = refs[4:]
        x = x_ref[0] + gm_ref[0] * _dot(a_ref[0], wo_ref[...])
    else:
        x_ref = refs[0]
        refs = refs[1:]
        x = x_ref[0]
    sh_ref, sc_ref, gt_ref, g_ref, wg_ref, wu_ref, wd_ref, o_ref = refs
    h = _rms_mod(x, g_ref[...], sh_ref[0], sc_ref[0]).astype(BF16)
    acc = None
    for f in range(wg_ref.shape[1] // tf):
        cols = slice(f * tf, (f + 1) * tf)
        g = _dot(h, wg_ref[:, cols])
        u = _dot(h, wu_ref[:, cols])
        a = (g * jax.nn.sigmoid(g) * u).astype(BF16)
        d = _dot(a, wd_ref[cols, :])
        acc = d if acc is None else acc + d
    o_ref[0] = x + (0.5 * gt_ref[0]) * acc


def _ffn(x, shift, scale, gate, g, wg, wu, wd, which, tm, tf, mixer=None):
    b, s, d = x.shape
    f = wg.shape[-1]
    row = pl.BlockSpec((1, tm, d), lambda bi, i: (bi, i, 0))
    vec = pl.BlockSpec((1, 1, d), lambda bi, i: (bi, 0, 0))
    full = lambda shp: pl.BlockSpec(shp, lambda bi, i: (0, 0), pipeline_mode=pl.Buffered(1))
    picked = lambda r, c: pl.BlockSpec((None, None, r, c), lambda bi, i: (*which, 0, 0),
                                       pipeline_mode=pl.Buffered(1))
    ins, specs = [x], [row]
    if mixer is not None:
        a, gate_mix, w_o = mixer
        k = a.shape[2]
        ins += [a, gate_mix, w_o]
        specs += [pl.BlockSpec((1, tm, k), lambda bi, i: (bi, i, 0)), vec, full((k, d))]
    ins += [shift, scale, gate, g, wg, wu, wd]
    specs += [vec, vec, vec, full((1, d)), picked(d, f), picked(d, f), picked(f, d)]
    return pl.pallas_call(
        functools.partial(_ffn_kernel, tf=tf, with_mixer=mixer is not None),
        grid=(b, s // tm),
        in_specs=specs,
        out_specs=row,
        out_shape=jax.ShapeDtypeStruct((b, s, d), F32),
        compiler_params=_params("arbitrary", "arbitrary"),
        name="ffn_swiglu",
    )(*ins)


def _headnorm_pair(y, g, dim):
    lane = lax.broadcasted_iota(jnp.int32, (1, LANES), 1)
    left = lane < dim
    y2 = y * y
    ss_a = jnp.sum(jnp.where(left, y2, 0.0), axis=-1, keepdims=True)
    ss_b = jnp.sum(jnp.where(left, 0.0, y2), axis=-1, keepdims=True)
    inv = jnp.where(left, lax.rsqrt(ss_a * (1.0 / dim) + EPS), lax.rsqrt(ss_b * (1.0 / dim) + EPS))
    return y * inv * g


def _moba_proj_kernel(x_ref, sh_ref, sc_ref, g_ref, w_ref, wvt_ref, hg_ref, qk_ref, vt_ref, *, tn):
    h = _rms_mod(x_ref[0], g_ref[...], sh_ref[0], sc_ref[0]).astype(BF16)
    for n in range(w_ref.shape[1] // tn):
        res = _dot(h, w_ref[:, n * tn:(n + 1) * tn])
        for s in range(tn // LANES):
            sl = slice(n * tn + s * LANES, n * tn + (s + 1) * LANES)
            qk_ref[0, :, sl] = _headnorm_pair(res[:, s * LANES:(s + 1) * LANES], hg_ref[:, sl],
                                              MOBA_HEAD_DIM).astype(BF16)
    for n in range(wvt_ref.shape[0] // tn):
        rows = slice(n * tn, (n + 1) * tn)
        vt_ref[0, rows, :] = _dot_nt(wvt_ref[rows, :], h).astype(BF16)


def _moba_proj(x, shift, scale, g, w_qk, w_vt, head_gain, tm, tn):
    b, s, d = x.shape
    nqk = w_qk.shape[1]
    nv = w_vt.shape[0]
    vec = pl.BlockSpec((1, 1, d), lambda bi, i: (bi, 0, 0))
    full = lambda shp: pl.BlockSpec(shp, lambda bi, i: (0, 0))
    return pl.pallas_call(
        functools.partial(_moba_proj_kernel, tn=tn),
        grid=(b, s // tm),
        in_specs=[pl.BlockSpec((1, tm, d), lambda bi, i: (bi, i, 0)), vec, vec, full((1, d)),
                  full((d, nqk)), full((nv, d)), full((1, nqk))],
        out_specs=[pl.BlockSpec((1, tm, nqk), lambda bi, i: (bi, i, 0)),
                   pl.BlockSpec((1, nv, tm), lambda bi, i: (bi, 0, i))],
        out_shape=[jax.ShapeDtypeStruct((b, s, nqk), BF16), jax.ShapeDtypeStruct((b, nv, s), BF16)],
        compiler_params=_params("arbitrary", "arbitrary"),
        name="moba_qkv_proj",
    )(x, shift, scale, g, w_qk, w_vt, head_gain)


def _softmax_step_t(s, m, l, acc, vt, bounded):
    if bounded:
        p = jnp.exp2(s)
        return m, l + jnp.sum(p, axis=0, keepdims=True), acc + _dot(vt, p.astype(BF16))
    m_new = jnp.maximum(m, jnp.max(s, axis=0, keepdims=True))
    alpha = jnp.exp2(m - m_new)
    p = jnp.exp2(s - m_new)
    l_new = alpha * l + jnp.sum(p, axis=0, keepdims=True)
    acc_new = acc * alpha + _dot(vt, p.astype(BF16))
    return m_new, l_new, acc_new


def _moba_attn_kernel(qmin_ref, kmax_ref, qgmin_ref, kbmax_ref, q_ref, k_ref, vt_ref, bt_ref, pq_ref, pk_ref, o_ref,
                      km_scr, sel_scr, *, nb, t, npairs, bounded):
    blk = MOBA_BLOCK
    hd = MOBA_HEAD_DIM
    nsub = t // blk
    nh = 2 * npairs
    bi = pl.program_id(0)
    qi = pl.program_id(2)
    lane = lax.broadcasted_iota(jnp.int32, (1, LANES), 1)
    left = lane < hd

    @pl.when(qi == 0)
    def _():
        km_scr[...] = jnp.zeros_like(km_scr)
        for r in range(nb):
            kblk = k_ref[0, r * blk:(r + 1) * blk, :].astype(F32)
            km_scr[r:r + 1, :] = jnp.sum(kblk, axis=0, keepdims=True) * (1.0 / blk)

    q_heads = []
    for p in range(npairs):
        q2 = q_ref[0, :, p * LANES:(p + 1) * LANES]
        zero = jnp.zeros_like(q2)
        q_heads += [jnp.where(left, q2, zero), jnp.where(left, zero, q2)]

    col = lax.broadcasted_iota(jnp.int32, (1, t), 1)
    q_idx = qi * t + col
    q_sub = sum((col >= r * blk).astype(jnp.int32) for r in range(1, nsub))
    q_blk = qi * nsub + q_sub
    q_blk_start = q_blk * blk

    blk_row = lax.broadcasted_iota(jnp.int32, (nb, t), 0)
    for h in range(nh):
        km = km_scr[:, (h // 2) * LANES:(h // 2 + 1) * LANES]
        km_hi = km.astype(BF16)
        km_lo = (km - km_hi.astype(F32)).astype(BF16)
        gate = (_dot_nt(km_hi, q_heads[h]) + _dot_nt(km_lo, q_heads[h]))[:nb]
        gate = jnp.where(blk_row < q_blk, gate, -jnp.inf)
        sel = jnp.zeros((nb, t), F32)
        for _ in range(MOBA_TOPK):
            mx = jnp.max(gate, axis=0, keepdims=True)
            idx = jnp.min(jnp.where(gate == mx, blk_row, nb), axis=0, keepdims=True)
            pick = (blk_row == idx) & (mx > -jnp.inf)
            sel = jnp.where(pick, 1.0, sel)
            gate = jnp.where(pick, -jnp.inf, gate)
        sel_scr[h] = sel

    pq = pq_ref[0, 0]

    def gather_bias(h, dist):
        table = jnp.broadcast_to(bt_ref[h // 2, h % 2:h % 2 + 1, :], (dist.shape[0], LANES))
        return jnp.concatenate(
            [jnp.take_along_axis(table, dist[:, g * LANES:(g + 1) * LANES], axis=1, mode="promise_in_bounds")
             for g in range(dist.shape[1] // LANES)], axis=1)

    def far_bias(h):
        return bt_ref[h // 2, h % 2:h % 2 + 1, MAX_DIST:MAX_DIST + 1]

    def chosen_rows(h, j, lanes=slice(None)):
        return [sel_scr[h, pl.ds(j * nsub + r, 1), lanes] > 0.0 for r in range(nsub)]

    def own_chunk_bounded(carry):
        assert nsub == 2
        carry = list(carry)
        start = pl.multiple_of(qi * t, t)
        k2 = k_ref[0, pl.ds(start, t), :]
        vt = vt_ref[0, :, pl.ds(start, t)]
        pk0 = pk_ref[0, pl.ds(start, blk), :]
        pk1 = pk_ref[0, pl.ds(start + blk, blk), :]
        causal = (lax.broadcasted_iota(jnp.int32, (blk, blk), 0) <= lax.broadcasted_iota(jnp.int32, (blk, blk), 1))
        d00 = jnp.where(causal, jnp.clip(pq[:, :blk] - pk0, 0, MAX_DIST), MASKED_DIST)
        d01 = jnp.clip(pq[:, blk:] - pk0, 0, MAX_DIST)
        d11 = jnp.where(causal, jnp.clip(pq[:, blk:] - pk1, 0, MAX_DIST), MASKED_DIST)
        quads = []
        for h in range(nh):
            kh = k2[:, (h // 2) * LANES:(h // 2 + 1) * LANES]
            qh = q_heads[h]
            quads.append((_dot_nt(kh[:blk], qh[:blk]), _dot_nt(kh[:blk], qh[blk:]), _dot_nt(kh[blk:], qh[blk:])))
        for h in range(nh):
            m, l, acc = carry[h]
            s00, s01, s11 = quads[h]
            e00 = jnp.exp2(s00 + gather_bias(h, d00))
            e01 = jnp.exp2(s01 + gather_bias(h, d01))
            e11 = jnp.exp2(s11 + gather_bias(h, d11))
            w01 = jnp.where(chosen_rows(h, qi, slice(blk, t))[0], 1.0, 0.0)
            vt0 = vt[h * hd:(h + 1) * hd, :blk]
            vt1 = vt[h * hd:(h + 1) * hd, blk:]
            l_add = jnp.concatenate(
                [jnp.sum(e00, axis=0, keepdims=True),
                 jnp.sum(e01, axis=0, keepdims=True) * w01 + jnp.sum(e11, axis=0, keepdims=True)], axis=1)
            acc_add = jnp.concatenate(
                [_dot(vt0, e00.astype(BF16)),
                 _dot(vt0, e01.astype(BF16)) * w01 + _dot(vt1, e11.astype(BF16))], axis=1)
            carry[h] = (m, l + l_add, acc + acc_add)
        return tuple(carry)

    def chunk(j, carry, mode):
        carry = list(carry)
        start = pl.multiple_of(j * t, t)
        k2 = k_ref[0, pl.ds(start, t), :]
        vt = vt_ref[0, :, pl.ds(start, t)]
        s_heads = [_dot_nt(k2[:, (h // 2) * LANES:(h // 2 + 1) * LANES], q_heads[h]) for h in range(nh)]
        if mode in ("near", "diag"):
            dist = jnp.clip(pq - pk_ref[0, pl.ds(start, t), :], 0, MAX_DIST)
        if mode == "corner":
            dist_c = jnp.clip(pq[:, :LANES] - pk_ref[0, pl.ds(start + (nsub - 1) * blk, blk), :], 0, MAX_DIST)
        if mode == "diag":
            key_idx = start + lax.broadcasted_iota(jnp.int32, (blk, t), 0)
        for h in range(nh):
            m, l, acc = carry[h]
            s = s_heads[h]
            vt_h = vt[h * hd:(h + 1) * hd, :]
            chosen = chosen_rows(h, j)
            if mode in ("far", "corner") and bounded:
                two_c = jnp.exp2(far_bias(h))
                for r in range(nsub):
                    piece = s[r * blk:(r + 1) * blk]
                    w = jnp.where(chosen[r], two_c, 0.0)
                    if mode == "corner" and r == nsub - 1:
                        piece = jnp.concatenate([piece[:, :LANES] + gather_bias(h, dist_c), piece[:, LANES:]], axis=1)
                        w = jnp.where(col < LANES, jnp.where(chosen[r], 1.0, 0.0), w)
                    p = jnp.exp2(piece)
                    l = l + jnp.sum(p, axis=0, keepdims=True) * w
                    acc = acc + _dot(vt_h[:, r * blk:(r + 1) * blk], p.astype(BF16)) * w
                carry[h] = (m, l, acc)
            elif mode == "far":
                pieces = [s[r * blk:(r + 1) * blk] for r in range(nsub)]
                c = far_bias(h)
                tmax = [jnp.where(chosen[r], jnp.max(pieces[r], axis=0, keepdims=True) + c, NEG)
                        for r in range(nsub)]
                m_new = functools.reduce(jnp.maximum, tmax, m)
                alpha = jnp.exp2(m - m_new)
                ps = [jnp.exp2(pieces[r] - jnp.where(chosen[r], m_new - c, jnp.inf)) for r in range(nsub)]
                l_new = alpha * l + sum(jnp.sum(p, axis=0, keepdims=True) for p in ps)
                p = jnp.concatenate(ps, axis=0) if nsub > 1 else ps[0]
                carry[h] = (m_new, l_new, acc * alpha + _dot(vt_h, p.astype(BF16)))
            else:
                s = s + gather_bias(h, dist)
                masked = []
                for r in range(nsub):
                    piece = s[r * blk:(r + 1) * blk]
                    if mode == "diag":
                        kidx = key_idx + r * blk
                        keep_own = jnp.where(kidx <= q_idx, jnp.where(kidx >= q_blk_start, piece, NEG), NEG)
                        masked.append(jnp.where(chosen[r], piece, keep_own))
                    else:
                        masked.append(jnp.where(chosen[r], piece, NEG))
                s = jnp.concatenate(masked, axis=0) if nsub > 1 else masked[0]
                carry[h] = _softmax_step_t(s, m, l, acc, vt_h, bounded)
        return tuple(carry)

    def past_chunk(j, carry):
        q_lo = qmin_ref[bi, qi]
        far = q_lo - kmax_ref[bi, j] >= MAX_DIST
        if not bounded:
            return lax.cond(far, lambda c: chunk(j, c, "far"), lambda c: chunk(j, c, "near"), carry)
        groups = t // LANES
        q_rest = functools.reduce(jnp.minimum, [qgmin_ref[bi, qi * groups + g] for g in range(1, groups)])
        k_first = functools.reduce(jnp.maximum, [kbmax_ref[bi, j * nsub + r] for r in range(nsub - 1)])
        corner = (q_lo - k_first >= MAX_DIST) & (q_rest - kbmax_ref[bi, j * nsub + nsub - 1] >= MAX_DIST)
        return lax.cond(far, lambda c: chunk(j, c, "far"),
                        lambda c: lax.cond(corner, lambda c2: chunk(j, c2, "corner"),
                                           lambda c2: chunk(j, c2, "near"), c), carry)

    init = (jnp.full((1, t), NEG, F32), jnp.zeros((1, t), F32), jnp.zeros((hd, t), F32))
    if bounded:
        carry = lax.fori_loop(0, qi, past_chunk, own_chunk_bounded((init,) * nh))
    else:
        carry = lax.fori_loop(0, qi, past_chunk, (init,) * nh)
        carry = chunk(qi, carry, "diag")
    out_t = jnp.concatenate([acc / l for (_, l, acc) in carry], axis=0)
    o_ref[0] = out_t.T.astype(BF16)


def _moba_position_tables(positions, t):
    b, s = positions.shape
    pos_tiles = positions.reshape(b, s // t, t)
    return (jnp.min(pos_tiles, axis=-1), jnp.max(pos_tiles, axis=-1),
            jnp.min(positions.reshape(b, s // LANES, LANES), axis=-1),
            jnp.max(positions.reshape(b, s // MOBA_BLOCK, MOBA_BLOCK), axis=-1),
            positions.reshape(b, s // t, 1, t), positions.reshape(b, s, 1))


def _moba_attn(qk, vt, bias_table, pos_tables, t, npairs, bounded):
    b, s, _ = qk.shape
    nb = s // MOBA_BLOCK
    ngroups = MOBA_HEADS * MOBA_HEAD_DIM // (LANES * npairs)
    w = npairs * LANES
    q_min, k_max, qg_min, kb_max, pos_row, pos_col = pos_tables
    grid_spec = pltpu.PrefetchScalarGridSpec(
        num_scalar_prefetch=4,
        grid=(b, ngroups, s // t),
        in_specs=[pl.BlockSpec((1, t, w), lambda bi, g, qi, *_: (bi, qi, g)),
                  pl.BlockSpec((1, s, w), lambda bi, g, qi, *_: (bi, 0, ngroups + g)),
                  pl.BlockSpec((1, w, s), lambda bi, g, qi, *_: (bi, g, 0)),
                  pl.BlockSpec((npairs, 2, LANES), lambda bi, g, qi, *_: (g, 0, 0)),
                  pl.BlockSpec((1, 1, 1, t), lambda bi, g, qi, *_: (bi, qi, 0, 0)),
                  pl.BlockSpec((1, s, 1), lambda bi, g, qi, *_: (bi, 0, 0))],
        out_specs=pl.BlockSpec((1, t, w), lambda bi, g, qi, *_: (bi, qi, g)),
        scratch_shapes=[pltpu.VMEM((LANES, w), F32), pltpu.VMEM((2 * npairs, nb, t), F32)],
    )
    return pl.pallas_call(
        functools.partial(_moba_attn_kernel, nb=nb, t=t, npairs=npairs, bounded=bounded),
        grid_spec=grid_spec,
        out_shape=jax.ShapeDtypeStruct((b, s, MOBA_HEADS * MOBA_HEAD_DIM), BF16),
        compiler_params=_params("arbitrary", "arbitrary", "arbitrary"),
        name="moba_attention",
    )(q_min, k_max, qg_min, kb_max, qk, qk, vt, bias_table, pos_row, pos_col)


def _rope_tab_kernel(pos_ref, inv_ref, c_ref, s_ref):
    half = MLA_ROPE // 2
    ang = pos_ref[0].astype(F32) * inv_ref[...]
    lane = lax.broadcasted_iota(jnp.int32, (1, LANES), 1)
    sn = jnp.sin(ang)
    c_ref[0] = jnp.cos(ang)
    s_ref[0] = jnp.where(lane < MLA_NOPE + half, -sn, sn)


def _rope_tables(pos_col, inv_lane, tm):
    b, s, _ = pos_col.shape
    out = pl.BlockSpec((1, tm, LANES), lambda bi, i: (bi, i, 0))
    shape = jax.ShapeDtypeStruct((b, s, LANES), F32)
    return pl.pallas_call(
        _rope_tab_kernel,
        grid=(b, s // tm),
        in_specs=[pl.BlockSpec((1, tm, 1), lambda bi, i: (bi, i, 0)),
                  pl.BlockSpec((1, LANES), lambda bi, i: (0, 0))],
        out_specs=[out, out],
        out_shape=[shape, shape],
        compiler_params=_params("arbitrary", "arbitrary"),
        name="rope_tables",
    )(pos_col, inv_lane)


def _mla_q_kernel(x_ref, sh_ref, sc_ref, g_ref, wdq_ref, qag_ref, wuq_ref, wuqp_ref, qg_ref, qgp_ref,
                  c_ref, s_ref, o_ref):
    h = _rms_mod(x_ref[0], g_ref[...], sh_ref[0], sc_ref[0]).astype(BF16)
    dq = _dot(h, wdq_ref[...])
    dqn = (dq * lax.rsqrt(jnp.mean(dq * dq, axis=-1, keepdims=True) + EPS) * qag_ref[...]).astype(BF16)
    uq = _dot(dqn, wuq_ref[...])
    up = _dot(dqn, wuqp_ref[...])
    cg = c_ref[0] * qg_ref[...]
    sg = s_ref[0] * qgp_ref[...]
    scale = MLA_QK ** -0.5 * LOG2E
    for hd in range(MLA_HEADS):
        sl = slice(hd * LANES, (hd + 1) * LANES)
        u = uq[:, sl]
        inv = lax.rsqrt(jnp.sum(u * u, axis=-1, keepdims=True) * (1.0 / MLA_QK) + EPS) * scale
        o_ref[0, :, sl] = ((u * cg + up[:, sl] * sg) * inv).astype(BF16)


def _mla_q(x, shift, scale, g, wdq, qag, wuq_p, wuq_partner, qg_p, qg_partner, cos, sin, tm):
    b, s, d = x.shape
    r = wdq.shape[1]
    nq = wuq_p.shape[1]
    vec = pl.BlockSpec((1, 1, d), lambda bi, i: (bi, 0, 0))
    tab = pl.BlockSpec((1, tm, LANES), lambda bi, i: (bi, i, 0))
    full = lambda shp: pl.BlockSpec(shp, lambda bi, i: (0, 0))
    return pl.pallas_call(
        _mla_q_kernel,
        grid=(b, s // tm),
        in_specs=[pl.BlockSpec((1, tm, d), lambda bi, i: (bi, i, 0)), vec, vec, full((1, d)),
                  full((d, r)), full((1, r)), full((r, nq)), full((r, nq)), full((1, LANES)), full((1, LANES)),
                  tab, tab],
        out_specs=pl.BlockSpec((1, tm, nq), lambda bi, i: (bi, i, 0)),
        out_shape=jax.ShapeDtypeStruct((b, s, nq), BF16),
        compiler_params=_params("arbitrary", "arbitrary"),
        name="mla_q_path",
    )(x, shift, scale, g, wdq, qag, wuq_p, wuq_partner, qg_p, qg_partner, cos, sin)


def _mla_kv_kernel(x_ref, sh_ref, sc_ref, g_ref, wdkv_ref, kvg_ref, wuk_ref, wuvt_ref, kg_ref, kgp_ref,
                   c_ref, s_ref, k_ref, vt_ref):
    hs = _rms_mod(x_ref[0], g_ref[...], sh_ref[0], sc_ref[0]).astype(BF16)
    ck = _dot(hs, wdkv_ref[...])
    ckv = ck[:, :MLA_KV_LORA]
    ckvn = (ckv * lax.rsqrt(jnp.mean(ckv * ckv, axis=-1, keepdims=True) + EPS) * kvg_ref[...]).astype(BF16)
    kn = _dot(ckvn, wuk_ref[...])
    vt_ref[0] = _dot_nt(wuvt_ref[...], ckvn).astype(BF16)
    lane = lax.broadcasted_iota(jnp.int32, (1, LANES), 1)
    on_rope = (lane >= MLA_NOPE) & (lane < MLA_QK)
    tail = ck[:, MLA_KV_LORA:MLA_KV_LORA + LANES]
    kpe = jnp.where(on_rope, pltpu.roll(tail, MLA_NOPE, axis=1), 0.0)
    partner = pltpu.roll(tail, MLA_ROPE, axis=1)
    cg = c_ref[0] * kg_ref[...]
    sg = s_ref[0] * kgp_ref[...]
    rot = kpe * cg + partner * sg
    ss_pe = jnp.sum(kpe * kpe, axis=-1, keepdims=True)
    for hd in range(MLA_HEADS):
        sl = slice(hd * LANES, (hd + 1) * LANES)
        nope = kn[:, sl]
        inv = lax.rsqrt((jnp.sum(nope * nope, axis=-1, keepdims=True) + ss_pe) * (1.0 / MLA_QK) + EPS)
        k_ref[0, :, sl] = ((nope * cg + rot) * inv).astype(BF16)


def _mla_kv(x, shift, scale, g, wdkv_p, kvg, wuk_p, wuv_t, kg_p, kg_partner, cos, sin, tm):
    b, s, d = x.shape
    r = wdkv_p.shape[1]
    nk = wuk_p.shape[1]
    nv = wuv_t.shape[0]
    vec = pl.BlockSpec((1, 1, d), lambda bi, i: (bi, 0, 0))
    tab = pl.BlockSpec((1, tm, LANES), lambda bi, i: (bi, i, 0))
    full = lambda shp: pl.BlockSpec(shp, lambda bi, i: (0, 0))
    return pl.pallas_call(
        _mla_kv_kernel,
        grid=(b, s // tm),
        in_specs=[pl.BlockSpec((1, tm, d), lambda bi, i: (bi, i, 0)), vec, vec, full((1, d)),
                  full((d, r)), full((1, MLA_KV_LORA)), full((MLA_KV_LORA, nk)), full((nv, MLA_KV_LORA)),
                  full((1, LANES)), full((1, LANES)), tab, tab],
        out_specs=[pl.BlockSpec((1, tm, nk), lambda bi, i: (bi, i, 0)),
                   pl.BlockSpec((1, nv, tm), lambda bi, i: (bi, 0, i))],
        out_shape=[jax.ShapeDtypeStruct((b, s, nk), BF16), jax.ShapeDtypeStruct((b, nv, s), BF16)],
        compiler_params=_params("arbitrary", "arbitrary"),
        name="mla_shared_kv",
    )(x, shift, scale, g, wdkv_p, kvg, wuk_p, wuv_t, kg_p, kg_partner, cos, sin)


def _mla_attn_kernel(q_ref, k_ref, vt_ref, o_ref, *, t, nh, bounded):
    qi = pl.program_id(2)
    q2 = q_ref[0]
    q_heads = [q2[:, h * LANES:(h + 1) * LANES] for h in range(nh)]

    def chunk(j, carry, masked):
        carry = list(carry)
        start = pl.multiple_of(j * t, t)
        k2 = k_ref[0, pl.ds(start, t), :]
        vt = vt_ref[0, :, pl.ds(start, t)]
        s_heads = [_dot_nt(k2[:, h * LANES:(h + 1) * LANES], q_heads[h]) for h in range(nh)]
        if masked:
            row = lax.broadcasted_iota(jnp.int32, (t, t), 0)
            col = lax.broadcasted_iota(jnp.int32, (t, t), 1)
            allowed = row <= col
        for h in range(nh):
            m, l, acc = carry[h]
            s = jnp.where(allowed, s_heads[h], NEG) if masked else s_heads[h]
            carry[h] = _softmax_step_t(s, m, l, acc, vt[h * MLA_V:(h + 1) * MLA_V, :], bounded)
        return tuple(carry)

    def own_chunk_bounded(carry):
        carry = list(carry)
        half = t // 2
        start = pl.multiple_of(qi * t, t)
        k2 = k_ref[0, pl.ds(start, t), :]
        vt = vt_ref[0, :, pl.ds(start, t)]
        causal = (lax.broadcasted_iota(jnp.int32, (half, half), 0) <= lax.broadcasted_iota(jnp.int32, (half, half), 1))
        quads = []
        for h in range(nh):
            kh = k2[:, h * LANES:(h + 1) * LANES]
            qh = q_heads[h]
            quads.append((_dot_nt(kh[:half], qh[:half]), _dot_nt(kh[:half], qh[half:]), _dot_nt(kh[half:], qh[half:])))
        for h in range(nh):
            m, l, acc = carry[h]
            s00, s01, s11 = quads[h]
            e00 = jnp.exp2(jnp.where(causal, s00, NEG))
            e01 = jnp.exp2(s01)
            e11 = jnp.exp2(jnp.where(causal, s11, NEG))
            vt0 = vt[h * MLA_V:(h + 1) * MLA_V, :half]
            vt1 = vt[h * MLA_V:(h + 1) * MLA_V, half:]
            l_add = jnp.concatenate(
                [jnp.sum(e00, axis=0, keepdims=True),
                 jnp.sum(e01, axis=0, keepdims=True) + jnp.sum(e11, axis=0, keepdims=True)], axis=1)
            acc_add = jnp.concatenate(
                [_dot(vt0, e00.astype(BF16)), _dot(vt0, e01.astype(BF16)) + _dot(vt1, e11.astype(BF16))], axis=1)
            carry[h] = (m, l + l_add, acc + acc_add)
        return tuple(carry)

    init = (jnp.full((1, t), NEG, F32), jnp.zeros((1, t), F32), jnp.zeros((MLA_V, t), F32))
    carry = lax.fori_loop(0, qi, lambda j, c: chunk(j, c, False), (init,) * nh)
    carry = own_chunk_bounded(carry) if bounded else chunk(qi, carry, True)
    out_t = jnp.concatenate([acc / l for (_, l, acc) in carry], axis=0)
    o_ref[0] = out_t.T.astype(BF16)


def _mla_attn(q, k, vt, t, nh, bounded):
    b, s, _ = q.shape
    return pl.pallas_call(
        functools.partial(_mla_attn_kernel, t=t, nh=nh, bounded=bounded),
        grid=(b, MLA_HEADS // nh, s // t),
        in_specs=[pl.BlockSpec((1, t, nh * LANES), lambda bi, g, qi: (bi, qi, g)),
                  pl.BlockSpec((1, s, nh * LANES), lambda bi, g, qi: (bi, 0, g)),
                  pl.BlockSpec((1, nh * MLA_V, s), lambda bi, g, qi: (bi, g, 0))],
        out_specs=pl.BlockSpec((1, t, nh * MLA_V), lambda bi, g, qi: (bi, qi, g)),
        out_shape=jax.ShapeDtypeStruct((b, s, MLA_HEADS * MLA_V), BF16),
        compiler_params=_params("arbitrary", "arbitrary", "arbitrary"),
        name="mla_attention",
    )(q, k, vt)


def _t5_bias_table(rel_bias):
    n = jnp.arange(MASKED_DIST)
    max_exact = REL_BUCKETS // 2
    nf = jnp.maximum(n, 1).astype(F32)
    large = max_exact + (jnp.log(nf / max_exact) / math.log(REL_MAX_DIST / max_exact)
                         * (REL_BUCKETS - max_exact)).astype(jnp.int32)
    large = jnp.minimum(large, REL_BUCKETS - 1)
    bucket = jnp.where(n < max_exact, n, large)
    table = jnp.concatenate([rel_bias.T[:, bucket] * LOG2E, jnp.full((MOBA_HEADS, 1), NEG, F32)], axis=1)
    return table.reshape(MOBA_HEADS // 2, 2, BIAS_TABLE)


def _pad_heads(w, width):
    k = w.shape[0]
    w3 = w.reshape(k, -1, width)
    return jnp.pad(w3, ((0, 0), (0, 0), (0, LANES - width))).reshape(k, -1)


def _pad_lanes(g):
    return jnp.pad(g, (0, LANES - g.shape[0])).reshape(1, LANES)


def _swap_rope_halves(a):
    half = MLA_ROPE // 2
    return jnp.concatenate([jnp.zeros_like(a[..., :MLA_NOPE]), a[..., MLA_NOPE + half:],
                            a[..., MLA_NOPE:MLA_NOPE + half]], axis=-1)


def _scores_bounded(q_gain, k_gain, dim, scale, extra=0.0):
    bound = dim * jnp.max(jnp.abs(q_gain)) * jnp.max(jnp.abs(k_gain)) * scale * BOUND_SLACK + extra
    return bound <= SCORE_BOUND_MAX


def kernel(x, c, positions, ada_w, ada_b, norm_g, ffn_w_gate, ffn_w_up, ffn_w_down, rel_bias,
           moba_w_qkv, moba_q_g, moba_k_g, moba_w_o, kv_ada_w, kv_ada_b, kv_norm_g, w_dkv,
           kv_a_norm_g, w_uk, w_uv, mla_k_g, mla_w_dq, mla_q_a_norm_g, mla_w_uq, mla_q_g, mla_w_o):
    b, s, d = x.shape
    depth = ada_w.shape[0]
    n_moba = moba_w_qkv.shape[0]
    assert b <= SUBLANES and s % max(FFN_ROWS, MOBA_TILE, MLA_TILE) == 0 and MOBA_TILE == 2 * MOBA_BLOCK
    tm_ffn = FFN_ROWS
    tm_proj = PROJ_ROWS

    c_pad = jnp.pad(c, ((0, SUBLANES - b), (0, 0)))
    mods = _mods(c_pad, ada_w, ada_b.reshape(depth, 1, -1), tn=ADALN_COLS)[:, :b]
    kv_mods = _mods(c_pad, kv_ada_w[None], kv_ada_b.reshape(1, 1, -1), tn=d)[0, :b]

    def mod_vecs(m):
        return [v.reshape(b, 1, d) for v in jnp.split(m, m.shape[-1] // d, axis=-1)]

    w_gate, w_up, w_down = (w.astype(BF16) for w in (ffn_w_gate, ffn_w_up, ffn_w_down))

    for layer in range(depth):
        sh1, sc1, g1, sh2, sc2, g2, sh3, sc3, g3 = mod_vecs(mods[layer])
        ng = norm_g[layer]

        def ffn(xin, idx, sh, sc, gt, mixer=None):
            return _ffn(xin, sh, sc, gt, ng[2 * idx:2 * idx + 1], w_gate, w_up, w_down, (layer, idx),
                        tm_ffn, FFN_CHUNK, mixer)

        x = ffn(x, 0, sh1, sc1, g1)

        if layer < n_moba:
            n_qk = 2 * MOBA_HEADS * MOBA_HEAD_DIM
            q_scale = MOBA_HEAD_DIM ** -0.5 * LOG2E
            head_gain = jnp.concatenate([jnp.tile(moba_q_g[layer] * q_scale, MOBA_HEADS),
                                         jnp.tile(moba_k_g[layer], MOBA_HEADS)]).reshape(1, -1)
            w_qkv = moba_w_qkv[layer]
            qk, vt = _moba_proj(x, sh2, sc2, ng[1:2], w_qkv[:, :n_qk].astype(BF16),
                                w_qkv[:, n_qk:].T.astype(BF16), head_gain, tm_proj, PROJ_COLS)
            bias_table = _t5_bias_table(rel_bias)
            moba = functools.partial(_moba_attn, qk, vt, bias_table, _moba_position_tables(positions, MOBA_TILE),
                                     MOBA_TILE, ATTN_HEADS // 2)
            ok = _scores_bounded(moba_q_g[layer], moba_k_g[layer], MOBA_HEAD_DIM, q_scale,
                                 jnp.max(jnp.abs(rel_bias)) * LOG2E)
            o = lax.cond(ok, lambda: moba(True), lambda: moba(False))
            w_o = moba_w_o[layer]
        else:
            j = layer - n_moba
            if j == 0:
                half = MLA_ROPE // 2
                inv = ROPE_BASE ** (-jnp.arange(half, dtype=F32) / half)
                inv_lane = jnp.zeros((LANES,), F32).at[MLA_NOPE:MLA_QK].set(jnp.tile(inv, 2)).reshape(1, LANES)
                cos, sin = _rope_tables(positions.reshape(b, s, 1), inv_lane, tm_proj)
                kv_sh, kv_sc = mod_vecs(kv_mods)
                w_kpe = w_dkv[:, MLA_KV_LORA:]
                w_kpe_swapped = jnp.concatenate([w_kpe[:, half:], w_kpe[:, :half]], axis=1)
                wdkv_p = jnp.pad(jnp.concatenate([w_dkv, w_kpe_swapped], axis=1),
                                 ((0, 0), (0, LANES - 2 * MLA_ROPE))).astype(BF16)
                shared_k, shared_v = _mla_kv(
                    x, kv_sh, kv_sc, kv_norm_g.reshape(1, d), wdkv_p, kv_a_norm_g.reshape(1, -1),
                    _pad_heads(w_uk, MLA_NOPE).astype(BF16), w_uv.T.astype(BF16), _pad_lanes(mla_k_g),
                    _pad_lanes(_swap_rope_halves(mla_k_g)), cos, sin, tm_proj)
            w_uq = mla_w_uq[j]
            w_uq_partner = _swap_rope_halves(w_uq.reshape(w_uq.shape[0], MLA_HEADS, MLA_QK)).reshape(w_uq.shape)
            q = _mla_q(x, sh2, sc2, ng[1:2], mla_w_dq[j].astype(BF16), mla_q_a_norm_g[j].reshape(1, -1),
                       _pad_heads(w_uq, MLA_QK).astype(BF16), _pad_heads(w_uq_partner, MLA_QK).astype(BF16),
                       _pad_lanes(mla_q_g[j]), _pad_lanes(_swap_rope_halves(mla_q_g[j])), cos, sin, tm_proj)
            mla = functools.partial(_mla_attn, q, shared_k, shared_v, MLA_TILE, ATTN_HEADS)
            ok = _scores_bounded(mla_q_g[j], mla_k_g, MLA_QK, MLA_QK ** -0.5 * LOG2E)
            o = lax.cond(ok, lambda: mla(True), lambda: mla(False))
            w_o = mla_w_o[j]

        x = ffn(x, 1, sh3, sc3, g3, mixer=(o, g2, w_o.astype(BF16)))
    return x
```

```python
import functools
import math

import jax
import jax.numpy as jnp
from jax import lax
from jax.experimental import pallas as pl
from jax.experimental.pallas import tpu as pltpu

F32 = jnp.float32
BF16 = jnp.bfloat16

LANES = 128
SUBLANES = 8
VMEM_LIMIT_BYTES = 56 * 1024 * 1024

MOBA_HEADS = 16
MOBA_HEAD_DIM = 64
MOBA_BLOCK = 256
MOBA_TOPK = 3
REL_BUCKETS = 32
REL_MAX_DIST = 128
MLA_HEADS = 16
MLA_KV_LORA = 256
MLA_NOPE = 64
MLA_ROPE = 32
MLA_V = 64
MLA_QK = MLA_NOPE + MLA_ROPE
ROPE_BASE = 10000.0
EPS = 1e-6
NEG = -1e30

FFN_ROWS = 1024
FFN_CHUNK = 256
PROJ_ROWS = 512
PROJ_COLS = 512
ADALN_COLS = 2304
MOBA_TILE = 512
MLA_TILE = 512
ATTN_HEADS = 8
BOUND_SLACK = 1.01
LOG2E = math.log2(math.e)
BIAS_TABLE = LANES
MAX_DIST = BIAS_TABLE - 2
MASKED_DIST = BIAS_TABLE - 1
SCORE_BOUND_MAX = 60.0


def _params(*sem):
    return pltpu.CompilerParams(dimension_semantics=sem, vmem_limit_bytes=VMEM_LIMIT_BYTES)


def _dot(a, b):
    return jnp.dot(a, b, preferred_element_type=F32)


def _dot_nt(a, b):
    return lax.dot_general(a, b, (((1,), (1,)), ((), ())), preferred_element_type=F32)


def _rms_mod(x, g, shift, scale):
    ms = jnp.mean(x * x, axis=-1, keepdims=True)
    y = x * lax.rsqrt(ms + EPS) * g
    return y * (1.0 + scale) + shift


def _mod_spec(mod, d):
    _, layer, k = mod
    return pl.BlockSpec((None, SUBLANES, d), lambda bi, *_: (layer, 0, k))


def _mod_row(ref):
    return ref[pl.ds(pl.program_id(0), 1), :]


def _mods_kernel(c_ref, w_ref, b_ref, o_ref):
    c = c_ref[...]
    ca = c * jax.nn.sigmoid(c)
    w = w_ref[0]
    w_hi = w.astype(BF16)
    w_lo = (w - w_hi.astype(F32)).astype(BF16)
    ca_hi = ca.astype(BF16)
    ca_lo = (ca - ca_hi.astype(F32)).astype(BF16)
    o_ref[0] = _dot(ca_hi, w_hi) + _dot(ca_lo, w_hi) + _dot(ca_hi, w_lo) + b_ref[0]


def _mods(c_pad, w, b, tn):
    nl, d, n = w.shape
    return pl.pallas_call(
        _mods_kernel,
        grid=(nl, n // tn),
        in_specs=[pl.BlockSpec((SUBLANES, d), lambda l, j: (0, 0)),
                  pl.BlockSpec((1, d, tn), lambda l, j: (l, 0, j)),
                  pl.BlockSpec((1, 1, tn), lambda l, j: (l, 0, j))],
        out_specs=pl.BlockSpec((1, SUBLANES, tn), lambda l, j: (l, 0, j)),
        out_shape=jax.ShapeDtypeStruct((nl, SUBLANES, n), F32),
        compiler_params=_params("arbitrary", "arbitrary"),
        name="adaln_mods",
    )(c_pad, w, b)


def _ffn_kernel(*refs, tf, with_mixer):
    if with_mixer:
        x_ref, a_ref, gm_ref, wo_ref = refs[:4]
        refs = refs[4:]
        x = x_ref[0] + _mod_row(gm_ref) * _dot(a_ref[0], wo_ref[...])
    else:
        x_ref = refs[0]
        refs = refs[1:]
        x = x_ref[0]
    sh_ref, sc_ref, gt_ref, g_ref, wg_ref, wu_ref, wd_ref, o_ref = refs
    h = _rms_mod(x, g_ref[...], _mod_row(sh_ref), _mod_row(sc_ref)).astype(BF16)
    acc = None
    for f in range(wg_ref.shape[1] // tf):
        cols = slice(f * tf, (f + 1) * tf)
        g = _dot(h, wg_ref[:, cols])
        u = _dot(h, wu_ref[:, cols])
        a = (g * jax.nn.sigmoid(g) * u).astype(BF16)
        d = _dot(a, wd_ref[cols, :])
        acc = d if acc is None else acc + d
    o_ref[0] = x + (0.5 * _mod_row(gt_ref)) * acc


def _ffn(x, shift, scale, gate, g, wg, wu, wd, which, tm, tf, mixer=None):
    b, s, d = x.shape
    f = wg.shape[-1]
    row = pl.BlockSpec((1, tm, d), lambda bi, i: (bi, i, 0))
    full = lambda shp: pl.BlockSpec(shp, lambda bi, i: (0, 0), pipeline_mode=pl.Buffered(1))
    picked = lambda r, c: pl.BlockSpec((None, None, r, c), lambda bi, i: (*which, 0, 0),
                                       pipeline_mode=pl.Buffered(1))
    ins, specs = [x], [row]
    if mixer is not None:
        a, gate_mix, w_o = mixer
        k = a.shape[2]
        ins += [a, gate_mix[0], w_o]
        specs += [pl.BlockSpec((1, tm, k), lambda bi, i: (bi, i, 0)), _mod_spec(gate_mix, d), full((k, d))]
    ins += [shift[0], scale[0], gate[0], g, wg, wu, wd]
    specs += [_mod_spec(shift, d), _mod_spec(scale, d), _mod_spec(gate, d), full((1, d)),
              picked(d, f), picked(d, f), picked(f, d)]
    return pl.pallas_call(
        functools.partial(_ffn_kernel, tf=tf, with_mixer=mixer is not None),
        grid=(b, s // tm),
        in_specs=specs,
        out_specs=row,
        out_shape=jax.ShapeDtypeStruct((b, s, d), F32),
        compiler_params=_params("arbitrary", "arbitrary"),
        name="ffn_swiglu",
    )(*ins)


def _headnorm_pair(y, g, dim):
    lane = lax.broadcasted_iota(jnp.int32, (1, LANES), 1)
    left = lane < dim
    y2 = y * y
    ss_a = jnp.sum(jnp.where(left, y2, 0.0), axis=-1, keepdims=True)
    ss_b = jnp.sum(jnp.where(left, 0.0, y2), axis=-1, keepdims=True)
    inv = jnp.where(left, lax.rsqrt(ss_a * (1.0 / dim) + EPS), lax.rsqrt(ss_b * (1.0 / dim) + EPS))
    return y * inv * g


def _moba_proj_kernel(x_ref, sh_ref, sc_ref, g_ref, w_ref, wvt_ref, hg_ref, qk_ref, vt_ref, *, tn):
    h = _rms_mod(x_ref[0], g_ref[...], _mod_row(sh_ref), _mod_row(sc_ref)).astype(BF16)
    for n in range(w_ref.shape[1] // tn):
        res = _dot(h, w_ref[:, n * tn:(n + 1) * tn])
        for s in range(tn // LANES):
            sl = slice(n * tn + s * LANES, n * tn + (s + 1) * LANES)
            qk_ref[0, :, sl] = _headnorm_pair(res[:, s * LANES:(s + 1) * LANES], hg_ref[:, sl],
                                              MOBA_HEAD_DIM).astype(BF16)
    for n in range(wvt_ref.shape[0] // tn):
        rows = slice(n * tn, (n + 1) * tn)
        vt_ref[0, rows, :] = _dot_nt(wvt_ref[rows, :], h).astype(BF16)


def _moba_proj(x, shift, scale, g, w_qk, w_vt, head_gain, tm, tn):
    b, s, d = x.shape
    nqk = w_qk.shape[1]
    nv = w_vt.shape[0]
    full = lambda shp: pl.BlockSpec(shp, lambda bi, i: (0, 0))
    return pl.pallas_call(
        functools.partial(_moba_proj_kernel, tn=tn),
        grid=(b, s // tm),
        in_specs=[pl.BlockSpec((1, tm, d), lambda bi, i: (bi, i, 0)), _mod_spec(shift, d), _mod_spec(scale, d),
                  full((1, d)),
                  full((d, nqk)), full((nv, d)), full((1, nqk))],
        out_specs=[pl.BlockSpec((1, tm, nqk), lambda bi, i: (bi, i, 0)),
                   pl.BlockSpec((1, nv, tm), lambda bi, i: (bi, 0, i))],
        out_shape=[jax.ShapeDtypeStruct((b, s, nqk), BF16), jax.ShapeDtypeStruct((b, nv, s), BF16)],
        compiler_params=_params("arbitrary", "arbitrary"),
        name="moba_qkv_proj",
    )(x, shift[0], scale[0], g, w_qk, w_vt, head_gain)


def _softmax_step_t(s, m, l, acc, vt, bounded):
    if bounded:
        p = jnp.exp2(s)
        return m, l + jnp.sum(p, axis=0, keepdims=True), acc + _dot(vt, p.astype(BF16))
    m_new = jnp.maximum(m, jnp.max(s, axis=0, keepdims=True))
    alpha = jnp.exp2(m - m_new)
    p = jnp.exp2(s - m_new)
    l_new = alpha * l + jnp.sum(p, axis=0, keepdims=True)
    acc_new = acc * alpha + _dot(vt, p.astype(BF16))
    return m_new, l_new, acc_new


def _moba_attn_kernel(qmin_ref, kmax_ref, qgmin_ref, kbmax_ref, q_ref, k_ref, vt_ref, bt_ref, pq_ref, pk_ref, o_ref,
                      km_scr, sel_scr, *, nb, t, npairs, bounded):
    blk = MOBA_BLOCK
    hd = MOBA_HEAD_DIM
    nsub = t // blk
    nh = 2 * npairs
    bi = pl.program_id(0)
    qi = pl.program_id(2)
    lane = lax.broadcasted_iota(jnp.int32, (1, LANES), 1)
    left = lane < hd

    @pl.when(qi == 0)
    def _():
        km_scr[...] = jnp.zeros_like(km_scr)
        for r in range(nb):
            kblk = k_ref[0, r * blk:(r + 1) * blk, :].astype(F32)
            km_scr[r:r + 1, :] = jnp.sum(kblk, axis=0, keepdims=True) * (1.0 / blk)

    q_heads = []
    for p in range(npairs):
        q2 = q_ref[0, :, p * LANES:(p + 1) * LANES]
        zero = jnp.zeros_like(q2)
        q_heads += [jnp.where(left, q2, zero), jnp.where(left, zero, q2)]

    col = lax.broadcasted_iota(jnp.int32, (1, t), 1)
    q_idx = qi * t + col
    q_sub = sum((col >= r * blk).astype(jnp.int32) for r in range(1, nsub))
    q_blk = qi * nsub + q_sub
    q_blk_start = q_blk * blk

    blk_row = lax.broadcasted_iota(jnp.int32, (nb, t), 0)
    for h in range(nh):
        km = km_scr[:, (h // 2) * LANES:(h // 2 + 1) * LANES]
        km_hi = km.astype(BF16)
        km_lo = (km - km_hi.astype(F32)).astype(BF16)
        gate = (_dot_nt(km_hi, q_heads[h]) + _dot_nt(km_lo, q_heads[h]))[:nb]
        gate = jnp.where(blk_row < q_blk, gate, -jnp.inf)
        sel = jnp.zeros((nb, t), F32)
        for _ in range(MOBA_TOPK):
            mx = jnp.max(gate, axis=0, keepdims=True)
            idx = jnp.min(jnp.where(gate == mx, blk_row, nb), axis=0, keepdims=True)
            pick = (blk_row == idx) & (mx > -jnp.inf)
            sel = jnp.where(pick, 1.0, sel)
            gate = jnp.where(pick, -jnp.inf, gate)
        sel_scr[h] = sel

    pq = pq_ref[0, 0]

    def gather_bias(h, dist):
        table = jnp.broadcast_to(bt_ref[h // 2, h % 2:h % 2 + 1, :], (dist.shape[0], LANES))
        return jnp.concatenate(
            [jnp.take_along_axis(table, dist[:, g * LANES:(g + 1) * LANES], axis=1, mode="promise_in_bounds")
             for g in range(dist.shape[1] // LANES)], axis=1)

    def far_bias(h):
        return bt_ref[h // 2, h % 2:h % 2 + 1, MAX_DIST:MAX_DIST + 1]

    def chosen_rows(h, j, lanes=slice(None)):
        return [sel_scr[h, pl.ds(j * nsub + r, 1), lanes] > 0.0 for r in range(nsub)]

    def own_chunk_bounded(carry):
        assert nsub == 2
        carry = list(carry)
        start = pl.multiple_of(qi * t, t)
        k2 = k_ref[0, pl.ds(start, t), :]
        vt = vt_ref[0, :, pl.ds(start, t)]
        pk0 = pk_ref[0, pl.ds(start, blk), :]
        pk1 = pk_ref[0, pl.ds(start + blk, blk), :]
        causal = (lax.broadcasted_iota(jnp.int32, (blk, blk), 0) <= lax.broadcasted_iota(jnp.int32, (blk, blk), 1))
        d00 = jnp.where(causal, jnp.clip(pq[:, :blk] - pk0, 0, MAX_DIST), MASKED_DIST)
        d01 = jnp.clip(pq[:, blk:] - pk0, 0, MAX_DIST)
        d11 = jnp.where(causal, jnp.clip(pq[:, blk:] - pk1, 0, MAX_DIST), MASKED_DIST)
        quads = []
        for h in range(nh):
            kh = k2[:, (h // 2) * LANES:(h // 2 + 1) * LANES]
            qh = q_heads[h]
            quads.append((_dot_nt(kh[:blk], qh[:blk]), _dot_nt(kh[:blk], qh[blk:]), _dot_nt(kh[blk:], qh[blk:])))
        for h in range(nh):
            m, l, acc = carry[h]
            s00, s01, s11 = quads[h]
            e00 = jnp.exp2(s00 + gather_bias(h, d00))
            e01 = jnp.exp2(s01 + gather_bias(h, d01))
            e11 = jnp.exp2(s11 + gather_bias(h, d11))
            w01 = jnp.where(chosen_rows(h, qi, slice(blk, t))[0], 1.0, 0.0)
            vt0 = vt[h * hd:(h + 1) * hd, :blk]
            vt1 = vt[h * hd:(h + 1) * hd, blk:]
            l_add = jnp.concatenate(
                [jnp.sum(e00, axis=0, keepdims=True),
                 jnp.sum(e01, axis=0, keepdims=True) * w01 + jnp.sum(e11, axis=0, keepdims=True)], axis=1)
            acc_add = jnp.concatenate(
                [_dot(vt0, e00.astype(BF16)),
                 _dot(vt0, e01.astype(BF16)) * w01 + _dot(vt1, e11.astype(BF16))], axis=1)
            carry[h] = (m, l + l_add, acc + acc_add)
        return tuple(carry)

    def chunk(j, carry, mode):
        carry = list(carry)
        start = pl.multiple_of(j * t, t)
        k2 = k_ref[0, pl.ds(start, t), :]
        vt = vt_ref[0, :, pl.ds(start, t)]
        s_heads = [_dot_nt(k2[:, (h // 2) * LANES:(h // 2 + 1) * LANES], q_heads[h]) for h in range(nh)]
        if mode in ("near", "diag"):
            dist = jnp.clip(pq - pk_ref[0, pl.ds(start, t), :], 0, MAX_DIST)
        if mode == "corner":
            dist_c = jnp.clip(pq[:, :LANES] - pk_ref[0, pl.ds(start + (nsub - 1) * blk, blk), :], 0, MAX_DIST)
        if mode == "diag":
            key_idx = start + lax.broadcasted_iota(jnp.int32, (blk, t), 0)
        for h in range(nh):
            m, l, acc = carry[h]
            s = s_heads[h]
            vt_h = vt[h * hd:(h + 1) * hd, :]
            chosen = chosen_rows(h, j)
            if mode in ("far", "corner") and bounded:
                two_c = jnp.exp2(far_bias(h))
                for r in range(nsub):
                    piece = s[r * blk:(r + 1) * blk]
                    w = jnp.where(chosen[r], two_c, 0.0)
                    if mode == "corner" and r == nsub - 1:
                        piece = jnp.concatenate([piece[:, :LANES] + gather_bias(h, dist_c), piece[:, LANES:]], axis=1)
                        w = jnp.where(col < LANES, jnp.where(chosen[r], 1.0, 0.0), w)
                    p = jnp.exp2(piece)
                    l = l + jnp.sum(p, axis=0, keepdims=True) * w
                    acc = acc + _dot(vt_h[:, r * blk:(r + 1) * blk], p.astype(BF16)) * w
                carry[h] = (m, l, acc)
            elif mode == "far":
                pieces = [s[r * blk:(r + 1) * blk] for r in range(nsub)]
                c = far_bias(h)
                tmax = [jnp.where(chosen[r], jnp.max(pieces[r], axis=0, keepdims=True) + c, NEG)
                        for r in range(nsub)]
                m_new = functools.reduce(jnp.maximum, tmax, m)
                alpha = jnp.exp2(m - m_new)
                ps = [jnp.exp2(pieces[r] - jnp.where(chosen[r], m_new - c, jnp.inf)) for r in range(nsub)]
                l_new = alpha * l + sum(jnp.sum(p, axis=0, keepdims=True) for p in ps)
                p = jnp.concatenate(ps, axis=0) if nsub > 1 else ps[0]
                carry[h] = (m_new, l_new, acc * alpha + _dot(vt_h, p.astype(BF16)))
            else:
                s = s + gather_bias(h, dist)
                masked = []
                for r in range(nsub):
                    piece = s[r * blk:(r + 1) * blk]
                    if mode == "diag":
                        kidx = key_idx + r * blk
                        keep_own = jnp.where(kidx <= q_idx, jnp.where(kidx >= q_blk_start, piece, NEG), NEG)
                        masked.append(jnp.where(chosen[r], piece, keep_own))
                    else:
                        masked.append(jnp.where(chosen[r], piece, NEG))
                s = jnp.concatenate(masked, axis=0) if nsub > 1 else masked[0]
                carry[h] = _softmax_step_t(s, m, l, acc, vt_h, bounded)
        return tuple(carry)

    def past_chunk(j, carry):
        q_lo = qmin_ref[bi, qi]
        far = q_lo - kmax_ref[bi, j] >= MAX_DIST
        if not bounded:
            return lax.cond(far, lambda c: chunk(j, c, "far"), lambda c: chunk(j, c, "near"), carry)
        groups = t // LANES
        q_rest = functools.reduce(jnp.minimum, [qgmin_ref[bi, qi * groups + g] for g in range(1, groups)])
        k_first = functools.reduce(jnp.maximum, [kbmax_ref[bi, j * nsub + r] for r in range(nsub - 1)])
        corner = (q_lo - k_first >= MAX_DIST) & (q_rest - kbmax_ref[bi, j * nsub + nsub - 1] >= MAX_DIST)
        return lax.cond(far, lambda c: chunk(j, c, "far"),
                        lambda c: lax.cond(corner, lambda c2: chunk(j, c2, "corner"),
                                           lambda c2: chunk(j, c2, "near"), c), carry)

    init = (jnp.full((1, t), NEG, F32), jnp.zeros((1, t), F32), jnp.zeros((hd, t), F32))
    if bounded:
        carry = lax.fori_loop(0, qi, past_chunk, own_chunk_bounded((init,) * nh))
    else:
        carry = lax.fori_loop(0, qi, past_chunk, (init,) * nh)
        carry = chunk(qi, carry, "diag")
    out_t = jnp.concatenate([acc / l for (_, l, acc) in carry], axis=0)
    o_ref[0] = out_t.T.astype(BF16)


def _moba_position_tables(positions, t):
    b, s = positions.shape
    pos_tiles = positions.reshape(b, s // t, t)
    return (jnp.min(pos_tiles, axis=-1), jnp.max(pos_tiles, axis=-1),
            jnp.min(positions.reshape(b, s // LANES, LANES), axis=-1),
            jnp.max(positions.reshape(b, s // MOBA_BLOCK, MOBA_BLOCK), axis=-1),
            positions.reshape(b, s // t, 1, t), positions.reshape(b, s, 1))


def _moba_attn(qk, vt, bias_table, pos_tables, t, npairs, bounded):
    b, s, _ = qk.shape
    nb = s // MOBA_BLOCK
    ngroups = MOBA_HEADS * MOBA_HEAD_DIM // (LANES * npairs)
    w = npairs * LANES
    q_min, k_max, qg_min, kb_max, pos_row, pos_col = pos_tables
    grid_spec = pltpu.PrefetchScalarGridSpec(
        num_scalar_prefetch=4,
        grid=(b, ngroups, s // t),
        in_specs=[pl.BlockSpec((1, t, w), lambda bi, g, qi, *_: (bi, qi, g)),
                  pl.BlockSpec((1, s, w), lambda bi, g, qi, *_: (bi, 0, ngroups + g)),
                  pl.BlockSpec((1, w, s), lambda bi, g, qi, *_: (bi, g, 0)),
                  pl.BlockSpec((npairs, 2, LANES), lambda bi, g, qi, *_: (g, 0, 0)),
                  pl.BlockSpec((1, 1, 1, t), lambda bi, g, qi, *_: (bi, qi, 0, 0)),
                  pl.BlockSpec((1, s, 1), lambda bi, g, qi, *_: (bi, 0, 0))],
        out_specs=pl.BlockSpec((1, t, w), lambda bi, g, qi, *_: (bi, qi, g)),
        scratch_shapes=[pltpu.VMEM((LANES, w), F32), pltpu.VMEM((2 * npairs, nb, t), F32)],
    )
    return pl.pallas_call(
        functools.partial(_moba_attn_kernel, nb=nb, t=t, npairs=npairs, bounded=bounded),
        grid_spec=grid_spec,
        out_shape=jax.ShapeDtypeStruct((b, s, MOBA_HEADS * MOBA_HEAD_DIM), BF16),
        compiler_params=_params("arbitrary", "arbitrary", "arbitrary"),
        name="moba_attention",
    )(q_min, k_max, qg_min, kb_max, qk, qk, vt, bias_table, pos_row, pos_col)


def _rope_tab_kernel(pos_ref, inv_ref, c_ref, s_ref):
    half = MLA_ROPE // 2
    ang = pos_ref[0].astype(F32) * inv_ref[...]
    lane = lax.broadcasted_iota(jnp.int32, (1, LANES), 1)
    sn = jnp.sin(ang)
    c_ref[0] = jnp.cos(ang)
    s_ref[0] = jnp.where(lane < MLA_NOPE + half, -sn, sn)


def _rope_tables(pos_col, inv_lane, tm):
    b, s, _ = pos_col.shape
    out = pl.BlockSpec((1, tm, LANES), lambda bi, i: (bi, i, 0))
    shape = jax.ShapeDtypeStruct((b, s, LANES), F32)
    return pl.pallas_call(
        _rope_tab_kernel,
        grid=(b, s // tm),
        in_specs=[pl.BlockSpec((1, tm, 1), lambda bi, i: (bi, i, 0)),
                  pl.BlockSpec((1, LANES), lambda bi, i: (0, 0))],
        out_specs=[out, out],
        out_shape=[shape, shape],
        compiler_params=_params("arbitrary", "arbitrary"),
        name="rope_tables",
    )(pos_col, inv_lane)


def _mla_q_kernel(x_ref, sh_ref, sc_ref, g_ref, wdq_ref, qag_ref, wuq_ref, wuqp_ref, qg_ref, qgp_ref,
                  c_ref, s_ref, o_ref):
    h = _rms_mod(x_ref[0], g_ref[...], _mod_row(sh_ref), _mod_row(sc_ref)).astype(BF16)
    dq = _dot(h, wdq_ref[...])
    dqn = (dq * lax.rsqrt(jnp.mean(dq * dq, axis=-1, keepdims=True) + EPS) * qag_ref[...]).astype(BF16)
    uq = _dot(dqn, wuq_ref[...])
    up = _dot(dqn, wuqp_ref[...])
    cg = c_ref[0] * qg_ref[...]
    sg = s_ref[0] * qgp_ref[...]
    scale = MLA_QK ** -0.5 * LOG2E
    for hd in range(MLA_HEADS):
        sl = slice(hd * LANES, (hd + 1) * LANES)
        u = uq[:, sl]
        inv = lax.rsqrt(jnp.sum(u * u, axis=-1, keepdims=True) * (1.0 / MLA_QK) + EPS) * scale
        o_ref[0, :, sl] = ((u * cg + up[:, sl] * sg) * inv).astype(BF16)


def _mla_q(x, shift, scale, g, wdq, qag, wuq_p, wuq_partner, qg_p, qg_partner, cos, sin, tm):
    b, s, d = x.shape
    r = wdq.shape[1]
    nq = wuq_p.shape[1]
    tab = pl.BlockSpec((1, tm, LANES), lambda bi, i: (bi, i, 0))
    full = lambda shp: pl.BlockSpec(shp, lambda bi, i: (0, 0))
    return pl.pallas_call(
        _mla_q_kernel,
        grid=(b, s // tm),
        in_specs=[pl.BlockSpec((1, tm, d), lambda bi, i: (bi, i, 0)), _mod_spec(shift, d), _mod_spec(scale, d),
                  full((1, d)),
                  full((d, r)), full((1, r)), full((r, nq)), full((r, nq)), full((1, LANES)), full((1, LANES)),
                  tab, tab],
        out_specs=pl.BlockSpec((1, tm, nq), lambda bi, i: (bi, i, 0)),
        out_shape=jax.ShapeDtypeStruct((b, s, nq), BF16),
        compiler_params=_params("arbitrary", "arbitrary"),
        name="mla_q_path",
    )(x, shift[0], scale[0], g, wdq, qag, wuq_p, wuq_partner, qg_p, qg_partner, cos, sin)


def _mla_kv_kernel(x_ref, sh_ref, sc_ref, g_ref, wdkv_ref, kvg_ref, wuk_ref, wuvt_ref, kg_ref, kgp_ref,
                   c_ref, s_ref, k_ref, vt_ref):
    hs = _rms_mod(x_ref[0], g_ref[...], _mod_row(sh_ref), _mod_row(sc_ref)).astype(BF16)
    ck = _dot(hs, wdkv_ref[...])
    ckv = ck[:, :MLA_KV_LORA]
    ckvn = (ckv * lax.rsqrt(jnp.mean(ckv * ckv, axis=-1, keepdims=True) + EPS) * kvg_ref[...]).astype(BF16)
    kn = _dot(ckvn, wuk_ref[...])
    vt_ref[0] = _dot_nt(wuvt_ref[...], ckvn).astype(BF16)
    lane = lax.broadcasted_iota(jnp.int32, (1, LANES), 1)
    on_rope = (lane >= MLA_NOPE) & (lane < MLA_QK)
    tail = ck[:, MLA_KV_LORA:MLA_KV_LORA + LANES]
    kpe = jnp.where(on_rope, pltpu.roll(tail, MLA_NOPE, axis=1), 0.0)
    partner = pltpu.roll(tail, MLA_ROPE, axis=1)
    cg = c_ref[0] * kg_ref[...]
    sg = s_ref[0] * kgp_ref[...]
    rot = kpe * cg + partner * sg
    ss_pe = jnp.sum(kpe * kpe, axis=-1, keepdims=True)
    for hd in range(MLA_HEADS):
        sl = slice(hd * LANES, (hd + 1) * LANES)
        nope = kn[:, sl]
        inv = lax.rsqrt((jnp.sum(nope * nope, axis=-1, keepdims=True) + ss_pe) * (1.0 / MLA_QK) + EPS)
        k_ref[0, :, sl] = ((nope * cg + rot) * inv).astype(BF16)


def _mla_kv(x, shift, scale, g, wdkv_p, kvg, wuk_p, wuv_t, kg_p, kg_partner, cos, sin, tm):
    b, s, d = x.shape
    r = wdkv_p.shape[1]
    nk = wuk_p.shape[1]
    nv = wuv_t.shape[0]
    tab = pl.BlockSpec((1, tm, LANES), lambda bi, i: (bi, i, 0))
    full = lambda shp: pl.BlockSpec(shp, lambda bi, i: (0, 0))
    return pl.pallas_call(
        _mla_kv_kernel,
        grid=(b, s // tm),
        in_specs=[pl.BlockSpec((1, tm, d), lambda bi, i: (bi, i, 0)), _mod_spec(shift, d), _mod_spec(scale, d),
                  full((1, d)),
                  full((d, r)), full((1, MLA_KV_LORA)), full((MLA_KV_LORA, nk)), full((nv, MLA_KV_LORA)),
                  full((1, LANES)), full((1, LANES)), tab, tab],
        out_specs=[pl.BlockSpec((1, tm, nk), lambda bi, i: (bi, i, 0)),
                   pl.BlockSpec((1, nv, tm), lambda bi, i: (bi, 0, i))],
        out_shape=[jax.ShapeDtypeStruct((b, s, nk), BF16), jax.ShapeDtypeStruct((b, nv, s), BF16)],
        compiler_params=_params("arbitrary", "arbitrary"),
        name="mla_shared_kv",
    )(x, shift[0], scale[0], g, wdkv_p, kvg, wuk_p, wuv_t, kg_p, kg_partner, cos, sin)


def _mla_attn_kernel(q_ref, k_ref, vt_ref, o_ref, *, t, nh, bounded):
    qi = pl.program_id(2)
    q2 = q_ref[0]
    q_heads = [q2[:, h * LANES:(h + 1) * LANES] for h in range(nh)]

    def chunk(j, carry, masked):
        carry = list(carry)
        start = pl.multiple_of(j * t, t)
        k2 = k_ref[0, pl.ds(start, t), :]
        vt = vt_ref[0, :, pl.ds(start, t)]
        s_heads = [_dot_nt(k2[:, h * LANES:(h + 1) * LANES], q_heads[h]) for h in range(nh)]
        if masked:
            row = lax.broadcasted_iota(jnp.int32, (t, t), 0)
            col = lax.broadcasted_iota(jnp.int32, (t, t), 1)
            allowed = row <= col
        for h in range(nh):
            m, l, acc = carry[h]
            s = jnp.where(allowed, s_heads[h], NEG) if masked else s_heads[h]
            carry[h] = _softmax_step_t(s, m, l, acc, vt[h * MLA_V:(h + 1) * MLA_V, :], bounded)
        return tuple(carry)

    def own_chunk_bounded(carry):
        carry = list(carry)
        half = t // 2
        start = pl.multiple_of(qi * t, t)
        k2 = k_ref[0, pl.ds(start, t), :]
        vt = vt_ref[0, :, pl.ds(start, t)]
        causal = (lax.broadcasted_iota(jnp.int32, (half, half), 0) <= lax.broadcasted_iota(jnp.int32, (half, half), 1))
        quads = []
        for h in range(nh):
            kh = k2[:, h * LANES:(h + 1) * LANES]
            qh = q_heads[h]
            quads.append((_dot_nt(kh[:half], qh[:half]), _dot_nt(kh[:half], qh[half:]), _dot_nt(kh[half:], qh[half:])))
        for h in range(nh):
            m, l, acc = carry[h]
            s00, s01, s11 = quads[h]
            e00 = jnp.exp2(jnp.where(causal, s00, NEG))
            e01 = jnp.exp2(s01)
            e11 = jnp.exp2(jnp.where(causal, s11, NEG))
            vt0 = vt[h * MLA_V:(h + 1) * MLA_V, :half]
            vt1 = vt[h * MLA_V:(h + 1) * MLA_V, half:]
            l_add = jnp.concatenate(
                [jnp.sum(e00, axis=0, keepdims=True),
                 jnp.sum(e01, axis=0, keepdims=True) + jnp.sum(e11, axis=0, keepdims=True)], axis=1)
            acc_add = jnp.concatenate(
                [_dot(vt0, e00.astype(BF16)), _dot(vt0, e01.astype(BF16)) + _dot(vt1, e11.astype(BF16))], axis=1)
            carry[h] = (m, l + l_add, acc + acc_add)
        return tuple(carry)

    init = (jnp.full((1, t), NEG, F32), jnp.zeros((1, t), F32), jnp.zeros((MLA_V, t), F32))
    carry = lax.fori_loop(0, qi, lambda j, c: chunk(j, c, False), (init,) * nh)
    carry = own_chunk_bounded(carry) if bounded else chunk(qi, carry, True)
    out_t = jnp.concatenate([acc / l for (_, l, acc) in carry], axis=0)
    o_ref[0] = out_t.T.astype(BF16)


def _mla_attn(q, k, vt, t, nh, bounded):
    b, s, _ = q.shape
    return pl.pallas_call(
        functools.partial(_mla_attn_kernel, t=t, nh=nh, bounded=bounded),
        grid=(b, MLA_HEADS // nh, s // t),
        in_specs=[pl.BlockSpec((1, t, nh * LANES), lambda bi, g, qi: (bi, qi, g)),
                  pl.BlockSpec((1, s, nh * LANES), lambda bi, g, qi: (bi, 0, g)),
                  pl.BlockSpec((1, nh * MLA_V, s), lambda bi, g, qi: (bi, g, 0))],
        out_specs=pl.BlockSpec((1, t, nh * MLA_V), lambda bi, g, qi: (bi, qi, g)),
        out_shape=jax.ShapeDtypeStruct((b, s, MLA_HEADS * MLA_V), BF16),
        compiler_params=_params("arbitrary", "arbitrary", "arbitrary"),
        name="mla_attention",
    )(q, k, vt)


def _t5_bias_table(rel_bias):
    n = jnp.arange(MASKED_DIST)
    max_exact = REL_BUCKETS // 2
    nf = jnp.maximum(n, 1).astype(F32)
    large = max_exact + (jnp.log(nf / max_exact) / math.log(REL_MAX_DIST / max_exact)
                         * (REL_BUCKETS - max_exact)).astype(jnp.int32)
    large = jnp.minimum(large, REL_BUCKETS - 1)
    bucket = jnp.where(n < max_exact, n, large)
    table = jnp.concatenate([rel_bias.T[:, bucket] * LOG2E, jnp.full((MOBA_HEADS, 1), NEG, F32)], axis=1)
    return table.reshape(MOBA_HEADS // 2, 2, BIAS_TABLE)


def _pad_heads(w, width):
    k = w.shape[0]
    w3 = w.reshape(k, -1, width)
    return jnp.pad(w3, ((0, 0), (0, 0), (0, LANES - width))).reshape(k, -1)


def _pad_lanes(g):
    return jnp.pad(g, (0, LANES - g.shape[0])).reshape(1, LANES)


def _swap_rope_halves(a):
    half = MLA_ROPE // 2
    return jnp.concatenate([jnp.zeros_like(a[..., :MLA_NOPE]), a[..., MLA_NOPE + half:],
                            a[..., MLA_NOPE:MLA_NOPE + half]], axis=-1)


def _scores_bounded(q_gain, k_gain, dim, scale, extra=0.0):
    bound = dim * jnp.max(jnp.abs(q_gain)) * jnp.max(jnp.abs(k_gain)) * scale * BOUND_SLACK + extra
    return bound <= SCORE_BOUND_MAX


def kernel(x, c, positions, ada_w, ada_b, norm_g, ffn_w_gate, ffn_w_up, ffn_w_down, rel_bias,
           moba_w_qkv, moba_q_g, moba_k_g, moba_w_o, kv_ada_w, kv_ada_b, kv_norm_g, w_dkv,
           kv_a_norm_g, w_uk, w_uv, mla_k_g, mla_w_dq, mla_q_a_norm_g, mla_w_uq, mla_q_g, mla_w_o):
    b, s, d = x.shape
    depth = ada_w.shape[0]
    n_moba = moba_w_qkv.shape[0]
    assert b <= SUBLANES and s % max(FFN_ROWS, MOBA_TILE, MLA_TILE) == 0 and MOBA_TILE == 2 * MOBA_BLOCK
    tm_ffn = FFN_ROWS
    tm_proj = PROJ_ROWS

    c_pad = jnp.pad(c, ((0, SUBLANES - b), (0, 0)))
    mods = _mods(c_pad, ada_w, ada_b.reshape(depth, 1, -1), tn=ADALN_COLS)
    kv_mods = _mods(c_pad, kv_ada_w[None], kv_ada_b.reshape(1, 1, -1), tn=d)

    w_gate, w_up, w_down = (w.astype(BF16) for w in (ffn_w_gate, ffn_w_up, ffn_w_down))

    for layer in range(depth):
        sh1, sc1, g1, sh2, sc2, g2, sh3, sc3, g3 = [(mods, layer, k) for k in range(mods.shape[-1] // d)]
        ng = norm_g[layer]

        def ffn(xin, idx, sh, sc, gt, mixer=None):
            return _ffn(xin, sh, sc, gt, ng[2 * idx:2 * idx + 1], w_gate, w_up, w_down, (layer, idx),
                        tm_ffn, FFN_CHUNK, mixer)

        x = ffn(x, 0, sh1, sc1, g1)

        if layer < n_moba:
            n_qk = 2 * MOBA_HEADS * MOBA_HEAD_DIM
            q_scale = MOBA_HEAD_DIM ** -0.5 * LOG2E
            head_gain = jnp.concatenate([jnp.tile(moba_q_g[layer] * q_scale, MOBA_HEADS),
                                         jnp.tile(moba_k_g[layer], MOBA_HEADS)]).reshape(1, -1)
            w_qkv = moba_w_qkv[layer].astype(BF16)
            qk, vt = _moba_proj(x, sh2, sc2, ng[1:2], w_qkv[:, :n_qk], w_qkv[:, n_qk:].T, head_gain,
                                2 * tm_proj, PROJ_COLS)
            bias_table = _t5_bias_table(rel_bias)
            moba = functools.partial(_moba_attn, qk, vt, bias_table, _moba_position_tables(positions, MOBA_TILE),
                                     MOBA_TILE, ATTN_HEADS // 2)
            ok = _scores_bounded(moba_q_g[layer], moba_k_g[layer], MOBA_HEAD_DIM, q_scale,
                                 jnp.max(jnp.abs(rel_bias)) * LOG2E)
            o = lax.cond(ok, lambda: moba(True), lambda: moba(False))
            w_o = moba_w_o[layer]
        else:
            j = layer - n_moba
            if j == 0:
                half = MLA_ROPE // 2
                inv = ROPE_BASE ** (-jnp.arange(half, dtype=F32) / half)
                inv_lane = jnp.zeros((LANES,), F32).at[MLA_NOPE:MLA_QK].set(jnp.tile(inv, 2)).reshape(1, LANES)
                cos, sin = _rope_tables(positions.reshape(b, s, 1), inv_lane, tm_proj)
                kv_sh, kv_sc = (kv_mods, 0, 0), (kv_mods, 0, 1)
                w_dkv16 = w_dkv.astype(BF16)
                w_kpe = w_dkv16[:, MLA_KV_LORA:]
                w_kpe_swapped = jnp.concatenate([w_kpe[:, half:], w_kpe[:, :half]], axis=1)
                wdkv_p = jnp.pad(jnp.concatenate([w_dkv16, w_kpe_swapped], axis=1),
                                 ((0, 0), (0, LANES - 2 * MLA_ROPE)))
                shared_k, shared_v = _mla_kv(
                    x, kv_sh, kv_sc, kv_norm_g.reshape(1, d), wdkv_p, kv_a_norm_g.reshape(1, -1),
                    _pad_heads(w_uk.astype(BF16), MLA_NOPE), w_uv.astype(BF16).T, _pad_lanes(mla_k_g),
                    _pad_lanes(_swap_rope_halves(mla_k_g)), cos, sin, tm_proj)
            w_uq = mla_w_uq[j].astype(BF16)
            w_uq_partner = _swap_rope_halves(w_uq.reshape(w_uq.shape[0], MLA_HEADS, MLA_QK)).reshape(w_uq.shape)
            q = _mla_q(x, sh2, sc2, ng[1:2], mla_w_dq[j].astype(BF16), mla_q_a_norm_g[j].reshape(1, -1),
                       _pad_heads(w_uq, MLA_QK), _pad_heads(w_uq_partner, MLA_QK),
                       _pad_lanes(mla_q_g[j]), _pad_lanes(_swap_rope_halves(mla_q_g[j])), cos, sin, tm_proj)
            mla = functools.partial(_mla_attn, q, shared_k, shared_v, MLA_TILE, ATTN_HEADS)
            ok = _scores_bounded(mla_q_g[j], mla_k_g, MLA_QK, MLA_QK ** -0.5 * LOG2E)
            o = lax.cond(ok, lambda: mla(True), lambda: mla(False))
            w_o = mla_w_o[j]

        x = ffn(x, 1, sh3, sc3, g3, mixer=(o, g2, w_o.astype(BF16)))
    return x
```

```python
import functools
import math

import jax
import jax.numpy as jnp
from jax import lax
from jax.experimental import pallas as pl
from jax.experimental.pallas import tpu as pltpu

F32 = jnp.float32
BF16 = jnp.bfloat16

LANES = 128
SUBLANES = 8
VMEM_LIMIT_BYTES = 56 * 1024 * 1024

MOBA_HEADS = 16
MOBA_HEAD_DIM = 64
MOBA_BLOCK = 256
MOBA_TOPK = 3
REL_BUCKETS = 32
REL_MAX_DIST = 128
MLA_HEADS = 16
MLA_KV_LORA = 256
MLA_NOPE = 64
MLA_ROPE = 32
MLA_V = 64
MLA_QK = MLA_NOPE + MLA_ROPE
ROPE_BASE = 10000.0
EPS = 1e-6
NEG = -1e30

FFN_ROWS = 1024
FFN_CHUNK = 256
PROJ_ROWS = 512
PROJ_COLS = 512
ADALN_COLS = 2304
MOBA_TILE = 512
MLA_TILE = 512
ATTN_HEADS = 8
BOUND_SLACK = 1.01
LOG2E = math.log2(math.e)
BIAS_TABLE = LANES
MAX_DIST = BIAS_TABLE - 2
MASKED_DIST = BIAS_TABLE - 1
SCORE_BOUND_MAX = 60.0


def _params(*sem):
    return pltpu.CompilerParams(dimension_semantics=sem, vmem_limit_bytes=VMEM_LIMIT_BYTES)


def _dot(a, b):
    return jnp.dot(a, b, preferred_element_type=F32)


def _dot_nt(a, b):
    return lax.dot_general(a, b, (((1,), (1,)), ((), ())), preferred_element_type=F32)


def _rms_mod(x, g, shift, scale):
    ms = jnp.mean(x * x, axis=-1, keepdims=True)
    y = x * lax.rsqrt(ms + EPS) * g
    return y * (1.0 + scale) + shift


def _mod_spec(mod, d):
    _, layer, k = mod
    return pl.BlockSpec((None, SUBLANES, d), lambda bi, *_: (layer, 0, k))


def _mod_row(ref):
    return ref[pl.ds(pl.program_id(0), 1), :]


def _mods_kernel(c_ref, w_ref, b_ref, o_ref):
    c = c_ref[...]
    ca = c * jax.nn.sigmoid(c)
    w = w_ref[0]
    w_hi = w.astype(BF16)
    w_lo = (w - w_hi.astype(F32)).astype(BF16)
    ca_hi = ca.astype(BF16)
    ca_lo = (ca - ca_hi.astype(F32)).astype(BF16)
    o_ref[0] = _dot(ca_hi, w_hi) + _dot(ca_lo, w_hi) + _dot(ca_hi, w_lo) + b_ref[0]


def _mods(c_pad, w, b, tn):
    nl, d, n = w.shape
    return pl.pallas_call(
        _mods_kernel,
        grid=(nl, n // tn),
        in_specs=[pl.BlockSpec((SUBLANES, d), lambda l, j: (0, 0)),
                  pl.BlockSpec((1, d, tn), lambda l, j: (l, 0, j)),
                  pl.BlockSpec((1, 1, tn), lambda l, j: (l, 0, j))],
        out_specs=pl.BlockSpec((1, SUBLANES, tn), lambda l, j: (l, 0, j)),
        out_shape=jax.ShapeDtypeStruct((nl, SUBLANES, n), F32),
        compiler_params=_params("arbitrary", "arbitrary"),
        name="adaln_mods",
    )(c_pad, w, b)


def _ffn_kernel(*refs, tf, with_mixer):
    if with_mixer:
        x_ref, a_ref, gm_ref, wo_ref = refs[:4]
        refs = refs[4:]
        x = x_ref[0] + _mod_row(gm_ref) * _dot(a_ref[0], wo_ref[...])
    else:
        x_ref = refs[0]
        refs = refs[1:]
        x = x_ref[0]
    sh_ref, sc_ref, gt_ref, g_ref, wg_ref, wu_ref, wd_ref, o_ref = refs
    h = _rms_mod(x, g_ref[...], _mod_row(sh_ref), _mod_row(sc_ref)).astype(BF16)
    acc = None
    for f in range(wg_ref.shape[1] // tf):
        cols = slice(f * tf, (f + 1) * tf)
        g = _dot(h, wg_ref[:, cols])
        u = _dot(h, wu_ref[:, cols])
        a = (g * jax.nn.sigmoid(g) * u).astype(BF16)
        d = _dot(a, wd_ref[cols, :])
        acc = d if acc is None else acc + d
    o_ref[0] = x + (0.5 * _mod_row(gt_ref)) * acc


def _ffn(x, shift, scale, gate, g, wg, wu, wd, which, tm, tf, mixer=None):
    b, s, d = x.shape
    f = wg.shape[-1]
    row = pl.BlockSpec((1, tm, d), lambda bi, i: (bi, i, 0))
    full = lambda shp: pl.BlockSpec(shp, lambda bi, i: (0, 0), pipeline_mode=pl.Buffered(1))
    picked = lambda r, c: pl.BlockSpec((None, None, r, c), lambda bi, i: (*which, 0, 0),
                                       pipeline_mode=pl.Buffered(1))
    ins, specs = [x], [row]
    if mixer is not None:
        a, gate_mix, w_o = mixer
        k = a.shape[2]
        ins += [a, gate_mix[0], w_o]
        specs += [pl.BlockSpec((1, tm, k), lambda bi, i: (bi, i, 0)), _mod_spec(gate_mix, d), full((k, d))]
    ins += [shift[0], scale[0], gate[0], g, wg, wu, wd]
    specs += [_mod_spec(shift, d), _mod_spec(scale, d), _mod_spec(gate, d), full((1, d)),
              picked(d, f), picked(d, f), picked(f, d)]
    return pl.pallas_call(
        functools.partial(_ffn_kernel, tf=tf, with_mixer=mixer is not None),
        grid=(b, s // tm),
        in_specs=specs,
        out_specs=row,
        out_shape=jax.ShapeDtypeStruct((b, s, d), F32),
        compiler_params=_params("arbitrary", "arbitrary"),
        name="ffn_swiglu",
    )(*ins)


def _headnorm_pair(y, g, dim):
    lane = lax.broadcasted_iota(jnp.int32, (1, LANES), 1)
    left = lane < dim
    y2 = y * y
    ss_a = jnp.sum(jnp.where(left, y2, 0.0), axis=-1, keepdims=True)
    ss_b = jnp.sum(jnp.where(left, 0.0, y2), axis=-1, keepdims=True)
    inv = jnp.where(left, lax.rsqrt(ss_a * (1.0 / dim) + EPS), lax.rsqrt(ss_b * (1.0 / dim) + EPS))
    return y * inv * g


def _moba_proj_kernel(x_ref, sh_ref, sc_ref, g_ref, w_ref, wvt_ref, hg_ref, qk_ref, vt_ref, *, tn):
    h = _rms_mod(x_ref[0], g_ref[...], _mod_row(sh_ref), _mod_row(sc_ref)).astype(BF16)
    for n in range(w_ref.shape[1] // tn):
        res = _dot(h, w_ref[:, n * tn:(n + 1) * tn])
        for s in range(tn // LANES):
            sl = slice(n * tn + s * LANES, n * tn + (s + 1) * LANES)
            qk_ref[0, :, sl] = _headnorm_pair(res[:, s * LANES:(s + 1) * LANES], hg_ref[:, sl],
                                              MOBA_HEAD_DIM).astype(BF16)
    for n in range(wvt_ref.shape[0] // tn):
        rows = slice(n * tn, (n + 1) * tn)
        vt_ref[0, rows, :] = _dot_nt(wvt_ref[rows, :], h).astype(BF16)


def _moba_proj(x, shift, scale, g, w_qk, w_vt, head_gain, tm, tn):
    b, s, d = x.shape
    nqk = w_qk.shape[1]
    nv = w_vt.shape[0]
    full = lambda shp: pl.BlockSpec(shp, lambda bi, i: (0, 0))
    return pl.pallas_call(
        functools.partial(_moba_proj_kernel, tn=tn),
        grid=(b, s // tm),
        in_specs=[pl.BlockSpec((1, tm, d), lambda bi, i: (bi, i, 0)), _mod_spec(shift, d), _mod_spec(scale, d),
                  full((1, d)),
                  full((d, nqk)), full((nv, d)), full((1, nqk))],
        out_specs=[pl.BlockSpec((1, tm, nqk), lambda bi, i: (bi, i, 0)),
                   pl.BlockSpec((1, nv, tm), lambda bi, i: (bi, 0, i))],
        out_shape=[jax.ShapeDtypeStruct((b, s, nqk), BF16), jax.ShapeDtypeStruct((b, nv, s), BF16)],
        compiler_params=_params("arbitrary", "arbitrary"),
        name="moba_qkv_proj",
    )(x, shift[0], scale[0], g, w_qk, w_vt, head_gain)


def _softmax_step_t(s, m, l, acc, vt, bounded):
    if bounded:
        p = jnp.exp2(s)
        return m, l + jnp.sum(p, axis=0, keepdims=True), acc + _dot(vt, p.astype(BF16))
    m_new = jnp.maximum(m, jnp.max(s, axis=0, keepdims=True))
    alpha = jnp.exp2(m - m_new)
    p = jnp.exp2(s - m_new)
    l_new = alpha * l + jnp.sum(p, axis=0, keepdims=True)
    acc_new = acc * alpha + _dot(vt, p.astype(BF16))
    return m_new, l_new, acc_new


def _moba_attn_kernel(qmin_ref, kmax_ref, qgmin_ref, kbmax_ref, q_ref, k_ref, vt_ref, bt_ref, pq_ref, pk_ref, o_ref,
                      km_scr, sel_scr, *, nb, t, npairs, bounded):
    blk = MOBA_BLOCK
    hd = MOBA_HEAD_DIM
    nsub = t // blk
    nh = 2 * npairs
    bi = pl.program_id(0)
    qi = pl.program_id(2)
    lane = lax.broadcasted_iota(jnp.int32, (1, LANES), 1)
    left = lane < hd

    @pl.when(qi == 0)
    def _():
        km_scr[...] = jnp.zeros_like(km_scr)
        for r in range(nb):
            kblk = k_ref[0, r * blk:(r + 1) * blk, :].astype(F32)
            km_scr[r:r + 1, :] = jnp.sum(kblk, axis=0, keepdims=True) * (1.0 / blk)

    q_heads = []
    for p in range(npairs):
        q2 = q_ref[0, :, p * LANES:(p + 1) * LANES]
        zero = jnp.zeros_like(q2)
        q_heads += [jnp.where(left, q2, zero), jnp.where(left, zero, q2)]

    col = lax.broadcasted_iota(jnp.int32, (1, t), 1)
    q_idx = qi * t + col
    q_sub = sum((col >= r * blk).astype(jnp.int32) for r in range(1, nsub))
    q_blk = qi * nsub + q_sub
    q_blk_start = q_blk * blk

    blk_row = lax.broadcasted_iota(jnp.int32, (nb, t), 0)
    for h in range(nh):
        km = km_scr[:, (h // 2) * LANES:(h // 2 + 1) * LANES]
        km_hi = km.astype(BF16)
        km_lo = (km - km_hi.astype(F32)).astype(BF16)
        gate = (_dot_nt(km_hi, q_heads[h]) + _dot_nt(km_lo, q_heads[h]))[:nb]
        gate = jnp.where(blk_row < q_blk, gate, -jnp.inf)
        sel = jnp.zeros((nb, t), F32)
        for _ in range(MOBA_TOPK):
            mx = jnp.max(gate, axis=0, keepdims=True)
            idx = jnp.min(jnp.where(gate == mx, blk_row, nb), axis=0, keepdims=True)
            pick = (blk_row == idx) & (mx > -jnp.inf)
            sel = jnp.where(pick, 1.0, sel)
            gate = jnp.where(pick, -jnp.inf, gate)
        sel_scr[h] = sel

    pq = pq_ref[0, 0]

    def gather_bias(h, dist):
        table = jnp.broadcast_to(bt_ref[h // 2, h % 2:h % 2 + 1, :], (dist.shape[0], LANES))
        return jnp.concatenate(
            [jnp.take_along_axis(table, dist[:, g * LANES:(g + 1) * LANES], axis=1, mode="promise_in_bounds")
             for g in range(dist.shape[1] // LANES)], axis=1)

    def far_bias(h):
        return bt_ref[h // 2, h % 2:h % 2 + 1, MAX_DIST:MAX_DIST + 1]

    def chosen_rows(h, j, lanes=slice(None)):
        return [sel_scr[h, pl.ds(j * nsub + r, 1), lanes] > 0.0 for r in range(nsub)]

    def own_chunk_bounded(carry):
        assert nsub == 2
        carry = list(carry)
        start = pl.multiple_of(qi * t, t)
        k2 = k_ref[0, pl.ds(start, t), :]
        vt = vt_ref[0, :, pl.ds(start, t)]
        pk0 = pk_ref[0, pl.ds(start, blk), :]
        pk1 = pk_ref[0, pl.ds(start + blk, blk), :]
        causal = (lax.broadcasted_iota(jnp.int32, (blk, blk), 0) <= lax.broadcasted_iota(jnp.int32, (blk, blk), 1))
        d00 = jnp.where(causal, jnp.clip(pq[:, :blk] - pk0, 0, MAX_DIST), MASKED_DIST)
        d01 = jnp.clip(pq[:, blk:] - pk0, 0, MAX_DIST)
        d11 = jnp.where(causal, jnp.clip(pq[:, blk:] - pk1, 0, MAX_DIST), MASKED_DIST)
        quads = []
        for h in range(nh):
            kh = k2[:, (h // 2) * LANES:(h // 2 + 1) * LANES]
            qh = q_heads[h]
            quads.append((_dot_nt(kh[:blk], qh[:blk]), _dot_nt(kh[:blk], qh[blk:]), _dot_nt(kh[blk:], qh[blk:])))
        for h in range(nh):
            m, l, acc = carry[h]
            s00, s01, s11 = quads[h]
            e00 = jnp.exp2(s00 + gather_bias(h, d00))
            e01 = jnp.exp2(s01 + gather_bias(h, d01))
            e11 = jnp.exp2(s11 + gather_bias(h, d11))
            w01 = jnp.where(chosen_rows(h, qi, slice(blk, t))[0], 1.0, 0.0)
            vt0 = vt[h * hd:(h + 1) * hd, :blk]
            vt1 = vt[h * hd:(h + 1) * hd, blk:]
            l_add = jnp.concatenate(
                [jnp.sum(e00, axis=0, keepdims=True),
                 jnp.sum(e01, axis=0, keepdims=True) * w01 + jnp.sum(e11, axis=0, keepdims=True)], axis=1)
            acc_add = jnp.concatenate(
                [_dot(vt0, e00.astype(BF16)),
                 _dot(vt0, e01.astype(BF16)) * w01 + _dot(vt1, e11.astype(BF16))], axis=1)
            carry[h] = (m, l + l_add, acc + acc_add)
        return tuple(carry)

    def chunk(j, carry, mode):
        carry = list(carry)
        start = pl.multiple_of(j * t, t)
        k2 = k_ref[0, pl.ds(start, t), :]
        vt = vt_ref[0, :, pl.ds(start, t)]
        s_heads = [_dot_nt(k2[:, (h // 2) * LANES:(h // 2 + 1) * LANES], q_heads[h]) for h in range(nh)]
        if mode in ("near", "diag"):
            dist = jnp.clip(pq - pk_ref[0, pl.ds(start, t), :], 0, MAX_DIST)
        if mode == "corner":
            dist_c = jnp.clip(pq[:, :LANES] - pk_ref[0, pl.ds(start + (nsub - 1) * blk, blk), :], 0, MAX_DIST)
        if mode == "diag":
            key_idx = start + lax.broadcasted_iota(jnp.int32, (blk, t), 0)
        for h in range(nh):
            m, l, acc = carry[h]
            s = s_heads[h]
            vt_h = vt[h * hd:(h + 1) * hd, :]
            chosen = chosen_rows(h, j)
            if mode in ("far", "corner") and bounded:
                two_c = jnp.exp2(far_bias(h))
                for r in range(nsub):
                    piece = s[r * blk:(r + 1) * blk]
                    w = jnp.where(chosen[r], two_c, 0.0)
                    if mode == "corner" and r == nsub - 1:
                        piece = jnp.concatenate([piece[:, :LANES] + gather_bias(h, dist_c), piece[:, LANES:]], axis=1)
                        w = jnp.where(col < LANES, jnp.where(chosen[r], 1.0, 0.0), w)
                    p = jnp.exp2(piece)
                    l = l + jnp.sum(p, axis=0, keepdims=True) * w
                    acc = acc + _dot(vt_h[:, r * blk:(r + 1) * blk], p.astype(BF16)) * w
                carry[h] = (m, l, acc)
            elif mode == "far":
                pieces = [s[r * blk:(r + 1) * blk] for r in range(nsub)]
                c = far_bias(h)
                tmax = [jnp.where(chosen[r], jnp.max(pieces[r], axis=0, keepdims=True) + c, NEG)
                        for r in range(nsub)]
                m_new = functools.reduce(jnp.maximum, tmax, m)
                alpha = jnp.exp2(m - m_new)
                ps = [jnp.exp2(pieces[r] - jnp.where(chosen[r], m_new - c, jnp.inf)) for r in range(nsub)]
                l_new = alpha * l + sum(jnp.sum(p, axis=0, keepdims=True) for p in ps)
                p = jnp.concatenate(ps, axis=0) if nsub > 1 else ps[0]
                carry[h] = (m_new, l_new, acc * alpha + _dot(vt_h, p.astype(BF16)))
            else:
                s = s + gather_bias(h, dist)
                masked = []
                for r in range(nsub):
                    piece = s[r * blk:(r + 1) * blk]
                    if mode == "diag":
                        kidx = key_idx + r * blk
                        keep_own = jnp.where(kidx <= q_idx, jnp.where(kidx >= q_blk_start, piece, NEG), NEG)
                        masked.append(jnp.where(chosen[r], piece, keep_own))
                    else:
                        masked.append(jnp.where(chosen[r], piece, NEG))
                s = jnp.concatenate(masked, axis=0) if nsub > 1 else masked[0]
                carry[h] = _softmax_step_t(s, m, l, acc, vt_h, bounded)
        return tuple(carry)

    def past_chunk(j, carry):
        q_lo = qmin_ref[bi, qi]
        far = q_lo - kmax_ref[bi, j] >= MAX_DIST
        if not bounded:
            return lax.cond(far, lambda c: chunk(j, c, "far"), lambda c: chunk(j, c, "near"), carry)
        groups = t // LANES
        q_rest = functools.reduce(jnp.minimum, [qgmin_ref[bi, qi * groups + g] for g in range(1, groups)])
        k_first = functools.reduce(jnp.maximum, [kbmax_ref[bi, j * nsub + r] for r in range(nsub - 1)])
        corner = (q_lo - k_first >= MAX_DIST) & (q_rest - kbmax_ref[bi, j * nsub + nsub - 1] >= MAX_DIST)
        return lax.cond(far, lambda c: chunk(j, c, "far"),
                        lambda c: lax.cond(corner, lambda c2: chunk(j, c2, "corner"),
                                           lambda c2: chunk(j, c2, "near"), c), carry)

    init = (jnp.full((1, t), NEG, F32), jnp.zeros((1, t), F32), jnp.zeros((hd, t), F32))
    if bounded:
        carry = lax.fori_loop(0, qi, past_chunk, own_chunk_bounded((init,) * nh))
    else:
        carry = lax.fori_loop(0, qi, past_chunk, (init,) * nh)
        carry = chunk(qi, carry, "diag")
    out_t = jnp.concatenate([acc / l for (_, l, acc) in carry], axis=0)
    o_ref[0] = out_t.T.astype(BF16)


def _moba_position_tables(positions, t):
    b, s = positions.shape
    pos_tiles = positions.reshape(b, s // t, t)
    return (jnp.min(pos_tiles, axis=-1), jnp.max(pos_tiles, axis=-1),
            jnp.min(positions.reshape(b, s // LANES, LANES), axis=-1),
            jnp.max(positions.reshape(b, s // MOBA_BLOCK, MOBA_BLOCK), axis=-1),
            positions.reshape(b, s // t, 1, t), positions.reshape(b, s, 1))


def _moba_attn(qk, vt, bias_table, pos_tables, t, npairs, bounded):
    b, s, _ = qk.shape
    nb = s // MOBA_BLOCK
    ngroups = MOBA_HEADS * MOBA_HEAD_DIM // (LANES * npairs)
    w = npairs * LANES
    q_min, k_max, qg_min, kb_max, pos_row, pos_col = pos_tables
    grid_spec = pltpu.PrefetchScalarGridSpec(
        num_scalar_prefetch=4,
        grid=(b, ngroups, s // t),
        in_specs=[pl.BlockSpec((1, t, w), lambda bi, g, qi, *_: (bi, qi, g)),
                  pl.BlockSpec((1, s, w), lambda bi, g, qi, *_: (bi, 0, ngroups + g)),
                  pl.BlockSpec((1, w, s), lambda bi, g, qi, *_: (bi, g, 0)),
                  pl.BlockSpec((npairs, 2, LANES), lambda bi, g, qi, *_: (g, 0, 0)),
                  pl.BlockSpec((1, 1, 1, t), lambda bi, g, qi, *_: (bi, qi, 0, 0)),
                  pl.BlockSpec((1, s, 1), lambda bi, g, qi, *_: (bi, 0, 0))],
        out_specs=pl.BlockSpec((1, t, w), lambda bi, g, qi, *_: (bi, qi, g)),
        scratch_shapes=[pltpu.VMEM((LANES, w), F32), pltpu.VMEM((2 * npairs, nb, t), F32)],
    )
    return pl.pallas_call(
        functools.partial(_moba_attn_kernel, nb=nb, t=t, npairs=npairs, bounded=bounded),
        grid_spec=grid_spec,
        out_shape=jax.ShapeDtypeStruct((b, s, MOBA_HEADS * MOBA_HEAD_DIM), BF16),
        compiler_params=_params("arbitrary", "arbitrary", "arbitrary"),
        name="moba_attention",
    )(q_min, k_max, qg_min, kb_max, qk, qk, vt, bias_table, pos_row, pos_col)


def _rope_tab_kernel(pos_ref, inv_ref, c_ref, s_ref):
    half = MLA_ROPE // 2
    ang = pos_ref[0].astype(F32) * inv_ref[...]
    lane = lax.broadcasted_iota(jnp.int32, (1, LANES), 1)
    sn = jnp.sin(ang)
    c_ref[0] = jnp.cos(ang)
    s_ref[0] = jnp.where(lane < MLA_NOPE + half, -sn, sn)


def _rope_tables(pos_col, inv_lane, tm):
    b, s, _ = pos_col.shape
    out = pl.BlockSpec((1, tm, LANES), lambda bi, i: (bi, i, 0))
    shape = jax.ShapeDtypeStruct((b, s, LANES), F32)
    return pl.pallas_call(
        _rope_tab_kernel,
        grid=(b, s // tm),
        in_specs=[pl.BlockSpec((1, tm, 1), lambda bi, i: (bi, i, 0)),
                  pl.BlockSpec((1, LANES), lambda bi, i: (0, 0))],
        out_specs=[out, out],
        out_shape=[shape, shape],
        compiler_params=_params("arbitrary", "arbitrary"),
        name="rope_tables",
    )(pos_col, inv_lane)


def _mla_q_kernel(x_ref, sh_ref, sc_ref, g_ref, wdq_ref, qag_ref, wuq_ref, wuqp_ref, qg_ref, qgp_ref,
                  c_ref, s_ref, o_ref):
    h = _rms_mod(x_ref[0], g_ref[...], _mod_row(sh_ref), _mod_row(sc_ref)).astype(BF16)
    dq = _dot(h, wdq_ref[...])
    dqn = (dq * lax.rsqrt(jnp.mean(dq * dq, axis=-1, keepdims=True) + EPS) * qag_ref[...]).astype(BF16)
    uq = _dot(dqn, wuq_ref[...])
    up = _dot(dqn, wuqp_ref[...])
    cg = c_ref[0] * qg_ref[...]
    sg = s_ref[0] * qgp_ref[...]
    scale = MLA_QK ** -0.5 * LOG2E
    for hd in range(MLA_HEADS):
        sl = slice(hd * LANES, (hd + 1) * LANES)
        u = uq[:, sl]
        inv = lax.rsqrt(jnp.sum(u * u, axis=-1, keepdims=True) * (1.0 / MLA_QK) + EPS) * scale
        o_ref[0, :, sl] = ((u * cg + up[:, sl] * sg) * inv).astype(BF16)


def _mla_q(x, shift, scale, g, wdq, qag, wuq_p, wuq_partner, qg_p, qg_partner, cos, sin, tm):
    b, s, d = x.shape
    r = wdq.shape[1]
    nq = wuq_p.shape[1]
    tab = pl.BlockSpec((1, tm, LANES), lambda bi, i: (bi, i, 0))
    full = lambda shp: pl.BlockSpec(shp, lambda bi, i: (0, 0))
    return pl.pallas_call(
        _mla_q_kernel,
        grid=(b, s // tm),
        in_specs=[pl.BlockSpec((1, tm, d), lambda bi, i: (bi, i, 0)), _mod_spec(shift, d), _mod_spec(scale, d),
                  full((1, d)),
                  full((d, r)), full((1, r)), full((r, nq)), full((r, nq)), full((1, LANES)), full((1, LANES)),
                  tab, tab],
        out_specs=pl.BlockSpec((1, tm, nq), lambda bi, i: (bi, i, 0)),
        out_shape=jax.ShapeDtypeStruct((b, s, nq), BF16),
        compiler_params=_params("arbitrary", "arbitrary"),
        name="mla_q_path",
    )(x, shift[0], scale[0], g, wdq, qag, wuq_p, wuq_partner, qg_p, qg_partner, cos, sin)


def _mla_qkv_kernel(x_ref, qsh_ref, qsc_ref, qn_ref, wdq_ref, qag_ref, wuq_ref, wuqp_ref, qg_ref, qgp_ref,
                    ksh_ref, ksc_ref, kn_ref, wdkv_ref, kvg_ref, wuk_ref, wuvt_ref, kg_ref, kgp_ref,
                    c_ref, s_ref, q_ref, k_ref, vt_ref):
    x = x_ref[0]
    xn = x * lax.rsqrt(jnp.mean(x * x, axis=-1, keepdims=True) + EPS)
    cos, sin = c_ref[0], s_ref[0]
    h = ((xn * qn_ref[...]) * (1.0 + _mod_row(qsc_ref)) + _mod_row(qsh_ref)).astype(BF16)
    dq = _dot(h, wdq_ref[...])
    dqn = (dq * lax.rsqrt(jnp.mean(dq * dq, axis=-1, keepdims=True) + EPS) * qag_ref[...]).astype(BF16)
    uq = _dot(dqn, wuq_ref[...])
    up = _dot(dqn, wuqp_ref[...])
    hs = ((xn * kn_ref[...]) * (1.0 + _mod_row(ksc_ref)) + _mod_row(ksh_ref)).astype(BF16)
    ck = _dot(hs, wdkv_ref[...])
    cg = cos * qg_ref[...]
    sg = sin * qgp_ref[...]
    scale = MLA_QK ** -0.5 * LOG2E

    def q_heads(lo, hi):
        for hd in range(lo, hi):
            sl = slice(hd * LANES, (hd + 1) * LANES)
            u = uq[:, sl]
            inv = lax.rsqrt(jnp.sum(u * u, axis=-1, keepdims=True) * (1.0 / MLA_QK) + EPS) * scale
            q_ref[0, :, sl] = ((u * cg + up[:, sl] * sg) * inv).astype(BF16)

    q_heads(0, MLA_HEADS // 2)
    ckv = ck[:, :MLA_KV_LORA]
    ckvn = (ckv * lax.rsqrt(jnp.mean(ckv * ckv, axis=-1, keepdims=True) + EPS) * kvg_ref[...]).astype(BF16)
    kn = _dot(ckvn, wuk_ref[...])
    vt_ref[0] = _dot_nt(wuvt_ref[...], ckvn).astype(BF16)
    q_heads(MLA_HEADS // 2, MLA_HEADS)
    lane = lax.broadcasted_iota(jnp.int32, (1, LANES), 1)
    on_rope = (lane >= MLA_NOPE) & (lane < MLA_QK)
    tail = ck[:, MLA_KV_LORA:MLA_KV_LORA + LANES]
    kpe = jnp.where(on_rope, pltpu.roll(tail, MLA_NOPE, axis=1), 0.0)
    partner = pltpu.roll(tail, MLA_ROPE, axis=1)
    kcg = cos * kg_ref[...]
    ksg = sin * kgp_ref[...]
    rot = kpe * kcg + partner * ksg
    ss_pe = jnp.sum(kpe * kpe, axis=-1, keepdims=True)
    for hd in range(MLA_HEADS):
        sl = slice(hd * LANES, (hd + 1) * LANES)
        nope = kn[:, sl]
        inv = lax.rsqrt((jnp.sum(nope * nope, axis=-1, keepdims=True) + ss_pe) * (1.0 / MLA_QK) + EPS)
        k_ref[0, :, sl] = ((nope * kcg + rot) * inv).astype(BF16)


def _mla_qkv(x, q_args, kv_args, cos, sin, tm):
    b, s, d = x.shape
    nq = q_args[5].shape[1]
    nk = kv_args[5].shape[1]
    nv = kv_args[6].shape[0]
    tab = pl.BlockSpec((1, tm, LANES), lambda bi, i: (bi, i, 0))
    full = lambda a: pl.BlockSpec(a.shape, lambda bi, i: (0, 0))

    def path(args):
        shift, scale = args[:2]
        return [shift[0], scale[0], *args[2:]], [_mod_spec(shift, d), _mod_spec(scale, d)] + [full(a) for a in args[2:]]

    q_ins, q_specs = path(q_args)
    kv_ins, kv_specs = path(kv_args)
    return pl.pallas_call(
        _mla_qkv_kernel,
        grid=(b, s // tm),
        in_specs=[pl.BlockSpec((1, tm, d), lambda bi, i: (bi, i, 0))] + q_specs + kv_specs + [tab, tab],
        out_specs=[pl.BlockSpec((1, tm, nq), lambda bi, i: (bi, i, 0)),
                   pl.BlockSpec((1, tm, nk), lambda bi, i: (bi, i, 0)),
                   pl.BlockSpec((1, nv, tm), lambda bi, i: (bi, 0, i))],
        out_shape=[jax.ShapeDtypeStruct((b, s, nq), BF16), jax.ShapeDtypeStruct((b, s, nk), BF16),
                   jax.ShapeDtypeStruct((b, nv, s), BF16)],
        compiler_params=_params("arbitrary", "arbitrary"),
        name="mla_qkv_proj",
    )(x, *q_ins, *kv_ins, cos, sin)


def _mla_attn_kernel(q_ref, k_ref, vt_ref, o_ref, *, t, nh, bounded):
    qi = pl.program_id(2)
    q2 = q_ref[0]
    q_heads = [q2[:, h * LANES:(h + 1) * LANES] for h in range(nh)]

    def chunk(j, carry, masked):
        carry = list(carry)
        start = pl.multiple_of(j * t, t)
        k2 = k_ref[0, pl.ds(start, t), :]
        vt = vt_ref[0, :, pl.ds(start, t)]
        s_heads = [_dot_nt(k2[:, h * LANES:(h + 1) * LANES], q_heads[h]) for h in range(nh)]
        if masked:
            row = lax.broadcasted_iota(jnp.int32, (t, t), 0)
            col = lax.broadcasted_iota(jnp.int32, (t, t), 1)
            allowed = row <= col
        for h in range(nh):
            m, l, acc = carry[h]
            s = jnp.where(allowed, s_heads[h], NEG) if masked else s_heads[h]
            carry[h] = _softmax_step_t(s, m, l, acc, vt[h * MLA_V:(h + 1) * MLA_V, :], bounded)
        return tuple(carry)

    def own_chunk_bounded(carry):
        carry = list(carry)
        half = t // 2
        start = pl.multiple_of(qi * t, t)
        k2 = k_ref[0, pl.ds(start, t), :]
        vt = vt_ref[0, :, pl.ds(start, t)]
        causal = (lax.broadcasted_iota(jnp.int32, (half, half), 0) <= lax.broadcasted_iota(jnp.int32, (half, half), 1))
        quads = []
        for h in range(nh):
            kh = k2[:, h * LANES:(h + 1) * LANES]
            qh = q_heads[h]
            quads.append((_dot_nt(kh[:half], qh[:half]), _dot_nt(kh[:half], qh[half:]), _dot_nt(kh[half:], qh[half:])))
        for h in range(nh):
            m, l, acc = carry[h]
            s00, s01, s11 = quads[h]
            e00 = jnp.exp2(jnp.where(causal, s00, NEG))
            e01 = jnp.exp2(s01)
            e11 = jnp.exp2(jnp.where(causal, s11, NEG))
            vt0 = vt[h * MLA_V:(h + 1) * MLA_V, :half]
            vt1 = vt[h * MLA_V:(h + 1) * MLA_V, half:]
            l_add = jnp.concatenate(
                [jnp.sum(e00, axis=0, keepdims=True),
                 jnp.sum(e01, axis=0, keepdims=True) + jnp.sum(e11, axis=0, keepdims=True)], axis=1)
            acc_add = jnp.concatenate(
                [_dot(vt0, e00.astype(BF16)), _dot(vt0, e01.astype(BF16)) + _dot(vt1, e11.astype(BF16))], axis=1)
            carry[h] = (m, l + l_add, acc + acc_add)
        return tuple(carry)

    init = (jnp.full((1, t), NEG, F32), jnp.zeros((1, t), F32), jnp.zeros((MLA_V, t), F32))
    carry = lax.fori_loop(0, qi, lambda j, c: chunk(j, c, False), (init,) * nh)
    carry = own_chunk_bounded(carry) if bounded else chunk(qi, carry, True)
    out_t = jnp.concatenate([acc / l for (_, l, acc) in carry], axis=0)
    o_ref[0] = out_t.T.astype(BF16)


def _mla_attn(q, k, vt, t, nh, bounded):
    b, s, _ = q.shape
    return pl.pallas_call(
        functools.partial(_mla_attn_kernel, t=t, nh=nh, bounded=bounded),
        grid=(b, MLA_HEADS // nh, s // t),
        in_specs=[pl.BlockSpec((1, t, nh * LANES), lambda bi, g, qi: (bi, qi, g)),
                  pl.BlockSpec((1, s, nh * LANES), lambda bi, g, qi: (bi, 0, g)),
                  pl.BlockSpec((1, nh * MLA_V, s), lambda bi, g, qi: (bi, g, 0))],
        out_specs=pl.BlockSpec((1, t, nh * MLA_V), lambda bi, g, qi: (bi, qi, g)),
        out_shape=jax.ShapeDtypeStruct((b, s, MLA_HEADS * MLA_V), BF16),
        compiler_params=_params("arbitrary", "arbitrary", "arbitrary"),
        name="mla_attention",
    )(q, k, vt)


def _t5_bias_table(rel_bias):
    n = jnp.arange(MASKED_DIST)
    max_exact = REL_BUCKETS // 2
    nf = jnp.maximum(n, 1).astype(F32)
    large = max_exact + (jnp.log(nf / max_exact) / math.log(REL_MAX_DIST / max_exact)
                         * (REL_BUCKETS - max_exact)).astype(jnp.int32)
    large = jnp.minimum(large, REL_BUCKETS - 1)
    bucket = jnp.where(n < max_exact, n, large)
    table = jnp.concatenate([rel_bias.T[:, bucket] * LOG2E, jnp.full((MOBA_HEADS, 1), NEG, F32)], axis=1)
    return table.reshape(MOBA_HEADS // 2, 2, BIAS_TABLE)


def _pad_heads(w, width):
    k = w.shape[0]
    w3 = w.reshape(k, -1, width)
    return jnp.pad(w3, ((0, 0), (0, 0), (0, LANES - width))).reshape(k, -1)


def _pad_lanes(g):
    return jnp.pad(g, (0, LANES - g.shape[0])).reshape(1, LANES)


def _swap_rope_halves(a):
    half = MLA_ROPE // 2
    return jnp.concatenate([jnp.zeros_like(a[..., :MLA_NOPE]), a[..., MLA_NOPE + half:],
                            a[..., MLA_NOPE:MLA_NOPE + half]], axis=-1)


def _scores_bounded(q_gain, k_gain, dim, scale, extra=0.0):
    bound = dim * jnp.max(jnp.abs(q_gain)) * jnp.max(jnp.abs(k_gain)) * scale * BOUND_SLACK + extra
    return bound <= SCORE_BOUND_MAX


def kernel(x, c, positions, ada_w, ada_b, norm_g, ffn_w_gate, ffn_w_up, ffn_w_down, rel_bias,
           moba_w_qkv, moba_q_g, moba_k_g, moba_w_o, kv_ada_w, kv_ada_b, kv_norm_g, w_dkv,
           kv_a_norm_g, w_uk, w_uv, mla_k_g, mla_w_dq, mla_q_a_norm_g, mla_w_uq, mla_q_g, mla_w_o):
    b, s, d = x.shape
    depth = ada_w.shape[0]
    n_moba = moba_w_qkv.shape[0]
    assert b <= SUBLANES and s % max(FFN_ROWS, MOBA_TILE, MLA_TILE) == 0 and MOBA_TILE == 2 * MOBA_BLOCK
    tm_ffn = FFN_ROWS
    tm_proj = PROJ_ROWS

    c_pad = jnp.pad(c, ((0, SUBLANES - b), (0, 0)))
    mods = _mods(c_pad, ada_w, ada_b.reshape(depth, 1, -1), tn=ADALN_COLS)
    kv_mods = _mods(c_pad, kv_ada_w[None], kv_ada_b.reshape(1, 1, -1), tn=d)

    w_gate, w_up, w_down = (w.astype(BF16) for w in (ffn_w_gate, ffn_w_up, ffn_w_down))

    for layer in range(depth):
        sh1, sc1, g1, sh2, sc2, g2, sh3, sc3, g3 = [(mods, layer, k) for k in range(mods.shape[-1] // d)]
        ng = norm_g[layer]

        def ffn(xin, idx, sh, sc, gt, mixer=None):
            return _ffn(xin, sh, sc, gt, ng[2 * idx:2 * idx + 1], w_gate, w_up, w_down, (layer, idx),
                        tm_ffn, FFN_CHUNK, mixer)

        x = ffn(x, 0, sh1, sc1, g1)

        if layer < n_moba:
            n_qk = 2 * MOBA_HEADS * MOBA_HEAD_DIM
            q_scale = MOBA_HEAD_DIM ** -0.5 * LOG2E
            head_gain = jnp.concatenate([jnp.tile(moba_q_g[layer] * q_scale, MOBA_HEADS),
                                         jnp.tile(moba_k_g[layer], MOBA_HEADS)]).reshape(1, -1)
            w_qkv = moba_w_qkv[layer].astype(BF16)
            qk, vt = _moba_proj(x, sh2, sc2, ng[1:2], w_qkv[:, :n_qk], w_qkv[:, n_qk:].T, head_gain,
                                2 * tm_proj, PROJ_COLS)
            bias_table = _t5_bias_table(rel_bias)
            moba = functools.partial(_moba_attn, qk, vt, bias_table, _moba_position_tables(positions, MOBA_TILE),
                                     MOBA_TILE, ATTN_HEADS // 2)
            ok = _scores_bounded(moba_q_g[layer], moba_k_g[layer], MOBA_HEAD_DIM, q_scale,
                                 jnp.max(jnp.abs(rel_bias)) * LOG2E)
            o = lax.cond(ok, lambda: moba(True), lambda: moba(False))
            w_o = moba_w_o[layer]
        else:
            j = layer - n_moba
            if j == 0:
                half = MLA_ROPE // 2
                inv = ROPE_BASE ** (-jnp.arange(half, dtype=F32) / half)
                inv_lane = jnp.zeros((LANES,), F32).at[MLA_NOPE:MLA_QK].set(jnp.tile(inv, 2)).reshape(1, LANES)
                cos, sin = _rope_tables(positions.reshape(b, s, 1), inv_lane, tm_proj)
                kv_sh, kv_sc = (kv_mods, 0, 0), (kv_mods, 0, 1)
                w_dkv16 = w_dkv.astype(BF16)
                w_kpe = w_dkv16[:, MLA_KV_LORA:]
                w_kpe_swapped = jnp.concatenate([w_kpe[:, half:], w_kpe[:, :half]], axis=1)
                wdkv_p = jnp.pad(jnp.concatenate([w_dkv16, w_kpe_swapped], axis=1),
                                 ((0, 0), (0, LANES - 2 * MLA_ROPE)))
                kv_args = (kv_sh, kv_sc, kv_norm_g.reshape(1, d), wdkv_p, kv_a_norm_g.reshape(1, -1),
                           _pad_heads(w_uk.astype(BF16), MLA_NOPE), w_uv.astype(BF16).T, _pad_lanes(mla_k_g),
                           _pad_lanes(_swap_rope_halves(mla_k_g)))
            w_uq = mla_w_uq[j].astype(BF16)
            w_uq_partner = _swap_rope_halves(w_uq.reshape(w_uq.shape[0], MLA_HEADS, MLA_QK)).reshape(w_uq.shape)
            q_args = (sh2, sc2, ng[1:2], mla_w_dq[j].astype(BF16), mla_q_a_norm_g[j].reshape(1, -1),
                      _pad_heads(w_uq, MLA_QK), _pad_heads(w_uq_partner, MLA_QK),
                      _pad_lanes(mla_q_g[j]), _pad_lanes(_swap_rope_halves(mla_q_g[j])))
            if j == 0:
                q, shared_k, shared_v = _mla_qkv(x, q_args, kv_args, cos, sin, tm_proj)
            else:
                q = _mla_q(x, *q_args, cos, sin, tm_proj)
            mla = functools.partial(_mla_attn, q, shared_k, shared_v, MLA_TILE, ATTN_HEADS)
            ok = _scores_bounded(mla_q_g[j], mla_k_g, MLA_QK, MLA_QK ** -0.5 * LOG2E)
            o = lax.cond(ok, lambda: mla(True), lambda: mla(False))
            w_o = mla_w_o[j]

        x = ffn(x, 1, sh3, sc3, g3, mixer=(o, g2, w_o.astype(BF16)))
    return x
```

```python
import functools
import math

import jax
import jax.numpy as jnp
from jax import lax
from jax.experimental import pallas as pl
from jax.experimental.pallas import tpu as pltpu

F32 = jnp.float32
BF16 = jnp.bfloat16

LANES = 128
SUBLANES = 8
VMEM_LIMIT_BYTES = 56 * 1024 * 1024

MOBA_HEADS = 16
MOBA_HEAD_DIM = 64
MOBA_BLOCK = 256
MOBA_TOPK = 3
REL_BUCKETS = 32
REL_MAX_DIST = 128
MLA_HEADS = 16
MLA_KV_LORA = 256
MLA_NOPE = 64
MLA_ROPE = 32
MLA_V = 64
MLA_QK = MLA_NOPE + MLA_ROPE
ROPE_BASE = 10000.0
EPS = 1e-6
NEG = -1e30

FFN_ROWS = 1024
FFN_CHUNK = 256
PROJ_ROWS = 512
PROJ_COLS = 512
ADALN_COLS = 2304
MOBA_TILE = 512
MLA_TILE = 512
ATTN_HEADS = 8
BOUND_SLACK = 1.01
LOG2E = math.log2(math.e)
BIAS_TABLE = LANES
MAX_DIST = BIAS_TABLE - 2
MASKED_DIST = BIAS_TABLE - 1
SCORE_BOUND_MAX = 60.0


def _params(*sem):
    return pltpu.CompilerParams(dimension_semantics=sem, vmem_limit_bytes=VMEM_LIMIT_BYTES)


def _dot(a, b):
    return jnp.dot(a, b, preferred_element_type=F32)


def _dot_nt(a, b):
    return lax.dot_general(a, b, (((1,), (1,)), ((), ())), preferred_element_type=F32)


def _rms_mod(x, g, shift, scale):
    ms = jnp.mean(x * x, axis=-1, keepdims=True)
    y = x * lax.rsqrt(ms + EPS) * g
    return y * (1.0 + scale) + shift


def _mod_spec(mod, d):
    _, layer, k = mod
    return pl.BlockSpec((None, SUBLANES, d), lambda bi, *_: (layer, 0, k))


def _mod_row(ref):
    return ref[pl.ds(pl.program_id(0), 1), :]


def _mods_kernel(c_ref, w_ref, b_ref, o_ref):
    c = c_ref[...]
    ca = c * jax.nn.sigmoid(c)
    w = w_ref[0]
    w_hi = w.astype(BF16)
    w_lo = (w - w_hi.astype(F32)).astype(BF16)
    ca_hi = ca.astype(BF16)
    ca_lo = (ca - ca_hi.astype(F32)).astype(BF16)
    o_ref[0] = _dot(ca_hi, w_hi) + _dot(ca_lo, w_hi) + _dot(ca_hi, w_lo) + b_ref[0]


def _mods(c_pad, w, b, tn):
    nl, d, n = w.shape
    return pl.pallas_call(
        _mods_kernel,
        grid=(nl, n // tn),
        in_specs=[pl.BlockSpec((SUBLANES, d), lambda l, j: (0, 0)),
                  pl.BlockSpec((1, d, tn), lambda l, j: (l, 0, j)),
                  pl.BlockSpec((1, 1, tn), lambda l, j: (l, 0, j))],
        out_specs=pl.BlockSpec((1, SUBLANES, tn), lambda l, j: (l, 0, j)),
        out_shape=jax.ShapeDtypeStruct((nl, SUBLANES, n), F32),
        compiler_params=_params("arbitrary", "arbitrary"),
        name="adaln_mods",
    )(c_pad, w, b)


def _ffn_kernel(*refs, tf, with_mixer):
    if with_mixer:
        x_ref, a_ref, gm_ref, wo_ref = refs[:4]
        refs = refs[4:]
        x = x_ref[0] + _mod_row(gm_ref) * _dot(a_ref[0], wo_ref[...])
    else:
        x_ref = refs[0]
        refs = refs[1:]
        x = x_ref[0]
    sh_ref, sc_ref, gt_ref, g_ref, wg_ref, wu_ref, wd_ref, o_ref = refs
    h = _rms_mod(x, g_ref[...], _mod_row(sh_ref), _mod_row(sc_ref)).astype(BF16)
    acc = None
    for f in range(wg_ref.shape[1] // tf):
        cols = slice(f * tf, (f + 1) * tf)
        g = _dot(h, wg_ref[:, cols])
        u = _dot(h, wu_ref[:, cols])
        a = (g * jax.nn.sigmoid(g) * u).astype(BF16)
        d = _dot(a, wd_ref[cols, :])
        acc = d if acc is None else acc + d
    o_ref[0] = x + (0.5 * _mod_row(gt_ref)) * acc


def _ffn(x, shift, scale, gate, g, wg, wu, wd, which, tm, tf, mixer=None):
    b, s, d = x.shape
    f = wg.shape[-1]
    row = pl.BlockSpec((1, tm, d), lambda bi, i: (bi, i, 0))
    full = lambda shp: pl.BlockSpec(shp, lambda bi, i: (0, 0), pipeline_mode=pl.Buffered(1))
    picked = lambda r, c: pl.BlockSpec((None, None, r, c), lambda bi, i: (*which, 0, 0),
                                       pipeline_mode=pl.Buffered(1))
    ins, specs = [x], [row]
    if mixer is not None:
        a, gate_mix, w_o = mixer
        k = a.shape[2]
        ins += [a, gate_mix[0], w_o]
        specs += [pl.BlockSpec((1, tm, k), lambda bi, i: (bi, i, 0)), _mod_spec(gate_mix, d), full((k, d))]
    ins += [shift[0], scale[0], gate[0], g, wg, wu, wd]
    specs += [_mod_spec(shift, d), _mod_spec(scale, d), _mod_spec(gate, d), full((1, d)),
              picked(d, f), picked(d, f), picked(f, d)]
    return pl.pallas_call(
        functools.partial(_ffn_kernel, tf=tf, with_mixer=mixer is not None),
        grid=(b, s // tm),
        in_specs=specs,
        out_specs=row,
        out_shape=jax.ShapeDtypeStruct((b, s, d), F32),
        compiler_params=_params("arbitrary", "arbitrary"),
        name="ffn_swiglu",
    )(*ins)


def _headnorm_pair(y, g, dim):
    lane = lax.broadcasted_iota(jnp.int32, (1, LANES), 1)
    left = lane < dim
    y2 = y * y
    ss_a = jnp.sum(jnp.where(left, y2, 0.0), axis=-1, keepdims=True)
    ss_b = jnp.sum(jnp.where(left, 0.0, y2), axis=-1, keepdims=True)
    inv = jnp.where(left, lax.rsqrt(ss_a * (1.0 / dim) + EPS), lax.rsqrt(ss_b * (1.0 / dim) + EPS))
    return y * inv * g


def _moba_proj_kernel(x_ref, sh_ref, sc_ref, g_ref, w_ref, wvt_ref, hg_ref, qk_ref, vt_ref, *, tn):
    h = _rms_mod(x_ref[0], g_ref[...], _mod_row(sh_ref), _mod_row(sc_ref)).astype(BF16)
    for n in range(w_ref.shape[1] // tn):
        res = _dot(h, w_ref[:, n * tn:(n + 1) * tn])
        for s in range(tn // LANES):
            sl = slice(n * tn + s * LANES, n * tn + (s + 1) * LANES)
            qk_ref[0, :, sl] = _headnorm_pair(res[:, s * LANES:(s + 1) * LANES], hg_ref[:, sl],
                                              MOBA_HEAD_DIM).astype(BF16)
    for n in range(wvt_ref.shape[0] // tn):
        rows = slice(n * tn, (n + 1) * tn)
        vt_ref[0, rows, :] = _dot_nt(wvt_ref[rows, :], h).astype(BF16)


def _moba_proj(x, shift, scale, g, w_qk, w_vt, head_gain, tm, tn):
    b, s, d = x.shape
    nqk = w_qk.shape[1]
    nv = w_vt.shape[0]
    full = lambda shp: pl.BlockSpec(shp, lambda bi, i: (0, 0))
    return pl.pallas_call(
        functools.partial(_moba_proj_kernel, tn=tn),
        grid=(b, s // tm),
        in_specs=[pl.BlockSpec((1, tm, d), lambda bi, i: (bi, i, 0)), _mod_spec(shift, d), _mod_spec(scale, d),
                  full((1, d)),
                  full((d, nqk)), full((nv, d)), full((1, nqk))],
        out_specs=[pl.BlockSpec((1, tm, nqk), lambda bi, i: (bi, i, 0)),
                   pl.BlockSpec((1, nv, tm), lambda bi, i: (bi, 0, i))],
        out_shape=[jax.ShapeDtypeStruct((b, s, nqk), BF16), jax.ShapeDtypeStruct((b, nv, s), BF16)],
        compiler_params=_params("arbitrary", "arbitrary"),
        name="moba_qkv_proj",
    )(x, shift[0], scale[0], g, w_qk, w_vt, head_gain)


def _softmax_step_t(s, m, l, acc, vt, bounded):
    if bounded:
        p = jnp.exp2(s)
        return m, l + jnp.sum(p, axis=0, keepdims=True), acc + _dot(vt, p.astype(BF16))
    m_new = jnp.maximum(m, jnp.max(s, axis=0, keepdims=True))
    alpha = jnp.exp2(m - m_new)
    p = jnp.exp2(s - m_new)
    l_new = alpha * l + jnp.sum(p, axis=0, keepdims=True)
    acc_new = acc * alpha + _dot(vt, p.astype(BF16))
    return m_new, l_new, acc_new


def _moba_attn_kernel(qmin_ref, kmax_ref, qgmin_ref, kbmax_ref, fpairs_ref, q_ref, k_ref, vt_ref, bt_ref, pq_ref, pk_ref, o_ref,
                      km_scr, sel_scr, *, nb, t, npairs, bounded):
    blk = MOBA_BLOCK
    hd = MOBA_HEAD_DIM
    nsub = t // blk
    nh = 2 * npairs
    bi = pl.program_id(0)
    qi = pl.program_id(2)
    lane = lax.broadcasted_iota(jnp.int32, (1, LANES), 1)
    left = lane < hd

    @pl.when(qi == 0)
    def _():
        km_scr[...] = jnp.zeros_like(km_scr)
        for r in range(nb):
            kblk = k_ref[0, r * blk:(r + 1) * blk, :].astype(F32)
            km_scr[r:r + 1, :] = jnp.sum(kblk, axis=0, keepdims=True) * (1.0 / blk)

    q_heads = []
    for p in range(npairs):
        q2 = q_ref[0, :, p * LANES:(p + 1) * LANES]
        zero = jnp.zeros_like(q2)
        q_heads += [jnp.where(left, q2, zero), jnp.where(left, zero, q2)]

    col = lax.broadcasted_iota(jnp.int32, (1, t), 1)
    q_idx = qi * t + col
    q_sub = sum((col >= r * blk).astype(jnp.int32) for r in range(1, nsub))
    q_blk = qi * nsub + q_sub
    q_blk_start = q_blk * blk

    blk_row = lax.broadcasted_iota(jnp.int32, (nb, t), 0)
    for h in range(nh):
        km = km_scr[:, (h // 2) * LANES:(h // 2 + 1) * LANES]
        km_hi = km.astype(BF16)
        km_lo = (km - km_hi.astype(F32)).astype(BF16)
        gate = (_dot_nt(km_hi, q_heads[h]) + _dot_nt(km_lo, q_heads[h]))[:nb]
        gate = jnp.where(blk_row < q_blk, gate, -jnp.inf)
        sel = jnp.zeros((nb, t), F32)
        for _ in range(MOBA_TOPK):
            mx = jnp.max(gate, axis=0, keepdims=True)
            idx = jnp.min(jnp.where(gate == mx, blk_row, nb), axis=0, keepdims=True)
            pick = (blk_row == idx) & (mx > -jnp.inf)
            sel = jnp.where(pick, 1.0, sel)
            gate = jnp.where(pick, -jnp.inf, gate)
        sel_scr[h] = sel

    pq = pq_ref[0, 0]

    def gather_bias(h, dist):
        table = jnp.broadcast_to(bt_ref[h // 2, h % 2:h % 2 + 1, :], (dist.shape[0], LANES))
        return jnp.concatenate(
            [jnp.take_along_axis(table, dist[:, g * LANES:(g + 1) * LANES], axis=1, mode="promise_in_bounds")
             for g in range(dist.shape[1] // LANES)], axis=1)

    def far_bias(h):
        return bt_ref[h // 2, h % 2:h % 2 + 1, MAX_DIST:MAX_DIST + 1]

    def chosen_rows(h, j, lanes=slice(None), width=1):
        return [sel_scr[h, pl.ds(j * nsub + r, 1), lanes] > 0.0 for r in range(width * nsub)]

    def own_chunk_bounded(carry):
        assert nsub == 2
        carry = list(carry)
        start = pl.multiple_of(qi * t, t)
        k2 = k_ref[0, pl.ds(start, t), :]
        vt = vt_ref[0, :, pl.ds(start, t)]
        pk0 = pk_ref[0, pl.ds(start, blk), :]
        pk1 = pk_ref[0, pl.ds(start + blk, blk), :]
        causal = (lax.broadcasted_iota(jnp.int32, (blk, blk), 0) <= lax.broadcasted_iota(jnp.int32, (blk, blk), 1))
        d00 = jnp.where(causal, jnp.clip(pq[:, :blk] - pk0, 0, MAX_DIST), MASKED_DIST)
        d01 = jnp.clip(pq[:, blk:] - pk0, 0, MAX_DIST)
        d11 = jnp.where(causal, jnp.clip(pq[:, blk:] - pk1, 0, MAX_DIST), MASKED_DIST)
        quads = []
        for h in range(nh):
            kh = k2[:, (h // 2) * LANES:(h // 2 + 1) * LANES]
            qh = q_heads[h]
            quads.append((_dot_nt(kh[:blk], qh[:blk]), _dot_nt(kh[:blk], qh[blk:]), _dot_nt(kh[blk:], qh[blk:])))
        for h in range(nh):
            m, l, acc = carry[h]
            s00, s01, s11 = quads[h]
            e00 = jnp.exp2(s00 + gather_bias(h, d00))
            e01 = jnp.exp2(s01 + gather_bias(h, d01))
            e11 = jnp.exp2(s11 + gather_bias(h, d11))
            w01 = jnp.where(chosen_rows(h, qi, slice(blk, t))[0], 1.0, 0.0)
            vt0 = vt[h * hd:(h + 1) * hd, :blk]
            vt1 = vt[h * hd:(h + 1) * hd, blk:]
            l_add = jnp.concatenate(
                [jnp.sum(e00, axis=0, keepdims=True),
                 jnp.sum(e01, axis=0, keepdims=True) * w01 + jnp.sum(e11, axis=0, keepdims=True)], axis=1)
            acc_add = jnp.concatenate(
                [_dot(vt0, e00.astype(BF16)),
                 _dot(vt0, e01.astype(BF16)) * w01 + _dot(vt1, e11.astype(BF16))], axis=1)
            carry[h] = (m, l + l_add, acc + acc_add)
        return tuple(carry)

    def chunk(j, carry, mode, width=1):
        carry = list(carry)
        start = pl.multiple_of(j * t, t)
        k2 = k_ref[0, pl.ds(start, width * t), :]
        vt = vt_ref[0, :, pl.ds(start, width * t)]
        s_heads = [_dot_nt(k2[:, (h // 2) * LANES:(h // 2 + 1) * LANES], q_heads[h]) for h in range(nh)]
        if mode in ("near", "diag"):
            dist = jnp.clip(pq - pk_ref[0, pl.ds(start, t), :], 0, MAX_DIST)
        if mode == "corner":
            dist_c = jnp.clip(pq[:, :LANES] - pk_ref[0, pl.ds(start + (nsub - 1) * blk, blk), :], 0, MAX_DIST)
        if mode == "diag":
            key_idx = start + lax.broadcasted_iota(jnp.int32, (blk, t), 0)
        for h in range(nh):
            m, l, acc = carry[h]
            s = s_heads[h]
            vt_h = vt[h * hd:(h + 1) * hd, :]
            chosen = chosen_rows(h, j, width=width)
            if mode in ("far", "corner") and bounded:
                two_c = jnp.exp2(far_bias(h))
                for r in range(width * nsub):
                    piece = s[r * blk:(r + 1) * blk]
                    w = jnp.where(chosen[r], two_c, 0.0)
                    if mode == "corner" and r == nsub - 1:
                        piece = jnp.concatenate([piece[:, :LANES] + gather_bias(h, dist_c), piece[:, LANES:]], axis=1)
                        w = jnp.where(col < LANES, jnp.where(chosen[r], 1.0, 0.0), w)
                    p = jnp.exp2(piece)
                    l = l + jnp.sum(p, axis=0, keepdims=True) * w
                    acc = acc + _dot(vt_h[:, r * blk:(r + 1) * blk], p.astype(BF16)) * w
                carry[h] = (m, l, acc)
            elif mode == "far":
                pieces = [s[r * blk:(r + 1) * blk] for r in range(nsub)]
                c = far_bias(h)
                tmax = [jnp.where(chosen[r], jnp.max(pieces[r], axis=0, keepdims=True) + c, NEG)
                        for r in range(nsub)]
                m_new = functools.reduce(jnp.maximum, tmax, m)
                alpha = jnp.exp2(m - m_new)
                ps = [jnp.exp2(pieces[r] - jnp.where(chosen[r], m_new - c, jnp.inf)) for r in range(nsub)]
                l_new = alpha * l + sum(jnp.sum(p, axis=0, keepdims=True) for p in ps)
                p = jnp.concatenate(ps, axis=0) if nsub > 1 else ps[0]
                carry[h] = (m_new, l_new, acc * alpha + _dot(vt_h, p.astype(BF16)))
            else:
                s = s + gather_bias(h, dist)
                masked = []
                for r in range(nsub):
                    piece = s[r * blk:(r + 1) * blk]
                    if mode == "diag":
                        kidx = key_idx + r * blk
                        keep_own = jnp.where(kidx <= q_idx, jnp.where(kidx >= q_blk_start, piece, NEG), NEG)
                        masked.append(jnp.where(chosen[r], piece, keep_own))
                    else:
                        masked.append(jnp.where(chosen[r], piece, NEG))
                s = jnp.concatenate(masked, axis=0) if nsub > 1 else masked[0]
                carry[h] = _softmax_step_t(s, m, l, acc, vt_h, bounded)
        return tuple(carry)

    def past_chunk(j, carry):
        q_lo = qmin_ref[bi, qi]
        far = q_lo - kmax_ref[bi, j] >= MAX_DIST
        if not bounded:
            return lax.cond(far, lambda c: chunk(j, c, "far"), lambda c: chunk(j, c, "near"), carry)
        groups = t // LANES
        q_rest = functools.reduce(jnp.minimum, [qgmin_ref[bi, qi * groups + g] for g in range(1, groups)])
        k_first = functools.reduce(jnp.maximum, [kbmax_ref[bi, j * nsub + r] for r in range(nsub - 1)])
        corner = (q_lo - k_first >= MAX_DIST) & (q_rest - kbmax_ref[bi, j * nsub + nsub - 1] >= MAX_DIST)
        return lax.cond(far, lambda c: chunk(j, c, "far"),
                        lambda c: lax.cond(corner, lambda c2: chunk(j, c2, "corner"),
                                           lambda c2: chunk(j, c2, "near"), c), carry)

    init = (jnp.full((1, t), NEG, F32), jnp.zeros((1, t), F32), jnp.zeros((hd, t), F32))
    if bounded:
        carry = own_chunk_bounded((init,) * nh)
        pairs = fpairs_ref[bi, qi]
        carry = lax.fori_loop(0, pairs, lambda p, c: chunk(2 * p, c, "far", width=2), carry)
        carry = lax.fori_loop(2 * pairs, qi, past_chunk, carry)
    else:
        carry = lax.fori_loop(0, qi, past_chunk, (init,) * nh)
        carry = chunk(qi, carry, "diag")
    out_t = jnp.concatenate([acc / l for (_, l, acc) in carry], axis=0)
    o_ref[0] = out_t.T.astype(BF16)


def _moba_position_tables(positions, t):
    b, s = positions.shape
    pos_tiles = positions.reshape(b, s // t, t)
    q_min, k_max = jnp.min(pos_tiles, axis=-1), jnp.max(pos_tiles, axis=-1)
    n = s // t
    far = (q_min[:, :, None] - k_max[:, None, :] >= MAX_DIST) & (jnp.arange(n)[None, :] < jnp.arange(n)[:, None])[None]
    pair_far = far[:, :, 0:2 * (n // 2):2] & far[:, :, 1:2 * (n // 2):2]
    far_pairs = jnp.sum(jnp.cumprod(pair_far.astype(jnp.int32), axis=-1), axis=-1)
    return (q_min, k_max,
            jnp.min(positions.reshape(b, s // LANES, LANES), axis=-1),
            jnp.max(positions.reshape(b, s // MOBA_BLOCK, MOBA_BLOCK), axis=-1), far_pairs,
            positions.reshape(b, s // t, 1, t), positions.reshape(b, s, 1))


def _moba_attn(qk, vt, bias_table, pos_tables, t, npairs, bounded):
    b, s, _ = qk.shape
    nb = s // MOBA_BLOCK
    ngroups = MOBA_HEADS * MOBA_HEAD_DIM // (LANES * npairs)
    w = npairs * LANES
    q_min, k_max, qg_min, kb_max, far_pairs, pos_row, pos_col = pos_tables
    grid_spec = pltpu.PrefetchScalarGridSpec(
        num_scalar_prefetch=5,
        grid=(b, ngroups, s // t),
        in_specs=[pl.BlockSpec((1, t, w), lambda bi, g, qi, *_: (bi, qi, g)),
                  pl.BlockSpec((1, s, w), lambda bi, g, qi, *_: (bi, 0, ngroups + g)),
                  pl.BlockSpec((1, w, s), lambda bi, g, qi, *_: (bi, g, 0)),
                  pl.BlockSpec((npairs, 2, LANES), lambda bi, g, qi, *_: (g, 0, 0)),
                  pl.BlockSpec((1, 1, 1, t), lambda bi, g, qi, *_: (bi, qi, 0, 0)),
                  pl.BlockSpec((1, s, 1), lambda bi, g, qi, *_: (bi, 0, 0))],
        out_specs=pl.BlockSpec((1, t, w), lambda bi, g, qi, *_: (bi, qi, g)),
        scratch_shapes=[pltpu.VMEM((LANES, w), F32), pltpu.VMEM((2 * npairs, nb, t), F32)],
    )
    return pl.pallas_call(
        functools.partial(_moba_attn_kernel, nb=nb, t=t, npairs=npairs, bounded=bounded),
        grid_spec=grid_spec,
        out_shape=jax.ShapeDtypeStruct((b, s, MOBA_HEADS * MOBA_HEAD_DIM), BF16),
        compiler_params=_params("arbitrary", "arbitrary", "arbitrary"),
        name="moba_attention",
    )(q_min, k_max, qg_min, kb_max, far_pairs, qk, qk, vt, bias_table, pos_row, pos_col)


def _rope_tab_kernel(pos_ref, inv_ref, c_ref, s_ref):
    half = MLA_ROPE // 2
    ang = pos_ref[0].astype(F32) * inv_ref[...]
    lane = lax.broadcasted_iota(jnp.int32, (1, LANES), 1)
    sn = jnp.sin(ang)
    c_ref[0] = jnp.cos(ang)
    s_ref[0] = jnp.where(lane < MLA_NOPE + half, -sn, sn)


def _rope_tables(pos_col, inv_lane, tm):
    b, s, _ = pos_col.shape
    out = pl.BlockSpec((1, tm, LANES), lambda bi, i: (bi, i, 0))
    shape = jax.ShapeDtypeStruct((b, s, LANES), F32)
    return pl.pallas_call(
        _rope_tab_kernel,
        grid=(b, s // tm),
        in_specs=[pl.BlockSpec((1, tm, 1), lambda bi, i: (bi, i, 0)),
                  pl.BlockSpec((1, LANES), lambda bi, i: (0, 0))],
        out_specs=[out, out],
        out_shape=[shape, shape],
        compiler_params=_params("arbitrary", "arbitrary"),
        name="rope_tables",
    )(pos_col, inv_lane)


def _mla_q_kernel(x_ref, sh_ref, sc_ref, g_ref, wdq_ref, qag_ref, wuq_ref, wuqp_ref, qg_ref, qgp_ref,
                  c_ref, s_ref, o_ref):
    h = _rms_mod(x_ref[0], g_ref[...], _mod_row(sh_ref), _mod_row(sc_ref)).astype(BF16)
    dq = _dot(h, wdq_ref[...])
    dqn = (dq * lax.rsqrt(jnp.mean(dq * dq, axis=-1, keepdims=True) + EPS) * qag_ref[...]).astype(BF16)
    uq = _dot(dqn, wuq_ref[...])
    up = _dot(dqn, wuqp_ref[...])
    cg = c_ref[0] * qg_ref[...]
    sg = s_ref[0] * qgp_ref[...]
    scale = MLA_QK ** -0.5 * LOG2E
    for hd in range(MLA_HEADS):
        sl = slice(hd * LANES, (hd + 1) * LANES)
        u = uq[:, sl]
        inv = lax.rsqrt(jnp.sum(u * u, axis=-1, keepdims=True) * (1.0 / MLA_QK) + EPS) * scale
        o_ref[0, :, sl] = ((u * cg + up[:, sl] * sg) * inv).astype(BF16)


def _mla_q(x, shift, scale, g, wdq, qag, wuq_p, wuq_partner, qg_p, qg_partner, cos, sin, tm):
    b, s, d = x.shape
    r = wdq.shape[1]
    nq = wuq_p.shape[1]
    tab = pl.BlockSpec((1, tm, LANES), lambda bi, i: (bi, i, 0))
    full = lambda shp: pl.BlockSpec(shp, lambda bi, i: (0, 0))
    return pl.pallas_call(
        _mla_q_kernel,
        grid=(b, s // tm),
        in_specs=[pl.BlockSpec((1, tm, d), lambda bi, i: (bi, i, 0)), _mod_spec(shift, d), _mod_spec(scale, d),
                  full((1, d)),
                  full((d, r)), full((1, r)), full((r, nq)), full((r, nq)), full((1, LANES)), full((1, LANES)),
                  tab, tab],
        out_specs=pl.BlockSpec((1, tm, nq), lambda bi, i: (bi, i, 0)),
        out_shape=jax.ShapeDtypeStruct((b, s, nq), BF16),
        compiler_params=_params("arbitrary", "arbitrary"),
        name="mla_q_path",
    )(x, shift[0], scale[0], g, wdq, qag, wuq_p, wuq_partner, qg_p, qg_partner, cos, sin)


def _mla_qkv_kernel(x_ref, qsh_ref, qsc_ref, qn_ref, wdq_ref, qag_ref, wuq_ref, wuqp_ref, qg_ref, qgp_ref,
                    ksh_ref, ksc_ref, kn_ref, wdkv_ref, kvg_ref, wuk_ref, wuvt_ref, kg_ref, kgp_ref,
                    c_ref, s_ref, q_ref, k_ref, vt_ref):
    x = x_ref[0]
    xn = x * lax.rsqrt(jnp.mean(x * x, axis=-1, keepdims=True) + EPS)
    cos, sin = c_ref[0], s_ref[0]
    h = ((xn * qn_ref[...]) * (1.0 + _mod_row(qsc_ref)) + _mod_row(qsh_ref)).astype(BF16)
    dq = _dot(h, wdq_ref[...])
    dqn = (dq * lax.rsqrt(jnp.mean(dq * dq, axis=-1, keepdims=True) + EPS) * qag_ref[...]).astype(BF16)
    uq = _dot(dqn, wuq_ref[...])
    up = _dot(dqn, wuqp_ref[...])
    hs = ((xn * kn_ref[...]) * (1.0 + _mod_row(ksc_ref)) + _mod_row(ksh_ref)).astype(BF16)
    ck = _dot(hs, wdkv_ref[...])
    cg = cos * qg_ref[...]
    sg = sin * qgp_ref[...]
    scale = MLA_QK ** -0.5 * LOG2E

    def q_heads(lo, hi):
        for hd in range(lo, hi):
            sl = slice(hd * LANES, (hd + 1) * LANES)
            u = uq[:, sl]
            inv = lax.rsqrt(jnp.sum(u * u, axis=-1, keepdims=True) * (1.0 / MLA_QK) + EPS) * scale
            q_ref[0, :, sl] = ((u * cg + up[:, sl] * sg) * inv).astype(BF16)

    q_heads(0, MLA_HEADS // 2)
    ckv = ck[:, :MLA_KV_LORA]
    ckvn = (ckv * lax.rsqrt(jnp.mean(ckv * ckv, axis=-1, keepdims=True) + EPS) * kvg_ref[...]).astype(BF16)
    kn = _dot(ckvn, wuk_ref[...])
    vt_ref[0] = _dot_nt(wuvt_ref[...], ckvn).astype(BF16)
    q_heads(MLA_HEADS // 2, MLA_HEADS)
    lane = lax.broadcasted_iota(jnp.int32, (1, LANES), 1)
    on_rope = (lane >= MLA_NOPE) & (lane < MLA_QK)
    tail = ck[:, MLA_KV_LORA:MLA_KV_LORA + LANES]
    kpe = jnp.where(on_rope, pltpu.roll(tail, MLA_NOPE, axis=1), 0.0)
    partner = pltpu.roll(tail, MLA_ROPE, axis=1)
    kcg = cos * kg_ref[...]
    ksg = sin * kgp_ref[...]
    rot = kpe * kcg + partner * ksg
    ss_pe = jnp.sum(kpe * kpe, axis=-1, keepdims=True)
    for hd in range(MLA_HEADS):
        sl = slice(hd * LANES, (hd + 1) * LANES)
        nope = kn[:, sl]
        inv = lax.rsqrt((jnp.sum(nope * nope, axis=-1, keepdims=True) + ss_pe) * (1.0 / MLA_QK) + EPS)
        k_ref[0, :, sl] = ((nope * kcg + rot) * inv).astype(BF16)


def _mla_qkv(x, q_args, kv_args, cos, sin, tm):
    b, s, d = x.shape
    nq = q_args[5].shape[1]
    nk = kv_args[5].shape[1]
    nv = kv_args[6].shape[0]
    tab = pl.BlockSpec((1, tm, LANES), lambda bi, i: (bi, i, 0))
    full = lambda a: pl.BlockSpec(a.shape, lambda bi, i: (0, 0))

    def path(args):
        shift, scale = args[:2]
        return [shift[0], scale[0], *args[2:]], [_mod_spec(shift, d), _mod_spec(scale, d)] + [full(a) for a in args[2:]]

    q_ins, q_specs = path(q_args)
    kv_ins, kv_specs = path(kv_args)
    return pl.pallas_call(
        _mla_qkv_kernel,
        grid=(b, s // tm),
        in_specs=[pl.BlockSpec((1, tm, d), lambda bi, i: (bi, i, 0))] + q_specs + kv_specs + [tab, tab],
        out_specs=[pl.BlockSpec((1, tm, nq), lambda bi, i: (bi, i, 0)),
                   pl.BlockSpec((1, tm, nk), lambda bi, i: (bi, i, 0)),
                   pl.BlockSpec((1, nv, tm), lambda bi, i: (bi, 0, i))],
        out_shape=[jax.ShapeDtypeStruct((b, s, nq), BF16), jax.ShapeDtypeStruct((b, s, nk), BF16),
                   jax.ShapeDtypeStruct((b, nv, s), BF16)],
        compiler_params=_params("arbitrary", "arbitrary"),
        name="mla_qkv_proj",
    )(x, *q_ins, *kv_ins, cos, sin)


def _mla_attn_kernel(q_ref, k_ref, vt_ref, o_ref, *, t, nh, bounded):
    qi = pl.program_id(2)
    q2 = q_ref[0]
    q_heads = [q2[:, h * LANES:(h + 1) * LANES] for h in range(nh)]

    def chunk(j, carry, masked):
        carry = list(carry)
        start = pl.multiple_of(j * t, t)
        k2 = k_ref[0, pl.ds(start, t), :]
        vt = vt_ref[0, :, pl.ds(start, t)]
        s_heads = [_dot_nt(k2[:, h * LANES:(h + 1) * LANES], q_heads[h]) for h in range(nh)]
        if masked:
            row = lax.broadcasted_iota(jnp.int32, (t, t), 0)
            col = lax.broadcasted_iota(jnp.int32, (t, t), 1)
            allowed = row <= col
        for h in range(nh):
            m, l, acc = carry[h]
            s = jnp.where(allowed, s_heads[h], NEG) if masked else s_heads[h]
            carry[h] = _softmax_step_t(s, m, l, acc, vt[h * MLA_V:(h + 1) * MLA_V, :], bounded)
        return tuple(carry)

    def own_chunk_bounded(carry):
        carry = list(carry)
        half = t // 2
        start = pl.multiple_of(qi * t, t)
        k2 = k_ref[0, pl.ds(start, t), :]
        vt = vt_ref[0, :, pl.ds(start, t)]
        causal = (lax.broadcasted_iota(jnp.int32, (half, half), 0) <= lax.broadcasted_iota(jnp.int32, (half, half), 1))
        quads = []
        for h in range(nh):
            kh = k2[:, h * LANES:(h + 1) * LANES]
            qh = q_heads[h]
            quads.append((_dot_nt(kh[:half], qh[:half]), _dot_nt(kh[:half], qh[half:]), _dot_nt(kh[half:], qh[half:])))
        for h in range(nh):
            m, l, acc = carry[h]
            s00, s01, s11 = quads[h]
            e00 = jnp.exp2(jnp.where(causal, s00, NEG))
            e01 = jnp.exp2(s01)
            e11 = jnp.exp2(jnp.where(causal, s11, NEG))
            vt0 = vt[h * MLA_V:(h + 1) * MLA_V, :half]
            vt1 = vt[h * MLA_V:(h + 1) * MLA_V, half:]
            l_add = jnp.concatenate(
                [jnp.sum(e00, axis=0, keepdims=True),
                 jnp.sum(e01, axis=0, keepdims=True) + jnp.sum(e11, axis=0, keepdims=True)], axis=1)
            acc_add = jnp.concatenate(
                [_dot(vt0, e00.astype(BF16)), _dot(vt0, e01.astype(BF16)) + _dot(vt1, e11.astype(BF16))], axis=1)
            carry[h] = (m, l + l_add, acc + acc_add)
        return tuple(carry)

    init = (jnp.full((1, t), NEG, F32), jnp.zeros((1, t), F32), jnp.zeros((MLA_V, t), F32))
    carry = lax.fori_loop(0, qi, lambda j, c: chunk(j, c, False), (init,) * nh)
    carry = own_chunk_bounded(carry) if bounded else chunk(qi, carry, True)
    out_t = jnp.concatenate([acc / l for (_, l, acc) in carry], axis=0)
    o_ref[0] = out_t.T.astype(BF16)


def _mla_attn(q, k, vt, t, nh, bounded):
    b, s, _ = q.shape
    return pl.pallas_call(
        functools.partial(_mla_attn_kernel, t=t, nh=nh, bounded=bounded),
        grid=(b, MLA_HEADS // nh, s // t),
        in_specs=[pl.BlockSpec((1, t, nh * LANES), lambda bi, g, qi: (bi, qi, g)),
                  pl.BlockSpec((1, s, nh * LANES), lambda bi, g, qi: (bi, 0, g)),
                  pl.BlockSpec((1, nh * MLA_V, s), lambda bi, g, qi: (bi, g, 0))],
        out_specs=pl.BlockSpec((1, t, nh * MLA_V), lambda bi, g, qi: (bi, qi, g)),
        out_shape=jax.ShapeDtypeStruct((b, s, MLA_HEADS * MLA_V), BF16),
        compiler_params=_params("arbitrary", "arbitrary", "arbitrary"),
        name="mla_attention",
    )(q, k, vt)


def _t5_bias_table(rel_bias):
    n = jnp.arange(MASKED_DIST)
    max_exact = REL_BUCKETS // 2
    nf = jnp.maximum(n, 1).astype(F32)
    large = max_exact + (jnp.log(nf / max_exact) / math.log(REL_MAX_DIST / max_exact)
                         * (REL_BUCKETS - max_exact)).astype(jnp.int32)
    large = jnp.minimum(large, REL_BUCKETS - 1)
    bucket = jnp.where(n < max_exact, n, large)
    table = jnp.concatenate([rel_bias.T[:, bucket] * LOG2E, jnp.full((MOBA_HEADS, 1), NEG, F32)], axis=1)
    return table.reshape(MOBA_HEADS // 2, 2, BIAS_TABLE)


def _pad_heads(w, width):
    k = w.shape[0]
    w3 = w.reshape(k, -1, width)
    return jnp.pad(w3, ((0, 0), (0, 0), (0, LANES - width))).reshape(k, -1)


def _pad_lanes(g):
    return jnp.pad(g, (0, LANES - g.shape[0])).reshape(1, LANES)


def _swap_rope_halves(a):
    half = MLA_ROPE // 2
    return jnp.concatenate([jnp.zeros_like(a[..., :MLA_NOPE]), a[..., MLA_NOPE + half:],
                            a[..., MLA_NOPE:MLA_NOPE + half]], axis=-1)


def _scores_bounded(q_gain, k_gain, dim, scale, extra=0.0):
    bound = dim * jnp.max(jnp.abs(q_gain)) * jnp.max(jnp.abs(k_gain)) * scale * BOUND_SLACK + extra
    return bound <= SCORE_BOUND_MAX


def kernel(x, c, positions, ada_w, ada_b, norm_g, ffn_w_gate, ffn_w_up, ffn_w_down, rel_bias,
           moba_w_qkv, moba_q_g, moba_k_g, moba_w_o, kv_ada_w, kv_ada_b, kv_norm_g, w_dkv,
           kv_a_norm_g, w_uk, w_uv, mla_k_g, mla_w_dq, mla_q_a_norm_g, mla_w_uq, mla_q_g, mla_w_o):
    b, s, d = x.shape
    depth = ada_w.shape[0]
    n_moba = moba_w_qkv.shape[0]
    assert b <= SUBLANES and s % max(FFN_ROWS, MOBA_TILE, MLA_TILE) == 0 and MOBA_TILE == 2 * MOBA_BLOCK
    tm_ffn = FFN_ROWS
    tm_proj = PROJ_ROWS

    c_pad = jnp.pad(c, ((0, SUBLANES - b), (0, 0)))
    mods = _mods(c_pad, ada_w, ada_b.reshape(depth, 1, -1), tn=ADALN_COLS)
    kv_mods = _mods(c_pad, kv_ada_w[None], kv_ada_b.reshape(1, 1, -1), tn=d)

    w_gate, w_up, w_down = (w.astype(BF16) for w in (ffn_w_gate, ffn_w_up, ffn_w_down))

    for layer in range(depth):
        sh1, sc1, g1, sh2, sc2, g2, sh3, sc3, g3 = [(mods, layer, k) for k in range(mods.shape[-1] // d)]
        ng = norm_g[layer]

        def ffn(xin, idx, sh, sc, gt, mixer=None):
            return _ffn(xin, sh, sc, gt, ng[2 * idx:2 * idx + 1], w_gate, w_up, w_down, (layer, idx),
                        tm_ffn, FFN_CHUNK, mixer)

        x = ffn(x, 0, sh1, sc1, g1)

        if layer < n_moba:
            n_qk = 2 * MOBA_HEADS * MOBA_HEAD_DIM
            q_scale = MOBA_HEAD_DIM ** -0.5 * LOG2E
            head_gain = jnp.concatenate([jnp.tile(moba_q_g[layer] * q_scale, MOBA_HEADS),
                                         jnp.tile(moba_k_g[layer], MOBA_HEADS)]).reshape(1, -1)
            w_qkv = moba_w_qkv[layer].astype(BF16)
            qk, vt = _moba_proj(x, sh2, sc2, ng[1:2], w_qkv[:, :n_qk], w_qkv[:, n_qk:].T, head_gain,
                                2 * tm_proj, PROJ_COLS)
            bias_table = _t5_bias_table(rel_bias)
            moba = functools.partial(_moba_attn, qk, vt, bias_table, _moba_position_tables(positions, MOBA_TILE),
                                     MOBA_TILE, ATTN_HEADS // 2)
            ok = _scores_bounded(moba_q_g[layer], moba_k_g[layer], MOBA_HEAD_DIM, q_scale,
                                 jnp.max(jnp.abs(rel_bias)) * LOG2E)
            o = lax.cond(ok, lambda: moba(True), lambda: moba(False))
            w_o = moba_w_o[layer]
        else:
            j = layer - n_moba
            if j == 0:
                half = MLA_ROPE // 2
                inv = ROPE_BASE ** (-jnp.arange(half, dtype=F32) / half)
                inv_lane = jnp.zeros((LANES,), F32).at[MLA_NOPE:MLA_QK].set(jnp.tile(inv, 2)).reshape(1, LANES)
                cos, sin = _rope_tables(positions.reshape(b, s, 1), inv_lane, tm_proj)
                kv_sh, kv_sc = (kv_mods, 0, 0), (kv_mods, 0, 1)
                w_dkv16 = w_dkv.astype(BF16)
                w_kpe = w_dkv16[:, MLA_KV_LORA:]
                w_kpe_swapped = jnp.concatenate([w_kpe[:, half:], w_kpe[:, :half]], axis=1)
                wdkv_p = jnp.pad(jnp.concatenate([w_dkv16, w_kpe_swapped], axis=1),
                                 ((0, 0), (0, LANES - 2 * MLA_ROPE)))
                kv_args = (kv_sh, kv_sc, kv_norm_g.reshape(1, d), wdkv_p, kv_a_norm_g.reshape(1, -1),
                           _pad_heads(w_uk.astype(BF16), MLA_NOPE), w_uv.astype(BF16).T, _pad_lanes(mla_k_g),
                           _pad_lanes(_swap_rope_halves(mla_k_g)))
            w_uq = mla_w_uq[j].astype(BF16)
            w_uq_partner = _swap_rope_halves(w_uq.reshape(w_uq.shape[0], MLA_HEADS, MLA_QK)).reshape(w_uq.shape)
            q_args = (sh2, sc2, ng[1:2], mla_w_dq[j].astype(BF16), mla_q_a_norm_g[j].reshape(1, -1),
                      _pad_heads(w_uq, MLA_QK), _pad_heads(w_uq_partner, MLA_QK),
                      _pad_lanes(mla_q_g[j]), _pad_lanes(_swap_rope_halves(mla_q_g[j])))
            if j == 0:
                q, shared_k, shared_v = _mla_qkv(x, q_args, kv_args, cos, sin, tm_proj)
            else:
                q = _mla_q(x, *q_args, cos, sin, tm_proj)
            mla = functools.partial(_mla_attn, q, shared_k, shared_v, MLA_TILE, ATTN_HEADS)
            ok = _scores_bounded(mla_q_g[j], mla_k_g, MLA_QK, MLA_QK ** -0.5 * LOG2E)
            o = lax.cond(ok, lambda: mla(True), lambda: mla(False))
            w_o = mla_w_o[j]

        x = ffn(x, 1, sh3, sc3, g3, mixer=(o, g2, w_o.astype(BF16)))
    return x
```

```python
import functools
import math

import jax
import jax.numpy as jnp
from jax import lax
from jax.experimental import pallas as pl
from jax.experimental.pallas import tpu as pltpu

F32 = jnp.float32
BF16 = jnp.bfloat16

LANES = 128
SUBLANES = 8
VMEM_LIMIT_BYTES = 56 * 1024 * 1024

MOBA_HEADS = 16
MOBA_HEAD_DIM = 64
MOBA_BLOCK = 256
MOBA_TOPK = 3
REL_BUCKETS = 32
REL_MAX_DIST = 128
MLA_HEADS = 16
MLA_KV_LORA = 256
MLA_NOPE = 64
MLA_ROPE = 32
MLA_V = 64
MLA_QK = MLA_NOPE + MLA_ROPE
ROPE_BASE = 10000.0
EPS = 1e-6
NEG = -1e30

FFN_ROWS = 1024
FFN_CHUNK = 256
PROJ_ROWS = 512
PROJ_COLS = 512
ADALN_COLS = 2304
MOBA_TILE = 512
MLA_TILE = 512
ATTN_HEADS = 8
BOUND_SLACK = 1.01
LOG2E = math.log2(math.e)
BIAS_TABLE = LANES
MAX_DIST = BIAS_TABLE - 2
MASKED_DIST = BIAS_TABLE - 1
SCORE_BOUND_MAX = 60.0


def _params(*sem):
    return pltpu.CompilerParams(dimension_semantics=sem, vmem_limit_bytes=VMEM_LIMIT_BYTES)


def _dot(a, b):
    return jnp.dot(a, b, preferred_element_type=F32)


def _dot_nt(a, b):
    return lax.dot_general(a, b, (((1,), (1,)), ((), ())), preferred_element_type=F32)


def _rms_mod(x, g, shift, scale):
    ms = jnp.mean(x * x, axis=-1, keepdims=True)
    y = x * lax.rsqrt(ms + EPS) * g
    return y * (1.0 + scale) + shift


def _mod_spec(mod, d):
    _, layer, k = mod
    return pl.BlockSpec((None, SUBLANES, d), lambda bi, *_: (layer, 0, k))


def _mod_row(ref):
    return ref[pl.ds(pl.program_id(0), 1), :]


def _mods_kernel(c_ref, w_ref, b_ref, o_ref):
    c = c_ref[...]
    ca = c * jax.nn.sigmoid(c)
    w = w_ref[0]
    w_hi = w.astype(BF16)
    w_lo = (w - w_hi.astype(F32)).astype(BF16)
    ca_hi = ca.astype(BF16)
    ca_lo = (ca - ca_hi.astype(F32)).astype(BF16)
    o_ref[0] = _dot(ca_hi, w_hi) + _dot(ca_lo, w_hi) + _dot(ca_hi, w_lo) + b_ref[0]


def _mods(c_pad, w, b, tn):
    nl, d, n = w.shape
    return pl.pallas_call(
        _mods_kernel,
        grid=(nl, n // tn),
        in_specs=[pl.BlockSpec((SUBLANES, d), lambda l, j: (0, 0)),
                  pl.BlockSpec((1, d, tn), lambda l, j: (l, 0, j)),
                  pl.BlockSpec((1, 1, tn), lambda l, j: (l, 0, j))],
        out_specs=pl.BlockSpec((1, SUBLANES, tn), lambda l, j: (l, 0, j)),
        out_shape=jax.ShapeDtypeStruct((nl, SUBLANES, n), F32),
        compiler_params=_params("arbitrary", "arbitrary"),
        name="adaln_mods",
    )(c_pad, w, b)


def _ffn_kernel(*refs, tf, with_mixer):
    if with_mixer:
        x_ref, a_ref, gm_ref, wo_ref = refs[:4]
        refs = refs[4:]
        x = x_ref[0] + _mod_row(gm_ref) * _dot(a_ref[0], wo_ref[...])
    else:
        x_ref = refs[0]
        refs = refs[1:]
        x = x_ref[0]
    sh_ref, sc_ref, gt_ref, g_ref, wg_ref, wu_ref, wd_ref, o_ref = refs
    h = _rms_mod(x, g_ref[...], _mod_row(sh_ref), _mod_row(sc_ref)).astype(BF16)
    acc = None
    for f in range(wg_ref.shape[1] // tf):
        cols = slice(f * tf, (f + 1) * tf)
        g = _dot(h, wg_ref[:, cols])
        u = _dot(h, wu_ref[:, cols])
        a = (g * jax.nn.sigmoid(g) * u).astype(BF16)
        d = _dot(a, wd_ref[cols, :])
        acc = d if acc is None else acc + d
    o_ref[0] = x + (0.5 * _mod_row(gt_ref)) * acc


def _ffn(x, shift, scale, gate, g, wg, wu, wd, which, tm, tf, mixer=None):
    b, s, d = x.shape
    f = wg.shape[-1]
    row = pl.BlockSpec((1, tm, d), lambda bi, i: (bi, i, 0))
    full = lambda shp: pl.BlockSpec(shp, lambda bi, i: (0, 0), pipeline_mode=pl.Buffered(1))
    picked = lambda r, c: pl.BlockSpec((None, None, r, c), lambda bi, i: (*which, 0, 0),
                                       pipeline_mode=pl.Buffered(1))
    ins, specs = [x], [row]
    if mixer is not None:
        a, gate_mix, w_o = mixer
        k = a.shape[2]
        ins += [a, gate_mix[0], w_o]
        specs += [pl.BlockSpec((1, tm, k), lambda bi, i: (bi, i, 0)), _mod_spec(gate_mix, d), full((k, d))]
    ins += [shift[0], scale[0], gate[0], g, wg, wu, wd]
    specs += [_mod_spec(shift, d), _mod_spec(scale, d), _mod_spec(gate, d), full((1, d)),
              picked(d, f), picked(d, f), picked(f, d)]
    return pl.pallas_call(
        functools.partial(_ffn_kernel, tf=tf, with_mixer=mixer is not None),
        grid=(b, s // tm),
        in_specs=specs,
        out_specs=row,
        out_shape=jax.ShapeDtypeStruct((b, s, d), F32),
        compiler_params=_params("arbitrary", "arbitrary"),
        name="ffn_swiglu",
    )(*ins)


def _headnorm_pair(y, g, dim):
    lane = lax.broadcasted_iota(jnp.int32, (1, LANES), 1)
    left = lane < dim
    y2 = y * y
    ss_a = jnp.sum(jnp.where(left, y2, 0.0), axis=-1, keepdims=True)
    ss_b = jnp.sum(jnp.where(left, 0.0, y2), axis=-1, keepdims=True)
    inv = jnp.where(left, lax.rsqrt(ss_a * (1.0 / dim) + EPS), lax.rsqrt(ss_b * (1.0 / dim) + EPS))
    return y * inv * g


def _moba_proj_kernel(x_ref, sh_ref, sc_ref, g_ref, w_ref, wvt_ref, hg_ref, qk_ref, vt_ref, *, tn):
    h = _rms_mod(x_ref[0], g_ref[...], _mod_row(sh_ref), _mod_row(sc_ref)).astype(BF16)
    for n in range(w_ref.shape[1] // tn):
        res = _dot(h, w_ref[:, n * tn:(n + 1) * tn])
        for s in range(tn // LANES):
            sl = slice(n * tn + s * LANES, n * tn + (s + 1) * LANES)
            qk_ref[0, :, sl] = _headnorm_pair(res[:, s * LANES:(s + 1) * LANES], hg_ref[:, sl],
                                              MOBA_HEAD_DIM).astype(BF16)
    for n in range(wvt_ref.shape[0] // tn):
        rows = slice(n * tn, (n + 1) * tn)
        vt_ref[0, rows, :] = _dot_nt(wvt_ref[rows, :], h).astype(BF16)


def _moba_proj(x, shift, scale, g, w_qk, w_vt, head_gain, tm, tn):
    b, s, d = x.shape
    nqk = w_qk.shape[1]
    nv = w_vt.shape[0]
    full = lambda shp: pl.BlockSpec(shp, lambda bi, i: (0, 0))
    return pl.pallas_call(
        functools.partial(_moba_proj_kernel, tn=tn),
        grid=(b, s // tm),
        in_specs=[pl.BlockSpec((1, tm, d), lambda bi, i: (bi, i, 0)), _mod_spec(shift, d), _mod_spec(scale, d),
                  full((1, d)),
                  full((d, nqk)), full((nv, d)), full((1, nqk))],
        out_specs=[pl.BlockSpec((1, tm, nqk), lambda bi, i: (bi, i, 0)),
                   pl.BlockSpec((1, nv, tm), lambda bi, i: (bi, 0, i))],
        out_shape=[jax.ShapeDtypeStruct((b, s, nqk), BF16), jax.ShapeDtypeStruct((b, nv, s), BF16)],
        compiler_params=_params("arbitrary", "arbitrary"),
        name="moba_qkv_proj",
    )(x, shift[0], scale[0], g, w_qk, w_vt, head_gain)


def _softmax_step_t(s, m, l, acc, vt, bounded):
    if bounded:
        p = jnp.exp2(s)
        return m, l + jnp.sum(p, axis=0, keepdims=True), acc + _dot(vt, p.astype(BF16))
    m_new = jnp.maximum(m, jnp.max(s, axis=0, keepdims=True))
    alpha = jnp.exp2(m - m_new)
    p = jnp.exp2(s - m_new)
    l_new = alpha * l + jnp.sum(p, axis=0, keepdims=True)
    acc_new = acc * alpha + _dot(vt, p.astype(BF16))
    return m_new, l_new, acc_new


def _moba_attn_kernel(qmin_ref, kmax_ref, qgmin_ref, kbmax_ref, fpairs_ref, q_ref, k_ref, vt_ref, bt_ref, pq_ref, pk_ref, o_ref,
                      km_scr, sel_scr, *, nb, t, npairs, bounded):
    blk = MOBA_BLOCK
    hd = MOBA_HEAD_DIM
    nsub = t // blk
    nh = 2 * npairs
    bi = pl.program_id(0)
    qi = pl.program_id(2)
    lane = lax.broadcasted_iota(jnp.int32, (1, LANES), 1)
    left = lane < hd

    @pl.when(qi == 0)
    def _():
        km_scr[...] = jnp.zeros_like(km_scr)
        for r in range(nb):
            kblk = k_ref[0, r * blk:(r + 1) * blk, :].astype(F32)
            km_scr[r:r + 1, :] = jnp.sum(kblk, axis=0, keepdims=True) * (1.0 / blk)

    q_heads = []
    for p in range(npairs):
        q2 = q_ref[0, :, p * LANES:(p + 1) * LANES]
        zero = jnp.zeros_like(q2)
        q_heads += [jnp.where(left, q2, zero), jnp.where(left, zero, q2)]

    col = lax.broadcasted_iota(jnp.int32, (1, t), 1)
    q_idx = qi * t + col
    q_sub = sum((col >= r * blk).astype(jnp.int32) for r in range(1, nsub))
    q_blk = qi * nsub + q_sub
    q_blk_start = q_blk * blk

    blk_row = lax.broadcasted_iota(jnp.int32, (nb, t), 0)
    for h in range(nh):
        km = km_scr[:, (h // 2) * LANES:(h // 2 + 1) * LANES]
        km_hi = km.astype(BF16)
        km_lo = (km - km_hi.astype(F32)).astype(BF16)
        gate = (_dot_nt(km_hi, q_heads[h]) + _dot_nt(km_lo, q_heads[h]))[:nb]
        gate = jnp.where(blk_row < q_blk, gate, -jnp.inf)
        sel = jnp.zeros((nb, t), F32)
        for _ in range(MOBA_TOPK):
            mx = jnp.max(gate, axis=0, keepdims=True)
            idx = jnp.min(jnp.where(gate == mx, blk_row, nb), axis=0, keepdims=True)
            pick = (blk_row == idx) & (mx > -jnp.inf)
            sel = jnp.where(pick, 1.0, sel)
            gate = jnp.where(pick, -jnp.inf, gate)
        sel_scr[h] = sel

    pq = pq_ref[0, 0]

    def gather_bias(h, dist):
        table = jnp.broadcast_to(bt_ref[h // 2, h % 2:h % 2 + 1, :], (dist.shape[0], LANES))
        return jnp.concatenate(
            [jnp.take_along_axis(table, dist[:, g * LANES:(g + 1) * LANES], axis=1, mode="promise_in_bounds")
             for g in range(dist.shape[1] // LANES)], axis=1)

    def far_bias(h):
        return bt_ref[h // 2, h % 2:h % 2 + 1, MAX_DIST:MAX_DIST + 1]

    def chosen_rows(h, j, lanes=slice(None), width=1):
        return [sel_scr[h, pl.ds(j * nsub + r, 1), lanes] > 0.0 for r in range(width * nsub)]

    def own_chunk_bounded(carry):
        assert nsub == 2
        carry = list(carry)
        start = pl.multiple_of(qi * t, t)
        k2 = k_ref[0, pl.ds(start, t), :]
        vt = vt_ref[0, :, pl.ds(start, t)]
        pk0 = pk_ref[0, pl.ds(start, blk), :]
        pk1 = pk_ref[0, pl.ds(start + blk, blk), :]
        causal = (lax.broadcasted_iota(jnp.int32, (blk, blk), 0) <= lax.broadcasted_iota(jnp.int32, (blk, blk), 1))
        d00 = jnp.where(causal, jnp.clip(pq[:, :blk] - pk0, 0, MAX_DIST), MASKED_DIST)
        d01 = jnp.clip(pq[:, blk:] - pk0, 0, MAX_DIST)
        d11 = jnp.where(causal, jnp.clip(pq[:, blk:] - pk1, 0, MAX_DIST), MASKED_DIST)
        quads = []
        for h in range(nh):
            kh = k2[:, (h // 2) * LANES:(h // 2 + 1) * LANES]
            qh = q_heads[h]
            quads.append((_dot_nt(kh[:blk], qh[:blk]), _dot_nt(kh[:blk], qh[blk:]), _dot_nt(kh[blk:], qh[blk:])))
        for h in range(nh):
            m, l, acc = carry[h]
            s00, s01, s11 = quads[h]
            e00 = jnp.exp2(s00 + gather_bias(h, d00))
            e01 = jnp.exp2(s01 + gather_bias(h, d01))
            e11 = jnp.exp2(s11 + gather_bias(h, d11))
            w01 = jnp.where(chosen_rows(h, qi, slice(blk, t))[0], 1.0, 0.0)
            vt0 = vt[h * hd:(h + 1) * hd, :blk]
            vt1 = vt[h * hd:(h + 1) * hd, blk:]
            l_add = jnp.concatenate(
                [jnp.sum(e00, axis=0, keepdims=True),
                 jnp.sum(e01, axis=0, keepdims=True) * w01 + jnp.sum(e11, axis=0, keepdims=True)], axis=1)
            acc_add = jnp.concatenate(
                [_dot(vt0, e00.astype(BF16)),
                 _dot(vt0, e01.astype(BF16)) * w01 + _dot(vt1, e11.astype(BF16))], axis=1)
            carry[h] = (m, l + l_add, acc + acc_add)
        return tuple(carry)

    def chunk(j, carry, mode, width=1):
        carry = list(carry)
        start = pl.multiple_of(j * t, t)
        k2 = k_ref[0, pl.ds(start, width * t), :]
        vt = vt_ref[0, :, pl.ds(start, width * t)]
        s_heads = [_dot_nt(k2[:, (h // 2) * LANES:(h // 2 + 1) * LANES], q_heads[h]) for h in range(nh)]
        if mode in ("near", "diag"):
            dist = jnp.clip(pq - pk_ref[0, pl.ds(start, t), :], 0, MAX_DIST)
        if mode == "corner":
            dist_c = jnp.clip(pq[:, :LANES] - pk_ref[0, pl.ds(start + (nsub - 1) * blk, blk), :], 0, MAX_DIST)
        if mode == "diag":
            key_idx = start + lax.broadcasted_iota(jnp.int32, (blk, t), 0)
        for h in range(nh):
            m, l, acc = carry[h]
            s = s_heads[h]
            vt_h = vt[h * hd:(h + 1) * hd, :]
            chosen = chosen_rows(h, j, width=width)
            if mode in ("far", "corner") and bounded:
                two_c = jnp.exp2(far_bias(h))
                for r in range(width * nsub):
                    piece = s[r * blk:(r + 1) * blk]
                    w = jnp.where(chosen[r], two_c, 0.0)
                    if mode == "corner" and r == nsub - 1:
                        piece = jnp.concatenate([piece[:, :LANES] + gather_bias(h, dist_c), piece[:, LANES:]], axis=1)
                        w = jnp.where(col < LANES, jnp.where(chosen[r], 1.0, 0.0), w)
                    p = jnp.exp2(piece)
                    l = l + jnp.sum(p, axis=0, keepdims=True) * w
                    acc = acc + _dot(vt_h[:, r * blk:(r + 1) * blk], p.astype(BF16)) * w
                carry[h] = (m, l, acc)
            elif mode == "far":
                pieces = [s[r * blk:(r + 1) * blk] for r in range(nsub)]
                c = far_bias(h)
                tmax = [jnp.where(chosen[r], jnp.max(pieces[r], axis=0, keepdims=True) + c, NEG)
                        for r in range(nsub)]
                m_new = functools.reduce(jnp.maximum, tmax, m)
                alpha = jnp.exp2(m - m_new)
                ps = [jnp.exp2(pieces[r] - jnp.where(chosen[r], m_new - c, jnp.inf)) for r in range(nsub)]
                l_new = alpha * l + sum(jnp.sum(p, axis=0, keepdims=True) for p in ps)
                p = jnp.concatenate(ps, axis=0) if nsub > 1 else ps[0]
                carry[h] = (m_new, l_new, acc * alpha + _dot(vt_h, p.astype(BF16)))
            else:
                s = s + gather_bias(h, dist)
                masked = []
                for r in range(nsub):
                    piece = s[r * blk:(r + 1) * blk]
                    if mode == "diag":
                        kidx = key_idx + r * blk
                        keep_own = jnp.where(kidx <= q_idx, jnp.where(kidx >= q_blk_start, piece, NEG), NEG)
                        masked.append(jnp.where(chosen[r], piece, keep_own))
                    else:
                        masked.append(jnp.where(chosen[r], piece, NEG))
                s = jnp.concatenate(masked, axis=0) if nsub > 1 else masked[0]
                carry[h] = _softmax_step_t(s, m, l, acc, vt_h, bounded)
        return tuple(carry)

    def past_chunk(j, carry):
        q_lo = qmin_ref[bi, qi]
        far = q_lo - kmax_ref[bi, j] >= MAX_DIST
        if not bounded:
            return lax.cond(far, lambda c: chunk(j, c, "far"), lambda c: chunk(j, c, "near"), carry)
        groups = t // LANES
        q_rest = functools.reduce(jnp.minimum, [qgmin_ref[bi, qi * groups + g] for g in range(1, groups)])
        k_first = functools.reduce(jnp.maximum, [kbmax_ref[bi, j * nsub + r] for r in range(nsub - 1)])
        corner = (q_lo - k_first >= MAX_DIST) & (q_rest - kbmax_ref[bi, j * nsub + nsub - 1] >= MAX_DIST)
        return lax.cond(far, lambda c: chunk(j, c, "far"),
                        lambda c: lax.cond(corner, lambda c2: chunk(j, c2, "corner"),
                                           lambda c2: chunk(j, c2, "near"), c), carry)

    init = (jnp.full((1, t), NEG, F32), jnp.zeros((1, t), F32), jnp.zeros((hd, t), F32))
    if bounded:
        carry = own_chunk_bounded((init,) * nh)
        pairs = fpairs_ref[bi, qi]
        carry = lax.fori_loop(0, pairs, lambda p, c: chunk(2 * p, c, "far", width=2), carry)
        carry = lax.fori_loop(2 * pairs, qi, past_chunk, carry)
    else:
        carry = lax.fori_loop(0, qi, past_chunk, (init,) * nh)
        carry = chunk(qi, carry, "diag")
    out_t = jnp.concatenate([acc / l for (_, l, acc) in carry], axis=0)
    o_ref[0] = out_t.T.astype(BF16)


def _moba_position_tables(positions, t):
    b, s = positions.shape
    pos_tiles = positions.reshape(b, s // t, t)
    q_min, k_max = jnp.min(pos_tiles, axis=-1), jnp.max(pos_tiles, axis=-1)
    n = s // t
    far = (q_min[:, :, None] - k_max[:, None, :] >= MAX_DIST) & (jnp.arange(n)[None, :] < jnp.arange(n)[:, None])[None]
    pair_far = far[:, :, 0:2 * (n // 2):2] & far[:, :, 1:2 * (n // 2):2]
    first_near = jnp.where(pair_far, n // 2, jnp.arange(n // 2, dtype=jnp.int32))
    far_pairs = jnp.min(first_near, axis=-1, initial=n // 2)
    return (q_min, k_max,
            jnp.min(positions.reshape(b, s // LANES, LANES), axis=-1),
            jnp.max(positions.reshape(b, s // MOBA_BLOCK, MOBA_BLOCK), axis=-1), far_pairs,
            positions.reshape(b, s // t, 1, t), positions.reshape(b, s, 1))


def _moba_attn(qk, vt, bias_table, pos_tables, t, npairs, bounded):
    b, s, _ = qk.shape
    nb = s // MOBA_BLOCK
    ngroups = MOBA_HEADS * MOBA_HEAD_DIM // (LANES * npairs)
    w = npairs * LANES
    q_min, k_max, qg_min, kb_max, far_pairs, pos_row, pos_col = pos_tables
    grid_spec = pltpu.PrefetchScalarGridSpec(
        num_scalar_prefetch=5,
        grid=(b, ngroups, s // t),
        in_specs=[pl.BlockSpec((1, t, w), lambda bi, g, qi, *_: (bi, qi, g)),
                  pl.BlockSpec((1, s, w), lambda bi, g, qi, *_: (bi, 0, ngroups + g)),
                  pl.BlockSpec((1, w, s), lambda bi, g, qi, *_: (bi, g, 0)),
                  pl.BlockSpec((npairs, 2, LANES), lambda bi, g, qi, *_: (g, 0, 0)),
                  pl.BlockSpec((1, 1, 1, t), lambda bi, g, qi, *_: (bi, qi, 0, 0)),
                  pl.BlockSpec((1, s, 1), lambda bi, g, qi, *_: (bi, 0, 0))],
        out_specs=pl.BlockSpec((1, t, w), lambda bi, g, qi, *_: (bi, qi, g)),
        scratch_shapes=[pltpu.VMEM((LANES, w), F32), pltpu.VMEM((2 * npairs, nb, t), F32)],
    )
    return pl.pallas_call(
        functools.partial(_moba_attn_kernel, nb=nb, t=t, npairs=npairs, bounded=bounded),
        grid_spec=grid_spec,
        out_shape=jax.ShapeDtypeStruct((b, s, MOBA_HEADS * MOBA_HEAD_DIM), BF16),
        compiler_params=_params("arbitrary", "arbitrary", "arbitrary"),
        name="moba_attention",
    )(q_min, k_max, qg_min, kb_max, far_pairs, qk, qk, vt, bias_table, pos_row, pos_col)


def _rope_tab_kernel(pos_ref, inv_ref, c_ref, s_ref):
    half = MLA_ROPE // 2
    ang = pos_ref[0].astype(F32) * inv_ref[...]
    lane = lax.broadcasted_iota(jnp.int32, (1, LANES), 1)
    sn = jnp.sin(ang)
    c_ref[0] = jnp.cos(ang)
    s_ref[0] = jnp.where(lane < MLA_NOPE + half, -sn, sn)


def _rope_tables(pos_col, inv_lane, tm):
    b, s, _ = pos_col.shape
    out = pl.BlockSpec((1, tm, LANES), lambda bi, i: (bi, i, 0))
    shape = jax.ShapeDtypeStruct((b, s, LANES), F32)
    return pl.pallas_call(
        _rope_tab_kernel,
        grid=(b, s // tm),
        in_specs=[pl.BlockSpec((1, tm, 1), lambda bi, i: (bi, i, 0)),
                  pl.BlockSpec((1, LANES), lambda bi, i: (0, 0))],
        out_specs=[out, out],
        out_shape=[shape, shape],
        compiler_params=_params("arbitrary", "arbitrary"),
        name="rope_tables",
    )(pos_col, inv_lane)


def _mla_q_kernel(x_ref, sh_ref, sc_ref, g_ref, wdq_ref, qag_ref, wuq_ref, wuqp_ref, qg_ref, qgp_ref,
                  c_ref, s_ref, o_ref):
    h = _rms_mod(x_ref[0], g_ref[...], _mod_row(sh_ref), _mod_row(sc_ref)).astype(BF16)
    dq = _dot(h, wdq_ref[...])
    dqn = (dq * lax.rsqrt(jnp.mean(dq * dq, axis=-1, keepdims=True) + EPS) * qag_ref[...]).astype(BF16)
    uq = _dot(dqn, wuq_ref[...])
    up = _dot(dqn, wuqp_ref[...])
    cg = c_ref[0] * qg_ref[...]
    sg = s_ref[0] * qgp_ref[...]
    scale = MLA_QK ** -0.5 * LOG2E
    for hd in range(MLA_HEADS):
        sl = slice(hd * LANES, (hd + 1) * LANES)
        u = uq[:, sl]
        inv = lax.rsqrt(jnp.sum(u * u, axis=-1, keepdims=True) * (1.0 / MLA_QK) + EPS) * scale
        o_ref[0, :, sl] = ((u * cg + up[:, sl] * sg) * inv).astype(BF16)


def _mla_q(x, shift, scale, g, wdq, qag, wuq_p, wuq_partner, qg_p, qg_partner, cos, sin, tm):
    b, s, d = x.shape
    r = wdq.shape[1]
    nq = wuq_p.shape[1]
    tab = pl.BlockSpec((1, tm, LANES), lambda bi, i: (bi, i, 0))
    full = lambda shp: pl.BlockSpec(shp, lambda bi, i: (0, 0))
    return pl.pallas_call(
        _mla_q_kernel,
        grid=(b, s // tm),
        in_specs=[pl.BlockSpec((1, tm, d), lambda bi, i: (bi, i, 0)), _mod_spec(shift, d), _mod_spec(scale, d),
                  full((1, d)),
                  full((d, r)), full((1, r)), full((r, nq)), full((r, nq)), full((1, LANES)), full((1, LANES)),
                  tab, tab],
        out_specs=pl.BlockSpec((1, tm, nq), lambda bi, i: (bi, i, 0)),
        out_shape=jax.ShapeDtypeStruct((b, s, nq), BF16),
        compiler_params=_params("arbitrary", "arbitrary"),
        name="mla_q_path",
    )(x, shift[0], scale[0], g, wdq, qag, wuq_p, wuq_partner, qg_p, qg_partner, cos, sin)


def _mla_qkv_kernel(x_ref, qsh_ref, qsc_ref, qn_ref, wdq_ref, qag_ref, wuq_ref, wuqp_ref, qg_ref, qgp_ref,
                    ksh_ref, ksc_ref, kn_ref, wdkv_ref, kvg_ref, wuk_ref, wuvt_ref, kg_ref, kgp_ref,
                    c_ref, s_ref, q_ref, k_ref, vt_ref):
    x = x_ref[0]
    xn = x * lax.rsqrt(jnp.mean(x * x, axis=-1, keepdims=True) + EPS)
    cos, sin = c_ref[0], s_ref[0]
    h = ((xn * qn_ref[...]) * (1.0 + _mod_row(qsc_ref)) + _mod_row(qsh_ref)).astype(BF16)
    dq = _dot(h, wdq_ref[...])
    dqn = (dq * lax.rsqrt(jnp.mean(dq * dq, axis=-1, keepdims=True) + EPS) * qag_ref[...]).astype(BF16)
    uq = _dot(dqn, wuq_ref[...])
    up = _dot(dqn, wuqp_ref[...])
    hs = ((xn * kn_ref[...]) * (1.0 + _mod_row(ksc_ref)) + _mod_row(ksh_ref)).astype(BF16)
    ck = _dot(hs, wdkv_ref[...])
    cg = cos * qg_ref[...]
    sg = sin * qgp_ref[...]
    scale = MLA_QK ** -0.5 * LOG2E

    def q_heads(lo, hi):
        for hd in range(lo, hi):
            sl = slice(hd * LANES, (hd + 1) * LANES)
            u = uq[:, sl]
            inv = lax.rsqrt(jnp.sum(u * u, axis=-1, keepdims=True) * (1.0 / MLA_QK) + EPS) * scale
            q_ref[0, :, sl] = ((u * cg + up[:, sl] * sg) * inv).astype(BF16)

    q_heads(0, MLA_HEADS // 2)
    ckv = ck[:, :MLA_KV_LORA]
    ckvn = (ckv * lax.rsqrt(jnp.mean(ckv * ckv, axis=-1, keepdims=True) + EPS) * kvg_ref[...]).astype(BF16)
    kn = _dot(ckvn, wuk_ref[...])
    vt_ref[0] = _dot_nt(wuvt_ref[...], ckvn).astype(BF16)
    q_heads(MLA_HEADS // 2, MLA_HEADS)
    lane = lax.broadcasted_iota(jnp.int32, (1, LANES), 1)
    on_rope = (lane >= MLA_NOPE) & (lane < MLA_QK)
    tail = ck[:, MLA_KV_LORA:MLA_KV_LORA + LANES]
    kpe = jnp.where(on_rope, pltpu.roll(tail, MLA_NOPE, axis=1), 0.0)
    partner = pltpu.roll(tail, MLA_ROPE, axis=1)
    kcg = cos * kg_ref[...]
    ksg = sin * kgp_ref[...]
    rot = kpe * kcg + partner * ksg
    ss_pe = jnp.sum(kpe * kpe, axis=-1, keepdims=True)
    for hd in range(MLA_HEADS):
        sl = slice(hd * LANES, (hd + 1) * LANES)
        nope = kn[:, sl]
        inv = lax.rsqrt((jnp.sum(nope * nope, axis=-1, keepdims=True) + ss_pe) * (1.0 / MLA_QK) + EPS)
        k_ref[0, :, sl] = ((nope * kcg + rot) * inv).astype(BF16)


def _mla_qkv(x, q_args, kv_args, cos, sin, tm):
    b, s, d = x.shape
    nq = q_args[5].shape[1]
    nk = kv_args[5].shape[1]
    nv = kv_args[6].shape[0]
    tab = pl.BlockSpec((1, tm, LANES), lambda bi, i: (bi, i, 0))
    full = lambda a: pl.BlockSpec(a.shape, lambda bi, i: (0, 0))

    def path(args):
        shift, scale = args[:2]
        return [shift[0], scale[0], *args[2:]], [_mod_spec(shift, d), _mod_spec(scale, d)] + [full(a) for a in args[2:]]

    q_ins, q_specs = path(q_args)
    kv_ins, kv_specs = path(kv_args)
    return pl.pallas_call(
        _mla_qkv_kernel,
        grid=(b, s // tm),
        in_specs=[pl.BlockSpec((1, tm, d), lambda bi, i: (bi, i, 0))] + q_specs + kv_specs + [tab, tab],
        out_specs=[pl.BlockSpec((1, tm, nq), lambda bi, i: (bi, i, 0)),
                   pl.BlockSpec((1, tm, nk), lambda bi, i: (bi, i, 0)),
                   pl.BlockSpec((1, nv, tm), lambda bi, i: (bi, 0, i))],
        out_shape=[jax.ShapeDtypeStruct((b, s, nq), BF16), jax.ShapeDtypeStruct((b, s, nk), BF16),
                   jax.ShapeDtypeStruct((b, nv, s), BF16)],
        compiler_params=_params("arbitrary", "arbitrary"),
        name="mla_qkv_proj",
    )(x, *q_ins, *kv_ins, cos, sin)


def _mla_attn_kernel(q_ref, k_ref, vt_ref, o_ref, *, t, nh, bounded):
    qi = pl.program_id(2)
    q2 = q_ref[0]
    q_heads = [q2[:, h * LANES:(h + 1) * LANES] for h in range(nh)]

    def chunk(j, carry, masked):
        carry = list(carry)
        start = pl.multiple_of(j * t, t)
        k2 = k_ref[0, pl.ds(start, t), :]
        vt = vt_ref[0, :, pl.ds(start, t)]
        s_heads = [_dot_nt(k2[:, h * LANES:(h + 1) * LANES], q_heads[h]) for h in range(nh)]
        if masked:
            row = lax.broadcasted_iota(jnp.int32, (t, t), 0)
            col = lax.broadcasted_iota(jnp.int32, (t, t), 1)
            allowed = row <= col
        for h in range(nh):
            m, l, acc = carry[h]
            s = jnp.where(allowed, s_heads[h], NEG) if masked else s_heads[h]
            carry[h] = _softmax_step_t(s, m, l, acc, vt[h * MLA_V:(h + 1) * MLA_V, :], bounded)
        return tuple(carry)

    def own_chunk_bounded(carry):
        carry = list(carry)
        half = t // 2
        start = pl.multiple_of(qi * t, t)
        k2 = k_ref[0, pl.ds(start, t), :]
        vt = vt_ref[0, :, pl.ds(start, t)]
        causal = (lax.broadcasted_iota(jnp.int32, (half, half), 0) <= lax.broadcasted_iota(jnp.int32, (half, half), 1))
        quads = []
        for h in range(nh):
            kh = k2[:, h * LANES:(h + 1) * LANES]
            qh = q_heads[h]
            quads.append((_dot_nt(kh[:half], qh[:half]), _dot_nt(kh[:half], qh[half:]), _dot_nt(kh[half:], qh[half:])))
        for h in range(nh):
            m, l, acc = carry[h]
            s00, s01, s11 = quads[h]
            e00 = jnp.exp2(jnp.where(causal, s00, NEG))
            e01 = jnp.exp2(s01)
            e11 = jnp.exp2(jnp.where(causal, s11, NEG))
            vt0 = vt[h * MLA_V:(h + 1) * MLA_V, :half]
            vt1 = vt[h * MLA_V:(h + 1) * MLA_V, half:]
            l_add = jnp.concatenate(
                [jnp.sum(e00, axis=0, keepdims=True),
                 jnp.sum(e01, axis=0, keepdims=True) + jnp.sum(e11, axis=0, keepdims=True)], axis=1)
            acc_add = jnp.concatenate(
                [_dot(vt0, e00.astype(BF16)), _dot(vt0, e01.astype(BF16)) + _dot(vt1, e11.astype(BF16))], axis=1)
            carry[h] = (m, l + l_add, acc + acc_add)
        return tuple(carry)

    init = (jnp.full((1, t), NEG, F32), jnp.zeros((1, t), F32), jnp.zeros((MLA_V, t), F32))
    carry = lax.fori_loop(0, qi, lambda j, c: chunk(j, c, False), (init,) * nh)
    carry = own_chunk_bounded(carry) if bounded else chunk(qi, carry, True)
    out_t = jnp.concatenate([acc / l for (_, l, acc) in carry], axis=0)
    o_ref[0] = out_t.T.astype(BF16)


def _mla_attn(q, k, vt, t, nh, bounded):
    b, s, _ = q.shape
    return pl.pallas_call(
        functools.partial(_mla_attn_kernel, t=t, nh=nh, bounded=bounded),
        grid=(b, MLA_HEADS // nh, s // t),
        in_specs=[pl.BlockSpec((1, t, nh * LANES), lambda bi, g, qi: (bi, qi, g)),
                  pl.BlockSpec((1, s, nh * LANES), lambda bi, g, qi: (bi, 0, g)),
                  pl.BlockSpec((1, nh * MLA_V, s), lambda bi, g, qi: (bi, g, 0))],
        out_specs=pl.BlockSpec((1, t, nh * MLA_V), lambda bi, g, qi: (bi, qi, g)),
        out_shape=jax.ShapeDtypeStruct((b, s, MLA_HEADS * MLA_V), BF16),
        compiler_params=_params("arbitrary", "arbitrary", "arbitrary"),
        name="mla_attention",
    )(q, k, vt)


def _t5_bias_table(rel_bias):
    n = jnp.arange(MASKED_DIST)
    max_exact = REL_BUCKETS // 2
    nf = jnp.maximum(n, 1).astype(F32)
    large = max_exact + (jnp.log(nf / max_exact) / math.log(REL_MAX_DIST / max_exact)
                         * (REL_BUCKETS - max_exact)).astype(jnp.int32)
    large = jnp.minimum(large, REL_BUCKETS - 1)
    bucket = jnp.where(n < max_exact, n, large)
    table = jnp.concatenate([rel_bias.T[:, bucket] * LOG2E, jnp.full((MOBA_HEADS, 1), NEG, F32)], axis=1)
    return table.reshape(MOBA_HEADS // 2, 2, BIAS_TABLE)


def _pad_heads(w, width):
    k = w.shape[0]
    w3 = w.reshape(k, -1, width)
    return jnp.pad(w3, ((0, 0), (0, 0), (0, LANES - width))).reshape(k, -1)


def _pad_lanes(g):
    return jnp.pad(g, (0, LANES - g.shape[0])).reshape(1, LANES)


def _swap_rope_halves(a):
    half = MLA_ROPE // 2
    return jnp.concatenate([jnp.zeros_like(a[..., :MLA_NOPE]), a[..., MLA_NOPE + half:],
                            a[..., MLA_NOPE:MLA_NOPE + half]], axis=-1)


def _scores_bounded(q_gain, k_gain, dim, scale, extra=0.0):
    bound = dim * jnp.max(jnp.abs(q_gain)) * jnp.max(jnp.abs(k_gain)) * scale * BOUND_SLACK + extra
    return bound <= SCORE_BOUND_MAX


def kernel(x, c, positions, ada_w, ada_b, norm_g, ffn_w_gate, ffn_w_up, ffn_w_down, rel_bias,
           moba_w_qkv, moba_q_g, moba_k_g, moba_w_o, kv_ada_w, kv_ada_b, kv_norm_g, w_dkv,
           kv_a_norm_g, w_uk, w_uv, mla_k_g, mla_w_dq, mla_q_a_norm_g, mla_w_uq, mla_q_g, mla_w_o):
    b, s, d = x.shape
    depth = ada_w.shape[0]
    n_moba = moba_w_qkv.shape[0]
    assert b <= SUBLANES and s % max(FFN_ROWS, MOBA_TILE, MLA_TILE) == 0 and MOBA_TILE == 2 * MOBA_BLOCK
    tm_ffn = FFN_ROWS
    tm_proj = PROJ_ROWS

    c_pad = jnp.pad(c, ((0, SUBLANES - b), (0, 0)))
    mods = _mods(c_pad, ada_w, ada_b.reshape(depth, 1, -1), tn=ADALN_COLS)
    kv_mods = _mods(c_pad, kv_ada_w[None], kv_ada_b.reshape(1, 1, -1), tn=d)

    w_gate, w_up, w_down = (w.astype(BF16) for w in (ffn_w_gate, ffn_w_up, ffn_w_down))

    for layer in range(depth):
        sh1, sc1, g1, sh2, sc2, g2, sh3, sc3, g3 = [(mods, layer, k) for k in range(mods.shape[-1] // d)]
        ng = norm_g[layer]

        def ffn(xin, idx, sh, sc, gt, mixer=None):
            return _ffn(xin, sh, sc, gt, ng[2 * idx:2 * idx + 1], w_gate, w_up, w_down, (layer, idx),
                        tm_ffn, FFN_CHUNK, mixer)

        x = ffn(x, 0, sh1, sc1, g1)

        if layer < n_moba:
            n_qk = 2 * MOBA_HEADS * MOBA_HEAD_DIM
            q_scale = MOBA_HEAD_DIM ** -0.5 * LOG2E
            head_gain = jnp.concatenate([jnp.tile(moba_q_g[layer] * q_scale, MOBA_HEADS),
                                         jnp.tile(moba_k_g[layer], MOBA_HEADS)]).reshape(1, -1)
            w_qkv = moba_w_qkv[layer].astype(BF16)
            qk, vt = _moba_proj(x, sh2, sc2, ng[1:2], w_qkv[:, :n_qk], w_qkv[:, n_qk:].T, head_gain,
                                2 * tm_proj, PROJ_COLS)
            bias_table = _t5_bias_table(rel_bias)
            moba = functools.partial(_moba_attn, qk, vt, bias_table, _moba_position_tables(positions, MOBA_TILE),
                                     MOBA_TILE, ATTN_HEADS // 2)
            ok = _scores_bounded(moba_q_g[layer], moba_k_g[layer], MOBA_HEAD_DIM, q_scale,
                                 jnp.max(jnp.abs(rel_bias)) * LOG2E)
            o = lax.cond(ok, lambda: moba(True), lambda: moba(False))
            w_o = moba_w_o[layer]
        else:
            j = layer - n_moba
            if j == 0:
                half = MLA_ROPE // 2
                inv = ROPE_BASE ** (-jnp.arange(half, dtype=F32) / half)
                inv_lane = jnp.zeros((LANES,), F32).at[MLA_NOPE:MLA_QK].set(jnp.tile(inv, 2)).reshape(1, LANES)
                cos, sin = _rope_tables(positions.reshape(b, s, 1), inv_lane, tm_proj)
                kv_sh, kv_sc = (kv_mods, 0, 0), (kv_mods, 0, 1)
                w_dkv16 = w_dkv.astype(BF16)
                w_kpe = w_dkv16[:, MLA_KV_LORA:]
                w_kpe_swapped = jnp.concatenate([w_kpe[:, half:], w_kpe[:, :half]], axis=1)
                wdkv_p = jnp.pad(jnp.concatenate([w_dkv16, w_kpe_swapped], axis=1),
                                 ((0, 0), (0, LANES - 2 * MLA_ROPE)))
                kv_args = (kv_sh, kv_sc, kv_norm_g.reshape(1, d), wdkv_p, kv_a_norm_g.reshape(1, -1),
                           _pad_heads(w_uk.astype(BF16), MLA_NOPE), w_uv.astype(BF16).T, _pad_lanes(mla_k_g),
                           _pad_lanes(_swap_rope_halves(mla_k_g)))
            w_uq = mla_w_uq[j].astype(BF16)
            w_uq_partner = _swap_rope_halves(w_uq.reshape(w_uq.shape[0], MLA_HEADS, MLA_QK)).reshape(w_uq.shape)
            q_args = (sh2, sc2, ng[1:2], mla_w_dq[j].astype(BF16), mla_q_a_norm_g[j].reshape(1, -1),
                      _pad_heads(w_uq, MLA_QK), _pad_heads(w_uq_partner, MLA_QK),
                      _pad_lanes(mla_q_g[j]), _pad_lanes(_swap_rope_halves(mla_q_g[j])))
            if j == 0:
                q, shared_k, shared_v = _mla_qkv(x, q_args, kv_args, cos, sin, tm_proj)
            else:
                q = _mla_q(x, *q_args, cos, sin, tm_proj)
            mla = functools.partial(_mla_attn, q, shared_k, shared_v, MLA_TILE, ATTN_HEADS)
            ok = _scores_bounded(mla_q_g[j], mla_k_g, MLA_QK, MLA_QK ** -0.5 * LOG2E)
            o = lax.cond(ok, lambda: mla(True), lambda: mla(False))
            w_o = mla_w_o[j]

        x = ffn(x, 1, sh3, sc3, g3, mixer=(o, g2, w_o.astype(BF16)))
    return x
```

```python
import functools
import math

import jax
import jax.numpy as jnp
from jax import lax
from jax.experimental import pallas as pl
from jax.experimental.pallas import tpu as pltpu

F32 = jnp.float32
BF16 = jnp.bfloat16

LANES = 128
SUBLANES = 8
VMEM_LIMIT_BYTES = 56 * 1024 * 1024

MOBA_HEADS = 16
MOBA_HEAD_DIM = 64
MOBA_BLOCK = 256
MOBA_TOPK = 3
REL_BUCKETS = 32
REL_MAX_DIST = 128
MLA_HEADS = 16
MLA_KV_LORA = 256
MLA_NOPE = 64
MLA_ROPE = 32
MLA_V = 64
MLA_QK = MLA_NOPE + MLA_ROPE
ROPE_BASE = 10000.0
EPS = 1e-6
NEG = -1e30

FFN_ROWS = 1024
FFN_CHUNK = 256
PROJ_ROWS = 512
PROJ_COLS = 512
ADALN_COLS = 2304
MOBA_TILE = 512
MLA_TILE = 512
ATTN_HEADS = 8
BOUND_SLACK = 1.01
LOG2E = math.log2(math.e)
BIAS_TABLE = LANES
MAX_DIST = BIAS_TABLE - 2
MASKED_DIST = BIAS_TABLE - 1
SCORE_BOUND_MAX = 60.0


def _params(*sem):
    return pltpu.CompilerParams(dimension_semantics=sem, vmem_limit_bytes=VMEM_LIMIT_BYTES)


def _dot(a, b):
    return jnp.dot(a, b, preferred_element_type=F32)


def _dot_nt(a, b):
    return lax.dot_general(a, b, (((1,), (1,)), ((), ())), preferred_element_type=F32)


def _rms_mod(x, g, shift, scale):
    ms = jnp.mean(x * x, axis=-1, keepdims=True)
    y = x * lax.rsqrt(ms + EPS) * g
    return y * (1.0 + scale) + shift


def _mod_spec(mod, d):
    _, layer, k = mod
    return pl.BlockSpec((None, SUBLANES, d), lambda bi, *_: (layer, 0, k))


def _mod_row(ref):
    return ref[pl.ds(pl.program_id(0), 1), :]


def _mods_kernel(c_ref, w_ref, b_ref, o_ref):
    c = c_ref[...]
    ca = c * jax.nn.sigmoid(c)
    w = w_ref[0]
    w_hi = w.astype(BF16)
    w_lo = (w - w_hi.astype(F32)).astype(BF16)
    ca_hi = ca.astype(BF16)
    ca_lo = (ca - ca_hi.astype(F32)).astype(BF16)
    o_ref[0] = _dot(ca_hi, w_hi) + _dot(ca_lo, w_hi) + _dot(ca_hi, w_lo) + b_ref[0]


def _mods(c_pad, w, b, tn):
    nl, d, n = w.shape
    return pl.pallas_call(
        _mods_kernel,
        grid=(nl, n // tn),
        in_specs=[pl.BlockSpec((SUBLANES, d), lambda l, j: (0, 0)),
                  pl.BlockSpec((1, d, tn), lambda l, j: (l, 0, j)),
                  pl.BlockSpec((1, 1, tn), lambda l, j: (l, 0, j))],
        out_specs=pl.BlockSpec((1, SUBLANES, tn), lambda l, j: (l, 0, j)),
        out_shape=jax.ShapeDtypeStruct((nl, SUBLANES, n), F32),
        compiler_params=_params("arbitrary", "arbitrary"),
        name="adaln_mods",
    )(c_pad, w, b)


def _ffn_kernel(*refs, tf, with_mixer):
    if with_mixer:
        x_ref, a_ref, gm_ref, wo_ref = refs[:4]
        refs = refs[4:]
        x = x_ref[0] + _mod_row(gm_ref) * _dot(a_ref[0], wo_ref[...])
    else:
        x_ref = refs[0]
        refs = refs[1:]
        x = x_ref[0]
    sh_ref, sc_ref, gt_ref, g_ref, wg_ref, wu_ref, wd_ref, o_ref = refs
    h = _rms_mod(x, g_ref[...], _mod_row(sh_ref), _mod_row(sc_ref)).astype(BF16)
    acc = None
    for f in range(wg_ref.shape[1] // tf):
        cols = slice(f * tf, (f + 1) * tf)
        g = _dot(h, wg_ref[:, cols])
        u = _dot(h, wu_ref[:, cols])
        a = (g * jax.nn.sigmoid(g) * u).astype(BF16)
        d = _dot(a, wd_ref[cols, :])
        acc = d if acc is None else acc + d
    o_ref[0] = x + (0.5 * _mod_row(gt_ref)) * acc


def _ffn(x, shift, scale, gate, g, wg, wu, wd, which, tm, tf, mixer=None):
    b, s, d = x.shape
    f = wg.shape[-1]
    row = pl.BlockSpec((1, tm, d), lambda bi, i: (bi, i, 0))
    full = lambda shp: pl.BlockSpec(shp, lambda bi, i: (0, 0), pipeline_mode=pl.Buffered(1))
    picked = lambda r, c: pl.BlockSpec((None, None, r, c), lambda bi, i: (*which, 0, 0),
                                       pipeline_mode=pl.Buffered(1))
    ins, specs = [x], [row]
    if mixer is not None:
        a, gate_mix, w_o = mixer
        k = a.shape[2]
        ins += [a, gate_mix[0], w_o]
        specs += [pl.BlockSpec((1, tm, k), lambda bi, i: (bi, i, 0)), _mod_spec(gate_mix, d), full((k, d))]
    ins += [shift[0], scale[0], gate[0], g, wg, wu, wd]
    specs += [_mod_spec(shift, d), _mod_spec(scale, d), _mod_spec(gate, d), full((1, d)),
              picked(d, f), picked(d, f), picked(f, d)]
    return pl.pallas_call(
        functools.partial(_ffn_kernel, tf=tf, with_mixer=mixer is not None),
        grid=(b, s // tm),
        in_specs=specs,
        out_specs=row,
        out_shape=jax.ShapeDtypeStruct((b, s, d), F32),
        compiler_params=_params("arbitrary", "arbitrary"),
        name="ffn_swiglu",
    )(*ins)


def _headnorm_pair(y, g, dim):
    lane = lax.broadcasted_iota(jnp.int32, (1, LANES), 1)
    left = lane < dim
    y2 = y * y
    ss_a = jnp.sum(jnp.where(left, y2, 0.0), axis=-1, keepdims=True)
    ss_b = jnp.sum(jnp.where(left, 0.0, y2), axis=-1, keepdims=True)
    inv = jnp.where(left, lax.rsqrt(ss_a * (1.0 / dim) + EPS), lax.rsqrt(ss_b * (1.0 / dim) + EPS))
    return y * inv * g


def _moba_proj_kernel(x_ref, sh_ref, sc_ref, g_ref, w_ref, wvt_ref, hg_ref, qk_ref, vt_ref, *, tn):
    h = _rms_mod(x_ref[0], g_ref[...], _mod_row(sh_ref), _mod_row(sc_ref)).astype(BF16)
    for n in range(w_ref.shape[1] // tn):
        res = _dot(h, w_ref[:, n * tn:(n + 1) * tn])
        for s in range(tn // LANES):
            sl = slice(n * tn + s * LANES, n * tn + (s + 1) * LANES)
            qk_ref[0, :, sl] = _headnorm_pair(res[:, s * LANES:(s + 1) * LANES], hg_ref[:, sl],
                                              MOBA_HEAD_DIM).astype(BF16)
    for n in range(wvt_ref.shape[0] // tn):
        rows = slice(n * tn, (n + 1) * tn)
        vt_ref[0, rows, :] = _dot_nt(wvt_ref[rows, :], h).astype(BF16)


def _moba_proj(x, shift, scale, g, w_qk, w_vt, head_gain, tm, tn):
    b, s, d = x.shape
    nqk = w_qk.shape[1]
    nv = w_vt.shape[0]
    full = lambda shp: pl.BlockSpec(shp, lambda bi, i: (0, 0))
    return pl.pallas_call(
        functools.partial(_moba_proj_kernel, tn=tn),
        grid=(b, s // tm),
        in_specs=[pl.BlockSpec((1, tm, d), lambda bi, i: (bi, i, 0)), _mod_spec(shift, d), _mod_spec(scale, d),
                  full((1, d)),
                  full((d, nqk)), full((nv, d)), full((1, nqk))],
        out_specs=[pl.BlockSpec((1, tm, nqk), lambda bi, i: (bi, i, 0)),
                   pl.BlockSpec((1, nv, tm), lambda bi, i: (bi, 0, i))],
        out_shape=[jax.ShapeDtypeStruct((b, s, nqk), BF16), jax.ShapeDtypeStruct((b, nv, s), BF16)],
        compiler_params=_params("arbitrary", "arbitrary"),
        name="moba_qkv_proj",
    )(x, shift[0], scale[0], g, w_qk, w_vt, head_gain)


def _softmax_step_t(s, m, l, acc, vt, bounded):
    if bounded:
        p = jnp.exp2(s)
        return m, l + jnp.sum(p, axis=0, keepdims=True), acc + _dot(vt, p.astype(BF16))
    m_new = jnp.maximum(m, jnp.max(s, axis=0, keepdims=True))
    alpha = jnp.exp2(m - m_new)
    p = jnp.exp2(s - m_new)
    l_new = alpha * l + jnp.sum(p, axis=0, keepdims=True)
    acc_new = acc * alpha + _dot(vt, p.astype(BF16))
    return m_new, l_new, acc_new


def _moba_attn_kernel(qmin_ref, kmax_ref, qgmin_ref, kbmax_ref, fpairs_ref, q_ref, k_ref, vt_ref, bt_ref, pq_ref, pk_ref, o_ref,
                      km_scr, sel_scr, *, nb, t, npairs, bounded):
    blk = MOBA_BLOCK
    hd = MOBA_HEAD_DIM
    nsub = t // blk
    nh = 2 * npairs
    bi = pl.program_id(0)
    qi = pl.program_id(2)
    lane = lax.broadcasted_iota(jnp.int32, (1, LANES), 1)
    left = lane < hd

    @pl.when(qi == 0)
    def _():
        km_scr[...] = jnp.zeros_like(km_scr)
        for r in range(nb):
            kblk = k_ref[0, r * blk:(r + 1) * blk, :].astype(F32)
            km_scr[r:r + 1, :] = jnp.sum(kblk, axis=0, keepdims=True) * (1.0 / blk)

    q_heads = []
    for p in range(npairs):
        q2 = q_ref[0, :, p * LANES:(p + 1) * LANES]
        zero = jnp.zeros_like(q2)
        q_heads += [jnp.where(left, q2, zero), jnp.where(left, zero, q2)]

    col = lax.broadcasted_iota(jnp.int32, (1, t), 1)
    q_idx = qi * t + col
    q_sub = sum((col >= r * blk).astype(jnp.int32) for r in range(1, nsub))
    q_blk = qi * nsub + q_sub
    q_blk_start = q_blk * blk

    blk_row = lax.broadcasted_iota(jnp.int32, (nb, t), 0)
    for h in range(nh):
        km = km_scr[:, (h // 2) * LANES:(h // 2 + 1) * LANES]
        km_hi = km.astype(BF16)
        km_lo = (km - km_hi.astype(F32)).astype(BF16)
        gate = (_dot_nt(km_hi, q_heads[h]) + _dot_nt(km_lo, q_heads[h]))[:nb]
        gate = jnp.where(blk_row < q_blk, gate, -jnp.inf)
        sel = jnp.zeros((nb, t), F32)
        for _ in range(MOBA_TOPK):
            mx = jnp.max(gate, axis=0, keepdims=True)
            idx = jnp.min(jnp.where(gate == mx, blk_row, nb), axis=0, keepdims=True)
            pick = (blk_row == idx) & (mx > -jnp.inf)
            sel = jnp.where(pick, 1.0, sel)
            gate = jnp.where(pick, -jnp.inf, gate)
        sel_scr[h] = sel

    pq = pq_ref[0, 0]

    def gather_bias(h, dist):
        table = jnp.broadcast_to(bt_ref[h // 2, h % 2:h % 2 + 1, :], (dist.shape[0], LANES))
        return jnp.concatenate(
            [jnp.take_along_axis(table, dist[:, g * LANES:(g + 1) * LANES], axis=1, mode="promise_in_bounds")
             for g in range(dist.shape[1] // LANES)], axis=1)

    def far_bias(h):
        return bt_ref[h // 2, h % 2:h % 2 + 1, MAX_DIST:MAX_DIST + 1]

    def chosen_rows(h, j, lanes=slice(None), width=1):
        return [sel_scr[h, pl.ds(j * nsub + r, 1), lanes] > 0.0 for r in range(width * nsub)]

    def own_chunk_bounded(carry):
        assert nsub == 2
        carry = list(carry)
        start = pl.multiple_of(qi * t, t)
        k2 = k_ref[0, pl.ds(start, t), :]
        vt = vt_ref[0, :, pl.ds(start, t)]
        pk0 = pk_ref[0, pl.ds(start, blk), :]
        pk1 = pk_ref[0, pl.ds(start + blk, blk), :]
        causal = (lax.broadcasted_iota(jnp.int32, (blk, blk), 0) <= lax.broadcasted_iota(jnp.int32, (blk, blk), 1))
        d00 = jnp.where(causal, jnp.clip(pq[:, :blk] - pk0, 0, MAX_DIST), MASKED_DIST)
        d01 = jnp.clip(pq[:, blk:] - pk0, 0, MAX_DIST)
        d11 = jnp.where(causal, jnp.clip(pq[:, blk:] - pk1, 0, MAX_DIST), MASKED_DIST)
        quads = []
        for h in range(nh):
            kh = k2[:, (h // 2) * LANES:(h // 2 + 1) * LANES]
            qh = q_heads[h]
            quads.append((_dot_nt(kh[:blk], qh[:blk]), _dot_nt(kh[:blk], qh[blk:]), _dot_nt(kh[blk:], qh[blk:])))
        for h in range(nh):
            m, l, acc = carry[h]
            s00, s01, s11 = quads[h]
            e00 = jnp.exp2(s00 + gather_bias(h, d00))
            e01 = jnp.exp2(s01 + gather_bias(h, d01))
            e11 = jnp.exp2(s11 + gather_bias(h, d11))
            w01 = jnp.where(chosen_rows(h, qi, slice(blk, t))[0], 1.0, 0.0)
            vt0 = vt[h * hd:(h + 1) * hd, :blk]
            vt1 = vt[h * hd:(h + 1) * hd, blk:]
            l_add = jnp.concatenate(
                [jnp.sum(e00, axis=0, keepdims=True),
                 jnp.sum(e01, axis=0, keepdims=True) * w01 + jnp.sum(e11, axis=0, keepdims=True)], axis=1)
            acc_add = jnp.concatenate(
                [_dot(vt0, e00.astype(BF16)),
                 _dot(vt0, e01.astype(BF16)) * w01 + _dot(vt1, e11.astype(BF16))], axis=1)
            carry[h] = (m, l + l_add, acc + acc_add)
        return tuple(carry)

    def chunk(j, carry, mode, width=1):
        last = width * nsub - 1
        carry = list(carry)
        start = pl.multiple_of(j * t, t)
        k2 = k_ref[0, pl.ds(start, width * t), :]
        vt = vt_ref[0, :, pl.ds(start, width * t)]
        s_heads = [_dot_nt(k2[:, (h // 2) * LANES:(h // 2 + 1) * LANES], q_heads[h]) for h in range(nh)]
        if mode in ("near", "diag"):
            dist = jnp.clip(pq - pk_ref[0, pl.ds(start, t), :], 0, MAX_DIST)
        if mode == "corner":
            dist_c = jnp.clip(pq[:, :LANES] - pk_ref[0, pl.ds(start + last * blk, blk), :], 0, MAX_DIST)
        if mode == "diag":
            key_idx = start + lax.broadcasted_iota(jnp.int32, (blk, t), 0)
        for h in range(nh):
            m, l, acc = carry[h]
            s = s_heads[h]
            vt_h = vt[h * hd:(h + 1) * hd, :]
            chosen = chosen_rows(h, j, width=width)
            if mode in ("far", "corner") and bounded:
                two_c = jnp.exp2(far_bias(h))
                for r in range(width * nsub):
                    piece = s[r * blk:(r + 1) * blk]
                    w = jnp.where(chosen[r], two_c, 0.0)
                    if mode == "corner" and r == last:
                        piece = jnp.concatenate([piece[:, :LANES] + gather_bias(h, dist_c), piece[:, LANES:]], axis=1)
                        w = jnp.where(col < LANES, jnp.where(chosen[r], 1.0, 0.0), w)
                    p = jnp.exp2(piece)
                    l = l + jnp.sum(p, axis=0, keepdims=True) * w
                    acc = acc + _dot(vt_h[:, r * blk:(r + 1) * blk], p.astype(BF16)) * w
                carry[h] = (m, l, acc)
            elif mode == "far":
                pieces = [s[r * blk:(r + 1) * blk] for r in range(nsub)]
                c = far_bias(h)
                tmax = [jnp.where(chosen[r], jnp.max(pieces[r], axis=0, keepdims=True) + c, NEG)
                        for r in range(nsub)]
                m_new = functools.reduce(jnp.maximum, tmax, m)
                alpha = jnp.exp2(m - m_new)
                ps = [jnp.exp2(pieces[r] - jnp.where(chosen[r], m_new - c, jnp.inf)) for r in range(nsub)]
                l_new = alpha * l + sum(jnp.sum(p, axis=0, keepdims=True) for p in ps)
                p = jnp.concatenate(ps, axis=0) if nsub > 1 else ps[0]
                carry[h] = (m_new, l_new, acc * alpha + _dot(vt_h, p.astype(BF16)))
            else:
                s = s + gather_bias(h, dist)
                masked = []
                for r in range(nsub):
                    piece = s[r * blk:(r + 1) * blk]
                    if mode == "diag":
                        kidx = key_idx + r * blk
                        keep_own = jnp.where(kidx <= q_idx, jnp.where(kidx >= q_blk_start, piece, NEG), NEG)
                        masked.append(jnp.where(chosen[r], piece, keep_own))
                    else:
                        masked.append(jnp.where(chosen[r], piece, NEG))
                s = jnp.concatenate(masked, axis=0) if nsub > 1 else masked[0]
                carry[h] = _softmax_step_t(s, m, l, acc, vt_h, bounded)
        return tuple(carry)

    def past_chunk(j, carry):
        q_lo = qmin_ref[bi, qi]
        far = q_lo - kmax_ref[bi, j] >= MAX_DIST
        if not bounded:
            return lax.cond(far, lambda c: chunk(j, c, "far"), lambda c: chunk(j, c, "near"), carry)
        groups = t // LANES
        q_rest = functools.reduce(jnp.minimum, [qgmin_ref[bi, qi * groups + g] for g in range(1, groups)])
        k_first = functools.reduce(jnp.maximum, [kbmax_ref[bi, j * nsub + r] for r in range(nsub - 1)])
        corner = (q_lo - k_first >= MAX_DIST) & (q_rest - kbmax_ref[bi, j * nsub + nsub - 1] >= MAX_DIST)
        return lax.cond(far, lambda c: chunk(j, c, "far"),
                        lambda c: lax.cond(corner, lambda c2: chunk(j, c2, "corner"),
                                           lambda c2: chunk(j, c2, "near"), c), carry)

    init = (jnp.full((1, t), NEG, F32), jnp.zeros((1, t), F32), jnp.zeros((hd, t), F32))
    if bounded:
        carry = own_chunk_bounded((init,) * nh)
        pairs = fpairs_ref[bi, qi]
        singles = qi - 2 * pairs
        carry = lax.fori_loop(0, singles, past_chunk, carry)
        carry = lax.fori_loop(0, pairs, lambda p, c: chunk(singles + 2 * p, c, "corner", width=2), carry)
    else:
        carry = lax.fori_loop(0, qi, past_chunk, (init,) * nh)
        carry = chunk(qi, carry, "diag")
    out_t = jnp.concatenate([acc / l for (_, l, acc) in carry], axis=0)
    o_ref[0] = out_t.T.astype(BF16)


def _moba_position_tables(positions, t):
    b, s = positions.shape
    n, nsub = s // t, t // MOBA_BLOCK
    pos_tiles = positions.reshape(b, n, t)
    q_min, k_max = jnp.min(pos_tiles, axis=-1), jnp.max(pos_tiles, axis=-1)
    qg_min = jnp.min(positions.reshape(b, s // LANES, LANES), axis=-1)
    kb_max = jnp.max(positions.reshape(b, s // MOBA_BLOCK, MOBA_BLOCK), axis=-1)
    far = q_min[:, :, None] - k_max[:, None, :] >= MAX_DIST
    q_rest = jnp.min(qg_min.reshape(b, n, t // LANES)[:, :, 1:], axis=-1)
    kb = kb_max.reshape(b, n, nsub)
    k_first = jnp.max(kb[:, :, :nsub - 1], axis=-1)
    corner = ((q_min[:, :, None] - k_first[:, None, :] >= MAX_DIST)
              & (q_rest[:, :, None] - kb[:, None, :, nsub - 1] >= MAX_DIST))
    counts = []
    for qi in range(n):
        ok = [far[:, qi, qi - 2 - 2 * p] & corner[:, qi, qi - 1 - 2 * p] for p in range(qi // 2)]
        first_bad = [jnp.where(o, qi // 2, p) for p, o in enumerate(ok)]
        counts.append(functools.reduce(jnp.minimum, first_bad, jnp.full((b,), qi // 2, jnp.int32)))
    far_pairs = jnp.stack(counts, axis=1).astype(jnp.int32)
    return (q_min, k_max, qg_min, kb_max, far_pairs, positions.reshape(b, n, 1, t), positions.reshape(b, s, 1))


def _moba_attn(qk, vt, bias_table, pos_tables, t, npairs, bounded):
    b, s, _ = qk.shape
    nb = s // MOBA_BLOCK
    ngroups = MOBA_HEADS * MOBA_HEAD_DIM // (LANES * npairs)
    w = npairs * LANES
    q_min, k_max, qg_min, kb_max, far_pairs, pos_row, pos_col = pos_tables
    grid_spec = pltpu.PrefetchScalarGridSpec(
        num_scalar_prefetch=5,
        grid=(b, ngroups, s // t),
        in_specs=[pl.BlockSpec((1, t, w), lambda bi, g, qi, *_: (bi, qi, g)),
                  pl.BlockSpec((1, s, w), lambda bi, g, qi, *_: (bi, 0, ngroups + g)),
                  pl.BlockSpec((1, w, s), lambda bi, g, qi, *_: (bi, g, 0)),
                  pl.BlockSpec((npairs, 2, LANES), lambda bi, g, qi, *_: (g, 0, 0)),
                  pl.BlockSpec((1, 1, 1, t), lambda bi, g, qi, *_: (bi, qi, 0, 0)),
                  pl.BlockSpec((1, s, 1), lambda bi, g, qi, *_: (bi, 0, 0))],
        out_specs=pl.BlockSpec((1, t, w), lambda bi, g, qi, *_: (bi, qi, g)),
        scratch_shapes=[pltpu.VMEM((LANES, w), F32), pltpu.VMEM((2 * npairs, nb, t), F32)],
    )
    return pl.pallas_call(
        functools.partial(_moba_attn_kernel, nb=nb, t=t, npairs=npairs, bounded=bounded),
        grid_spec=grid_spec,
        out_shape=jax.ShapeDtypeStruct((b, s, MOBA_HEADS * MOBA_HEAD_DIM), BF16),
        compiler_params=_params("arbitrary", "arbitrary", "arbitrary"),
        name="moba_attention",
    )(q_min, k_max, qg_min, kb_max, far_pairs, qk, qk, vt, bias_table, pos_row, pos_col)


def _rope_tab_kernel(pos_ref, inv_ref, c_ref, s_ref):
    half = MLA_ROPE // 2
    ang = pos_ref[0].astype(F32) * inv_ref[...]
    lane = lax.broadcasted_iota(jnp.int32, (1, LANES), 1)
    sn = jnp.sin(ang)
    c_ref[0] = jnp.cos(ang)
    s_ref[0] = jnp.where(lane < MLA_NOPE + half, -sn, sn)


def _rope_tables(pos_col, inv_lane, tm):
    b, s, _ = pos_col.shape
    out = pl.BlockSpec((1, tm, LANES), lambda bi, i: (bi, i, 0))
    shape = jax.ShapeDtypeStruct((b, s, LANES), F32)
    return pl.pallas_call(
        _rope_tab_kernel,
        grid=(b, s // tm),
        in_specs=[pl.BlockSpec((1, tm, 1), lambda bi, i: (bi, i, 0)),
                  pl.BlockSpec((1, LANES), lambda bi, i: (0, 0))],
        out_specs=[out, out],
        out_shape=[shape, shape],
        compiler_params=_params("arbitrary", "arbitrary"),
        name="rope_tables",
    )(pos_col, inv_lane)


def _mla_q_kernel(x_ref, sh_ref, sc_ref, g_ref, wdq_ref, qag_ref, wuq_ref, wuqp_ref, qg_ref, qgp_ref,
                  c_ref, s_ref, o_ref):
    h = _rms_mod(x_ref[0], g_ref[...], _mod_row(sh_ref), _mod_row(sc_ref)).astype(BF16)
    dq = _dot(h, wdq_ref[...])
    dqn = (dq * lax.rsqrt(jnp.mean(dq * dq, axis=-1, keepdims=True) + EPS) * qag_ref[...]).astype(BF16)
    uq = _dot(dqn, wuq_ref[...])
    up = _dot(dqn, wuqp_ref[...])
    cg = c_ref[0] * qg_ref[...]
    sg = s_ref[0] * qgp_ref[...]
    scale = MLA_QK ** -0.5 * LOG2E
    for hd in range(MLA_HEADS):
        sl = slice(hd * LANES, (hd + 1) * LANES)
        u = uq[:, sl]
        inv = lax.rsqrt(jnp.sum(u * u, axis=-1, keepdims=True) * (1.0 / MLA_QK) + EPS) * scale
        o_ref[0, :, sl] = ((u * cg + up[:, sl] * sg) * inv).astype(BF16)


def _mla_q(x, shift, scale, g, wdq, qag, wuq_p, wuq_partner, qg_p, qg_partner, cos, sin, tm):
    b, s, d = x.shape
    r = wdq.shape[1]
    nq = wuq_p.shape[1]
    tab = pl.BlockSpec((1, tm, LANES), lambda bi, i: (bi, i, 0))
    full = lambda shp: pl.BlockSpec(shp, lambda bi, i: (0, 0))
    return pl.pallas_call(
        _mla_q_kernel,
        grid=(b, s // tm),
        in_specs=[pl.BlockSpec((1, tm, d), lambda bi, i: (bi, i, 0)), _mod_spec(shift, d), _mod_spec(scale, d),
                  full((1, d)),
                  full((d, r)), full((1, r)), full((r, nq)), full((r, nq)), full((1, LANES)), full((1, LANES)),
                  tab, tab],
        out_specs=pl.BlockSpec((1, tm, nq), lambda bi, i: (bi, i, 0)),
        out_shape=jax.ShapeDtypeStruct((b, s, nq), BF16),
        compiler_params=_params("arbitrary", "arbitrary"),
        name="mla_q_path",
    )(x, shift[0], scale[0], g, wdq, qag, wuq_p, wuq_partner, qg_p, qg_partner, cos, sin)


def _mla_qkv_kernel(x_ref, qsh_ref, qsc_ref, qn_ref, wdq_ref, qag_ref, wuq_ref, wuqp_ref, qg_ref, qgp_ref,
                    ksh_ref, ksc_ref, kn_ref, wdkv_ref, kvg_ref, wuk_ref, wuvt_ref, kg_ref, kgp_ref,
                    c_ref, s_ref, q_ref, k_ref, vt_ref):
    x = x_ref[0]
    xn = x * lax.rsqrt(jnp.mean(x * x, axis=-1, keepdims=True) + EPS)
    cos, sin = c_ref[0], s_ref[0]
    h = ((xn * qn_ref[...]) * (1.0 + _mod_row(qsc_ref)) + _mod_row(qsh_ref)).astype(BF16)
    dq = _dot(h, wdq_ref[...])
    dqn = (dq * lax.rsqrt(jnp.mean(dq * dq, axis=-1, keepdims=True) + EPS) * qag_ref[...]).astype(BF16)
    uq = _dot(dqn, wuq_ref[...])
    up = _dot(dqn, wuqp_ref[...])
    hs = ((xn * kn_ref[...]) * (1.0 + _mod_row(ksc_ref)) + _mod_row(ksh_ref)).astype(BF16)
    ck = _dot(hs, wdkv_ref[...])
    cg = cos * qg_ref[...]
    sg = sin * qgp_ref[...]
    scale = MLA_QK ** -0.5 * LOG2E

    def q_heads(lo, hi):
        for hd in range(lo, hi):
            sl = slice(hd * LANES, (hd + 1) * LANES)
            u = uq[:, sl]
            inv = lax.rsqrt(jnp.sum(u * u, axis=-1, keepdims=True) * (1.0 / MLA_QK) + EPS) * scale
            q_ref[0, :, sl] = ((u * cg + up[:, sl] * sg) * inv).astype(BF16)

    q_heads(0, MLA_HEADS // 2)
    ckv = ck[:, :MLA_KV_LORA]
    ckvn = (ckv * lax.rsqrt(jnp.mean(ckv * ckv, axis=-1, keepdims=True) + EPS) * kvg_ref[...]).astype(BF16)
    kn = _dot(ckvn, wuk_ref[...])
    vt_ref[0] = _dot_nt(wuvt_ref[...], ckvn).astype(BF16)
    q_heads(MLA_HEADS // 2, MLA_HEADS)
    lane = lax.broadcasted_iota(jnp.int32, (1, LANES), 1)
    on_rope = (lane >= MLA_NOPE) & (lane < MLA_QK)
    tail = ck[:, MLA_KV_LORA:MLA_KV_LORA + LANES]
    kpe = jnp.where(on_rope, pltpu.roll(tail, MLA_NOPE, axis=1), 0.0)
    partner = pltpu.roll(tail, MLA_ROPE, axis=1)
    kcg = cos * kg_ref[...]
    ksg = sin * kgp_ref[...]
    rot = kpe * kcg + partner * ksg
    ss_pe = jnp.sum(kpe * kpe, axis=-1, keepdims=True)
    for hd in range(MLA_HEADS):
        sl = slice(hd * LANES, (hd + 1) * LANES)
        nope = kn[:, sl]
        inv = lax.rsqrt((jnp.sum(nope * nope, axis=-1, keepdims=True) + ss_pe) * (1.0 / MLA_QK) + EPS)
        k_ref[0, :, sl] = ((nope * kcg + rot) * inv).astype(BF16)


def _mla_qkv(x, q_args, kv_args, cos, sin, tm):
    b, s, d = x.shape
    nq = q_args[5].shape[1]
    nk = kv_args[5].shape[1]
    nv = kv_args[6].shape[0]
    tab = pl.BlockSpec((1, tm, LANES), lambda bi, i: (bi, i, 0))
    full = lambda a: pl.BlockSpec(a.shape, lambda bi, i: (0, 0))

    def path(args):
        shift, scale = args[:2]
        return [shift[0], scale[0], *args[2:]], [_mod_spec(shift, d), _mod_spec(scale, d)] + [full(a) for a in args[2:]]

    q_ins, q_specs = path(q_args)
    kv_ins, kv_specs = path(kv_args)
    return pl.pallas_call(
        _mla_qkv_kernel,
        grid=(b, s // tm),
        in_specs=[pl.BlockSpec((1, tm, d), lambda bi, i: (bi, i, 0))] + q_specs + kv_specs + [tab, tab],
        out_specs=[pl.BlockSpec((1, tm, nq), lambda bi, i: (bi, i, 0)),
                   pl.BlockSpec((1, tm, nk), lambda bi, i: (bi, i, 0)),
                   pl.BlockSpec((1, nv, tm), lambda bi, i: (bi, 0, i))],
        out_shape=[jax.ShapeDtypeStruct((b, s, nq), BF16), jax.ShapeDtypeStruct((b, s, nk), BF16),
                   jax.ShapeDtypeStruct((b, nv, s), BF16)],
        compiler_params=_params("arbitrary", "arbitrary"),
        name="mla_qkv_proj",
    )(x, *q_ins, *kv_ins, cos, sin)


def _mla_attn_kernel(q_ref, k_ref, vt_ref, o_ref, *, t, nh, bounded):
    qi = pl.program_id(2)
    q2 = q_ref[0]
    q_heads = [q2[:, h * LANES:(h + 1) * LANES] for h in range(nh)]

    def chunk(j, carry, masked):
        carry = list(carry)
        start = pl.multiple_of(j * t, t)
        k2 = k_ref[0, pl.ds(start, t), :]
        vt = vt_ref[0, :, pl.ds(start, t)]
        s_heads = [_dot_nt(k2[:, h * LANES:(h + 1) * LANES], q_heads[h]) for h in range(nh)]
        if masked:
            row = lax.broadcasted_iota(jnp.int32, (t, t), 0)
            col = lax.broadcasted_iota(jnp.int32, (t, t), 1)
            allowed = row <= col
        for h in range(nh):
            m, l, acc = carry[h]
            s = jnp.where(allowed, s_heads[h], NEG) if masked else s_heads[h]
            carry[h] = _softmax_step_t(s, m, l, acc, vt[h * MLA_V:(h + 1) * MLA_V, :], bounded)
        return tuple(carry)

    def own_chunk_bounded(carry):
        carry = list(carry)
        half = t // 2
        start = pl.multiple_of(qi * t, t)
        k2 = k_ref[0, pl.ds(start, t), :]
        vt = vt_ref[0, :, pl.ds(start, t)]
        causal = (lax.broadcasted_iota(jnp.int32, (half, half), 0) <= lax.broadcasted_iota(jnp.int32, (half, half), 1))
        quads = []
        for h in range(nh):
            kh = k2[:, h * LANES:(h + 1) * LANES]
            qh = q_heads[h]
            quads.append((_dot_nt(kh[:half], qh[:half]), _dot_nt(kh[:half], qh[half:]), _dot_nt(kh[half:], qh[half:])))
        for h in range(nh):
            m, l, acc = carry[h]
            s00, s01, s11 = quads[h]
            e00 = jnp.exp2(jnp.where(causal, s00, NEG))
            e01 = jnp.exp2(s01)
            e11 = jnp.exp2(jnp.where(causal, s11, NEG))
            vt0 = vt[h * MLA_V:(h + 1) * MLA_V, :half]
            vt1 = vt[h * MLA_V:(h + 1) * MLA_V, half:]
            l_add = jnp.concatenate(
                [jnp.sum(e00, axis=0, keepdims=True),
                 jnp.sum(e01, axis=0, keepdims=True) + jnp.sum(e11, axis=0, keepdims=True)], axis=1)
            acc_add = jnp.concatenate(
                [_dot(vt0, e00.astype(BF16)), _dot(vt0, e01.astype(BF16)) + _dot(vt1, e11.astype(BF16))], axis=1)
            carry[h] = (m, l + l_add, acc + acc_add)
        return tuple(carry)

    init = (jnp.full((1, t), NEG, F32), jnp.zeros((1, t), F32), jnp.zeros((MLA_V, t), F32))
    carry = lax.fori_loop(0, qi, lambda j, c: chunk(j, c, False), (init,) * nh)
    carry = own_chunk_bounded(carry) if bounded else chunk(qi, carry, True)
    out_t = jnp.concatenate([acc / l for (_, l, acc) in carry], axis=0)
    o_ref[0] = out_t.T.astype(BF16)


def _mla_attn(q, k, vt, t, nh, bounded):
    b, s, _ = q.shape
    return pl.pallas_call(
        functools.partial(_mla_attn_kernel, t=t, nh=nh, bounded=bounded),
        grid=(b, MLA_HEADS // nh, s // t),
        in_specs=[pl.BlockSpec((1, t, nh * LANES), lambda bi, g, qi: (bi, qi, g)),
                  pl.BlockSpec((1, s, nh * LANES), lambda bi, g, qi: (bi, 0, g)),
                  pl.BlockSpec((1, nh * MLA_V, s), lambda bi, g, qi: (bi, g, 0))],
        out_specs=pl.BlockSpec((1, t, nh * MLA_V), lambda bi, g, qi: (bi, qi, g)),
        out_shape=jax.ShapeDtypeStruct((b, s, MLA_HEADS * MLA_V), BF16),
        compiler_params=_params("arbitrary", "arbitrary", "arbitrary"),
        name="mla_attention",
    )(q, k, vt)


def _t5_bias_table(rel_bias):
    n = jnp.arange(MASKED_DIST)
    max_exact = REL_BUCKETS // 2
    nf = jnp.maximum(n, 1).astype(F32)
    large = max_exact + (jnp.log(nf / max_exact) / math.log(REL_MAX_DIST / max_exact)
                         * (REL_BUCKETS - max_exact)).astype(jnp.int32)
    large = jnp.minimum(large, REL_BUCKETS - 1)
    bucket = jnp.where(n < max_exact, n, large)
    table = jnp.concatenate([rel_bias.T[:, bucket] * LOG2E, jnp.full((MOBA_HEADS, 1), NEG, F32)], axis=1)
    return table.reshape(MOBA_HEADS // 2, 2, BIAS_TABLE)


def _pad_heads(w, width):
    k = w.shape[0]
    w3 = w.reshape(k, -1, width)
    return jnp.pad(w3, ((0, 0), (0, 0), (0, LANES - width))).reshape(k, -1)


def _pad_lanes(g):
    return jnp.pad(g, (0, LANES - g.shape[0])).reshape(1, LANES)


def _swap_rope_halves(a):
    half = MLA_ROPE // 2
    return jnp.concatenate([jnp.zeros_like(a[..., :MLA_NOPE]), a[..., MLA_NOPE + half:],
                            a[..., MLA_NOPE:MLA_NOPE + half]], axis=-1)


def _scores_bounded(q_gain, k_gain, dim, scale, extra=0.0):
    bound = dim * jnp.max(jnp.abs(q_gain)) * jnp.max(jnp.abs(k_gain)) * scale * BOUND_SLACK + extra
    return bound <= SCORE_BOUND_MAX


def kernel(x, c, positions, ada_w, ada_b, norm_g, ffn_w_gate, ffn_w_up, ffn_w_down, rel_bias,
           moba_w_qkv, moba_q_g, moba_k_g, moba_w_o, kv_ada_w, kv_ada_b, kv_norm_g, w_dkv,
           kv_a_norm_g, w_uk, w_uv, mla_k_g, mla_w_dq, mla_q_a_norm_g, mla_w_uq, mla_q_g, mla_w_o):
    b, s, d = x.shape
    depth = ada_w.shape[0]
    n_moba = moba_w_qkv.shape[0]
    assert b <= SUBLANES and s % max(FFN_ROWS, MOBA_TILE, MLA_TILE) == 0 and MOBA_TILE == 2 * MOBA_BLOCK
    tm_ffn = FFN_ROWS
    tm_proj = PROJ_ROWS

    c_pad = jnp.pad(c, ((0, SUBLANES - b), (0, 0)))
    mods = _mods(c_pad, ada_w, ada_b.reshape(depth, 1, -1), tn=ADALN_COLS)
    kv_mods = _mods(c_pad, kv_ada_w[None], kv_ada_b.reshape(1, 1, -1), tn=d)

    w_gate, w_up, w_down = (w.astype(BF16) for w in (ffn_w_gate, ffn_w_up, ffn_w_down))

    for layer in range(depth):
        sh1, sc1, g1, sh2, sc2, g2, sh3, sc3, g3 = [(mods, layer, k) for k in range(mods.shape[-1] // d)]
        ng = norm_g[layer]

        def ffn(xin, idx, sh, sc, gt, mixer=None):
            return _ffn(xin, sh, sc, gt, ng[2 * idx:2 * idx + 1], w_gate, w_up, w_down, (layer, idx),
                        tm_ffn, FFN_CHUNK, mixer)

        x = ffn(x, 0, sh1, sc1, g1)

        if layer < n_moba:
            n_qk = 2 * MOBA_HEADS * MOBA_HEAD_DIM
            q_scale = MOBA_HEAD_DIM ** -0.5 * LOG2E
            head_gain = jnp.concatenate([jnp.tile(moba_q_g[layer] * q_scale, MOBA_HEADS),
                                         jnp.tile(moba_k_g[layer], MOBA_HEADS)]).reshape(1, -1)
            w_qkv = moba_w_qkv[layer].astype(BF16)
            qk, vt = _moba_proj(x, sh2, sc2, ng[1:2], w_qkv[:, :n_qk], w_qkv[:, n_qk:].T, head_gain,
                                2 * tm_proj, PROJ_COLS)
            bias_table = _t5_bias_table(rel_bias)
            moba = functools.partial(_moba_attn, qk, vt, bias_table, _moba_position_tables(positions, MOBA_TILE),
                                     MOBA_TILE, ATTN_HEADS // 2)
            ok = _scores_bounded(moba_q_g[layer], moba_k_g[layer], MOBA_HEAD_DIM, q_scale,
                                 jnp.max(jnp.abs(rel_bias)) * LOG2E)
            o = lax.cond(ok, lambda: moba(True), lambda: moba(False))
            w_o = moba_w_o[layer]
        else:
            j = layer - n_moba
            if j == 0:
                half = MLA_ROPE // 2
                inv = ROPE_BASE ** (-jnp.arange(half, dtype=F32) / half)
                inv_lane = jnp.zeros((LANES,), F32).at[MLA_NOPE:MLA_QK].set(jnp.tile(inv, 2)).reshape(1, LANES)
                cos, sin = _rope_tables(positions.reshape(b, s, 1), inv_lane, tm_proj)
                kv_sh, kv_sc = (kv_mods, 0, 0), (kv_mods, 0, 1)
                w_dkv16 = w_dkv.astype(BF16)
                w_kpe = w_dkv16[:, MLA_KV_LORA:]
                w_kpe_swapped = jnp.concatenate([w_kpe[:, half:], w_kpe[:, :half]], axis=1)
                wdkv_p = jnp.pad(jnp.concatenate([w_dkv16, w_kpe_swapped], axis=1),
                                 ((0, 0), (0, LANES - 2 * MLA_ROPE)))
                kv_args = (kv_sh, kv_sc, kv_norm_g.reshape(1, d), wdkv_p, kv_a_norm_g.reshape(1, -1),
                           _pad_heads(w_uk.astype(BF16), MLA_NOPE), w_uv.astype(BF16).T, _pad_lanes(mla_k_g),
                           _pad_lanes(_swap_rope_halves(mla_k_g)))
            w_uq = mla_w_uq[j].astype(BF16)
            w_uq_partner = _swap_rope_halves(w_uq.reshape(w_uq.shape[0], MLA_HEADS, MLA_QK)).reshape(w_uq.shape)
            q_args = (sh2, sc2, ng[1:2], mla_w_dq[j].astype(BF16), mla_q_a_norm_g[j].reshape(1, -1),
                      _pad_heads(w_uq, MLA_QK), _pad_heads(w_uq_partner, MLA_QK),
                      _pad_lanes(mla_q_g[j]), _pad_lanes(_swap_rope_halves(mla_q_g[j])))
            if j == 0:
                q, shared_k, shared_v = _mla_qkv(x, q_args, kv_args, cos, sin, tm_proj)
            else:
                q = _mla_q(x, *q_args, cos, sin, tm_proj)
            mla = functools.partial(_mla_attn, q, shared_k, shared_v, MLA_TILE, ATTN_HEADS)
            ok = _scores_bounded(mla_q_g[j], mla_k_g, MLA_QK, MLA_QK ** -0.5 * LOG2E)
            o = lax.cond(ok, lambda: mla(True), lambda: mla(False))
            w_o = mla_w_o[j]

        x = ffn(x, 1, sh3, sc3, g3, mixer=(o, g2, w_o.astype(BF16)))
    return x
```

```python
import functools
import math

import jax
import jax.numpy as jnp
from jax import lax
from jax.experimental import pallas as pl
from jax.experimental.pallas import tpu as pltpu

F32 = jnp.float32
BF16 = jnp.bfloat16

LANES = 128
SUBLANES = 8
VMEM_LIMIT_BYTES = 56 * 1024 * 1024

MOBA_HEADS = 16
MOBA_HEAD_DIM = 64
MOBA_BLOCK = 256
MOBA_TOPK = 3
REL_BUCKETS = 32
REL_MAX_DIST = 128
MLA_HEADS = 16
MLA_KV_LORA = 256
MLA_NOPE = 64
MLA_ROPE = 32
MLA_V = 64
MLA_QK = MLA_NOPE + MLA_ROPE
ROPE_BASE = 10000.0
EPS = 1e-6
NEG = -1e30

FFN_ROWS = 1024
FFN_CHUNK = 256
PROJ_ROWS = 512
PROJ_COLS = 512
ADALN_COLS = 2304
MOBA_TILE = 512
MLA_TILE = 512
ATTN_HEADS = 8
BOUND_SLACK = 1.01
LOG2E = math.log2(math.e)
BIAS_TABLE = LANES
MAX_DIST = BIAS_TABLE - 2
MASKED_DIST = BIAS_TABLE - 1
SCORE_BOUND_MAX = 60.0


def _params(*sem):
    return pltpu.CompilerParams(dimension_semantics=sem, vmem_limit_bytes=VMEM_LIMIT_BYTES)


def _dot(a, b):
    return jnp.dot(a, b, preferred_element_type=F32)


def _dot_nt(a, b):
    return lax.dot_general(a, b, (((1,), (1,)), ((), ())), preferred_element_type=F32)


def _rms_mod(x, g, shift, scale):
    ms = jnp.mean(x * x, axis=-1, keepdims=True)
    y = x * lax.rsqrt(ms + EPS) * g
    return y * (1.0 + scale) + shift


def _mod_spec(mod, d):
    _, layer, k = mod
    return pl.BlockSpec((None, SUBLANES, d), lambda bi, *_: (layer, 0, k))


def _mod_row(ref):
    return ref[pl.ds(pl.program_id(0), 1), :]


def _mods_kernel(c_ref, w_ref, b_ref, o_ref):
    c = c_ref[...]
    ca = c * jax.nn.sigmoid(c)
    w = w_ref[0]
    w_hi = w.astype(BF16)
    w_lo = (w - w_hi.astype(F32)).astype(BF16)
    ca_hi = ca.astype(BF16)
    ca_lo = (ca - ca_hi.astype(F32)).astype(BF16)
    o_ref[0] = _dot(ca_hi, w_hi) + _dot(ca_lo, w_hi) + _dot(ca_hi, w_lo) + b_ref[0]


def _mods(c_pad, w, b, tn):
    nl, d, n = w.shape
    return pl.pallas_call(
        _mods_kernel,
        grid=(nl, n // tn),
        in_specs=[pl.BlockSpec((SUBLANES, d), lambda l, j: (0, 0)),
                  pl.BlockSpec((1, d, tn), lambda l, j: (l, 0, j)),
                  pl.BlockSpec((1, 1, tn), lambda l, j: (l, 0, j))],
        out_specs=pl.BlockSpec((1, SUBLANES, tn), lambda l, j: (l, 0, j)),
        out_shape=jax.ShapeDtypeStruct((nl, SUBLANES, n), F32),
        compiler_params=_params("arbitrary", "arbitrary"),
        name="adaln_mods",
    )(c_pad, w, b)


def _ffn_kernel(*refs, tf, with_mixer):
    if with_mixer:
        x_ref, a_ref, gm_ref, wo_ref = refs[:4]
        refs = refs[4:]
        x = x_ref[0] + _mod_row(gm_ref) * _dot(a_ref[0], wo_ref[...])
    else:
        x_ref = refs[0]
        refs = refs[1:]
        x = x_ref[0]
    sh_ref, sc_ref, gt_ref, g_ref, wg_ref, wu_ref, wd_ref, o_ref = refs
    h = _rms_mod(x, g_ref[...], _mod_row(sh_ref), _mod_row(sc_ref)).astype(BF16)
    acc = None
    for f in range(wg_ref.shape[1] // tf):
        cols = slice(f * tf, (f + 1) * tf)
        g = _dot(h, wg_ref[:, cols])
        u = _dot(h, wu_ref[:, cols])
        a = (g * jax.nn.sigmoid(g) * u).astype(BF16)
        d = _dot(a, wd_ref[cols, :])
        acc = d if acc is None else acc + d
    o_ref[0] = x + (0.5 * _mod_row(gt_ref)) * acc


def _ffn(x, shift, scale, gate, g, wg, wu, wd, which, tm, tf, mixer=None):
    b, s, d = x.shape
    f = wg.shape[-1]
    row = pl.BlockSpec((1, tm, d), lambda bi, i: (bi, i, 0))
    full = lambda shp: pl.BlockSpec(shp, lambda bi, i: (0, 0), pipeline_mode=pl.Buffered(1))
    picked = lambda r, c: pl.BlockSpec((None, None, r, c), lambda bi, i: (*which, 0, 0),
                                       pipeline_mode=pl.Buffered(1))
    ins, specs = [x], [row]
    if mixer is not None:
        a, gate_mix, w_o = mixer
        k = a.shape[2]
        ins += [a, gate_mix[0], w_o]
        specs += [pl.BlockSpec((1, tm, k), lambda bi, i: (bi, i, 0)), _mod_spec(gate_mix, d), full((k, d))]
    ins += [shift[0], scale[0], gate[0], g, wg, wu, wd]
    specs += [_mod_spec(shift, d), _mod_spec(scale, d), _mod_spec(gate, d), full((1, d)),
              picked(d, f), picked(d, f), picked(f, d)]
    return pl.pallas_call(
        functools.partial(_ffn_kernel, tf=tf, with_mixer=mixer is not None),
        grid=(b, s // tm),
        in_specs=specs,
        out_specs=row,
        out_shape=jax.ShapeDtypeStruct((b, s, d), F32),
        compiler_params=_params("arbitrary", "arbitrary"),
        name="ffn_swiglu",
    )(*ins)


def _headnorm_pair(y, g, dim):
    lane = lax.broadcasted_iota(jnp.int32, (1, LANES), 1)
    left = lane < dim
    y2 = y * y
    ss_a = jnp.sum(jnp.where(left, y2, 0.0), axis=-1, keepdims=True)
    ss_b = jnp.sum(jnp.where(left, 0.0, y2), axis=-1, keepdims=True)
    inv = jnp.where(left, lax.rsqrt(ss_a * (1.0 / dim) + EPS), lax.rsqrt(ss_b * (1.0 / dim) + EPS))
    return y * inv * g


def _moba_proj_kernel(x_ref, sh_ref, sc_ref, g_ref, w_ref, wvt_ref, hg_ref, qk_ref, vt_ref, *, tn):
    h = _rms_mod(x_ref[0], g_ref[...], _mod_row(sh_ref), _mod_row(sc_ref)).astype(BF16)
    for n in range(w_ref.shape[1] // tn):
        res = _dot(h, w_ref[:, n * tn:(n + 1) * tn])
        for s in range(tn // LANES):
            sl = slice(n * tn + s * LANES, n * tn + (s + 1) * LANES)
            qk_ref[0, :, sl] = _headnorm_pair(res[:, s * LANES:(s + 1) * LANES], hg_ref[:, sl],
                                              MOBA_HEAD_DIM).astype(BF16)
    for n in range(wvt_ref.shape[0] // tn):
        rows = slice(n * tn, (n + 1) * tn)
        vt_ref[0, rows, :] = _dot_nt(wvt_ref[rows, :], h).astype(BF16)


def _moba_proj(x, shift, scale, g, w_qk, w_vt, head_gain, tm, tn):
    b, s, d = x.shape
    nqk = w_qk.shape[1]
    nv = w_vt.shape[0]
    full = lambda shp: pl.BlockSpec(shp, lambda bi, i: (0, 0))
    return pl.pallas_call(
        functools.partial(_moba_proj_kernel, tn=tn),
        grid=(b, s // tm),
        in_specs=[pl.BlockSpec((1, tm, d), lambda bi, i: (bi, i, 0)), _mod_spec(shift, d), _mod_spec(scale, d),
                  full((1, d)),
                  full((d, nqk)), full((nv, d)), full((1, nqk))],
        out_specs=[pl.BlockSpec((1, tm, nqk), lambda bi, i: (bi, i, 0)),
                   pl.BlockSpec((1, nv, tm), lambda bi, i: (bi, 0, i))],
        out_shape=[jax.ShapeDtypeStruct((b, s, nqk), BF16), jax.ShapeDtypeStruct((b, nv, s), BF16)],
        compiler_params=_params("arbitrary", "arbitrary"),
        name="moba_qkv_proj",
    )(x, shift[0], scale[0], g, w_qk, w_vt, head_gain)


def _softmax_step_t(s, m, l, acc, vt, bounded):
    if bounded:
        p = jnp.exp2(s)
        return m, l + jnp.sum(p, axis=0, keepdims=True), acc + _dot(vt, p.astype(BF16))
    m_new = jnp.maximum(m, jnp.max(s, axis=0, keepdims=True))
    alpha = jnp.exp2(m - m_new)
    p = jnp.exp2(s - m_new)
    l_new = alpha * l + jnp.sum(p, axis=0, keepdims=True)
    acc_new = acc * alpha + _dot(vt, p.astype(BF16))
    return m_new, l_new, acc_new


def _moba_attn_kernel(qmin_ref, kmax_ref, qgmin_ref, kbmax_ref, fpairs_ref, q_ref, k_ref, vt_ref, bt_ref, pq_ref, pk_ref, o_ref,
                      km_scr, sel_scr, *, nb, t, npairs, bounded):
    blk = MOBA_BLOCK
    hd = MOBA_HEAD_DIM
    nsub = t // blk
    nh = 2 * npairs
    bi = pl.program_id(0)
    qi = pl.program_id(2)
    lane = lax.broadcasted_iota(jnp.int32, (1, LANES), 1)
    left = lane < hd

    @pl.when(qi == 0)
    def _():
        km_scr[...] = jnp.zeros_like(km_scr)
        for r in range(nb):
            kblk = k_ref[0, r * blk:(r + 1) * blk, :].astype(F32)
            km_scr[r:r + 1, :] = jnp.sum(kblk, axis=0, keepdims=True) * (1.0 / blk)

    q_heads = []
    for p in range(npairs):
        q2 = q_ref[0, :, p * LANES:(p + 1) * LANES]
        zero = jnp.zeros_like(q2)
        q_heads += [jnp.where(left, q2, zero), jnp.where(left, zero, q2)]

    col = lax.broadcasted_iota(jnp.int32, (1, t), 1)
    q_idx = qi * t + col
    q_sub = sum((col >= r * blk).astype(jnp.int32) for r in range(1, nsub))
    q_blk = qi * nsub + q_sub
    q_blk_start = q_blk * blk

    blk_row = lax.broadcasted_iota(jnp.int32, (nb, t), 0)
    for h in range(nh):
        km = km_scr[:, (h // 2) * LANES:(h // 2 + 1) * LANES]
        km_hi = km.astype(BF16)
        km_lo = (km - km_hi.astype(F32)).astype(BF16)
        gate = (_dot_nt(km_hi, q_heads[h]) + _dot_nt(km_lo, q_heads[h]))[:nb]
        gate = jnp.where(blk_row < q_blk, gate, -jnp.inf)
        sel = jnp.zeros((nb, t), F32)
        for _ in range(MOBA_TOPK):
            mx = jnp.max(gate, axis=0, keepdims=True)
            idx = jnp.min(jnp.where(gate == mx, blk_row, nb), axis=0, keepdims=True)
            pick = (blk_row == idx) & (mx > -jnp.inf)
            sel = jnp.where(pick, 1.0, sel)
            gate = jnp.where(pick, -jnp.inf, gate)
        sel_scr[h] = sel

    pq = pq_ref[0, 0]

    def gather_bias(h, dist):
        table = jnp.broadcast_to(bt_ref[h // 2, h % 2:h % 2 + 1, :], (dist.shape[0], LANES))
        return jnp.concatenate(
            [jnp.take_along_axis(table, dist[:, g * LANES:(g + 1) * LANES], axis=1, mode="promise_in_bounds")
             for g in range(dist.shape[1] // LANES)], axis=1)

    def far_bias(h):
        return bt_ref[h // 2, h % 2:h % 2 + 1, MAX_DIST:MAX_DIST + 1]

    def chosen_rows(h, j, lanes=slice(None), width=1):
        return [sel_scr[h, pl.ds(j * nsub + r, 1), lanes] > 0.0 for r in range(width * nsub)]

    def own_chunk_bounded(carry):
        assert nsub == 2
        carry = list(carry)
        start = pl.multiple_of(qi * t, t)
        k2 = k_ref[0, pl.ds(start, t), :]
        vt = vt_ref[0, :, pl.ds(start, t)]
        pk0 = pk_ref[0, pl.ds(start, blk), :]
        pk1 = pk_ref[0, pl.ds(start + blk, blk), :]
        causal = (lax.broadcasted_iota(jnp.int32, (blk, blk), 0) <= lax.broadcasted_iota(jnp.int32, (blk, blk), 1))
        d00 = jnp.where(causal, jnp.clip(pq[:, :blk] - pk0, 0, MAX_DIST), MASKED_DIST)
        d01 = jnp.clip(pq[:, blk:] - pk0, 0, MAX_DIST)
        d11 = jnp.where(causal, jnp.clip(pq[:, blk:] - pk1, 0, MAX_DIST), MASKED_DIST)
        quads = []
        for h in range(nh):
            kh = k2[:, (h // 2) * LANES:(h // 2 + 1) * LANES]
            qh = q_heads[h]
            quads.append((_dot_nt(kh[:blk], qh[:blk]), _dot_nt(kh[:blk], qh[blk:]), _dot_nt(kh[blk:], qh[blk:])))
        for h in range(nh):
            m, l, acc = carry[h]
            s00, s01, s11 = quads[h]
            e00 = jnp.exp2(s00 + gather_bias(h, d00))
            e01 = jnp.exp2(s01 + gather_bias(h, d01))
            e11 = jnp.exp2(s11 + gather_bias(h, d11))
            w01 = jnp.where(chosen_rows(h, qi, slice(blk, t))[0], 1.0, 0.0)
            vt0 = vt[h * hd:(h + 1) * hd, :blk]
            vt1 = vt[h * hd:(h + 1) * hd, blk:]
            l_add = jnp.concatenate(
                [jnp.sum(e00, axis=0, keepdims=True),
                 jnp.sum(e01, axis=0, keepdims=True) * w01 + jnp.sum(e11, axis=0, keepdims=True)], axis=1)
            acc_add = jnp.concatenate(
                [_dot(vt0, e00.astype(BF16)),
                 _dot(vt0, e01.astype(BF16)) * w01 + _dot(vt1, e11.astype(BF16))], axis=1)
            carry[h] = (m, l + l_add, acc + acc_add)
        return tuple(carry)

    def chunk(j, carry, mode, width=1):
        last = width * nsub - 1
        carry = list(carry)
        start = pl.multiple_of(j * t, t)
        k2 = k_ref[0, pl.ds(start, width * t), :]
        vt = vt_ref[0, :, pl.ds(start, width * t)]
        s_heads = [_dot_nt(k2[:, (h // 2) * LANES:(h // 2 + 1) * LANES], q_heads[h]) for h in range(nh)]
        if mode in ("near", "diag"):
            dist = jnp.clip(pq - pk_ref[0, pl.ds(start, t), :], 0, MAX_DIST)
        if mode == "corner":
            dist_c = jnp.clip(pq[:, :LANES] - pk_ref[0, pl.ds(start + last * blk, blk), :], 0, MAX_DIST)
        if mode == "diag":
            key_idx = start + lax.broadcasted_iota(jnp.int32, (blk, t), 0)
        for h in range(nh):
            m, l, acc = carry[h]
            s = s_heads[h]
            vt_h = vt[h * hd:(h + 1) * hd, :]
            chosen = chosen_rows(h, j, width=width)
            if mode in ("far", "corner") and bounded:
                two_c = jnp.exp2(far_bias(h))
                for r in range(width * nsub):
                    piece = s[r * blk:(r + 1) * blk]
                    w = jnp.where(chosen[r], two_c, 0.0)
                    if mode == "corner" and r == last:
                        piece = jnp.concatenate([piece[:, :LANES] + gather_bias(h, dist_c), piece[:, LANES:]], axis=1)
                        w = jnp.where(col < LANES, jnp.where(chosen[r], 1.0, 0.0), w)
                    p = jnp.exp2(piece)
                    l = l + jnp.sum(p, axis=0, keepdims=True) * w
                    acc = acc + _dot(vt_h[:, r * blk:(r + 1) * blk], p.astype(BF16)) * w
                carry[h] = (m, l, acc)
            elif mode == "far":
                pieces = [s[r * blk:(r + 1) * blk] for r in range(nsub)]
                c = far_bias(h)
                tmax = [jnp.where(chosen[r], jnp.max(pieces[r], axis=0, keepdims=True) + c, NEG)
                        for r in range(nsub)]
                m_new = functools.reduce(jnp.maximum, tmax, m)
                alpha = jnp.exp2(m - m_new)
                ps = [jnp.exp2(pieces[r] - jnp.where(chosen[r], m_new - c, jnp.inf)) for r in range(nsub)]
                l_new = alpha * l + sum(jnp.sum(p, axis=0, keepdims=True) for p in ps)
                p = jnp.concatenate(ps, axis=0) if nsub > 1 else ps[0]
                carry[h] = (m_new, l_new, acc * alpha + _dot(vt_h, p.astype(BF16)))
            else:
                s = s + gather_bias(h, dist)
                masked = []
                for r in range(nsub):
                    piece = s[r * blk:(r + 1) * blk]
                    if mode == "diag":
                        kidx = key_idx + r * blk
                        keep_own = jnp.where(kidx <= q_idx, jnp.where(kidx >= q_blk_start, piece, NEG), NEG)
                        masked.append(jnp.where(chosen[r], piece, keep_own))
                    else:
                        masked.append(jnp.where(chosen[r], piece, NEG))
                s = jnp.concatenate(masked, axis=0) if nsub > 1 else masked[0]
                carry[h] = _softmax_step_t(s, m, l, acc, vt_h, bounded)
        return tuple(carry)

    def past_chunk(j, carry):
        q_lo = qmin_ref[bi, qi]
        far = q_lo - kmax_ref[bi, j] >= MAX_DIST
        if not bounded:
            return lax.cond(far, lambda c: chunk(j, c, "far"), lambda c: chunk(j, c, "near"), carry)
        groups = t // LANES
        q_rest = functools.reduce(jnp.minimum, [qgmin_ref[bi, qi * groups + g] for g in range(1, groups)])
        k_first = functools.reduce(jnp.maximum, [kbmax_ref[bi, j * nsub + r] for r in range(nsub - 1)])
        corner = (q_lo - k_first >= MAX_DIST) & (q_rest - kbmax_ref[bi, j * nsub + nsub - 1] >= MAX_DIST)
        return lax.cond(far, lambda c: chunk(j, c, "far"),
                        lambda c: lax.cond(corner, lambda c2: chunk(j, c2, "corner"),
                                           lambda c2: chunk(j, c2, "near"), c), carry)

    init = (jnp.full((1, t), NEG, F32), jnp.zeros((1, t), F32), jnp.zeros((hd, t), F32))
    if bounded:
        carry = own_chunk_bounded((init,) * nh)
        pairs = fpairs_ref[bi, qi]
        singles = qi - 2 * pairs
        carry = lax.fori_loop(0, singles, past_chunk, carry)
        carry = lax.fori_loop(0, pairs, lambda p, c: chunk(singles + 2 * p, c, "corner", width=2), carry)
    else:
        carry = lax.fori_loop(0, qi, past_chunk, (init,) * nh)
        carry = chunk(qi, carry, "diag")
    out_t = jnp.concatenate([acc / l for (_, l, acc) in carry], axis=0)
    o_ref[0] = out_t.T.astype(BF16)


def _moba_position_tables(positions, t):
    b, s = positions.shape
    n, nsub = s // t, t // MOBA_BLOCK
    pos_tiles = positions.reshape(b, n, t)
    q_min, k_max = jnp.min(pos_tiles, axis=-1), jnp.max(pos_tiles, axis=-1)
    qg_min = jnp.min(positions.reshape(b, s // LANES, LANES), axis=-1)
    kb_max = jnp.max(positions.reshape(b, s // MOBA_BLOCK, MOBA_BLOCK), axis=-1)
    far = q_min[:, :, None] - k_max[:, None, :] >= MAX_DIST
    q_rest = jnp.min(qg_min.reshape(b, n, t // LANES)[:, :, 1:], axis=-1)
    kb = kb_max.reshape(b, n, nsub)
    k_first = jnp.max(kb[:, :, :nsub - 1], axis=-1)
    corner = ((q_min[:, :, None] - k_first[:, None, :] >= MAX_DIST)
              & (q_rest[:, :, None] - kb[:, None, :, nsub - 1] >= MAX_DIST))
    counts = []
    for qi in range(n):
        ok = [far[:, qi, qi - 2 - 2 * p] & corner[:, qi, qi - 1 - 2 * p] for p in range(qi // 2)]
        first_bad = [jnp.where(o, qi // 2, p) for p, o in enumerate(ok)]
        counts.append(functools.reduce(jnp.minimum, first_bad, jnp.full((b,), qi // 2, jnp.int32)))
    far_pairs = jnp.stack(counts, axis=1).astype(jnp.int32)
    return (q_min, k_max, qg_min, kb_max, far_pairs, positions.reshape(b, n, 1, t), positions.reshape(b, s, 1))


def _moba_attn(qk, vt, bias_table, pos_tables, t, npairs, bounded):
    b, s, _ = qk.shape
    nb = s // MOBA_BLOCK
    ngroups = MOBA_HEADS * MOBA_HEAD_DIM // (LANES * npairs)
    w = npairs * LANES
    q_min, k_max, qg_min, kb_max, far_pairs, pos_row, pos_col = pos_tables
    grid_spec = pltpu.PrefetchScalarGridSpec(
        num_scalar_prefetch=5,
        grid=(b, ngroups, s // t),
        in_specs=[pl.BlockSpec((1, t, w), lambda bi, g, qi, *_: (bi, qi, g)),
                  pl.BlockSpec((1, s, w), lambda bi, g, qi, *_: (bi, 0, ngroups + g)),
                  pl.BlockSpec((1, w, s), lambda bi, g, qi, *_: (bi, g, 0)),
                  pl.BlockSpec((npairs, 2, LANES), lambda bi, g, qi, *_: (g, 0, 0)),
                  pl.BlockSpec((1, 1, 1, t), lambda bi, g, qi, *_: (bi, qi, 0, 0)),
                  pl.BlockSpec((1, s, 1), lambda bi, g, qi, *_: (bi, 0, 0))],
        out_specs=pl.BlockSpec((1, t, w), lambda bi, g, qi, *_: (bi, qi, g)),
        scratch_shapes=[pltpu.VMEM((LANES, w), F32), pltpu.VMEM((2 * npairs, nb, t), F32)],
    )
    return pl.pallas_call(
        functools.partial(_moba_attn_kernel, nb=nb, t=t, npairs=npairs, bounded=bounded),
        grid_spec=grid_spec,
        out_shape=jax.ShapeDtypeStruct((b, s, MOBA_HEADS * MOBA_HEAD_DIM), BF16),
        compiler_params=_params("arbitrary", "arbitrary", "arbitrary"),
        name="moba_attention",
    )(q_min, k_max, qg_min, kb_max, far_pairs, qk, qk, vt, bias_table, pos_row, pos_col)


def _rope_tab_kernel(pos_ref, inv_ref, c_ref, s_ref):
    half = MLA_ROPE // 2
    ang = pos_ref[0].astype(F32) * inv_ref[...]
    lane = lax.broadcasted_iota(jnp.int32, (1, LANES), 1)
    sn = jnp.sin(ang)
    c_ref[0] = jnp.cos(ang)
    s_ref[0] = jnp.where(lane < MLA_NOPE + half, -sn, sn)


def _rope_tables(pos_col, inv_lane, tm):
    b, s, _ = pos_col.shape
    out = pl.BlockSpec((1, tm, LANES), lambda bi, i: (bi, i, 0))
    shape = jax.ShapeDtypeStruct((b, s, LANES), F32)
    return pl.pallas_call(
        _rope_tab_kernel,
        grid=(b, s // tm),
        in_specs=[pl.BlockSpec((1, tm, 1), lambda bi, i: (bi, i, 0)),
                  pl.BlockSpec((1, LANES), lambda bi, i: (0, 0))],
        out_specs=[out, out],
        out_shape=[shape, shape],
        compiler_params=_params("arbitrary", "arbitrary"),
        name="rope_tables",
    )(pos_col, inv_lane)


def _mla_q_kernel(x_ref, sh_ref, sc_ref, g_ref, wdq_ref, qag_ref, wuq_ref, wuqp_ref, qg_ref, qgp_ref,
                  c_ref, s_ref, o_ref):
    h = _rms_mod(x_ref[0], g_ref[...], _mod_row(sh_ref), _mod_row(sc_ref)).astype(BF16)
    dq = _dot(h, wdq_ref[...])
    dqn = (dq * lax.rsqrt(jnp.mean(dq * dq, axis=-1, keepdims=True) + EPS) * qag_ref[...]).astype(BF16)
    uq = _dot(dqn, wuq_ref[...])
    up = _dot(dqn, wuqp_ref[...])
    cg = c_ref[0] * qg_ref[...]
    sg = s_ref[0] * qgp_ref[...]
    scale = MLA_QK ** -0.5 * LOG2E
    for hd in range(MLA_HEADS):
        sl = slice(hd * LANES, (hd + 1) * LANES)
        u = uq[:, sl]
        inv = lax.rsqrt(jnp.sum(u * u, axis=-1, keepdims=True) * (1.0 / MLA_QK) + EPS) * scale
        o_ref[0, :, sl] = ((u * cg + up[:, sl] * sg) * inv).astype(BF16)


def _mla_q(x, shift, scale, g, wdq, qag, wuq_p, wuq_partner, qg_p, qg_partner, cos, sin, tm):
    b, s, d = x.shape
    r = wdq.shape[1]
    nq = wuq_p.shape[1]
    tab = pl.BlockSpec((1, tm, LANES), lambda bi, i: (bi, i, 0))
    full = lambda shp: pl.BlockSpec(shp, lambda bi, i: (0, 0))
    return pl.pallas_call(
        _mla_q_kernel,
        grid=(b, s // tm),
        in_specs=[pl.BlockSpec((1, tm, d), lambda bi, i: (bi, i, 0)), _mod_spec(shift, d), _mod_spec(scale, d),
                  full((1, d)),
                  full((d, r)), full((1, r)), full((r, nq)), full((r, nq)), full((1, LANES)), full((1, LANES)),
                  tab, tab],
        out_specs=pl.BlockSpec((1, tm, nq), lambda bi, i: (bi, i, 0)),
        out_shape=jax.ShapeDtypeStruct((b, s, nq), BF16),
        compiler_params=_params("arbitrary", "arbitrary"),
        name="mla_q_path",
    )(x, shift[0], scale[0], g, wdq, qag, wuq_p, wuq_partner, qg_p, qg_partner, cos, sin)


def _mla_qkv_kernel(x_ref, qsh_ref, qsc_ref, qn_ref, wdq_ref, qag_ref, wuq_ref, wuqp_ref, qg_ref, qgp_ref,
                    ksh_ref, ksc_ref, kn_ref, wdkv_ref, kvg_ref, wuk_ref, wuvt_ref, kg_ref, kgp_ref,
                    c_ref, s_ref, q_ref, k_ref, vt_ref):
    x = x_ref[0]
    xn = x * lax.rsqrt(jnp.mean(x * x, axis=-1, keepdims=True) + EPS)
    cos, sin = c_ref[0], s_ref[0]
    h = ((xn * qn_ref[...]) * (1.0 + _mod_row(qsc_ref)) + _mod_row(qsh_ref)).astype(BF16)
    dq = _dot(h, wdq_ref[...])
    dqn = (dq * lax.rsqrt(jnp.mean(dq * dq, axis=-1, keepdims=True) + EPS) * qag_ref[...]).astype(BF16)
    uq = _dot(dqn, wuq_ref[...])
    up = _dot(dqn, wuqp_ref[...])
    hs = ((xn * kn_ref[...]) * (1.0 + _mod_row(ksc_ref)) + _mod_row(ksh_ref)).astype(BF16)
    ck = _dot(hs, wdkv_ref[...])
    cg = cos * qg_ref[...]
    sg = sin * qgp_ref[...]
    scale = MLA_QK ** -0.5 * LOG2E

    def q_heads(lo, hi):
        for hd in range(lo, hi):
            sl = slice(hd * LANES, (hd + 1) * LANES)
            u = uq[:, sl]
            inv = lax.rsqrt(jnp.sum(u * u, axis=-1, keepdims=True) * (1.0 / MLA_QK) + EPS) * scale
            q_ref[0, :, sl] = ((u * cg + up[:, sl] * sg) * inv).astype(BF16)

    q_heads(0, MLA_HEADS // 2)
    ckv = ck[:, :MLA_KV_LORA]
    ckvn = (ckv * lax.rsqrt(jnp.mean(ckv * ckv, axis=-1, keepdims=True) + EPS) * kvg_ref[...]).astype(BF16)
    kn = _dot(ckvn, wuk_ref[...])
    vt_ref[0] = _dot_nt(wuvt_ref[...], ckvn).astype(BF16)
    q_heads(MLA_HEADS // 2, MLA_HEADS)
    lane = lax.broadcasted_iota(jnp.int32, (1, LANES), 1)
    on_rope = (lane >= MLA_NOPE) & (lane < MLA_QK)
    tail = ck[:, MLA_KV_LORA:MLA_KV_LORA + LANES]
    kpe = jnp.where(on_rope, pltpu.roll(tail, MLA_NOPE, axis=1), 0.0)
    partner = pltpu.roll(tail, MLA_ROPE, axis=1)
    kcg = cos * kg_ref[...]
    ksg = sin * kgp_ref[...]
    rot = kpe * kcg + partner * ksg
    ss_pe = jnp.sum(kpe * kpe, axis=-1, keepdims=True)
    for hd in range(MLA_HEADS):
        sl = slice(hd * LANES, (hd + 1) * LANES)
        nope = kn[:, sl]
        inv = lax.rsqrt((jnp.sum(nope * nope, axis=-1, keepdims=True) + ss_pe) * (1.0 / MLA_QK) + EPS)
        k_ref[0, :, sl] = ((nope * kcg + rot) * inv).astype(BF16)


def _mla_qkv(x, q_args, kv_args, cos, sin, tm):
    b, s, d = x.shape
    nq = q_args[5].shape[1]
    nk = kv_args[5].shape[1]
    nv = kv_args[6].shape[0]
    tab = pl.BlockSpec((1, tm, LANES), lambda bi, i: (bi, i, 0))
    full = lambda a: pl.BlockSpec(a.shape, lambda bi, i: (0, 0))

    def path(args):
        shift, scale = args[:2]
        return [shift[0], scale[0], *args[2:]], [_mod_spec(shift, d), _mod_spec(scale, d)] + [full(a) for a in args[2:]]

    q_ins, q_specs = path(q_args)
    kv_ins, kv_specs = path(kv_args)
    return pl.pallas_call(
        _mla_qkv_kernel,
        grid=(b, s // tm),
        in_specs=[pl.BlockSpec((1, tm, d), lambda bi, i: (bi, i, 0))] + q_specs + kv_specs + [tab, tab],
        out_specs=[pl.BlockSpec((1, tm, nq), lambda bi, i: (bi, i, 0)),
                   pl.BlockSpec((1, tm, nk), lambda bi, i: (bi, i, 0)),
                   pl.BlockSpec((1, nv, tm), lambda bi, i: (bi, 0, i))],
        out_shape=[jax.ShapeDtypeStruct((b, s, nq), BF16), jax.ShapeDtypeStruct((b, s, nk), BF16),
                   jax.ShapeDtypeStruct((b, nv, s), BF16)],
        compiler_params=_params("arbitrary", "arbitrary"),
        name="mla_qkv_proj",
    )(x, *q_ins, *kv_ins, cos, sin)


def _mla_attn_kernel(q_ref, k_ref, vt_ref, o_ref, *, t, nh, bounded):
    qi = pl.program_id(2)
    q2 = q_ref[0]
    q_heads = [q2[:, h * LANES:(h + 1) * LANES] for h in range(nh)]

    def chunk(j, carry, masked, width=1):
        carry = list(carry)
        start = pl.multiple_of(j * t, t)
        k2 = k_ref[0, pl.ds(start, width * t), :]
        vt = vt_ref[0, :, pl.ds(start, width * t)]
        s_heads = [_dot_nt(k2[:, h * LANES:(h + 1) * LANES], q_heads[h]) for h in range(nh)]
        if masked:
            row = lax.broadcasted_iota(jnp.int32, (t, t), 0)
            col = lax.broadcasted_iota(jnp.int32, (t, t), 1)
            allowed = row <= col
        for h in range(nh):
            m, l, acc = carry[h]
            vt_h = vt[h * MLA_V:(h + 1) * MLA_V, :]
            if width > 1:
                half = t // 2
                for r in range(2 * width):
                    p = jnp.exp2(s_heads[h][r * half:(r + 1) * half])
                    l = l + jnp.sum(p, axis=0, keepdims=True)
                    acc = acc + _dot(vt_h[:, r * half:(r + 1) * half], p.astype(BF16))
                carry[h] = (m, l, acc)
                continue
            s = jnp.where(allowed, s_heads[h], NEG) if masked else s_heads[h]
            carry[h] = _softmax_step_t(s, m, l, acc, vt_h, bounded)
        return tuple(carry)

    def own_chunk_bounded(carry):
        carry = list(carry)
        half = t // 2
        start = pl.multiple_of(qi * t, t)
        k2 = k_ref[0, pl.ds(start, t), :]
        vt = vt_ref[0, :, pl.ds(start, t)]
        causal = (lax.broadcasted_iota(jnp.int32, (half, half), 0) <= lax.broadcasted_iota(jnp.int32, (half, half), 1))
        quads = []
        for h in range(nh):
            kh = k2[:, h * LANES:(h + 1) * LANES]
            qh = q_heads[h]
            quads.append((_dot_nt(kh[:half], qh[:half]), _dot_nt(kh[:half], qh[half:]), _dot_nt(kh[half:], qh[half:])))
        for h in range(nh):
            m, l, acc = carry[h]
            s00, s01, s11 = quads[h]
            e00 = jnp.exp2(jnp.where(causal, s00, NEG))
            e01 = jnp.exp2(s01)
            e11 = jnp.exp2(jnp.where(causal, s11, NEG))
            vt0 = vt[h * MLA_V:(h + 1) * MLA_V, :half]
            vt1 = vt[h * MLA_V:(h + 1) * MLA_V, half:]
            l_add = jnp.concatenate(
                [jnp.sum(e00, axis=0, keepdims=True),
                 jnp.sum(e01, axis=0, keepdims=True) + jnp.sum(e11, axis=0, keepdims=True)], axis=1)
            acc_add = jnp.concatenate(
                [_dot(vt0, e00.astype(BF16)), _dot(vt0, e01.astype(BF16)) + _dot(vt1, e11.astype(BF16))], axis=1)
            carry[h] = (m, l + l_add, acc + acc_add)
        return tuple(carry)

    init = (jnp.full((1, t), NEG, F32), jnp.zeros((1, t), F32), jnp.zeros((MLA_V, t), F32))
    if bounded:
        pairs = qi // 2
        carry = lax.fori_loop(0, pairs, lambda p, c: chunk(2 * p, c, False, width=2), (init,) * nh)
        carry = lax.fori_loop(2 * pairs, qi, lambda j, c: chunk(j, c, False), carry)
        carry = own_chunk_bounded(carry)
    else:
        carry = lax.fori_loop(0, qi, lambda j, c: chunk(j, c, False), (init,) * nh)
        carry = chunk(qi, carry, True)
    out_t = jnp.concatenate([acc / l for (_, l, acc) in carry], axis=0)
    o_ref[0] = out_t.T.astype(BF16)


def _mla_attn(q, k, vt, t, nh, bounded):
    b, s, _ = q.shape
    return pl.pallas_call(
        functools.partial(_mla_attn_kernel, t=t, nh=nh, bounded=bounded),
        grid=(b, MLA_HEADS // nh, s // t),
        in_specs=[pl.BlockSpec((1, t, nh * LANES), lambda bi, g, qi: (bi, qi, g)),
                  pl.BlockSpec((1, s, nh * LANES), lambda bi, g, qi: (bi, 0, g)),
                  pl.BlockSpec((1, nh * MLA_V, s), lambda bi, g, qi: (bi, g, 0))],
        out_specs=pl.BlockSpec((1, t, nh * MLA_V), lambda bi, g, qi: (bi, qi, g)),
        out_shape=jax.ShapeDtypeStruct((b, s, MLA_HEADS * MLA_V), BF16),
        compiler_params=_params("arbitrary", "arbitrary", "arbitrary"),
        name="mla_attention",
    )(q, k, vt)


def _t5_bias_table(rel_bias):
    n = jnp.arange(MASKED_DIST)
    max_exact = REL_BUCKETS // 2
    nf = jnp.maximum(n, 1).astype(F32)
    large = max_exact + (jnp.log(nf / max_exact) / math.log(REL_MAX_DIST / max_exact)
                         * (REL_BUCKETS - max_exact)).astype(jnp.int32)
    large = jnp.minimum(large, REL_BUCKETS - 1)
    bucket = jnp.where(n < max_exact, n, large)
    table = jnp.concatenate([rel_bias.T[:, bucket] * LOG2E, jnp.full((MOBA_HEADS, 1), NEG, F32)], axis=1)
    return table.reshape(MOBA_HEADS // 2, 2, BIAS_TABLE)


def _pad_heads(w, width):
    k = w.shape[0]
    w3 = w.reshape(k, -1, width)
    return jnp.pad(w3, ((0, 0), (0, 0), (0, LANES - width))).reshape(k, -1)


def _pad_lanes(g):
    return jnp.pad(g, (0, LANES - g.shape[0])).reshape(1, LANES)


def _swap_rope_halves(a):
    half = MLA_ROPE // 2
    return jnp.concatenate([jnp.zeros_like(a[..., :MLA_NOPE]), a[..., MLA_NOPE + half:],
                            a[..., MLA_NOPE:MLA_NOPE + half]], axis=-1)


def _scores_bounded(q_gain, k_gain, dim, scale, extra=0.0):
    bound = dim * jnp.max(jnp.abs(q_gain)) * jnp.max(jnp.abs(k_gain)) * scale * BOUND_SLACK + extra
    return bound <= SCORE_BOUND_MAX


def kernel(x, c, positions, ada_w, ada_b, norm_g, ffn_w_gate, ffn_w_up, ffn_w_down, rel_bias,
           moba_w_qkv, moba_q_g, moba_k_g, moba_w_o, kv_ada_w, kv_ada_b, kv_norm_g, w_dkv,
           kv_a_norm_g, w_uk, w_uv, mla_k_g, mla_w_dq, mla_q_a_norm_g, mla_w_uq, mla_q_g, mla_w_o):
    b, s, d = x.shape
    depth = ada_w.shape[0]
    n_moba = moba_w_qkv.shape[0]
    assert b <= SUBLANES and s % max(FFN_ROWS, MOBA_TILE, MLA_TILE) == 0 and MOBA_TILE == 2 * MOBA_BLOCK
    tm_ffn = FFN_ROWS
    tm_proj = PROJ_ROWS

    c_pad = jnp.pad(c, ((0, SUBLANES - b), (0, 0)))
    mods = _mods(c_pad, ada_w, ada_b.reshape(depth, 1, -1), tn=ADALN_COLS)
    kv_mods = _mods(c_pad, kv_ada_w[None], kv_ada_b.reshape(1, 1, -1), tn=d)

    w_gate, w_up, w_down = (w.astype(BF16) for w in (ffn_w_gate, ffn_w_up, ffn_w_down))

    for layer in range(depth):
        sh1, sc1, g1, sh2, sc2, g2, sh3, sc3, g3 = [(mods, layer, k) for k in range(mods.shape[-1] // d)]
        ng = norm_g[layer]

        def ffn(xin, idx, sh, sc, gt, mixer=None):
            return _ffn(xin, sh, sc, gt, ng[2 * idx:2 * idx + 1], w_gate, w_up, w_down, (layer, idx),
                        tm_ffn, FFN_CHUNK, mixer)

        x = ffn(x, 0, sh1, sc1, g1)

        if layer < n_moba:
            n_qk = 2 * MOBA_HEADS * MOBA_HEAD_DIM
            q_scale = MOBA_HEAD_DIM ** -0.5 * LOG2E
            head_gain = jnp.concatenate([jnp.tile(moba_q_g[layer] * q_scale, MOBA_HEADS),
                                         jnp.tile(moba_k_g[layer], MOBA_HEADS)]).reshape(1, -1)
            w_qkv = moba_w_qkv[layer].astype(BF16)
            qk, vt = _moba_proj(x, sh2, sc2, ng[1:2], w_qkv[:, :n_qk], w_qkv[:, n_qk:].T, head_gain,
                                2 * tm_proj, PROJ_COLS)
            bias_table = _t5_bias_table(rel_bias)
            moba = functools.partial(_moba_attn, qk, vt, bias_table, _moba_position_tables(positions, MOBA_TILE),
                                     MOBA_TILE, ATTN_HEADS // 2)
            ok = _scores_bounded(moba_q_g[layer], moba_k_g[layer], MOBA_HEAD_DIM, q_scale,
                                 jnp.max(jnp.abs(rel_bias)) * LOG2E)
            o = lax.cond(ok, lambda: moba(True), lambda: moba(False))
            w_o = moba_w_o[layer]
        else:
            j = layer - n_moba
            if j == 0:
                half = MLA_ROPE // 2
                inv = ROPE_BASE ** (-jnp.arange(half, dtype=F32) / half)
                inv_lane = jnp.zeros((LANES,), F32).at[MLA_NOPE:MLA_QK].set(jnp.tile(inv, 2)).reshape(1, LANES)
                cos, sin = _rope_tables(positions.reshape(b, s, 1), inv_lane, tm_proj)
                kv_sh, kv_sc = (kv_mods, 0, 0), (kv_mods, 0, 1)
                w_dkv16 = w_dkv.astype(BF16)
                w_kpe = w_dkv16[:, MLA_KV_LORA:]
                w_kpe_swapped = jnp.concatenate([w_kpe[:, half:], w_kpe[:, :half]], axis=1)
                wdkv_p = jnp.pad(jnp.concatenate([w_dkv16, w_kpe_swapped], axis=1),
                                 ((0, 0), (0, LANES - 2 * MLA_ROPE)))
                kv_args = (kv_sh, kv_sc, kv_norm_g.reshape(1, d), wdkv_p, kv_a_norm_g.reshape(1, -1),
                           _pad_heads(w_uk.astype(BF16), MLA_NOPE), w_uv.astype(BF16).T, _pad_lanes(mla_k_g),
                           _pad_lanes(_swap_rope_halves(mla_k_g)))
            w_uq = mla_w_uq[j].astype(BF16)
            w_uq_partner = _swap_rope_halves(w_uq.reshape(w_uq.shape[0], MLA_HEADS, MLA_QK)).reshape(w_uq.shape)
            q_args = (sh2, sc2, ng[1:2], mla_w_dq[j].astype(BF16), mla_q_a_norm_g[j].reshape(1, -1),
                      _pad_heads(w_uq, MLA_QK), _pad_heads(w_uq_partner, MLA_QK),
                      _pad_lanes(mla_q_g[j]), _pad_lanes(_swap_rope_halves(mla_q_g[j])))
            if j == 0:
                q, shared_k, shared_v = _mla_qkv(x, q_args, kv_args, cos, sin, tm_proj)
            else:
                q = _mla_q(x, *q_args, cos, sin, tm_proj)
            mla = functools.partial(_mla_attn, q, shared_k, shared_v, MLA_TILE, ATTN_HEADS)
            ok = _scores_bounded(mla_q_g[j], mla_k_g, MLA_QK, MLA_QK ** -0.5 * LOG2E)
            o = lax.cond(ok, lambda: mla(True), lambda: mla(False))
            w_o = mla_w_o[j]

        x = ffn(x, 1, sh3, sc3, g3, mixer=(o, g2, w_o.astype(BF16)))
    return x
```

```python
import functools
import math

import jax
import jax.numpy as jnp
from jax import lax
from jax.experimental import pallas as pl
from jax.experimental.pallas import tpu as pltpu

F32 = jnp.float32
BF16 = jnp.bfloat16

LANES = 128
SUBLANES = 8
VMEM_LIMIT_BYTES = 56 * 1024 * 1024

MOBA_HEADS = 16
MOBA_HEAD_DIM = 64
MOBA_BLOCK = 256
MOBA_TOPK = 3
REL_BUCKETS = 32
REL_MAX_DIST = 128
MLA_HEADS = 16
MLA_KV_LORA = 256
MLA_NOPE = 64
MLA_ROPE = 32
MLA_V = 64
MLA_QK = MLA_NOPE + MLA_ROPE
ROPE_BASE = 10000.0
EPS = 1e-6
NEG = -1e30

FFN_ROWS = 1024
FFN_CHUNK = 256
PROJ_ROWS = 512
PROJ_COLS = 512
ADALN_COLS = 2304
MOBA_TILE = 512
MLA_TILE = 512
ATTN_HEADS = 8
BOUND_SLACK = 1.01
LOG2E = math.log2(math.e)
BIAS_TABLE = LANES
MAX_DIST = BIAS_TABLE - 2
MASKED_DIST = BIAS_TABLE - 1
SCORE_BOUND_MAX = 60.0


def _params(*sem):
    return pltpu.CompilerParams(dimension_semantics=sem, vmem_limit_bytes=VMEM_LIMIT_BYTES)


def _dot(a, b):
    return jnp.dot(a, b, preferred_element_type=F32)


def _dot_nt(a, b):
    return lax.dot_general(a, b, (((1,), (1,)), ((), ())), preferred_element_type=F32)


def _rms_mod(x, g, shift, scale):
    ms = jnp.mean(x * x, axis=-1, keepdims=True)
    y = x * lax.rsqrt(ms + EPS) * g
    return y * (1.0 + scale) + shift


def _mod_spec(mod, d):
    _, layer, k = mod
    return pl.BlockSpec((None, SUBLANES, d), lambda bi, *_: (layer, 0, k))


def _mod_row(ref):
    return ref[pl.ds(pl.program_id(0), 1), :]


def _mods_kernel(c_ref, w_ref, b_ref, o_ref):
    c = c_ref[...]
    ca = c * jax.nn.sigmoid(c)
    w = w_ref[0]
    w_hi = w.astype(BF16)
    w_lo = (w - w_hi.astype(F32)).astype(BF16)
    ca_hi = ca.astype(BF16)
    ca_lo = (ca - ca_hi.astype(F32)).astype(BF16)
    o_ref[0] = _dot(ca_hi, w_hi) + _dot(ca_lo, w_hi) + _dot(ca_hi, w_lo) + b_ref[0]


def _mods(c_pad, w, b, tn):
    nl, d, n = w.shape
    return pl.pallas_call(
        _mods_kernel,
        grid=(nl, n // tn),
        in_specs=[pl.BlockSpec((SUBLANES, d), lambda l, j: (0, 0)),
                  pl.BlockSpec((1, d, tn), lambda l, j: (l, 0, j)),
                  pl.BlockSpec((1, 1, tn), lambda l, j: (l, 0, j))],
        out_specs=pl.BlockSpec((1, SUBLANES, tn), lambda l, j: (l, 0, j)),
        out_shape=jax.ShapeDtypeStruct((nl, SUBLANES, n), F32),
        compiler_params=_params("arbitrary", "arbitrary"),
        name="adaln_mods",
    )(c_pad, w, b)


def _ffn_kernel(*refs, tf, with_mixer):
    if with_mixer:
        x_ref, a_ref, gm_ref, wo_ref = refs[:4]
        refs = refs[4:]
        x = x_ref[0] + _mod_row(gm_ref) * _dot(a_ref[0], wo_ref[...])
    else:
        x_ref = refs[0]
        refs = refs[1:]
        x = x_ref[0]
    sh_ref, sc_ref, gt_ref, g_ref, wg_ref, wu_ref, wd_ref, o_ref = refs
    h = _rms_mod(x, g_ref[...], _mod_row(sh_ref), _mod_row(sc_ref)).astype(BF16)
    acc = None
    for f in range(wg_ref.shape[1] // tf):
        cols = slice(f * tf, (f + 1) * tf)
        g = _dot(h, wg_ref[:, cols])
        u = _dot(h, wu_ref[:, cols])
        a = (g * jax.nn.sigmoid(g) * u).astype(BF16)
        d = _dot(a, wd_ref[cols, :])
        acc = d if acc is None else acc + d
    o_ref[0] = x + (0.5 * _mod_row(gt_ref)) * acc


def _ffn(x, shift, scale, gate, g, wg, wu, wd, which, tm, tf, mixer=None):
    b, s, d = x.shape
    f = wg.shape[-1]
    row = pl.BlockSpec((1, tm, d), lambda bi, i: (bi, i, 0))
    full = lambda shp: pl.BlockSpec(shp, lambda bi, i: (0, 0), pipeline_mode=pl.Buffered(1))
    picked = lambda r, c: pl.BlockSpec((None, None, r, c), lambda bi, i: (*which, 0, 0),
                                       pipeline_mode=pl.Buffered(1))
    ins, specs = [x], [row]
    if mixer is not None:
        a, gate_mix, w_o = mixer
        k = a.shape[2]
        ins += [a, gate_mix[0], w_o]
        specs += [pl.BlockSpec((1, tm, k), lambda bi, i: (bi, i, 0)), _mod_spec(gate_mix, d), full((k, d))]
    ins += [shift[0], scale[0], gate[0], g, wg, wu, wd]
    specs += [_mod_spec(shift, d), _mod_spec(scale, d), _mod_spec(gate, d), full((1, d)),
              picked(d, f), picked(d, f), picked(f, d)]
    return pl.pallas_call(
        functools.partial(_ffn_kernel, tf=tf, with_mixer=mixer is not None),
        grid=(b, s // tm),
        in_specs=specs,
        out_specs=row,
        out_shape=jax.ShapeDtypeStruct((b, s, d), F32),
        compiler_params=_params("arbitrary", "arbitrary"),
        name="ffn_swiglu",
    )(*ins)


def _headnorm_pair(y, g, dim):
    lane = lax.broadcasted_iota(jnp.int32, (1, LANES), 1)
    left = lane < dim
    y2 = y * y
    ss_a = jnp.sum(jnp.where(left, y2, 0.0), axis=-1, keepdims=True)
    ss_b = jnp.sum(jnp.where(left, 0.0, y2), axis=-1, keepdims=True)
    inv = jnp.where(left, lax.rsqrt(ss_a * (1.0 / dim) + EPS), lax.rsqrt(ss_b * (1.0 / dim) + EPS))
    return y * inv * g


def _moba_proj_kernel(x_ref, sh_ref, sc_ref, g_ref, w_ref, wvt_ref, hg_ref, qk_ref, vt_ref, *, tn):
    h = _rms_mod(x_ref[0], g_ref[...], _mod_row(sh_ref), _mod_row(sc_ref)).astype(BF16)
    for n in range(w_ref.shape[1] // tn):
        res = _dot(h, w_ref[:, n * tn:(n + 1) * tn])
        for s in range(tn // LANES):
            sl = slice(n * tn + s * LANES, n * tn + (s + 1) * LANES)
            qk_ref[0, :, sl] = _headnorm_pair(res[:, s * LANES:(s + 1) * LANES], hg_ref[:, sl],
                                              MOBA_HEAD_DIM).astype(BF16)
    for n in range(wvt_ref.shape[0] // tn):
        rows = slice(n * tn, (n + 1) * tn)
        vt_ref[0, rows, :] = _dot_nt(wvt_ref[rows, :], h).astype(BF16)


def _moba_proj(x, shift, scale, g, w_qk, w_vt, head_gain, tm, tn):
    b, s, d = x.shape
    nqk = w_qk.shape[1]
    nv = w_vt.shape[0]
    full = lambda shp: pl.BlockSpec(shp, lambda bi, i: (0, 0))
    return pl.pallas_call(
        functools.partial(_moba_proj_kernel, tn=tn),
        grid=(b, s // tm),
        in_specs=[pl.BlockSpec((1, tm, d), lambda bi, i: (bi, i, 0)), _mod_spec(shift, d), _mod_spec(scale, d),
                  full((1, d)),
                  full((d, nqk)), full((nv, d)), full((1, nqk))],
        out_specs=[pl.BlockSpec((1, tm, nqk), lambda bi, i: (bi, i, 0)),
                   pl.BlockSpec((1, nv, tm), lambda bi, i: (bi, 0, i))],
        out_shape=[jax.ShapeDtypeStruct((b, s, nqk), BF16), jax.ShapeDtypeStruct((b, nv, s), BF16)],
        compiler_params=_params("arbitrary", "arbitrary"),
        name="moba_qkv_proj",
    )(x, shift[0], scale[0], g, w_qk, w_vt, head_gain)


def _softmax_step_t(s, m, l, acc, vt, bounded):
    if bounded:
        p = jnp.exp2(s)
        return m, l + jnp.sum(p, axis=0, keepdims=True), acc + _dot(vt, p.astype(BF16))
    m_new = jnp.maximum(m, jnp.max(s, axis=0, keepdims=True))
    alpha = jnp.exp2(m - m_new)
    p = jnp.exp2(s - m_new)
    l_new = alpha * l + jnp.sum(p, axis=0, keepdims=True)
    acc_new = acc * alpha + _dot(vt, p.astype(BF16))
    return m_new, l_new, acc_new


def _moba_attn_kernel(qmin_ref, kmax_ref, qgmin_ref, kbmax_ref, fpairs_ref, q_ref, k_ref, vt_ref, bt_ref, pq_ref, pk_ref, o_ref,
                      km_scr, sel_scr, *, nb, t, npairs, bounded):
    blk = MOBA_BLOCK
    hd = MOBA_HEAD_DIM
    nsub = t // blk
    nh = 2 * npairs
    bi = pl.program_id(0)
    qi = pl.program_id(2)
    lane = lax.broadcasted_iota(jnp.int32, (1, LANES), 1)
    left = lane < hd

    @pl.when(qi == 0)
    def _():
        km_scr[...] = jnp.zeros_like(km_scr)
        for r in range(nb):
            kblk = k_ref[0, r * blk:(r + 1) * blk, :].astype(F32)
            km_scr[r:r + 1, :] = jnp.sum(kblk, axis=0, keepdims=True) * (1.0 / blk)

    q_heads = []
    for p in range(npairs):
        q2 = q_ref[0, :, p * LANES:(p + 1) * LANES]
        zero = jnp.zeros_like(q2)
        q_heads += [jnp.where(left, q2, zero), jnp.where(left, zero, q2)]

    col = lax.broadcasted_iota(jnp.int32, (1, t), 1)
    q_idx = qi * t + col
    q_sub = sum((col >= r * blk).astype(jnp.int32) for r in range(1, nsub))
    q_blk = qi * nsub + q_sub
    q_blk_start = q_blk * blk

    blk_row = lax.broadcasted_iota(jnp.int32, (nb, t), 0)
    for h in range(nh):
        km = km_scr[:, (h // 2) * LANES:(h // 2 + 1) * LANES]
        km_hi = km.astype(BF16)
        km_lo = (km - km_hi.astype(F32)).astype(BF16)
        gate = (_dot_nt(km_hi, q_heads[h]) + _dot_nt(km_lo, q_heads[h]))[:nb]
        gate = jnp.where(blk_row < q_blk, gate, -jnp.inf)
        sel = jnp.zeros((nb, t), F32)
        for _ in range(MOBA_TOPK):
            mx = jnp.max(gate, axis=0, keepdims=True)
            idx = jnp.min(jnp.where(gate == mx, blk_row, nb), axis=0, keepdims=True)
            pick = (blk_row == idx) & (mx > -jnp.inf)
            sel = jnp.where(pick, 1.0, sel)
            gate = jnp.where(pick, -jnp.inf, gate)
        sel_scr[h] = sel

    pq = pq_ref[0, 0]

    def gather_bias(h, dist):
        table = jnp.broadcast_to(bt_ref[h // 2, h % 2:h % 2 + 1, :], (dist.shape[0], LANES))
        return jnp.concatenate(
            [jnp.take_along_axis(table, dist[:, g * LANES:(g + 1) * LANES], axis=1, mode="promise_in_bounds")
             for g in range(dist.shape[1] // LANES)], axis=1)

    def far_bias(h):
        return bt_ref[h // 2, h % 2:h % 2 + 1, MAX_DIST:MAX_DIST + 1]

    def chosen_rows(h, j, lanes=slice(None), width=1):
        return [sel_scr[h, pl.ds(j * nsub + r, 1), lanes] > 0.0 for r in range(width * nsub)]

    def own_chunk_bounded(carry):
        assert nsub == 2
        carry = list(carry)
        start = pl.multiple_of(qi * t, t)
        k2 = k_ref[0, pl.ds(start, t), :]
        vt = vt_ref[0, :, pl.ds(start, t)]
        pk0 = pk_ref[0, pl.ds(start, blk), :]
        pk1 = pk_ref[0, pl.ds(start + blk, blk), :]
        causal = (lax.broadcasted_iota(jnp.int32, (blk, blk), 0) <= lax.broadcasted_iota(jnp.int32, (blk, blk), 1))
        d00 = jnp.where(causal, jnp.clip(pq[:, :blk] - pk0, 0, MAX_DIST), MASKED_DIST)
        d01 = jnp.clip(pq[:, blk:] - pk0, 0, MAX_DIST)
        d11 = jnp.where(causal, jnp.clip(pq[:, blk:] - pk1, 0, MAX_DIST), MASKED_DIST)
        quads = []
        for h in range(nh):
            kh = k2[:, (h // 2) * LANES:(h // 2 + 1) * LANES]
            qh = q_heads[h]
            quads.append((_dot_nt(kh[:blk], qh[:blk]), _dot_nt(kh[:blk], qh[blk:]), _dot_nt(kh[blk:], qh[blk:])))
        for h in range(nh):
            m, l, acc = carry[h]
            s00, s01, s11 = quads[h]
            e00 = jnp.exp2(s00 + gather_bias(h, d00))
            e01 = jnp.exp2(s01 + gather_bias(h, d01))
            e11 = jnp.exp2(s11 + gather_bias(h, d11))
            w01 = jnp.where(chosen_rows(h, qi, slice(blk, t))[0], 1.0, 0.0)
            vt0 = vt[h * hd:(h + 1) * hd, :blk]
            vt1 = vt[h * hd:(h + 1) * hd, blk:]
            l_add = jnp.concatenate(
                [jnp.sum(e00, axis=0, keepdims=True),
                 jnp.sum(e01, axis=0, keepdims=True) * w01 + jnp.sum(e11, axis=0, keepdims=True)], axis=1)
            acc_add = jnp.concatenate(
                [_dot(vt0, e00.astype(BF16)),
                 _dot(vt0, e01.astype(BF16)) * w01 + _dot(vt1, e11.astype(BF16))], axis=1)
            carry[h] = (m, l + l_add, acc + acc_add)
        return tuple(carry)

    def chunk(j, carry, mode, width=1):
        last = width * nsub - 1
        carry = list(carry)
        start = pl.multiple_of(j * t, t)
        s_heads = [_dot_nt(k_ref[0, pl.ds(start, width * t), (h // 2) * LANES:(h // 2 + 1) * LANES], q_heads[h])
                   for h in range(nh)]
        if mode in ("near", "diag"):
            dist = jnp.clip(pq - pk_ref[0, pl.ds(start, t), :], 0, MAX_DIST)
        if mode == "corner":
            dist_c = jnp.clip(pq[:, :LANES] - pk_ref[0, pl.ds(start + last * blk, blk), :], 0, MAX_DIST)
        if mode == "diag":
            key_idx = start + lax.broadcasted_iota(jnp.int32, (blk, t), 0)
        for h in range(nh):
            m, l, acc = carry[h]
            s = s_heads[h]
            chosen = chosen_rows(h, j, width=width)
            if mode in ("far", "corner") and bounded:
                two_c = jnp.exp2(far_bias(h))
                for r in range(width * nsub):
                    piece = s[r * blk:(r + 1) * blk]
                    w = jnp.where(chosen[r], two_c, 0.0)
                    if mode == "corner" and r == last:
                        piece = jnp.concatenate([piece[:, :LANES] + gather_bias(h, dist_c), piece[:, LANES:]], axis=1)
                        w = jnp.where(col < LANES, jnp.where(chosen[r], 1.0, 0.0), w)
                    p = jnp.exp2(piece)
                    l = l + jnp.sum(p, axis=0, keepdims=True) * w
                    blk_start = pl.multiple_of(start + r * blk, blk)
                    acc = acc + _dot(vt_ref[0, h * hd:(h + 1) * hd, pl.ds(blk_start, blk)], p.astype(BF16)) * w
                carry[h] = (m, l, acc)
                continue
            vt_h = vt_ref[0, h * hd:(h + 1) * hd, pl.ds(start, t)]
            if mode == "far":
                pieces = [s[r * blk:(r + 1) * blk] for r in range(nsub)]
                c = far_bias(h)
                tmax = [jnp.where(chosen[r], jnp.max(pieces[r], axis=0, keepdims=True) + c, NEG)
                        for r in range(nsub)]
                m_new = functools.reduce(jnp.maximum, tmax, m)
                alpha = jnp.exp2(m - m_new)
                ps = [jnp.exp2(pieces[r] - jnp.where(chosen[r], m_new - c, jnp.inf)) for r in range(nsub)]
                l_new = alpha * l + sum(jnp.sum(p, axis=0, keepdims=True) for p in ps)
                p = jnp.concatenate(ps, axis=0) if nsub > 1 else ps[0]
                carry[h] = (m_new, l_new, acc * alpha + _dot(vt_h, p.astype(BF16)))
            else:
                s = s + gather_bias(h, dist)
                masked = []
                for r in range(nsub):
                    piece = s[r * blk:(r + 1) * blk]
                    if mode == "diag":
                        kidx = key_idx + r * blk
                        keep_own = jnp.where(kidx <= q_idx, jnp.where(kidx >= q_blk_start, piece, NEG), NEG)
                        masked.append(jnp.where(chosen[r], piece, keep_own))
                    else:
                        masked.append(jnp.where(chosen[r], piece, NEG))
                s = jnp.concatenate(masked, axis=0) if nsub > 1 else masked[0]
                carry[h] = _softmax_step_t(s, m, l, acc, vt_h, bounded)
        return tuple(carry)

    def past_chunk(j, carry):
        q_lo = qmin_ref[bi, qi]
        far = q_lo - kmax_ref[bi, j] >= MAX_DIST
        if not bounded:
            return lax.cond(far, lambda c: chunk(j, c, "far"), lambda c: chunk(j, c, "near"), carry)
        groups = t // LANES
        q_rest = functools.reduce(jnp.minimum, [qgmin_ref[bi, qi * groups + g] for g in range(1, groups)])
        k_first = functools.reduce(jnp.maximum, [kbmax_ref[bi, j * nsub + r] for r in range(nsub - 1)])
        corner = (q_lo - k_first >= MAX_DIST) & (q_rest - kbmax_ref[bi, j * nsub + nsub - 1] >= MAX_DIST)
        return lax.cond(far, lambda c: chunk(j, c, "far"),
                        lambda c: lax.cond(corner, lambda c2: chunk(j, c2, "corner"),
                                           lambda c2: chunk(j, c2, "near"), c), carry)

    init = (jnp.full((1, t), NEG, F32), jnp.zeros((1, t), F32), jnp.zeros((hd, t), F32))
    if bounded:
        carry = own_chunk_bounded((init,) * nh)
        pairs = fpairs_ref[bi, qi]
        singles = qi - 2 * pairs
        carry = lax.fori_loop(0, singles, past_chunk, carry)
        carry = lax.fori_loop(0, pairs, lambda p, c: chunk(singles + 2 * p, c, "corner", width=2), carry)
    else:
        carry = lax.fori_loop(0, qi, past_chunk, (init,) * nh)
        carry = chunk(qi, carry, "diag")
    out_t = jnp.concatenate([acc / l for (_, l, acc) in carry], axis=0)
    o_ref[0] = out_t.T.astype(BF16)


def _moba_position_tables(positions, t):
    b, s = positions.shape
    n, nsub = s // t, t // MOBA_BLOCK
    pos_tiles = positions.reshape(b, n, t)
    q_min, k_max = jnp.min(pos_tiles, axis=-1), jnp.max(pos_tiles, axis=-1)
    qg_min = jnp.min(positions.reshape(b, s // LANES, LANES), axis=-1)
    kb_max = jnp.max(positions.reshape(b, s // MOBA_BLOCK, MOBA_BLOCK), axis=-1)
    far = q_min[:, :, None] - k_max[:, None, :] >= MAX_DIST
    q_rest = jnp.min(qg_min.reshape(b, n, t // LANES)[:, :, 1:], axis=-1)
    kb = kb_max.reshape(b, n, nsub)
    k_first = jnp.max(kb[:, :, :nsub - 1], axis=-1)
    corner = ((q_min[:, :, None] - k_first[:, None, :] >= MAX_DIST)
              & (q_rest[:, :, None] - kb[:, None, :, nsub - 1] >= MAX_DIST))
    counts = []
    for qi in range(n):
        ok = [far[:, qi, qi - 2 - 2 * p] & corner[:, qi, qi - 1 - 2 * p] for p in range(qi // 2)]
        first_bad = [jnp.where(o, qi // 2, p) for p, o in enumerate(ok)]
        counts.append(functools.reduce(jnp.minimum, first_bad, jnp.full((b,), qi // 2, jnp.int32)))
    far_pairs = jnp.stack(counts, axis=1).astype(jnp.int32)
    return (q_min, k_max, qg_min, kb_max, far_pairs, positions.reshape(b, n, 1, t), positions.reshape(b, s, 1))


def _moba_attn(qk, vt, bias_table, pos_tables, t, npairs, bounded):
    b, s, _ = qk.shape
    nb = s // MOBA_BLOCK
    ngroups = MOBA_HEADS * MOBA_HEAD_DIM // (LANES * npairs)
    w = npairs * LANES
    q_min, k_max, qg_min, kb_max, far_pairs, pos_row, pos_col = pos_tables
    grid_spec = pltpu.PrefetchScalarGridSpec(
        num_scalar_prefetch=5,
        grid=(b, ngroups, s // t),
        in_specs=[pl.BlockSpec((1, t, w), lambda bi, g, qi, *_: (bi, qi, g)),
                  pl.BlockSpec((1, s, w), lambda bi, g, qi, *_: (bi, 0, ngroups + g)),
                  pl.BlockSpec((1, w, s), lambda bi, g, qi, *_: (bi, g, 0)),
                  pl.BlockSpec((npairs, 2, LANES), lambda bi, g, qi, *_: (g, 0, 0)),
                  pl.BlockSpec((1, 1, 1, t), lambda bi, g, qi, *_: (bi, qi, 0, 0)),
                  pl.BlockSpec((1, s, 1), lambda bi, g, qi, *_: (bi, 0, 0))],
        out_specs=pl.BlockSpec((1, t, w), lambda bi, g, qi, *_: (bi, qi, g)),
        scratch_shapes=[pltpu.VMEM((LANES, w), F32), pltpu.VMEM((2 * npairs, nb, t), F32)],
    )
    return pl.pallas_call(
        functools.partial(_moba_attn_kernel, nb=nb, t=t, npairs=npairs, bounded=bounded),
        grid_spec=grid_spec,
        out_shape=jax.ShapeDtypeStruct((b, s, MOBA_HEADS * MOBA_HEAD_DIM), BF16),
        compiler_params=_params("arbitrary", "arbitrary", "arbitrary"),
        name="moba_attention",
    )(q_min, k_max, qg_min, kb_max, far_pairs, qk, qk, vt, bias_table, pos_row, pos_col)


def _rope_cos_sin(pos, inv_lane):
    half = MLA_ROPE // 2
    ang = pos.astype(F32) * inv_lane
    lane = lax.broadcasted_iota(jnp.int32, (1, LANES), 1)
    sn = jnp.sin(ang)
    return jnp.cos(ang), jnp.where(lane < MLA_NOPE + half, -sn, sn)


def _rope_tab_kernel(pos_ref, inv_ref, c_ref, s_ref):
    c_ref[0], s_ref[0] = _rope_cos_sin(pos_ref[0], inv_ref[...])


def _rope_tables(pos_col, inv_lane, tm):
    b, s, _ = pos_col.shape
    out = pl.BlockSpec((1, tm, LANES), lambda bi, i: (bi, i, 0))
    shape = jax.ShapeDtypeStruct((b, s, LANES), F32)
    return pl.pallas_call(
        _rope_tab_kernel,
        grid=(b, s // tm),
        in_specs=[pl.BlockSpec((1, tm, 1), lambda bi, i: (bi, i, 0)),
                  pl.BlockSpec((1, LANES), lambda bi, i: (0, 0))],
        out_specs=[out, out],
        out_shape=[shape, shape],
        compiler_params=_params("arbitrary", "arbitrary"),
        name="rope_tables",
    )(pos_col, inv_lane)


def _mla_q_kernel(x_ref, sh_ref, sc_ref, g_ref, wdq_ref, qag_ref, wuq_ref, wuqp_ref, qg_ref, qgp_ref,
                  c_ref, s_ref, o_ref):
    h = _rms_mod(x_ref[0], g_ref[...], _mod_row(sh_ref), _mod_row(sc_ref)).astype(BF16)
    dq = _dot(h, wdq_ref[...])
    dqn = (dq * lax.rsqrt(jnp.mean(dq * dq, axis=-1, keepdims=True) + EPS) * qag_ref[...]).astype(BF16)
    uq = _dot(dqn, wuq_ref[...])
    up = _dot(dqn, wuqp_ref[...])
    cg = c_ref[0] * qg_ref[...]
    sg = s_ref[0] * qgp_ref[...]
    scale = MLA_QK ** -0.5 * LOG2E
    for hd in range(MLA_HEADS):
        sl = slice(hd * LANES, (hd + 1) * LANES)
        u = uq[:, sl]
        inv = lax.rsqrt(jnp.sum(u * u, axis=-1, keepdims=True) * (1.0 / MLA_QK) + EPS) * scale
        o_ref[0, :, sl] = ((u * cg + up[:, sl] * sg) * inv).astype(BF16)


def _mla_q(x, shift, scale, g, wdq, qag, wuq_p, wuq_partner, qg_p, qg_partner, cos, sin, tm):
    b, s, d = x.shape
    r = wdq.shape[1]
    nq = wuq_p.shape[1]
    tab = pl.BlockSpec((1, tm, LANES), lambda bi, i: (bi, i, 0))
    full = lambda shp: pl.BlockSpec(shp, lambda bi, i: (0, 0))
    return pl.pallas_call(
        _mla_q_kernel,
        grid=(b, s // tm),
        in_specs=[pl.BlockSpec((1, tm, d), lambda bi, i: (bi, i, 0)), _mod_spec(shift, d), _mod_spec(scale, d),
                  full((1, d)),
                  full((d, r)), full((1, r)), full((r, nq)), full((r, nq)), full((1, LANES)), full((1, LANES)),
                  tab, tab],
        out_specs=pl.BlockSpec((1, tm, nq), lambda bi, i: (bi, i, 0)),
        out_shape=jax.ShapeDtypeStruct((b, s, nq), BF16),
        compiler_params=_params("arbitrary", "arbitrary"),
        name="mla_q_path",
    )(x, shift[0], scale[0], g, wdq, qag, wuq_p, wuq_partner, qg_p, qg_partner, cos, sin)


def _mla_qkv_kernel(x_ref, qsh_ref, qsc_ref, qn_ref, wdq_ref, qag_ref, wuq_ref, wuqp_ref, qg_ref, qgp_ref,
                    ksh_ref, ksc_ref, kn_ref, wdkv_ref, kvg_ref, wuk_ref, wuvt_ref, kg_ref, kgp_ref,
                    pos_ref, inv_ref, q_ref, k_ref, vt_ref):
    x = x_ref[0]
    xn = x * lax.rsqrt(jnp.mean(x * x, axis=-1, keepdims=True) + EPS)
    cos, sin = _rope_cos_sin(pos_ref[0], inv_ref[...])
    h = ((xn * qn_ref[...]) * (1.0 + _mod_row(qsc_ref)) + _mod_row(qsh_ref)).astype(BF16)
    dq = _dot(h, wdq_ref[...])
    dqn = (dq * lax.rsqrt(jnp.mean(dq * dq, axis=-1, keepdims=True) + EPS) * qag_ref[...]).astype(BF16)
    uq = _dot(dqn, wuq_ref[...])
    up = _dot(dqn, wuqp_ref[...])
    hs = ((xn * kn_ref[...]) * (1.0 + _mod_row(ksc_ref)) + _mod_row(ksh_ref)).astype(BF16)
    ck = _dot(hs, wdkv_ref[...])
    cg = cos * qg_ref[...]
    sg = sin * qgp_ref[...]
    scale = MLA_QK ** -0.5 * LOG2E

    def q_heads(lo, hi):
        for hd in range(lo, hi):
            sl = slice(hd * LANES, (hd + 1) * LANES)
            u = uq[:, sl]
            inv = lax.rsqrt(jnp.sum(u * u, axis=-1, keepdims=True) * (1.0 / MLA_QK) + EPS) * scale
            q_ref[0, :, sl] = ((u * cg + up[:, sl] * sg) * inv).astype(BF16)

    q_heads(0, MLA_HEADS // 2)
    ckv = ck[:, :MLA_KV_LORA]
    ckvn = (ckv * lax.rsqrt(jnp.mean(ckv * ckv, axis=-1, keepdims=True) + EPS) * kvg_ref[...]).astype(BF16)
    kn = _dot(ckvn, wuk_ref[...])
    vt_ref[0] = _dot_nt(wuvt_ref[...], ckvn).astype(BF16)
    q_heads(MLA_HEADS // 2, MLA_HEADS)
    lane = lax.broadcasted_iota(jnp.int32, (1, LANES), 1)
    on_rope = (lane >= MLA_NOPE) & (lane < MLA_QK)
    tail = ck[:, MLA_KV_LORA:MLA_KV_LORA + LANES]
    kpe = jnp.where(on_rope, pltpu.roll(tail, MLA_NOPE, axis=1), 0.0)
    partner = pltpu.roll(tail, MLA_ROPE, axis=1)
    kcg = cos * kg_ref[...]
    ksg = sin * kgp_ref[...]
    rot = kpe * kcg + partner * ksg
    ss_pe = jnp.sum(kpe * kpe, axis=-1, keepdims=True)
    for hd in range(MLA_HEADS):
        sl = slice(hd * LANES, (hd + 1) * LANES)
        nope = kn[:, sl]
        inv = lax.rsqrt((jnp.sum(nope * nope, axis=-1, keepdims=True) + ss_pe) * (1.0 / MLA_QK) + EPS)
        k_ref[0, :, sl] = ((nope * kcg + rot) * inv).astype(BF16)


def _mla_qkv(x, q_args, kv_args, pos_col, inv_lane, tm):
    b, s, d = x.shape
    nq = q_args[5].shape[1]
    nk = kv_args[5].shape[1]
    nv = kv_args[6].shape[0]
    pos_spec = pl.BlockSpec((1, tm, 1), lambda bi, i: (bi, i, 0))
    full = lambda a: pl.BlockSpec(a.shape, lambda bi, i: (0, 0))

    def path(args):
        shift, scale = args[:2]
        return [shift[0], scale[0], *args[2:]], [_mod_spec(shift, d), _mod_spec(scale, d)] + [full(a) for a in args[2:]]

    q_ins, q_specs = path(q_args)
    kv_ins, kv_specs = path(kv_args)
    return pl.pallas_call(
        _mla_qkv_kernel,
        grid=(b, s // tm),
        in_specs=([pl.BlockSpec((1, tm, d), lambda bi, i: (bi, i, 0))] + q_specs + kv_specs
                  + [pos_spec, full(inv_lane)]),
        out_specs=[pl.BlockSpec((1, tm, nq), lambda bi, i: (bi, i, 0)),
                   pl.BlockSpec((1, tm, nk), lambda bi, i: (bi, i, 0)),
                   pl.BlockSpec((1, nv, tm), lambda bi, i: (bi, 0, i))],
        out_shape=[jax.ShapeDtypeStruct((b, s, nq), BF16), jax.ShapeDtypeStruct((b, s, nk), BF16),
                   jax.ShapeDtypeStruct((b, nv, s), BF16)],
        compiler_params=_params("arbitrary", "arbitrary"),
        name="mla_qkv_proj",
    )(x, *q_ins, *kv_ins, pos_col, inv_lane)


def _mla_attn_kernel(q_ref, k_ref, vt_ref, o_ref, *, t, nh, bounded):
    qi = pl.program_id(2)
    q2 = q_ref[0]
    q_heads = [q2[:, h * LANES:(h + 1) * LANES] for h in range(nh)]

    def chunk(j, carry, masked, width=1):
        carry = list(carry)
        start = pl.multiple_of(j * t, t)
        s_heads = [_dot_nt(k_ref[0, pl.ds(start, width * t), h * LANES:(h + 1) * LANES], q_heads[h])
                   for h in range(nh)]
        if masked:
            row = lax.broadcasted_iota(jnp.int32, (t, t), 0)
            col = lax.broadcasted_iota(jnp.int32, (t, t), 1)
            allowed = row <= col
        for h in range(nh):
            m, l, acc = carry[h]
            if width > 1:
                half = t // 2
                for r in range(2 * width):
                    p = jnp.exp2(s_heads[h][r * half:(r + 1) * half])
                    l = l + jnp.sum(p, axis=0, keepdims=True)
                    piece_start = pl.multiple_of(start + r * half, half)
                    vt_piece = vt_ref[0, h * MLA_V:(h + 1) * MLA_V, pl.ds(piece_start, half)]
                    acc = acc + _dot(vt_piece, p.astype(BF16))
                carry[h] = (m, l, acc)
                continue
            s = jnp.where(allowed, s_heads[h], NEG) if masked else s_heads[h]
            vt_h = vt_ref[0, h * MLA_V:(h + 1) * MLA_V, pl.ds(start, t)]
            carry[h] = _softmax_step_t(s, m, l, acc, vt_h, bounded)
        return tuple(carry)

    def own_chunk_bounded(carry):
        carry = list(carry)
        half = t // 2
        start = pl.multiple_of(qi * t, t)
        k2 = k_ref[0, pl.ds(start, t), :]
        vt = vt_ref[0, :, pl.ds(start, t)]
        causal = (lax.broadcasted_iota(jnp.int32, (half, half), 0) <= lax.broadcasted_iota(jnp.int32, (half, half), 1))
        quads = []
        for h in range(nh):
            kh = k2[:, h * LANES:(h + 1) * LANES]
            qh = q_heads[h]
            quads.append((_dot_nt(kh[:half], qh[:half]), _dot_nt(kh[:half], qh[half:]), _dot_nt(kh[half:], qh[half:])))
        for h in range(nh):
            m, l, acc = carry[h]
            s00, s01, s11 = quads[h]
            e00 = jnp.exp2(jnp.where(causal, s00, NEG))
            e01 = jnp.exp2(s01)
            e11 = jnp.exp2(jnp.where(causal, s11, NEG))
            vt0 = vt[h * MLA_V:(h + 1) * MLA_V, :half]
            vt1 = vt[h * MLA_V:(h + 1) * MLA_V, half:]
            l_add = jnp.concatenate(
                [jnp.sum(e00, axis=0, keepdims=True),
                 jnp.sum(e01, axis=0, keepdims=True) + jnp.sum(e11, axis=0, keepdims=True)], axis=1)
            acc_add = jnp.concatenate(
                [_dot(vt0, e00.astype(BF16)), _dot(vt0, e01.astype(BF16)) + _dot(vt1, e11.astype(BF16))], axis=1)
            carry[h] = (m, l + l_add, acc + acc_add)
        return tuple(carry)

    init = (jnp.full((1, t), NEG, F32), jnp.zeros((1, t), F32), jnp.zeros((MLA_V, t), F32))
    if bounded:
        pairs = qi // 2
        carry = lax.fori_loop(0, pairs, lambda p, c: chunk(2 * p, c, False, width=2), (init,) * nh)
        carry = lax.fori_loop(2 * pairs, qi, lambda j, c: chunk(j, c, False), carry)
        carry = own_chunk_bounded(carry)
    else:
        carry = lax.fori_loop(0, qi, lambda j, c: chunk(j, c, False), (init,) * nh)
        carry = chunk(qi, carry, True)
    out_t = jnp.concatenate([acc / l for (_, l, acc) in carry], axis=0)
    o_ref[0] = out_t.T.astype(BF16)


def _mla_attn(q, k, vt, t, nh, bounded):
    b, s, _ = q.shape
    return pl.pallas_call(
        functools.partial(_mla_attn_kernel, t=t, nh=nh, bounded=bounded),
        grid=(b, MLA_HEADS // nh, s // t),
        in_specs=[pl.BlockSpec((1, t, nh * LANES), lambda bi, g, qi: (bi, qi, g)),
                  pl.BlockSpec((1, s, nh * LANES), lambda bi, g, qi: (bi, 0, g)),
                  pl.BlockSpec((1, nh * MLA_V, s), lambda bi, g, qi: (bi, g, 0))],
        out_specs=pl.BlockSpec((1, t, nh * MLA_V), lambda bi, g, qi: (bi, qi, g)),
        out_shape=jax.ShapeDtypeStruct((b, s, MLA_HEADS * MLA_V), BF16),
        compiler_params=_params("arbitrary", "arbitrary", "arbitrary"),
        name="mla_attention",
    )(q, k, vt)


def _t5_bias_table(rel_bias):
    n = jnp.arange(MASKED_DIST)
    max_exact = REL_BUCKETS // 2
    nf = jnp.maximum(n, 1).astype(F32)
    large = max_exact + (jnp.log(nf / max_exact) / math.log(REL_MAX_DIST / max_exact)
                         * (REL_BUCKETS - max_exact)).astype(jnp.int32)
    large = jnp.minimum(large, REL_BUCKETS - 1)
    bucket = jnp.where(n < max_exact, n, large)
    table = jnp.concatenate([rel_bias.T[:, bucket] * LOG2E, jnp.full((MOBA_HEADS, 1), NEG, F32)], axis=1)
    return table.reshape(MOBA_HEADS // 2, 2, BIAS_TABLE)


def _pad_heads(w, width):
    k = w.shape[0]
    w3 = w.reshape(k, -1, width)
    return jnp.pad(w3, ((0, 0), (0, 0), (0, LANES - width))).reshape(k, -1)


def _pad_lanes(g):
    return jnp.pad(g, (0, LANES - g.shape[0])).reshape(1, LANES)


def _swap_rope_halves(a):
    half = MLA_ROPE // 2
    return jnp.concatenate([jnp.zeros_like(a[..., :MLA_NOPE]), a[..., MLA_NOPE + half:],
                            a[..., MLA_NOPE:MLA_NOPE + half]], axis=-1)


def _scores_bounded(q_gain, k_gain, dim, scale, extra=0.0):
    bound = dim * jnp.max(jnp.abs(q_gain)) * jnp.max(jnp.abs(k_gain)) * scale * BOUND_SLACK + extra
    return bound <= SCORE_BOUND_MAX


def kernel(x, c, positions, ada_w, ada_b, norm_g, ffn_w_gate, ffn_w_up, ffn_w_down, rel_bias,
           moba_w_qkv, moba_q_g, moba_k_g, moba_w_o, kv_ada_w, kv_ada_b, kv_norm_g, w_dkv,
           kv_a_norm_g, w_uk, w_uv, mla_k_g, mla_w_dq, mla_q_a_norm_g, mla_w_uq, mla_q_g, mla_w_o):
    b, s, d = x.shape
    depth = ada_w.shape[0]
    n_moba = moba_w_qkv.shape[0]
    assert b <= SUBLANES and s % max(FFN_ROWS, MOBA_TILE, MLA_TILE) == 0 and MOBA_TILE == 2 * MOBA_BLOCK
    tm_ffn = FFN_ROWS
    tm_proj = PROJ_ROWS

    c_pad = jnp.pad(c, ((0, SUBLANES - b), (0, 0)))
    mods = _mods(c_pad, ada_w, ada_b.reshape(depth, 1, -1), tn=ADALN_COLS)
    kv_mods = _mods(c_pad, kv_ada_w[None], kv_ada_b.reshape(1, 1, -1), tn=d)

    w_gate, w_up, w_down = (w.astype(BF16) for w in (ffn_w_gate, ffn_w_up, ffn_w_down))

    for layer in range(depth):
        sh1, sc1, g1, sh2, sc2, g2, sh3, sc3, g3 = [(mods, layer, k) for k in range(mods.shape[-1] // d)]
        ng = norm_g[layer]

        def ffn(xin, idx, sh, sc, gt, mixer=None):
            return _ffn(xin, sh, sc, gt, ng[2 * idx:2 * idx + 1], w_gate, w_up, w_down, (layer, idx),
                        tm_ffn, FFN_CHUNK, mixer)

        x = ffn(x, 0, sh1, sc1, g1)

        if layer < n_moba:
            n_qk = 2 * MOBA_HEADS * MOBA_HEAD_DIM
            q_scale = MOBA_HEAD_DIM ** -0.5 * LOG2E
            head_gain = jnp.concatenate([jnp.tile(moba_q_g[layer] * q_scale, MOBA_HEADS),
                                         jnp.tile(moba_k_g[layer], MOBA_HEADS)]).reshape(1, -1)
            w_qkv = moba_w_qkv[layer].astype(BF16)
            qk, vt = _moba_proj(x, sh2, sc2, ng[1:2], w_qkv[:, :n_qk], w_qkv[:, n_qk:].T, head_gain,
                                2 * tm_proj, PROJ_COLS)
            bias_table = _t5_bias_table(rel_bias)
            moba = functools.partial(_moba_attn, qk, vt, bias_table, _moba_position_tables(positions, MOBA_TILE),
                                     MOBA_TILE, ATTN_HEADS // 2)
            ok = _scores_bounded(moba_q_g[layer], moba_k_g[layer], MOBA_HEAD_DIM, q_scale,
                                 jnp.max(jnp.abs(rel_bias)) * LOG2E)
            o = lax.cond(ok, lambda: moba(True), lambda: moba(False))
            w_o = moba_w_o[layer]
        else:
            j = layer - n_moba
            if j == 0:
                half = MLA_ROPE // 2
                inv = ROPE_BASE ** (-jnp.arange(half, dtype=F32) / half)
                inv_lane = jnp.zeros((LANES,), F32).at[MLA_NOPE:MLA_QK].set(jnp.tile(inv, 2)).reshape(1, LANES)
                pos_col = positions.reshape(b, s, 1)
                kv_sh, kv_sc = (kv_mods, 0, 0), (kv_mods, 0, 1)
                w_dkv16 = w_dkv.astype(BF16)
                w_kpe = w_dkv16[:, MLA_KV_LORA:]
                w_kpe_swapped = jnp.concatenate([w_kpe[:, half:], w_kpe[:, :half]], axis=1)
                wdkv_p = jnp.pad(jnp.concatenate([w_dkv16, w_kpe_swapped], axis=1),
                                 ((0, 0), (0, LANES - 2 * MLA_ROPE)))
                kv_args = (kv_sh, kv_sc, kv_norm_g.reshape(1, d), wdkv_p, kv_a_norm_g.reshape(1, -1),
                           _pad_heads(w_uk.astype(BF16), MLA_NOPE), w_uv.astype(BF16).T, _pad_lanes(mla_k_g),
                           _pad_lanes(_swap_rope_halves(mla_k_g)))
            w_uq = mla_w_uq[j].astype(BF16)
            w_uq_partner = _swap_rope_halves(w_uq.reshape(w_uq.shape[0], MLA_HEADS, MLA_QK)).reshape(w_uq.shape)
            q_args = (sh2, sc2, ng[1:2], mla_w_dq[j].astype(BF16), mla_q_a_norm_g[j].reshape(1, -1),
                      _pad_heads(w_uq, MLA_QK), _pad_heads(w_uq_partner, MLA_QK),
                      _pad_lanes(mla_q_g[j]), _pad_lanes(_swap_rope_halves(mla_q_g[j])))
            if j == 0:
                q, shared_k, shared_v = _mla_qkv(x, q_args, kv_args, pos_col, inv_lane, tm_proj)
            else:
                if j == 1:
                    cos, sin = _rope_tables(pos_col, inv_lane, tm_proj)
                q = _mla_q(x, *q_args, cos, sin, tm_proj)
            mla = functools.partial(_mla_attn, q, shared_k, shared_v, MLA_TILE, ATTN_HEADS)
            ok = _scores_bounded(mla_q_g[j], mla_k_g, MLA_QK, MLA_QK ** -0.5 * LOG2E)
            o = lax.cond(ok, lambda: mla(True), lambda: mla(False))
            w_o = mla_w_o[j]

        x = ffn(x, 1, sh3, sc3, g3, mixer=(o, g2, w_o.astype(BF16)))
    return x
```
